```python
import math
import jax, jax.numpy as jnp
from jax import lax
import numpy as np

D_MODEL = 1024
BATCH = 8
SEQ = 2048
DEPTH = 4

CTX_LEN = 256
GRID_W = 64
N_MIXERS = 4
D_FF = 4 * D_MODEL
NORM_EPS = 1e-6
ROPE_THETA = 10000.0
CONV_W = 4
CONV_PAD = (2, 1)
CHUNK = 64

GDN_QK_HEADS = 8
GDN_V_HEADS = 16
GDN_HEAD_DIM = 128
GDN_KEY_DIM = GDN_QK_HEADS * GDN_HEAD_DIM
GDN_VALUE_DIM = GDN_V_HEADS * GDN_HEAD_DIM
GDN_CONV_DIM = 2 * GDN_KEY_DIM + GDN_VALUE_DIM
GDN_IN_DIM = GDN_CONV_DIM + GDN_VALUE_DIM + 4 * GDN_V_HEADS

RET_HEADS = 4
RET_DK = 256
RET_DV = 512
RET_KEY_DIM = RET_HEADS * RET_DK
RET_VALUE_DIM = RET_HEADS * RET_DV
RET_IN_DIM = 2 * RET_KEY_DIM + 2 * RET_VALUE_DIM

LRU_WIDTH = 1280
LRU_BLOCKS = 10
LRU_BLOCK_W = LRU_WIDTH // LRU_BLOCKS
LRU_C = 8.0

SWA_Q_HEADS = 16
SWA_KV_HEADS = 4
SWA_GROUP = SWA_Q_HEADS // SWA_KV_HEADS
SWA_HEAD_DIM = 64
SWA_WINDOW = 128
SWA_BLOCK = 128
SWA_Q_DIM = SWA_Q_HEADS * SWA_HEAD_DIM
SWA_KV_DIM = SWA_KV_HEADS * SWA_HEAD_DIM
SWA_IN_DIM = SWA_Q_DIM + 2 * SWA_KV_DIM

kernel_name = 'hybrid_interleaved_diffusion_trunk'


def n_uses(kind):
    return (DEPTH - kind + N_MIXERS - 1) // N_MIXERS


def rmsnorm(x, g):
    xf = x.astype(jnp.float32)
    y = xf * lax.rsqrt(jnp.mean(xf * xf, axis=-1, keepdims=True) + NORM_EPS)
    return (y * g.astype(jnp.float32)).astype(x.dtype)


def l2norm(x):
    xf = x.astype(jnp.float32)
    return (xf * lax.rsqrt(jnp.sum(xf * xf, axis=-1, keepdims=True) + NORM_EPS)).astype(x.dtype)


def head_groupnorm(o, g):
    of = o.astype(jnp.float32)
    mu = jnp.mean(of, axis=-1, keepdims=True)
    var = jnp.mean(jnp.square(of - mu), axis=-1, keepdims=True)
    y = ((of - mu) * lax.rsqrt(var + NORM_EPS)).reshape(o.shape[0], o.shape[1], -1)
    return (y * g.astype(jnp.float32)).astype(o.dtype)


def adaln(cvec, w, b):
    return jnp.split(jax.nn.silu(cvec) @ w + b, 6, axis=-1)


def modulate(h, shift, scale):
    return h * (1.0 + scale) + shift


def sqrelu_mlp(h, w1, w2):
    return jnp.square(jax.nn.relu(h @ w1)) @ w2


def short_conv(x, w):
    return lax.conv_general_dilated(x, w[:, None, :].astype(x.dtype), window_strides=(1,),
                                    padding=[CONV_PAD], dimension_numbers=('NWC', 'WIO', 'NWC'),
                                    feature_group_count=x.shape[-1])


def axial_angles(rows, dh):
    t = jnp.arange(rows * GRID_W)
    row = (t // GRID_W).astype(jnp.float32)
    col = (t % GRID_W).astype(jnp.float32)
    nf = dh // 4
    inv = ROPE_THETA ** (-jnp.arange(nf, dtype=jnp.float32) / nf)
    return row[:, None] * inv, col[:, None] * inv


def axial_rope(x, ang):
    ang_r, ang_c = ang

    def rot(t, a):
        t1, t2 = jnp.split(t, 2, axis=-1)
        cos = jnp.cos(a)[None, :, None, :]
        sin = jnp.sin(a)[None, :, None, :]
        return jnp.concatenate([t1 * cos - t2 * sin, t2 * cos + t1 * sin], axis=-1)

    xr, xcol = jnp.split(x.astype(jnp.float32), 2, axis=-1)
    return jnp.concatenate([rot(xr, ang_r), rot(xcol, ang_c)], axis=-1).astype(x.dtype)


def split_tokens(y, n_ctx, need_ctx):
    return (y[:, :n_ctx], y[:, n_ctx:]) if need_ctx else (None, y)


def chunked_linear_attention(q, k, v, g, beta, s0):
    f32 = jnp.float32
    B, H, L, dk = q.shape
    dv = v.shape[-1]
    n = L // CHUNK

    def blocks(t):
        return t.astype(f32).reshape(B, H, n, CHUNK, *t.shape[3:])

    q, k, v, g = blocks(q), blocks(k), blocks(v), blocks(g)
    G = jnp.cumsum(g, axis=-1)
    idx = jnp.arange(CHUNK)
    incl = idx[:, None] >= idx[None, :]
    seg = G[..., :, None] - G[..., None, :]
    dec = jnp.where(incl, jnp.exp(jnp.where(incl, seg, 0.0)), 0.0)
    qk = jnp.einsum('bhncd,bhnsd->bhncs', q, k) * dec
    u = v
    extra = ()
    if beta is not None:
        beta = blocks(beta)
        strict = idx[:, None] > idx[None, :]
        a = jnp.where(strict, jnp.einsum('bhncd,bhnsd->bhncs', k, k) * dec * beta[..., :, None], 0.0)
        rhs = jnp.concatenate([k * (beta * jnp.exp(G))[..., None], v * beta[..., None]], axis=-1)
        sol = lax.linalg.triangular_solve(a + jnp.eye(CHUNK, dtype=f32), rhs, left_side=True,
                                          lower=True, unit_diagonal=True)
        extra = (sol[..., :dk],)
        u = sol[..., dk:]
    q_dec = q * jnp.exp(G)[..., None]
    k_dec = k * jnp.exp(G[..., -1:] - G)[..., None]
    g_tot = jnp.exp(G[..., -1])
    xs = tuple(jnp.moveaxis(t, 2, 0) for t in (q_dec, qk, u, k_dec, g_tot) + extra)

    def step(s, xs_n):
        qd, qkn, un, kd, gt = xs_n[:5]
        if beta is not None:
            un = un - jnp.einsum('bhcd,bhde->bhce', xs_n[5], s)
        o = jnp.einsum('bhcd,bhde->bhce', qd, s) + jnp.einsum('bhcs,bhse->bhce', qkn, un)
        s = s * gt[..., None, None] + jnp.einsum('bhcd,bhce->bhde', kd, un)
        return s, o

    s_fin, o = lax.scan(step, s0.astype(f32), xs)
    return jnp.moveaxis(o, 0, 2).reshape(B, H, L, dv), s_fin


def bidirectional_prefix_scan(q, k, v, g, beta, n_ctx):
    B, H, _, dk = q.shape
    dv = v.shape[-1]

    def piece(t, side, rev):
        if t is None:
            return None
        t = t[:, :, :n_ctx] if side == 0 else t[:, :, n_ctx:]
        return jnp.flip(t, axis=2) if rev else t

    o_ctx, o_lat = 0.0, 0.0
    for d in range(2):
        rev = d == 1
        b_d = None if beta is None else beta[d]
        args = (q, k, v, g[d], b_d)
        o_c, s_c = chunked_linear_attention(*[piece(t, 0, rev) for t in args],
                                            jnp.zeros((B, H, dk, dv), jnp.float32))
        o_l, _ = chunked_linear_attention(*[piece(t, 1, rev) for t in args], s_c)
        if rev:
            o_c, o_l = jnp.flip(o_c, axis=2), jnp.flip(o_l, axis=2)
        o_ctx = o_ctx + o_c
        o_lat = o_lat + o_l
    return jnp.concatenate([o_ctx, o_lat], axis=2).astype(v.dtype)


def gdn_mixer(hc, hl, w_in, conv_w, a_log, dt_bias, norm_g, w_out, need_ctx):
    n_ctx = hc.shape[1]
    proj = jnp.concatenate([hc, hl], axis=1) @ w_in
    B, T, _ = proj.shape
    qkv = proj[..., :GDN_CONV_DIM]
    z = proj[..., GDN_CONV_DIM:GDN_CONV_DIM + GDN_VALUE_DIM]
    ab = proj[..., GDN_CONV_DIM + GDN_VALUE_DIM:].reshape(B, T, 2, 2, GDN_V_HEADS)
    qkv = jnp.concatenate([jax.nn.silu(short_conv(qkv[:, :n_ctx], conv_w)),
                           jax.nn.silu(short_conv(qkv[:, n_ctx:], conv_w))], axis=1)
    rep = GDN_V_HEADS // GDN_QK_HEADS
    q = l2norm(qkv[..., :GDN_KEY_DIM].reshape(B, T, GDN_QK_HEADS, GDN_HEAD_DIM)) * GDN_HEAD_DIM ** -0.5
    k = l2norm(qkv[..., GDN_KEY_DIM:2 * GDN_KEY_DIM].reshape(B, T, GDN_QK_HEADS, GDN_HEAD_DIM))
    v = qkv[..., 2 * GDN_KEY_DIM:].reshape(B, T, GDN_V_HEADS, GDN_HEAD_DIM)
    q = jnp.repeat(q, rep, axis=2).transpose(0, 2, 1, 3)
    k = jnp.repeat(k, rep, axis=2).transpose(0, 2, 1, 3)
    v = v.transpose(0, 2, 1, 3)
    abf = ab.astype(jnp.float32)
    g = -jnp.exp(a_log.astype(jnp.float32)) * jax.nn.softplus(abf[:, :, :, 0] + dt_bias.astype(jnp.float32))
    beta = jax.nn.sigmoid(abf[:, :, :, 1])
    g = g.transpose(2, 0, 3, 1)
    beta = beta.transpose(2, 0, 3, 1)
    o = bidirectional_prefix_scan(q, k, v, g, beta, n_ctx).transpose(0, 2, 1, 3)
    if not need_ctx:
        o, z = o[:, n_ctx:], z[:, n_ctx:]
    z = z.reshape(o.shape)
    y = (rmsnorm(o, norm_g) * jax.nn.silu(z)).reshape(o.shape[0], o.shape[1], GDN_VALUE_DIM) @ w_out
    return split_tokens(y, n_ctx, need_ctx)


def retention_mixer(hc, hl, w_in, decay_logit, norm_g, w_out, ang, need_ctx):
    n_ctx = hc.shape[1]
    proj = jnp.concatenate([hc, hl], axis=1) @ w_in
    B, T, _ = proj.shape
    q = proj[..., :RET_KEY_DIM].reshape(B, T, RET_HEADS, RET_DK)
    k = proj[..., RET_KEY_DIM:2 * RET_KEY_DIM].reshape(B, T, RET_HEADS, RET_DK)
    v = proj[..., 2 * RET_KEY_DIM:2 * RET_KEY_DIM + RET_VALUE_DIM].reshape(B, T, RET_HEADS, RET_DV)
    gate = proj[..., 2 * RET_KEY_DIM + RET_VALUE_DIM:]
    q = jnp.concatenate([q[:, :n_ctx], axial_rope(q[:, n_ctx:], ang)], axis=1) * RET_DK ** -0.5
    k = jnp.concatenate([k[:, :n_ctx], axial_rope(k[:, n_ctx:], ang)], axis=1)
    q, k, v = (t.transpose(0, 2, 1, 3) for t in (q, k, v))
    log_gamma = jax.nn.log_sigmoid(decay_logit.astype(jnp.float32))
    g = jnp.broadcast_to(log_gamma[:, None, :, None], (2, B, RET_HEADS, T))
    o = bidirectional_prefix_scan(q, k, v, g, None, n_ctx).transpose(0, 2, 1, 3)
    if not need_ctx:
        o, gate = o[:, n_ctx:], gate[:, n_ctx:]
    y = (head_groupnorm(o, norm_g) * jax.nn.silu(gate)) @ w_out
    return split_tokens(y, n_ctx, need_ctx)


def lru_coeffs(x, w_gate, b_gate, lam):
    B, L, W = x.shape
    xb = x.reshape(B, L, LRU_BLOCKS, LRU_BLOCK_W)
    gates = jax.nn.sigmoid((jnp.einsum('blnc,nce->blne', xb, w_gate) + b_gate).astype(jnp.float32))
    r = gates[..., :LRU_BLOCK_W].reshape(B, L, W)
    i = gates[..., LRU_BLOCK_W:].reshape(B, L, W)
    log_a = -LRU_C * r * jax.nn.softplus(-lam.astype(jnp.float32))
    a = jnp.exp(log_a)
    b = jnp.sqrt(-jnp.expm1(2.0 * log_a)) * i * x.astype(jnp.float32)
    return a, b


def linear_scan(a, b, h0):
    def comb(l, r):
        return l[0] * r[0], r[0] * l[1] + r[1]
    a_cum, b_cum = lax.associative_scan(comb, (a, b), axis=1)
    h = a_cum * h0[:, None] + b_cum
    return h, h[:, -1]


def rglru_mixer(hc, hl, w_in, conv_w, conv_b, w_gate, b_gate, lam, w_out, need_ctx):
    n_ctx = hc.shape[1]
    proj = jnp.concatenate([hc, hl], axis=1) @ w_in
    B = proj.shape[0]
    xb, gb = proj[..., :LRU_WIDTH], proj[..., LRU_WIDTH:]
    xs_c = short_conv(xb[:, :n_ctx], conv_w) + conv_b
    xs_l = short_conv(xb[:, n_ctx:], conv_w) + conv_b
    h_ctx, h_lat = 0.0, 0.0
    for d in range(2):
        a_c, b_c = lru_coeffs(xs_c, w_gate[d], b_gate[d], lam[d])
        a_l, b_l = lru_coeffs(xs_l, w_gate[d], b_gate[d], lam[d])
        if d == 1:
            a_c, b_c, a_l, b_l = (jnp.flip(t, axis=1) for t in (a_c, b_c, a_l, b_l))
        hc_d, h_last = linear_scan(a_c, b_c, jnp.zeros((B, LRU_WIDTH), jnp.float32))
        hl_d, _ = linear_scan(a_l, b_l, h_last)
        if d == 1:
            hc_d, hl_d = jnp.flip(hc_d, axis=1), jnp.flip(hl_d, axis=1)
        h_ctx = h_ctx + hc_d
        h_lat = h_lat + hl_d
    h = jnp.concatenate([h_ctx, h_lat], axis=1).astype(proj.dtype)
    if not need_ctx:
        h, gb = h[:, n_ctx:], gb[:, n_ctx:]
    y = (h * jax.nn.gelu(gb)) @ w_out
    return split_tokens(y, n_ctx, need_ctx)


def sink_softmax(s, sink):
    sk = jnp.broadcast_to(sink.astype(jnp.float32)[None, :, :, None, None], s.shape[:-1] + (1,))
    return jax.nn.softmax(jnp.concatenate([s, sk], axis=-1), axis=-1)[..., :-1]


def swa_mixer(hc, hl, w_in, sink, w_out, ang, need_ctx):
    n_ctx = hc.shape[1]
    proj = jnp.concatenate([hc, hl], axis=1) @ w_in
    B, T, _ = proj.shape
    L = T - n_ctx
    q = proj[..., :SWA_Q_DIM].reshape(B, T, SWA_Q_HEADS, SWA_HEAD_DIM)
    k = proj[..., SWA_Q_DIM:SWA_Q_DIM + SWA_KV_DIM].reshape(B, T, SWA_KV_HEADS, SWA_HEAD_DIM)
    v = proj[..., SWA_Q_DIM + SWA_KV_DIM:].reshape(B, T, SWA_KV_HEADS, SWA_HEAD_DIM)
    kc, vc = k[:, :n_ctx], v[:, :n_ctx]
    ql = axial_rope(q[:, n_ctx:], ang).reshape(B, L, SWA_KV_HEADS, SWA_GROUP, SWA_HEAD_DIM)
    kl = axial_rope(k[:, n_ctx:], ang)
    vl = v[:, n_ctx:]
    scale = SWA_HEAD_DIM ** -0.5
    sink_kg = sink.reshape(SWA_KV_HEADS, SWA_GROUP)
    nb = L // SWA_BLOCK

    def band(t):
        tp = jnp.pad(t, ((0, 0), (SWA_BLOCK, SWA_BLOCK), (0, 0), (0, 0)))
        tp = tp.reshape(B, nb + 2, SWA_BLOCK, SWA_KV_HEADS, SWA_HEAD_DIM)
        return jnp.concatenate([tp[:, :-2], tp[:, 1:-1], tp[:, 2:]], axis=2)

    qi = jnp.arange(SWA_BLOCK)
    kj = jnp.arange(3 * SWA_BLOCK)
    rel = kj[None, :] - SWA_BLOCK - qi[:, None]
    key_pos = jnp.arange(nb)[:, None, None] * SWA_BLOCK + kj[None, None, :] - SWA_BLOCK
    mask = (jnp.abs(rel) <= SWA_WINDOW)[None] & (key_pos >= 0) & (key_pos < L)
    nloc = 3 * SWA_BLOCK

    def attend_block(xs):
        qb, kb, vb, mb = xs
        s_loc = jnp.einsum('bqkgd,bskd->bkgqs', qb, kb).astype(jnp.float32) * scale
        s_loc = jnp.where(mb, s_loc, -jnp.inf)
        s_ctx = jnp.einsum('bqkgd,bskd->bkgqs', qb, kc).astype(jnp.float32) * scale
        p = sink_softmax(jnp.concatenate([s_loc, s_ctx], axis=-1), sink_kg).astype(vb.dtype)
        return (jnp.einsum('bkgqs,bskd->bqkgd', p[..., :nloc], vb)
                + jnp.einsum('bkgqs,bskd->bqkgd', p[..., nloc:], vc))

    xs = (jnp.moveaxis(ql.reshape(B, nb, SWA_BLOCK, SWA_KV_HEADS, SWA_GROUP, SWA_HEAD_DIM), 1, 0),
          jnp.moveaxis(band(kl), 1, 0), jnp.moveaxis(band(vl), 1, 0), mask)
    o_l = lax.map(attend_block, xs)
    o_l = jnp.moveaxis(o_l, 0, 1).reshape(B, L, SWA_Q_DIM)
    if not need_ctx:
        return None, o_l @ w_out
    qc = q[:, :n_ctx].reshape(B, n_ctx, SWA_KV_HEADS, SWA_GROUP, SWA_HEAD_DIM)
    s_c = jnp.einsum('bqkgd,bskd->bkgqs', qc, kc).astype(jnp.float32) * scale
    p_c = sink_softmax(s_c, sink_kg).astype(vc.dtype)
    o_c = jnp.einsum('bkgqs,bskd->bqkgd', p_c, vc).reshape(B, n_ctx, SWA_Q_DIM)
    y = jnp.concatenate([o_c, o_l], axis=1) @ w_out
    return split_tokens(y, n_ctx, need_ctx)


def setup_inputs(seed: int = 0) -> dict:
    key = jax.random.key(seed)
    keys = iter(jax.random.split(key, 48))
    f32 = jnp.float32
    D = D_MODEL

    def nrm(shape, scale):
        return jax.random.normal(next(keys), shape, f32) * scale

    def unif(shape, lo, hi):
        return jax.random.uniform(next(keys), shape, f32, lo, hi)

    def gain(shape):
        return 1.0 + nrm(shape, 0.02)

    nA, nB, nC, nD = (n_uses(kind) for kind in range(N_MIXERS))
    dt = jnp.exp(unif((nA, 2, GDN_V_HEADS), math.log(1e-3), math.log(1e-1)))
    ret_gamma_logit = jnp.log(2.0 ** (5.0 + jnp.arange(RET_HEADS, dtype=f32)) - 1.0)
    lru_a = unif((nC, 2, LRU_WIDTH), 0.9, 0.999) ** (1.0 / LRU_C)
    return {
        'x': nrm((BATCH, SEQ, D), 1.0),
        'c': nrm((BATCH, D), 1.0),
        'ctx': nrm((BATCH, CTX_LEN, D), 1.0),
        'c_ctx': nrm((D,), 1.0),
        'norm_mix_g': gain((DEPTH, D)),
        'norm_ffn_g': gain((DEPTH, D)),
        'w_mod': nrm((DEPTH, D, 6 * D), 0.5 * D ** -0.5),
        'b_mod': nrm((DEPTH, 6 * D), 0.02),
        'w_ff1': nrm((DEPTH, D, D_FF), D ** -0.5),
        'w_ff2': nrm((DEPTH, D_FF, D), D_FF ** -0.5),
        'norm_out_g': gain((D,)),
        'gdn_w_in': nrm((nA, D, GDN_IN_DIM), D ** -0.5),
        'gdn_conv_w': nrm((nA, CONV_W, GDN_CONV_DIM), CONV_W ** -0.5),
        'gdn_a_log': jnp.log(unif((nA, 2, GDN_V_HEADS), 1.0, 16.0)),
        'gdn_dt_bias': dt + jnp.log(-jnp.expm1(-dt)),
        'gdn_norm_g': gain((nA, GDN_HEAD_DIM)),
        'gdn_w_out': nrm((nA, GDN_VALUE_DIM, D), GDN_VALUE_DIM ** -0.5),
        'ret_w_in': nrm((nB, D, RET_IN_DIM), D ** -0.5),
        'ret_decay_logit': ret_gamma_logit + nrm((nB, 2, RET_HEADS), 0.05),
        'ret_norm_g': gain((nB, RET_VALUE_DIM)),
        'ret_w_out': nrm((nB, RET_VALUE_DIM, D), RET_VALUE_DIM ** -0.5),
        'lru_w_in': nrm((nC, D, 2 * LRU_WIDTH), D ** -0.5),
        'lru_conv_w': nrm((nC, CONV_W, LRU_WIDTH), CONV_W ** -0.5),
        'lru_conv_b': nrm((nC, LRU_WIDTH), 0.02),
        'lru_w_gate': nrm((nC, 2, LRU_BLOCKS, LRU_BLOCK_W, 2 * LRU_BLOCK_W), LRU_BLOCK_W ** -0.5),
        'lru_b_gate': nrm((nC, 2, LRU_BLOCKS, 2 * LRU_BLOCK_W), 0.02),
        'lru_lambda': jnp.log(lru_a) - jnp.log1p(-lru_a),
        'lru_w_out': nrm((nC, LRU_WIDTH, D), LRU_WIDTH ** -0.5),
        'swa_w_in': nrm((nD, D, SWA_IN_DIM), D ** -0.5),
        'swa_sink': nrm((nD, SWA_Q_HEADS), 0.5),
        'swa_w_out': nrm((nD, SWA_Q_DIM, D), SWA_Q_DIM ** -0.5),
    }


def reference(x, c, ctx, c_ctx, norm_mix_g, norm_ffn_g, w_mod, b_mod, w_ff1, w_ff2, norm_out_g,
              gdn_w_in, gdn_conv_w, gdn_a_log, gdn_dt_bias, gdn_norm_g, gdn_w_out,
              ret_w_in, ret_decay_logit, ret_norm_g, ret_w_out,
              lru_w_in, lru_conv_w, lru_conv_b, lru_w_gate, lru_b_gate, lru_lambda, lru_w_out,
              swa_w_in, swa_sink, swa_w_out):
    n_tok = x.shape[1]
    rows = n_tok // GRID_W
    ang_swa = axial_angles(rows, SWA_HEAD_DIM)
    ang_ret = axial_angles(rows, RET_DK)
    xl, xc = x, ctx
    for i in range(DEPTH):
        kind, j = i % N_MIXERS, i // N_MIXERS
        need_ctx = i < DEPTH - 1
        ml = [m[:, None, :] for m in adaln(c, w_mod[i], b_mod[i])]
        mc = adaln(c_ctx, w_mod[i], b_mod[i])
        hl = modulate(rmsnorm(xl, norm_mix_g[i]), ml[0], ml[1])
        hc = modulate(rmsnorm(xc, norm_mix_g[i]), mc[0], mc[1])
        if kind == 0:
            yc, yl = gdn_mixer(hc, hl, gdn_w_in[j], gdn_conv_w[j], gdn_a_log[j], gdn_dt_bias[j],
                               gdn_norm_g[j], gdn_w_out[j], need_ctx)
        elif kind == 1:
            yc, yl = retention_mixer(hc, hl, ret_w_in[j], ret_decay_logit[j], ret_norm_g[j],
                                     ret_w_out[j], ang_ret, need_ctx)
        elif kind == 2:
            yc, yl = rglru_mixer(hc, hl, lru_w_in[j], lru_conv_w[j], lru_conv_b[j], lru_w_gate[j],
                                 lru_b_gate[j], lru_lambda[j], lru_w_out[j], need_ctx)
        else:
            yc, yl = swa_mixer(hc, hl, swa_w_in[j], swa_sink[j], swa_w_out[j], ang_swa, need_ctx)
        xl = xl + ml[2] * yl
        hl = modulate(rmsnorm(xl, norm_ffn_g[i]), ml[3], ml[4])
        xl = xl + ml[5] * sqrelu_mlp(hl, w_ff1[i], w_ff2[i])
        if need_ctx:
            xc = xc + mc[2] * yc
            hc = modulate(rmsnorm(xc, norm_ffn_g[i]), mc[3], mc[4])
            xc = xc + mc[5] * sqrelu_mlp(hc, w_ff1[i], w_ff2[i])
    return rmsnorm(xl, norm_out_g)
```

```python
import functools
import math

import jax
import jax.numpy as jnp
from jax import lax
from jax.experimental import pallas as pl
from jax.experimental.pallas import tpu as pltpu

F32 = jnp.float32
BF16 = jnp.bfloat16

D = 1024
D_FF = 4 * D
EPS = 1e-6
DEPTH = 4
ROPE_THETA = 10000.0
GRID_W = 64
CHUNK = 64
MOD_ROWS = 16

GDN_QK_HEADS = 8
GDN_V_HEADS = 16
GDN_HD = 128
GDN_KEY = GDN_QK_HEADS * GDN_HD
GDN_VAL = GDN_V_HEADS * GDN_HD
GDN_CONV = 2 * GDN_KEY + GDN_VAL
GDN_MAIN = GDN_CONV + GDN_VAL

RET_HEADS = 4
RET_DK = 256
RET_DV = 512
RET_KEY = RET_HEADS * RET_DK
RET_VAL = RET_HEADS * RET_DV

LRU_W = 1280
LRU_BLOCKS = 10
LRU_BW = LRU_W // LRU_BLOCKS
LRU_C = 8.0

SWA_QH = 16
SWA_KVH = 4
SWA_HD = 64
SWA_BLOCK = 128
SWA_WINDOW = 128
SWA_QD = SWA_QH * SWA_HD
SWA_KVD = SWA_KVH * SWA_HD

TM_IN = 1024
TM_OUT = 512
TM_CONV = 256
TN_IN = 512
FF_CHUNK = 512
VMEM_LIMIT = 48 * 1024 * 1024


def _dot(a, b):
    return jnp.dot(a, b, preferred_element_type=F32)


def _dot_nt(a, b):
    return lax.dot_general(a, b, (((1,), (1,)), ((), ())), preferred_element_type=F32)


def _dot_tn(a, b):
    return lax.dot_general(a, b, (((0,), (0,)), ((), ())), preferred_element_type=F32)


def _bf(x):
    return x.astype(BF16)


def _split3(x):
    hi = _bf(x)
    r = x - hi.astype(F32)
    mid = _bf(r)
    lo = _bf(r - mid.astype(F32))
    return hi, mid, lo


def _silu(x):
    return x * jax.nn.sigmoid(x)


def _softplus(x):
    return jnp.maximum(x, 0.0) + jnp.log1p(jnp.exp(-jnp.abs(x)))


def _gelu_tanh(x):
    cdf = 0.5 * (1.0 + jnp.tanh(math.sqrt(2.0 / math.pi) * (x + 0.044715 * (x * x * x))))
    return x * cdf


def _cparams(sem):
    return pltpu.CompilerParams(dimension_semantics=sem, vmem_limit_bytes=VMEM_LIMIT)


class _Rows:
    def __init__(self, batch, n_lat, n_ctx):
        self.batch, self.n_lat, self.n_ctx = batch, n_lat, n_ctx
        self.lat_rows = batch * n_lat
        self.rows = self.lat_rows + batch * n_ctx
        self.n_steps = (n_lat + n_ctx) // CHUNK
        self.ctx_steps = n_ctx // CHUNK
        self.lat_steps = n_lat // CHUNK

    def mod_row(self, i, tm):
        return jnp.where(i < self.lat_rows // tm, i // (self.n_lat // tm), self.batch)

    def chunk_block(self, b, d, s, blk):
        nc, nl = self.n_ctx // blk, self.n_lat // blk
        c_ctx = jnp.where(d == 0, s, nc - 1 - s)
        c_lat = jnp.where(d == 0, s - nc, nl - 1 - (s - nc))
        return jnp.where(s < nc, self.lat_rows // blk + b * nc + c_ctx, b * nl + c_lat)

    def lat_chunk(self, d, s, blk):
        nc, nl = self.n_ctx // blk, self.n_lat // blk
        return jnp.where(s < nc, 0, jnp.where(d == 0, s - nc, nl - 1 - (s - nc)))


def _adaln_kernel(c_ref, w_ref, b_ref, o_ref):
    s = _silu(c_ref[...])
    w = w_ref[0]
    s_hi = _bf(s)
    s_lo = _bf(s - s_hi.astype(F32))
    w_hi = _bf(w)
    w_lo = _bf(w - w_hi.astype(F32))
    y = _dot(s_hi, w_hi) + (_dot(s_lo, w_hi) + _dot(s_hi, w_lo))
    o_ref[0] = y + b_ref[0]


def _adaln(cc, w_mod, b_mod):
    depth, _, n = w_mod.shape
    tn = 1024
    out = pl.pallas_call(
        _adaln_kernel,
        grid=(depth, n // tn),
        in_specs=[pl.BlockSpec((MOD_ROWS, D), lambda l, j: (0, 0)),
                  pl.BlockSpec((1, D, tn), lambda l, j: (l, 0, j)),
                  pl.BlockSpec((1, 1, tn), lambda l, j: (l, 0, j))],
        out_specs=pl.BlockSpec((1, MOD_ROWS, tn), lambda l, j: (l, 0, j)),
        out_shape=jax.ShapeDtypeStruct((depth, MOD_ROWS, n), F32),
        compiler_params=_cparams(("parallel", "parallel")),
    )(cc, w_mod, b_mod.reshape(depth, 1, n))
    return out.reshape(depth * MOD_ROWS * 6, 1, D)


def _mod_spec(rw, layer, which, tm):
    base = layer * MOD_ROWS * 6
    return pl.BlockSpec((1, 1, D), lambda i, *_: (base + rw.mod_row(i, tm) * 6 + which, 0, 0))


def _norm_mod(x, gain, shift, scale):
    y = x * lax.rsqrt(jnp.mean(x * x, axis=-1, keepdims=True) + EPS) * gain
    return y * (1.0 + scale) + shift


def _inproj_kernel(x_ref, g_ref, sh_ref, sc_ref, w_ref, *rest, has_extra):
    if has_extra:
        w2_ref, o_ref, o2_ref, h_ref = rest
    else:
        o_ref, h_ref = rest

    @pl.when(pl.program_id(1) == 0)
    def _():
        h = _bf(_norm_mod(x_ref[...], g_ref[...], sh_ref[0], sc_ref[0]))
        h_ref[...] = h
        if has_extra:
            o2_ref[...] = _dot(h, w2_ref[...])

    o_ref[...] = _dot(h_ref[...], w_ref[...])


def _inproj(rw, x, gain, mod3, layer, w, w_extra=None):
    n = w.shape[1]
    tm, tn = TM_IN, TN_IN
    has_extra = w_extra is not None
    in_specs = [pl.BlockSpec((tm, D), lambda i, j: (i, 0)),
                pl.BlockSpec((1, D), lambda i, j: (0, 0)),
                _mod_spec(rw, layer, 0, tm),
                _mod_spec(rw, layer, 1, tm),
                pl.BlockSpec((D, tn), lambda i, j: (0, j))]
    out_specs = [pl.BlockSpec((tm, tn), lambda i, j: (i, j))]
    out_shape = [jax.ShapeDtypeStruct((rw.rows, n), F32)]
    args = [x, gain.reshape(1, D), mod3, mod3, w]
    if has_extra:
        ne = w_extra.shape[1]
        in_specs.append(pl.BlockSpec((D, ne), lambda i, j: (0, 0)))
        out_specs.append(pl.BlockSpec((tm, ne), lambda i, j: (i, 0)))
        out_shape.append(jax.ShapeDtypeStruct((rw.rows, ne), F32))
        args.append(w_extra)
    res = pl.pallas_call(
        functools.partial(_inproj_kernel, has_extra=has_extra),
        grid=(rw.rows // tm, n // tn),
        in_specs=in_specs, out_specs=out_specs, out_shape=out_shape,
        scratch_shapes=[pltpu.VMEM((tm, D), BF16)],
        compiler_params=_cparams(("parallel", "arbitrary")),
    )(*args)
    return res if has_extra else res[0]


def _mlp_kernel(x_ref, g_ref, sh_ref, sc_ref, gt_ref, w1_ref, w2_ref, go_ref, o_ref, h_ref, acc_ref,
                *, final_norm):
    k = pl.program_id(1)

    @pl.when(k == 0)
    def _():
        h_ref[...] = _bf(_norm_mod(x_ref[...], g_ref[...], sh_ref[0], sc_ref[0]))
        acc_ref[...] = jnp.zeros_like(acc_ref)

    a = jnp.square(jnp.maximum(_dot(h_ref[...], w1_ref[...]), 0.0))
    acc_ref[...] += _dot(_bf(a), w2_ref[...])

    @pl.when(k == pl.num_programs(1) - 1)
    def _():
        y = x_ref[...] + gt_ref[0] * acc_ref[...]
        if final_norm:
            y = y * lax.rsqrt(jnp.mean(y * y, axis=-1, keepdims=True) + EPS) * go_ref[...]
        o_ref[...] = y


def _mlp(rw, x, n_rows, gain, mod3, layer, w1, w2, out_gain, final_norm):
    tm, ck = TM_IN, FF_CHUNK
    return pl.pallas_call(
        functools.partial(_mlp_kernel, final_norm=final_norm),
        grid=(n_rows // tm, D_FF // ck),
        in_specs=[pl.BlockSpec((tm, D), lambda i, k: (i, 0)),
                  pl.BlockSpec((1, D), lambda i, k: (0, 0)),
                  _mod_spec(rw, layer, 3, tm),
                  _mod_spec(rw, layer, 4, tm),
                  _mod_spec(rw, layer, 5, tm),
                  pl.BlockSpec((D, ck), lambda i, k: (0, k)),
                  pl.BlockSpec((ck, D), lambda i, k: (k, 0)),
                  pl.BlockSpec((1, D), lambda i, k: (0, 0))],
        out_specs=pl.BlockSpec((tm, D), lambda i, k: (i, 0)),
        out_shape=jax.ShapeDtypeStruct((n_rows, D), F32),
        scratch_shapes=[pltpu.VMEM((tm, D), BF16), pltpu.VMEM((tm, D), F32)],
        compiler_params=_cparams(("parallel", "arbitrary")),
    )(x, gain.reshape(1, D), mod3, mod3, mod3, w1, w2, out_gain.reshape(1, D))


def _gdn_out_kernel(x_ref, gt_ref, o0_ref, o1_ref, z_ref, ng_ref, w_ref, out_ref, a_ref):
    ng = ng_ref[...]
    for h in range(GDN_V_HEADS):
        sl = slice(h * GDN_HD, (h + 1) * GDN_HD)
        o = o0_ref[0, :, sl] + o1_ref[0, :, sl]
        y = o * lax.rsqrt(jnp.mean(o * o, axis=-1, keepdims=True) + EPS) * ng
        a_ref[:, sl] = _bf(y * _silu(z_ref[:, sl]))
    out_ref[...] = x_ref[...] + gt_ref[0] * _dot(a_ref[...], w_ref[...])


def _ret_out_kernel(x_ref, gt_ref, o0_ref, o1_ref, z_ref, ng_ref, w_ref, out_ref, a_ref):
    for h in range(RET_HEADS):
        sl = slice(h * RET_DV, (h + 1) * RET_DV)
        o = o0_ref[0, :, sl] + o1_ref[0, :, sl]
        mu = jnp.mean(o, axis=-1, keepdims=True)
        oc = o - mu
        var = jnp.mean(oc * oc, axis=-1, keepdims=True)
        y = oc * lax.rsqrt(var + EPS) * ng_ref[:, sl]
        a_ref[:, sl] = _bf(y * _silu(z_ref[:, sl]))
    out_ref[...] = x_ref[...] + gt_ref[0] * _dot(a_ref[...], w_ref[...])


def _lru_out_kernel(x_ref, gt_ref, h0_ref, h1_ref, gb_ref, w_ref, out_ref):
    a = (h0_ref[...] + h1_ref[...]) * _gelu_tanh(gb_ref[...])
    out_ref[...] = x_ref[...] + gt_ref[0] * _dot(_bf(a), w_ref[...])


def _swa_out_kernel(x_ref, gt_ref, o_ref, w_ref, out_ref):
    out_ref[...] = x_ref[...] + gt_ref[0] * _dot(_bf(o_ref[...]), w_ref[...])


def _outproj_call(kern, rw, n_rows, layer, x, mod3, extra_specs, extra_args, w, scratch):
    tm = TM_OUT
    k = w.shape[0]
    return pl.pallas_call(
        kern,
        grid=(n_rows // tm,),
        in_specs=[pl.BlockSpec((tm, D), lambda i: (i, 0)), _mod_spec(rw, layer, 2, tm)] + extra_specs
                 + [pl.BlockSpec((k, D), lambda i: (0, 0))],
        out_specs=pl.BlockSpec((tm, D), lambda i: (i, 0)),
        out_shape=jax.ShapeDtypeStruct((n_rows, D), F32),
        scratch_shapes=scratch,
        compiler_params=_cparams(("parallel",)),
    )(x, mod3, *extra_args, w)


def _two_dir_out(kern, rw, layer, x, mod3, o, proj, z_block, norm_g, w):
    tm, k = TM_OUT, w.shape[0]
    ng = norm_g.reshape(1, -1)
    specs = [pl.BlockSpec((1, tm, k), lambda i: (0, i, 0)),
             pl.BlockSpec((1, tm, k), lambda i: (1, i, 0)),
             pl.BlockSpec((tm, k), lambda i: (i, z_block)),
             pl.BlockSpec(ng.shape, lambda i: (0, 0))]
    return _outproj_call(kern, rw, rw.rows, layer, x, mod3, specs, [o, o, proj, ng], w,
                         [pltpu.VMEM((tm, k), BF16)])


def _conv_core(x_ref, p_ref, n_ref, w_ref, first, last):
    x = x_ref[...]
    t = x.shape[0]
    prev = p_ref[...] * jnp.where(first, 0.0, 1.0)
    nxt = n_ref[...] * jnp.where(last, 0.0, 1.0)
    row8 = lax.broadcasted_iota(jnp.int32, prev.shape, 0)
    r2 = pltpu.roll(x, 2, 0)
    r1 = pltpu.roll(x, 1, 0)
    rp = pltpu.roll(x, t - 1, 0)
    f2 = jnp.where(row8 < 2, pltpu.roll(prev, 2, 0), r2[:8])
    f1 = jnp.where(row8 < 1, pltpu.roll(prev, 1, 0), r1[:8])
    l1 = jnp.where(row8 == 7, pltpu.roll(nxt, 7, 0), rp[t - 8:])
    xm2 = jnp.concatenate([f2, r2[8:]], axis=0)
    xm1 = jnp.concatenate([f1, r1[8:]], axis=0)
    xp1 = jnp.concatenate([rp[:t - 8], l1], axis=0)
    w = w_ref[...]
    return w[0:1] * xm2 + w[1:2] * xm1 + w[2:3] * x + w[3:4] * xp1


def _seg_flags(rw, i):
    per = rw.n_lat // TM_CONV
    is_ctx = i >= rw.lat_rows // TM_CONV
    assert rw.n_ctx == TM_CONV
    first = jnp.logical_or(is_ctx, i % per == 0)
    last = jnp.logical_or(is_ctx, i % per == per - 1)
    return first, last


def _gdn_conv_kernel(x_ref, p_ref, n_ref, w_ref, ab_ref, alog_ref, dtb_ref, o_ref, g_ref, b_ref, *, rw):
    i, j = pl.program_id(0), pl.program_id(1)
    first, last = _seg_flags(rw, i)
    s = _silu(_conv_core(x_ref, p_ref, n_ref, w_ref, first, last))
    q_blocks = GDN_KEY // s.shape[1]
    scale = jnp.where(j < q_blocks, GDN_HD ** -0.5, 1.0)
    is_qk = j < 2 * q_blocks
    for h in range(s.shape[1] // GDN_HD):
        sl = slice(h * GDN_HD, (h + 1) * GDN_HD)
        sh = s[:, sl]
        nrm = sh * lax.rsqrt(jnp.sum(sh * sh, axis=-1, keepdims=True) + EPS) * scale
        o_ref[:, sl] = jnp.where(is_qk, nrm, sh)

    @pl.when(j == 0)
    def _():
        ab = ab_ref[...]
        g_ref[...] = -jnp.exp(alog_ref[...]) * _softplus(ab + dtb_ref[...])
        b_ref[...] = jax.nn.sigmoid(ab)


def _lru_conv_kernel(x_ref, p_ref, n_ref, w_ref, bias_ref, o_ref, *, rw):
    first, last = _seg_flags(rw, pl.program_id(0))
    o_ref[...] = _conv_core(x_ref, p_ref, n_ref, w_ref, first, last) + bias_ref[...]


def _conv_specs(rw, cb):
    t8 = TM_CONV // 8
    n8 = rw.rows // 8
    return [pl.BlockSpec((TM_CONV, cb), lambda i, j: (i, j)),
            pl.BlockSpec((8, cb), lambda i, j: (jnp.maximum(i * t8 - 1, 0), j)),
            pl.BlockSpec((8, cb), lambda i, j: (jnp.minimum((i + 1) * t8, n8 - 1), j)),
            pl.BlockSpec((4, cb), lambda i, j: (0, j))]


def _dir_masks(d, n):
    ii = lax.broadcasted_iota(jnp.int32, (n, n), 0)
    jj = lax.broadcasted_iota(jnp.int32, (n, n), 1)
    t = (ii - jj) * jnp.where(d == 0, 1, -1)
    return t, t >= 0, t > 0


def _gdn_core_kernel(qkv_ref, g_ref, b_ref, o_ref, s_ref):
    d, s = pl.program_id(1), pl.program_id(2)

    @pl.when(s == 0)
    def _():
        s_ref[...] = jnp.zeros_like(s_ref)

    c = CHUNK
    t, incl, strict = _dir_masks(d, c)
    g = g_ref[0]
    beta = b_ref[0]
    cum = _bf(jnp.where(incl, 1.0, 0.0))
    cum_t = _bf(jnp.where(t <= 0, 1.0, 0.0))
    parts = _split3(g)
    gc = sum(_dot(cum, p) for p in parts)
    gr = sum(_dot_tn(p, cum_t) for p in parts)
    gtot = jnp.sum(g, axis=0, keepdims=True)

    for hq in range(GDN_QK_HEADS):
        q = qkv_ref[:, hq * GDN_HD:(hq + 1) * GDN_HD]
        k = qkv_ref[:, GDN_KEY + hq * GDN_HD:GDN_KEY + (hq + 1) * GDN_HD]
        kb = _bf(k)
        kk = _dot_nt(kb, kb)
        qk0 = _dot_nt(_bf(q), kb)
        for u in range(GDN_V_HEADS // GDN_QK_HEADS):
            h = hq * (GDN_V_HEADS // GDN_QK_HEADS) + u
            v = qkv_ref[:, 2 * GDN_KEY + h * GDN_HD:2 * GDN_KEY + (h + 1) * GDN_HD]
            gcol = gc[:, h:h + 1]
            grow = gr[h:h + 1, :]
            bcol = beta[:, h:h + 1]
            dec = jnp.where(incl, jnp.exp(jnp.where(incl, gcol - grow, 0.0)), 0.0)
            qk = qk0 * dec
            m = jnp.where(strict, -(kk * dec * bcol), 0.0)
            eg = jnp.exp(gcol)
            rhs = jnp.concatenate([k * (bcol * eg), v * bcol], axis=1)
            pows = [_bf(m)]
            for _ in range(5):
                pows.append(_bf(_dot(pows[-1], pows[-1])))

            def solve(y):
                for p in pows:
                    y = y + _dot(p, _bf(y))
                return y

            x = solve(rhs)
            m_hi = pows[0]
            m_lo = _bf(m - m_hi.astype(F32))
            x_hi = _bf(x)
            x_lo = _bf(x - x_hi.astype(F32))
            resid = (rhs - x) + (_dot(m_hi, x_hi) + (_dot(m_hi, x_lo) + _dot(m_lo, x_hi)))
            x = x + solve(resid)
            w_corr = x[:, :GDN_HD]
            u_raw = x[:, GDN_HD:]
            st = s_ref[h]
            wq = _bf(jnp.concatenate([w_corr, q * eg], axis=0))
            r = _dot(wq, _bf(st))
            un = u_raw - r[:c]
            unb = _bf(un)
            o_ref[0, :, h * GDN_HD:(h + 1) * GDN_HD] = r[c:] + _dot(_bf(qk), unb)
            gt = gtot[:, h:h + 1]
            kd = k * jnp.exp(gt - gcol)
            s_ref[h] = st * jnp.exp(gt) + _dot_tn(_bf(kd), unb)


def _gdn_core(rw, qkv, g_dir, b_dir):
    c = CHUNK
    rb = lambda b, d, s: rw.chunk_block(b, d, s, c)
    return pl.pallas_call(
        _gdn_core_kernel,
        grid=(rw.batch, 2, rw.n_steps),
        in_specs=[pl.BlockSpec((c, GDN_CONV), lambda b, d, s: (rb(b, d, s), 0)),
                  pl.BlockSpec((1, c, GDN_V_HEADS), lambda b, d, s: (d, rb(b, d, s), 0)),
                  pl.BlockSpec((1, c, GDN_V_HEADS), lambda b, d, s: (d, rb(b, d, s), 0))],
        out_specs=pl.BlockSpec((1, c, GDN_VAL), lambda b, d, s: (d, rb(b, d, s), 0)),
        out_shape=jax.ShapeDtypeStruct((2, rw.rows, GDN_VAL), F32),
        scratch_shapes=[pltpu.VMEM((GDN_V_HEADS, GDN_HD, GDN_HD), F32)],
        compiler_params=_cparams(("parallel", "parallel", "arbitrary")),
    )(qkv, g_dir, b_dir)


def _rope_half(x, cos, sin_signed):
    return x * cos + pltpu.roll(x, x.shape[1] // 2, 1) * sin_signed


def _ret_core_kernel(q_ref, k_ref, v_ref, cos_ref, sin_ref, dl_ref, o_ref, s_ref, *, ctx_steps):
    d, s = pl.program_id(1), pl.program_id(2)

    @pl.when(s == 0)
    def _():
        s_ref[...] = jnp.zeros_like(s_ref)

    c = CHUNK
    is_lat = s >= ctx_steps
    cos = jnp.where(is_lat, cos_ref[...], 1.0)
    sin = jnp.where(is_lat, sin_ref[...], 0.0)
    t, incl, _ = _dir_masks(d, c)
    tf = t.astype(F32)
    pos = lax.broadcasted_iota(jnp.int32, (c, 1), 0)
    ridx = jnp.where(d == 0, pos + 1, c - pos).astype(F32)
    dl = dl_ref[0]
    log_gamma = -_softplus(-dl)

    def rope(x):
        hw = RET_DK // 2
        return jnp.concatenate([_rope_half(x[:, :hw], cos[:, :hw], sin[:, :hw]),
                                _rope_half(x[:, hw:], cos[:, hw:], sin[:, hw:])], axis=1)

    for h in range(RET_HEADS):
        lg = log_gamma[:, h:h + 1]
        q = rope(q_ref[:, h * RET_DK:(h + 1) * RET_DK]) * (RET_DK ** -0.5)
        k = rope(k_ref[:, h * RET_DK:(h + 1) * RET_DK])
        vb = _bf(v_ref[:, h * RET_DV:(h + 1) * RET_DV])
        gcum = lg * ridx
        dec = jnp.where(incl, jnp.exp(jnp.where(incl, lg * tf, 0.0)), 0.0)
        qk = _dot_nt(_bf(q), _bf(k)) * dec
        st = s_ref[h]
        o_ref[0, :, h * RET_DV:(h + 1) * RET_DV] = (
            _dot(_bf(q * jnp.exp(gcum)), _bf(st)) + _dot(_bf(qk), vb))
        gtot = lg * float(c)
        kd = k * jnp.exp(gtot - gcum)
        s_ref[h] = st * jnp.exp(gtot) + _dot_tn(_bf(kd), vb)


def _ret_core(rw, proj, cos_tab, sin_tab, decay_logit):
    c = CHUNK
    rb = lambda b, d, s: rw.chunk_block(b, d, s, c)
    dl = jnp.zeros((2, 1, 128), F32).at[:, 0, :RET_HEADS].set(decay_logit)
    return pl.pallas_call(
        functools.partial(_ret_core_kernel, ctx_steps=rw.ctx_steps),
        grid=(rw.batch, 2, rw.n_steps),
        in_specs=[pl.BlockSpec((c, RET_KEY), lambda b, d, s: (rb(b, d, s), 0)),
                  pl.BlockSpec((c, RET_KEY), lambda b, d, s: (rb(b, d, s), 1)),
                  pl.BlockSpec((c, RET_VAL), lambda b, d, s: (rb(b, d, s), 1)),
                  pl.BlockSpec((c, RET_DK), lambda b, d, s: (rw.lat_chunk(d, s, c), 0)),
                  pl.BlockSpec((c, RET_DK), lambda b, d, s: (rw.lat_chunk(d, s, c), 0)),
                  pl.BlockSpec((1, 1, 128), lambda b, d, s: (d, 0, 0))],
        out_specs=pl.BlockSpec((1, c, RET_VAL), lambda b, d, s: (d, rb(b, d, s), 0)),
        out_shape=jax.ShapeDtypeStruct((2, rw.rows, RET_VAL), F32),
        scratch_shapes=[pltpu.VMEM((RET_HEADS, RET_DK, RET_DV), F32)],
        compiler_params=_cparams(("parallel", "parallel", "arbitrary")),
    )(proj, proj, proj, cos_tab, sin_tab, dl)


def _lru_scan_kernel(xs_ref, wg_ref, bg_ref, lam_ref, h_ref, carry_ref, *, rev):
    @pl.when(pl.program_id(1) == 0)
    def _():
        carry_ref[...] = jnp.zeros_like(carry_ref)

    t = xs_ref.shape[0]
    row = lax.broadcasted_iota(jnp.int32, (t, LRU_BW), 0)
    sp = _softplus(-lam_ref[...])
    for n in range(LRU_BLOCKS):
        sl = slice(n * LRU_BW, (n + 1) * LRU_BW)
        xn = xs_ref[:, sl]
        gates = jax.nn.sigmoid(_dot(_bf(xn), wg_ref[n]) + bg_ref[n])
        log_a = -LRU_C * gates[:, :LRU_BW] * sp[:, sl]
        a = jnp.exp(log_a)
        b = jnp.sqrt(-jnp.tanh(log_a) * (a * a + 1.0)) * gates[:, LRU_BW:] * xn
        k = 1
        while k < t:
            if rev:
                keep = row < t - k
                sa = jnp.where(keep, pltpu.roll(a, t - k, 0), 1.0)
                sb = jnp.where(keep, pltpu.roll(b, t - k, 0), 0.0)
            else:
                keep = row >= k
                sa = jnp.where(keep, pltpu.roll(a, k, 0), 1.0)
                sb = jnp.where(keep, pltpu.roll(b, k, 0), 0.0)
            b = b + a * sb
            a = a * sa
            k *= 2
        h = a * carry_ref[:, sl] + b
        h_ref[:, sl] = h
        carry_ref[:, sl] = h[0:1] if rev else h[t - 1:t]


def _lru_scan(rw, xs, w_gate, b_gate, lam, rev):
    t = TM_CONV
    d = 1 if rev else 0
    rb = lambda b, s: rw.chunk_block(b, d, s, t)
    return pl.pallas_call(
        functools.partial(_lru_scan_kernel, rev=rev),
        grid=(rw.batch, (rw.n_lat + rw.n_ctx) // t),
        in_specs=[pl.BlockSpec((t, LRU_W), lambda b, s: (rb(b, s), 0)),
                  pl.BlockSpec((LRU_BLOCKS, LRU_BW, 2 * LRU_BW), lambda b, s: (0, 0, 0)),
                  pl.BlockSpec((LRU_BLOCKS, 1, 2 * LRU_BW), lambda b, s: (0, 0, 0)),
                  pl.BlockSpec((1, LRU_W), lambda b, s: (0, 0))],
        out_specs=pl.BlockSpec((t, LRU_W), lambda b, s: (rb(b, s), 0)),
        out_shape=jax.ShapeDtypeStruct((rw.rows, LRU_W), F32),
        scratch_shapes=[pltpu.VMEM((1, LRU_W), F32)],
        compiler_params=_cparams(("parallel", "arbitrary")),
    )(xs, _bf(w_gate), b_gate.reshape(LRU_BLOCKS, 1, 2 * LRU_BW), lam.reshape(1, LRU_W))


def _rope16(x, tab):
    cos, s1, s2 = tab[:, :128], tab[:, 128:256], tab[:, 256:]
    cols = []
    for cidx in range(x.shape[1] // 128):
        xc = x[:, cidx * 128:(cidx + 1) * 128]
        cols.append(xc * cos + pltpu.roll(xc, 112, 1) * s1 + pltpu.roll(xc, 16, 1) * s2)
    return cols[0] if len(cols) == 1 else jnp.concatenate(cols, axis=1)


def _swa_kernel(sink_ref, q_ref, kp_ref, kc_ref, kn_ref, vp_ref, vc_ref, vn_ref, kx_ref, vx_ref,
                tp_ref, tc_ref, tn_ref, o_ref):
    qb = pl.program_id(1)
    nb = pl.num_programs(1)
    blk = SWA_BLOCK
    n_loc = 3 * blk
    n_keys = n_loc + kx_ref.shape[0]
    tab_c = tc_ref[...]
    q = _rope16(q_ref[...], tab_c) * (SWA_HD ** -0.5)
    k_all = jnp.concatenate([_rope16(kp_ref[...], tp_ref[...]), _rope16(kc_ref[...], tab_c),
                             _rope16(kn_ref[...], tn_ref[...]), kx_ref[...]], axis=0)
    v_all = jnp.concatenate([vp_ref[...], vc_ref[...], vn_ref[...], vx_ref[...]], axis=0)
    qi = lax.broadcasted_iota(jnp.int32, (blk, n_keys), 0)
    kj = lax.broadcasted_iota(jnp.int32, (blk, n_keys), 1)
    rel = kj - blk - qi
    in_window = jnp.logical_and(rel <= SWA_WINDOW, rel >= -SWA_WINDOW)
    has_prev = jnp.logical_or(kj >= blk, qb > 0)
    has_next = jnp.logical_or(kj < 2 * blk, qb < nb - 1)
    valid = jnp.logical_or(kj >= n_loc, jnp.logical_and(in_window, jnp.logical_and(has_prev, has_next)))
    low = lax.broadcasted_iota(jnp.int32, (n_keys, 128), 1) < SWA_HD

    def probs(scores, sink):
        sc = jnp.where(valid, scores, -jnp.inf)
        mx = jnp.maximum(jnp.max(sc, axis=-1, keepdims=True), sink)
        e = jnp.exp(sc - mx)
        den = jnp.sum(e, axis=-1, keepdims=True) + jnp.exp(sink - mx)
        return _bf(e * (1.0 / den))

    group = SWA_QH // SWA_KVH
    for g in range(SWA_KVH):
        col = slice((g // 2) * 128, (g // 2 + 1) * 128)
        native_low = g % 2 == 0
        sel = low if native_low else jnp.logical_not(low)
        k_nat = jnp.where(sel, k_all[:, col], 0.0)
        v_nat = jnp.where(sel, v_all[:, col], 0.0)
        k_oth = pltpu.roll(k_nat, SWA_HD, 1)
        v_oth = pltpu.roll(v_nat, SWA_HD, 1)
        k_lo, k_hi = (k_nat, k_oth) if native_low else (k_oth, k_nat)
        v_lo, v_hi = (v_nat, v_oth) if native_low else (v_oth, v_nat)
        k_lo, k_hi, v_lo, v_hi = _bf(k_lo), _bf(k_hi), _bf(v_lo), _bf(v_hi)
        for cidx in range(group // 2):
            qc = (g * group) // 2 + cidx
            qv = _bf(q[:, qc * 128:(qc + 1) * 128])
            p_lo = probs(_dot_nt(qv, k_lo), sink_ref[2 * qc])
            p_hi = probs(_dot_nt(qv, k_hi), sink_ref[2 * qc + 1])
            o_ref[:, qc * 128:(qc + 1) * 128] = _dot(p_lo, v_lo) + _dot(p_hi, v_hi)


def _swa_core(rw, proj, sink, tab):
    blk = SWA_BLOCK
    nb = rw.n_lat // blk
    kcol = SWA_QD // SWA_KVD
    vcol = kcol + 1
    ctx_blk = lambda b: rw.lat_rows // rw.n_ctx + b
    prev = lambda q: jnp.maximum(q - 1, 0)
    nxt = lambda q: jnp.minimum(q + 1, nb - 1)
    kv = lambda col, f: pl.BlockSpec((blk, SWA_KVD), lambda b, q: (b * nb + f(q), col))
    same = lambda q: q
    return pl.pallas_call(
        _swa_kernel,
        grid=(rw.batch, nb),
        in_specs=[pl.BlockSpec(memory_space=pltpu.SMEM),
                  pl.BlockSpec((blk, SWA_QD), lambda b, q: (b * nb + q, 0)),
                  kv(kcol, prev), kv(kcol, same), kv(kcol, nxt),
                  kv(vcol, prev), kv(vcol, same), kv(vcol, nxt),
                  pl.BlockSpec((rw.n_ctx, SWA_KVD), lambda b, q: (ctx_blk(b), kcol)),
                  pl.BlockSpec((rw.n_ctx, SWA_KVD), lambda b, q: (ctx_blk(b), vcol)),
                  pl.BlockSpec((blk, 384), lambda b, q: (prev(q), 0)),
                  pl.BlockSpec((blk, 384), lambda b, q: (q, 0)),
                  pl.BlockSpec((blk, 384), lambda b, q: (nxt(q), 0))],
        out_specs=pl.BlockSpec((blk, SWA_QD), lambda b, q: (b * nb + q, 0)),
        out_shape=jax.ShapeDtypeStruct((rw.lat_rows, SWA_QD), F32),
        compiler_params=_cparams(("parallel", "parallel")),
    )(sink, proj, proj, proj, proj, proj, proj, proj, proj, proj, tab, tab, tab)


def _axial_angles(n_lat, dh):
    t = jnp.arange(n_lat)
    row = (t // GRID_W).astype(F32)
    col = (t % GRID_W).astype(F32)
    nf = dh // 4
    inv = ROPE_THETA ** (-jnp.arange(nf, dtype=F32) / nf)
    return row[:, None] * inv, col[:, None] * inv


def _ret_tables(n_lat):
    ar, ac = _axial_angles(n_lat, RET_DK)
    cos = jnp.concatenate([jnp.cos(ar), jnp.cos(ar), jnp.cos(ac), jnp.cos(ac)], axis=1)
    sin = jnp.concatenate([-jnp.sin(ar), jnp.sin(ar), -jnp.sin(ac), jnp.sin(ac)], axis=1)
    return cos, sin


def _swa_table(n_lat):
    ar, ac = _axial_angles(n_lat, SWA_HD)
    z = jnp.zeros_like(ar)
    cos = jnp.concatenate([jnp.cos(ar), jnp.cos(ar), jnp.cos(ac), jnp.cos(ac)], axis=1)
    s1 = jnp.concatenate([-jnp.sin(ar), z, -jnp.sin(ac), z], axis=1)
    s2 = jnp.concatenate([z, jnp.sin(ar), z, jnp.sin(ac)], axis=1)
    return jnp.concatenate([jnp.tile(cos, (1, 2)), jnp.tile(s1, (1, 2)), jnp.tile(s2, (1, 2))], axis=1)


def _gdn_layer(rw, layer, x, mod3, gain, w_in, conv_w, a_log, dt_bias, norm_g, w_out):
    w_main = _bf(w_in[:, :GDN_MAIN])
    n_ab = w_in.shape[1] - GDN_MAIN
    w_ab = _bf(jnp.pad(w_in[:, GDN_MAIN:], ((0, 0), (0, 128 - n_ab))))
    proj, ab = _inproj(rw, x, gain, mod3, layer, w_main, w_ab)
    hv = GDN_V_HEADS
    alog_row = jnp.zeros((1, 128), F32).at[0, 0:hv].set(a_log[0]).at[0, 2 * hv:3 * hv].set(a_log[1])
    dtb_row = jnp.zeros((1, 128), F32).at[0, 0:hv].set(dt_bias[0]).at[0, 2 * hv:3 * hv].set(dt_bias[1])
    cb = 512
    row_spec = pl.BlockSpec((TM_CONV, 128), lambda i, j: (i, 0))
    vec_spec = pl.BlockSpec((1, 128), lambda i, j: (0, 0))
    qkv, g_all, b_all = pl.pallas_call(
        functools.partial(_gdn_conv_kernel, rw=rw),
        grid=(rw.rows // TM_CONV, GDN_CONV // cb),
        in_specs=_conv_specs(rw, cb) + [row_spec, vec_spec, vec_spec],
        out_specs=[pl.BlockSpec((TM_CONV, cb), lambda i, j: (i, j)), row_spec, row_spec],
        out_shape=[jax.ShapeDtypeStruct((rw.rows, GDN_CONV), F32),
                   jax.ShapeDtypeStruct((rw.rows, 128), F32),
                   jax.ShapeDtypeStruct((rw.rows, 128), F32)],
        compiler_params=_cparams(("parallel", "arbitrary")),
    )(proj, proj, proj, conv_w, ab, alog_row, dtb_row)
    g_dir = jnp.stack([g_all[:, 0:hv], g_all[:, 2 * hv:3 * hv]])
    b_dir = jnp.stack([b_all[:, hv:2 * hv], b_all[:, 3 * hv:4 * hv]])
    o = _gdn_core(rw, qkv, g_dir, b_dir)
    return _two_dir_out(_gdn_out_kernel, rw, layer, x, mod3, o, proj, GDN_CONV // GDN_VAL, norm_g, _bf(w_out))


def _ret_layer(rw, layer, x, mod3, gain, w_in, decay_logit, norm_g, w_out):
    proj = _inproj(rw, x, gain, mod3, layer, _bf(w_in))
    cos_tab, sin_tab = _ret_tables(rw.n_lat)
    o = _ret_core(rw, proj, cos_tab, sin_tab, decay_logit)
    return _two_dir_out(_ret_out_kernel, rw, layer, x, mod3, o, proj, (2 * RET_KEY + RET_VAL) // RET_VAL,
                        norm_g, _bf(w_out))


def _lru_layer(rw, layer, x, mod3, gain, w_in, conv_w, conv_b, w_gate, b_gate, lam, w_out):
    proj = _inproj(rw, x, gain, mod3, layer, _bf(w_in))
    xs = pl.pallas_call(
        functools.partial(_lru_conv_kernel, rw=rw),
        grid=(rw.rows // TM_CONV, 1),
        in_specs=_conv_specs(rw, LRU_W) + [pl.BlockSpec((1, LRU_W), lambda i, j: (0, 0))],
        out_specs=pl.BlockSpec((TM_CONV, LRU_W), lambda i, j: (i, 0)),
        out_shape=jax.ShapeDtypeStruct((rw.rows, LRU_W), F32),
        compiler_params=_cparams(("parallel", "arbitrary")),
    )(proj, proj, proj, conv_w, conv_b.reshape(1, LRU_W))
    h0 = _lru_scan(rw, xs, w_gate[0], b_gate[0], lam[0], rev=False)
    h1 = _lru_scan(rw, xs, w_gate[1], b_gate[1], lam[1], rev=True)
    tm = TM_OUT
    specs = [pl.BlockSpec((tm, LRU_W), lambda i: (i, 0)),
             pl.BlockSpec((tm, LRU_W), lambda i: (i, 0)),
             pl.BlockSpec((tm, LRU_W), lambda i: (i, 1))]
    return _outproj_call(_lru_out_kernel, rw, rw.rows, layer, x, mod3, specs, [h0, h1, proj], _bf(w_out), [])


def _swa_layer(rw, layer, x, mod3, gain, w_in, sink, w_out):
    proj = _inproj(rw, x, gain, mod3, layer, _bf(w_in))
    o = _swa_core(rw, proj, sink, _swa_table(rw.n_lat))
    specs = [pl.BlockSpec((TM_OUT, SWA_QD), lambda i: (i, 0))]
    return _outproj_call(_swa_out_kernel, rw, rw.lat_rows, layer, x, mod3, specs, [o], _bf(w_out), [])


def kernel(x, c, ctx, c_ctx, norm_mix_g, norm_ffn_g, w_mod, b_mod, w_ff1, w_ff2, norm_out_g, gdn_w_in, gdn_conv_w, gdn_a_log, gdn_dt_bias, gdn_norm_g, gdn_w_out, ret_w_in, ret_decay_logit, ret_norm_g, ret_w_out, lru_w_in, lru_conv_w, lru_conv_b, lru_w_gate, lru_b_gate, lru_lambda, lru_w_out, swa_w_in, swa_sink, swa_w_out):
    batch, n_lat, _ = x.shape
    n_ctx = ctx.shape[1]
    assert batch + 1 <= MOD_ROWS and n_lat % TM_IN == 0 and (batch * n_ctx) % TM_IN == 0
    assert w_mod.shape[0] == DEPTH and gdn_w_in.shape[0] == 1 and swa_w_in.shape[0] == 1
    rw = _Rows(batch, n_lat, n_ctx)
    xf = jnp.concatenate([x.reshape(rw.lat_rows, D), ctx.reshape(batch * n_ctx, D)], axis=0)
    cc = jnp.zeros((MOD_ROWS, D), F32).at[:batch].set(c).at[batch].set(c_ctx)
    mod3 = _adaln(cc, w_mod, b_mod)

    def mlp(layer, xin, n_rows, final):
        return _mlp(rw, xin, n_rows, norm_ffn_g[layer], mod3, layer, _bf(w_ff1[layer]), _bf(w_ff2[layer]),
                    norm_out_g, final)

    xf = _gdn_layer(rw, 0, xf, mod3, norm_mix_g[0], gdn_w_in[0], gdn_conv_w[0], gdn_a_log[0], gdn_dt_bias[0],
                    gdn_norm_g[0], gdn_w_out[0])
    xf = mlp(0, xf, rw.rows, False)
    xf = _ret_layer(rw, 1, xf, mod3, norm_mix_g[1], ret_w_in[0], ret_decay_logit[0], ret_norm_g[0], ret_w_out[0])
    xf = mlp(1, xf, rw.rows, False)
    xf = _lru_layer(rw, 2, xf, mod3, norm_mix_g[2], lru_w_in[0], lru_conv_w[0], lru_conv_b[0], lru_w_gate[0],
                    lru_b_gate[0], lru_lambda[0], lru_w_out[0])
    xf = mlp(2, xf, rw.rows, False)
    xl = _swa_layer(rw, 3, xf, mod3, norm_mix_g[3], swa_w_in[0], swa_sink[0], swa_w_out[0])
    out = mlp(3, xl, rw.lat_rows, True)
    return out.reshape(batch, n_lat, D)
```

```python
import functools
import math

import jax
import jax.numpy as jnp
from jax import lax
from jax.experimental import pallas as pl
from jax.experimental.pallas import tpu as pltpu

F32 = jnp.float32
BF16 = jnp.bfloat16

D = 1024
D_FF = 4 * D
EPS = 1e-6
DEPTH = 4
ROPE_THETA = 10000.0
GRID_W = 64
CHUNK = 64
MOD_ROWS = 16

GDN_QK_HEADS = 8
GDN_V_HEADS = 16
GDN_HD = 128
GDN_KEY = GDN_QK_HEADS * GDN_HD
GDN_VAL = GDN_V_HEADS * GDN_HD
GDN_CONV = 2 * GDN_KEY + GDN_VAL
GDN_MAIN = GDN_CONV + GDN_VAL

RET_HEADS = 4
RET_DK = 256
RET_DV = 512
RET_KEY = RET_HEADS * RET_DK
RET_VAL = RET_HEADS * RET_DV

LRU_W = 1280
LRU_BLOCKS = 10
LRU_BW = LRU_W // LRU_BLOCKS
LRU_C = 8.0

SWA_QH = 16
SWA_KVH = 4
SWA_HD = 64
SWA_BLOCK = 128
SWA_WINDOW = 128
SWA_QD = SWA_QH * SWA_HD
SWA_KVD = SWA_KVH * SWA_HD

TM_IN = 1024
TM_OUT = 512
TM_CONV = 256
TN_IN = 512
FF_CHUNK = 512
VMEM_LIMIT = 48 * 1024 * 1024


def _dot(a, b):
    return jnp.dot(a, b, preferred_element_type=F32)


def _dot_nt(a, b):
    return lax.dot_general(a, b, (((1,), (1,)), ((), ())), preferred_element_type=F32)


def _dot_tn(a, b):
    return lax.dot_general(a, b, (((0,), (0,)), ((), ())), preferred_element_type=F32)


def _bf(x):
    return x.astype(BF16)


def _split3(x):
    hi = _bf(x)
    r = x - hi.astype(F32)
    mid = _bf(r)
    lo = _bf(r - mid.astype(F32))
    return hi, mid, lo


def _silu(x):
    return x * jax.nn.sigmoid(x)


def _softplus(x):
    return jnp.maximum(x, 0.0) + jnp.log1p(jnp.exp(-jnp.abs(x)))


def _gelu_tanh(x):
    cdf = 0.5 * (1.0 + jnp.tanh(math.sqrt(2.0 / math.pi) * (x + 0.044715 * (x * x * x))))
    return x * cdf


def _cparams(sem):
    return pltpu.CompilerParams(dimension_semantics=sem, vmem_limit_bytes=VMEM_LIMIT)


class _Rows:
    def __init__(self, batch, n_lat, n_ctx):
        self.batch, self.n_lat, self.n_ctx = batch, n_lat, n_ctx
        self.lat_rows = batch * n_lat
        self.rows = self.lat_rows + batch * n_ctx
        self.n_steps = (n_lat + n_ctx) // CHUNK
        self.ctx_steps = n_ctx // CHUNK
        self.lat_steps = n_lat // CHUNK

    def mod_row(self, i, tm):
        return jnp.where(i < self.lat_rows // tm, i // (self.n_lat // tm), self.batch)

    def chunk_block(self, b, d, s, blk):
        nc, nl = self.n_ctx // blk, self.n_lat // blk
        c_ctx = jnp.where(d == 0, s, nc - 1 - s)
        c_lat = jnp.where(d == 0, s - nc, nl - 1 - (s - nc))
        return jnp.where(s < nc, self.lat_rows // blk + b * nc + c_ctx, b * nl + c_lat)

    def lat_chunk(self, d, s, blk):
        nc, nl = self.n_ctx // blk, self.n_lat // blk
        return jnp.where(s < nc, 0, jnp.where(d == 0, s - nc, nl - 1 - (s - nc)))


def _adaln_kernel(c_ref, w_ref, b_ref, o_ref):
    s = _silu(c_ref[...])
    w = w_ref[0]
    s_hi = _bf(s)
    s_lo = _bf(s - s_hi.astype(F32))
    w_hi = _bf(w)
    w_lo = _bf(w - w_hi.astype(F32))
    y = _dot(s_hi, w_hi) + (_dot(s_lo, w_hi) + _dot(s_hi, w_lo))
    o_ref[0] = y + b_ref[0]


def _adaln(cc, w_mod, b_mod):
    depth, _, n = w_mod.shape
    tn = 1024
    out = pl.pallas_call(
        _adaln_kernel,
        grid=(depth, n // tn),
        in_specs=[pl.BlockSpec((MOD_ROWS, D), lambda l, j: (0, 0)),
                  pl.BlockSpec((1, D, tn), lambda l, j: (l, 0, j)),
                  pl.BlockSpec((1, 1, tn), lambda l, j: (l, 0, j))],
        out_specs=pl.BlockSpec((1, MOD_ROWS, tn), lambda l, j: (l, 0, j)),
        out_shape=jax.ShapeDtypeStruct((depth, MOD_ROWS, n), F32),
        compiler_params=_cparams(("parallel", "parallel")),
    )(cc, w_mod, b_mod.reshape(depth, 1, n))
    return out.reshape(depth * MOD_ROWS * 6, 1, D)


def _mod_spec(rw, layer, which, tm):
    base = layer * MOD_ROWS * 6
    return pl.BlockSpec((1, 1, D), lambda i, *_: (base + rw.mod_row(i, tm) * 6 + which, 0, 0))


def _norm_mod(x, gain, shift, scale):
    y = x * lax.rsqrt(jnp.mean(x * x, axis=-1, keepdims=True) + EPS) * gain
    return y * (1.0 + scale) + shift


def _inproj_kernel(x_ref, g_ref, sh_ref, sc_ref, w_ref, *rest, has_extra):
    if has_extra:
        w2_ref, o_ref, o2_ref, h_ref = rest
    else:
        o_ref, h_ref = rest

    @pl.when(pl.program_id(1) == 0)
    def _():
        h = _bf(_norm_mod(x_ref[...], g_ref[...], sh_ref[0], sc_ref[0]))
        h_ref[...] = h
        if has_extra:
            o2_ref[...] = _dot(h, w2_ref[...])

    o_ref[...] = _dot(h_ref[...], w_ref[...])


def _inproj(rw, x, gain, mod3, layer, w, w_extra=None):
    n = w.shape[1]
    tm, tn = TM_IN, TN_IN
    has_extra = w_extra is not None
    in_specs = [pl.BlockSpec((tm, D), lambda i, j: (i, 0)),
                pl.BlockSpec((1, D), lambda i, j: (0, 0)),
                _mod_spec(rw, layer, 0, tm),
                _mod_spec(rw, layer, 1, tm),
                pl.BlockSpec((D, tn), lambda i, j: (0, j))]
    out_specs = [pl.BlockSpec((tm, tn), lambda i, j: (i, j))]
    out_shape = [jax.ShapeDtypeStruct((rw.rows, n), F32)]
    args = [x, gain.reshape(1, D), mod3, mod3, w]
    if has_extra:
        ne = w_extra.shape[1]
        in_specs.append(pl.BlockSpec((D, ne), lambda i, j: (0, 0)))
        out_specs.append(pl.BlockSpec((tm, ne), lambda i, j: (i, 0)))
        out_shape.append(jax.ShapeDtypeStruct((rw.rows, ne), F32))
        args.append(w_extra)
    res = pl.pallas_call(
        functools.partial(_inproj_kernel, has_extra=has_extra),
        grid=(rw.rows // tm, n // tn),
        in_specs=in_specs, out_specs=out_specs, out_shape=out_shape,
        scratch_shapes=[pltpu.VMEM((tm, D), BF16)],
        compiler_params=_cparams(("parallel", "arbitrary")),
    )(*args)
    return res if has_extra else res[0]


def _mlp_kernel(x_ref, g_ref, sh_ref, sc_ref, gt_ref, w1_ref, w2_ref, go_ref, o_ref, h_ref, acc_ref,
                *, final_norm):
    k = pl.program_id(1)

    @pl.when(k == 0)
    def _():
        h_ref[...] = _bf(_norm_mod(x_ref[...], g_ref[...], sh_ref[0], sc_ref[0]))
        acc_ref[...] = jnp.zeros_like(acc_ref)

    a = jnp.square(jnp.maximum(_dot(h_ref[...], w1_ref[...]), 0.0))
    acc_ref[...] += _dot(_bf(a), w2_ref[...])

    @pl.when(k == pl.num_programs(1) - 1)
    def _():
        y = x_ref[...] + gt_ref[0] * acc_ref[...]
        if final_norm:
            y = y * lax.rsqrt(jnp.mean(y * y, axis=-1, keepdims=True) + EPS) * go_ref[...]
        o_ref[...] = y


def _mlp(rw, x, n_rows, gain, mod3, layer, w1, w2, out_gain, final_norm):
    tm, ck = TM_IN, FF_CHUNK
    return pl.pallas_call(
        functools.partial(_mlp_kernel, final_norm=final_norm),
        grid=(n_rows // tm, D_FF // ck),
        in_specs=[pl.BlockSpec((tm, D), lambda i, k: (i, 0)),
                  pl.BlockSpec((1, D), lambda i, k: (0, 0)),
                  _mod_spec(rw, layer, 3, tm),
                  _mod_spec(rw, layer, 4, tm),
                  _mod_spec(rw, layer, 5, tm),
                  pl.BlockSpec((D, ck), lambda i, k: (0, k)),
                  pl.BlockSpec((ck, D), lambda i, k: (k, 0)),
                  pl.BlockSpec((1, D), lambda i, k: (0, 0))],
        out_specs=pl.BlockSpec((tm, D), lambda i, k: (i, 0)),
        out_shape=jax.ShapeDtypeStruct((n_rows, D), F32),
        scratch_shapes=[pltpu.VMEM((tm, D), BF16), pltpu.VMEM((tm, D), F32)],
        compiler_params=_cparams(("parallel", "arbitrary")),
    )(x, gain.reshape(1, D), mod3, mod3, mod3, w1, w2, out_gain.reshape(1, D))


def _gdn_out_kernel(x_ref, gt_ref, o0_ref, o1_ref, z_ref, ng_ref, w_ref, out_ref, a_ref):
    ng = ng_ref[...]
    for h in range(GDN_V_HEADS):
        sl = slice(h * GDN_HD, (h + 1) * GDN_HD)
        o = o0_ref[0, :, sl] + o1_ref[0, :, sl]
        y = o * lax.rsqrt(jnp.mean(o * o, axis=-1, keepdims=True) + EPS) * ng
        a_ref[:, sl] = _bf(y * _silu(z_ref[:, sl]))
    out_ref[...] = x_ref[...] + gt_ref[0] * _dot(a_ref[...], w_ref[...])


def _ret_out_kernel(x_ref, gt_ref, o0_ref, o1_ref, z_ref, ng_ref, w_ref, out_ref, a_ref):
    for h in range(RET_HEADS):
        sl = slice(h * RET_DV, (h + 1) * RET_DV)
        o = o0_ref[0, :, sl] + o1_ref[0, :, sl]
        mu = jnp.mean(o, axis=-1, keepdims=True)
        oc = o - mu
        var = jnp.mean(oc * oc, axis=-1, keepdims=True)
        y = oc * lax.rsqrt(var + EPS) * ng_ref[:, sl]
        a_ref[:, sl] = _bf(y * _silu(z_ref[:, sl]))
    out_ref[...] = x_ref[...] + gt_ref[0] * _dot(a_ref[...], w_ref[...])


def _lru_out_kernel(x_ref, gt_ref, h0_ref, h1_ref, gb_ref, w_ref, out_ref):
    a = (h0_ref[...] + h1_ref[...]) * _gelu_tanh(gb_ref[...])
    out_ref[...] = x_ref[...] + gt_ref[0] * _dot(_bf(a), w_ref[...])


def _swa_out_kernel(x_ref, gt_ref, o_ref, w_ref, out_ref):
    out_ref[...] = x_ref[...] + gt_ref[0] * _dot(_bf(o_ref[...]), w_ref[...])


def _outproj_call(kern, rw, n_rows, layer, x, mod3, extra_specs, extra_args, w, scratch):
    tm = TM_OUT
    k = w.shape[0]
    return pl.pallas_call(
        kern,
        grid=(n_rows // tm,),
        in_specs=[pl.BlockSpec((tm, D), lambda i: (i, 0)), _mod_spec(rw, layer, 2, tm)] + extra_specs
                 + [pl.BlockSpec((k, D), lambda i: (0, 0))],
        out_specs=pl.BlockSpec((tm, D), lambda i: (i, 0)),
        out_shape=jax.ShapeDtypeStruct((n_rows, D), F32),
        scratch_shapes=scratch,
        compiler_params=_cparams(("parallel",)),
    )(x, mod3, *extra_args, w)


def _two_dir_out(kern, rw, layer, x, mod3, o, proj, z_block, norm_g, w):
    tm, k = TM_OUT, w.shape[0]
    ng = norm_g.reshape(1, -1)
    specs = [pl.BlockSpec((1, tm, k), lambda i: (0, i, 0)),
             pl.BlockSpec((1, tm, k), lambda i: (1, i, 0)),
             pl.BlockSpec((tm, k), lambda i: (i, z_block)),
             pl.BlockSpec(ng.shape, lambda i: (0, 0))]
    return _outproj_call(kern, rw, rw.rows, layer, x, mod3, specs, [o, o, proj, ng], w,
                         [pltpu.VMEM((tm, k), BF16)])


def _conv_core(x_ref, p_ref, n_ref, w_ref, first, last):
    x = x_ref[...]
    t = x.shape[0]
    prev = p_ref[...] * jnp.where(first, 0.0, 1.0)
    nxt = n_ref[...] * jnp.where(last, 0.0, 1.0)
    row8 = lax.broadcasted_iota(jnp.int32, prev.shape, 0)
    r2 = pltpu.roll(x, 2, 0)
    r1 = pltpu.roll(x, 1, 0)
    rp = pltpu.roll(x, t - 1, 0)
    f2 = jnp.where(row8 < 2, pltpu.roll(prev, 2, 0), r2[:8])
    f1 = jnp.where(row8 < 1, pltpu.roll(prev, 1, 0), r1[:8])
    l1 = jnp.where(row8 == 7, pltpu.roll(nxt, 7, 0), rp[t - 8:])
    xm2 = jnp.concatenate([f2, r2[8:]], axis=0)
    xm1 = jnp.concatenate([f1, r1[8:]], axis=0)
    xp1 = jnp.concatenate([rp[:t - 8], l1], axis=0)
    w = w_ref[...]
    return w[0:1] * xm2 + w[1:2] * xm1 + w[2:3] * x + w[3:4] * xp1


def _seg_flags(rw, i):
    per = rw.n_lat // TM_CONV
    is_ctx = i >= rw.lat_rows // TM_CONV
    assert rw.n_ctx == TM_CONV
    first = jnp.logical_or(is_ctx, i % per == 0)
    last = jnp.logical_or(is_ctx, i % per == per - 1)
    return first, last


def _gdn_conv_kernel(x_ref, p_ref, n_ref, w_ref, ab_ref, alog_ref, dtb_ref, o_ref, g_ref, b_ref, *, rw):
    i, j = pl.program_id(0), pl.program_id(1)
    first, last = _seg_flags(rw, i)
    s = _silu(_conv_core(x_ref, p_ref, n_ref, w_ref, first, last))
    q_blocks = GDN_KEY // s.shape[1]
    scale = jnp.where(j < q_blocks, GDN_HD ** -0.5, 1.0)
    is_qk = j < 2 * q_blocks
    for h in range(s.shape[1] // GDN_HD):
        sl = slice(h * GDN_HD, (h + 1) * GDN_HD)
        sh = s[:, sl]
        nrm = sh * lax.rsqrt(jnp.sum(sh * sh, axis=-1, keepdims=True) + EPS) * scale
        o_ref[:, sl] = jnp.where(is_qk, nrm, sh)

    @pl.when(j == 0)
    def _():
        ab = ab_ref[...]
        g_ref[...] = -jnp.exp(alog_ref[...]) * _softplus(ab + dtb_ref[...])
        b_ref[...] = jax.nn.sigmoid(ab)


def _lru_conv_kernel(x_ref, p_ref, n_ref, w_ref, bias_ref, o_ref, *, rw):
    first, last = _seg_flags(rw, pl.program_id(0))
    o_ref[...] = _conv_core(x_ref, p_ref, n_ref, w_ref, first, last) + bias_ref[...]


def _conv_specs(rw, cb):
    t8 = TM_CONV // 8
    n8 = rw.rows // 8
    return [pl.BlockSpec((TM_CONV, cb), lambda i, j: (i, j)),
            pl.BlockSpec((8, cb), lambda i, j: (jnp.maximum(i * t8 - 1, 0), j)),
            pl.BlockSpec((8, cb), lambda i, j: (jnp.minimum((i + 1) * t8, n8 - 1), j)),
            pl.BlockSpec((4, cb), lambda i, j: (0, j))]


def _dir_masks(d, n):
    ii = lax.broadcasted_iota(jnp.int32, (n, n), 0)
    jj = lax.broadcasted_iota(jnp.int32, (n, n), 1)
    t = (ii - jj) * jnp.where(d == 0, 1, -1)
    return t, t >= 0, t > 0


def _gdn_core_kernel(qkv_ref, g_ref, b_ref, o_ref, s_ref):
    d, s = pl.program_id(1), pl.program_id(2)

    @pl.when(s == 0)
    def _():
        s_ref[...] = jnp.zeros_like(s_ref)

    c, hd = CHUNK, GDN_HD
    rep = GDN_V_HEADS // GDN_QK_HEADS
    heads = range(GDN_V_HEADS)
    t, incl, strict = _dir_masks(d, c)
    g = g_ref[0]
    beta = b_ref[0]
    cum = _bf(jnp.where(incl, 1.0, 0.0))
    cum_t = _bf(jnp.where(t <= 0, 1.0, 0.0))
    parts = _split3(g)
    gc = sum(_dot(cum, p) for p in parts)
    gr = sum(_dot_tn(p, cum_t) for p in parts)
    gtot = jnp.sum(g, axis=0, keepdims=True)

    q_l = [qkv_ref[:, hq * hd:(hq + 1) * hd] for hq in range(GDN_QK_HEADS)]
    k_l = [qkv_ref[:, GDN_KEY + hq * hd:GDN_KEY + (hq + 1) * hd] for hq in range(GDN_QK_HEADS)]
    v_l = [qkv_ref[:, 2 * GDN_KEY + h * hd:2 * GDN_KEY + (h + 1) * hd] for h in heads]
    gram = [_dot_nt(_bf(jnp.concatenate([k, q], axis=0)), _bf(k)) for q, k in zip(q_l, k_l)]
    gcol = [gc[:, h:h + 1] for h in heads]
    bcol = [beta[:, h:h + 1] for h in heads]
    dec = [jnp.where(incl, jnp.exp(jnp.where(incl, gcol[h] - gr[h:h + 1, :], 0.0)), 0.0) for h in heads]
    eg = [jnp.exp(gcol[h]) for h in heads]
    m = [jnp.where(strict, -(gram[h // rep][:c] * dec[h] * bcol[h]), 0.0) for h in heads]
    rhs = [jnp.concatenate([k_l[h // rep] * (bcol[h] * eg[h]), v_l[h] * bcol[h]], axis=1) for h in heads]

    m_hi = [_bf(x) for x in m]
    n = m
    p = [_dot(x, x) for x in m_hi]
    for _ in range(4):
        pb = [_bf(x) for x in p]
        r = [_dot(_bf(jnp.concatenate([p[h], n[h]], axis=0)), pb[h]) for h in heads]
        n = [n[h] + p[h] + r[h][c:] for h in heads]
        p = [r[h][:c] for h in heads]
    r = [_dot(_bf(n[h]), _bf(p[h])) for h in heads]
    n = [n[h] + p[h] + r[h] for h in heads]
    nb = [_bf(x) for x in n]

    x = [rhs[h] + _dot(nb[h], _bf(rhs[h])) for h in heads]
    m_lo = [_bf(m[h] - m_hi[h].astype(F32)) for h in heads]
    x_hi = [_bf(x[h]) for h in heads]
    x_lo = [_bf(x[h] - x_hi[h].astype(F32)) for h in heads]
    resid = [(rhs[h] - x[h]) + (_dot(m_hi[h], x_hi[h]) + (_dot(m_hi[h], x_lo[h]) + _dot(m_lo[h], x_hi[h])))
             for h in heads]
    x = [x[h] + (resid[h] + _dot(nb[h], _bf(resid[h]))) for h in heads]

    st = [s_ref[h] for h in heads]
    wq = [_bf(jnp.concatenate([x[h][:, :hd], q_l[h // rep] * eg[h]], axis=0)) for h in heads]
    r = [_dot(wq[h], _bf(st[h])) for h in heads]
    unb = [_bf(x[h][:, hd:] - r[h][:c]) for h in heads]
    qk = [_bf(gram[h // rep][c:] * dec[h]) for h in heads]
    for h in heads:
        o_ref[0, :, h * hd:(h + 1) * hd] = r[h][c:] + _dot(qk[h], unb[h])
    gt = [gtot[:, h:h + 1] for h in heads]
    kd = [_bf(k_l[h // rep] * jnp.exp(gt[h] - gcol[h])) for h in heads]
    for h in heads:
        s_ref[h] = st[h] * jnp.exp(gt[h]) + _dot_tn(kd[h], unb[h])


def _gdn_core(rw, qkv, g_dir, b_dir):
    c = CHUNK
    rb = lambda b, d, s: rw.chunk_block(b, d, s, c)
    return pl.pallas_call(
        _gdn_core_kernel,
        grid=(rw.batch, 2, rw.n_steps),
        in_specs=[pl.BlockSpec((c, GDN_CONV), lambda b, d, s: (rb(b, d, s), 0)),
                  pl.BlockSpec((1, c, GDN_V_HEADS), lambda b, d, s: (d, rb(b, d, s), 0)),
                  pl.BlockSpec((1, c, GDN_V_HEADS), lambda b, d, s: (d, rb(b, d, s), 0))],
        out_specs=pl.BlockSpec((1, c, GDN_VAL), lambda b, d, s: (d, rb(b, d, s), 0)),
        out_shape=jax.ShapeDtypeStruct((2, rw.rows, GDN_VAL), F32),
        scratch_shapes=[pltpu.VMEM((GDN_V_HEADS, GDN_HD, GDN_HD), F32)],
        compiler_params=_cparams(("parallel", "parallel", "arbitrary")),
    )(qkv, g_dir, b_dir)


def _rope_half(x, cos, sin_signed):
    return x * cos + pltpu.roll(x, x.shape[1] // 2, 1) * sin_signed


def _ret_core_kernel(q_ref, k_ref, v_ref, cos_ref, sin_ref, dl_ref, o_ref, s_ref, *, ctx_steps):
    d, s = pl.program_id(1), pl.program_id(2)

    @pl.when(s == 0)
    def _():
        s_ref[...] = jnp.zeros_like(s_ref)

    c = CHUNK
    is_lat = s >= ctx_steps
    cos = jnp.where(is_lat, cos_ref[...], 1.0)
    sin = jnp.where(is_lat, sin_ref[...], 0.0)
    t, incl, _ = _dir_masks(d, c)
    tf = t.astype(F32)
    pos = lax.broadcasted_iota(jnp.int32, (c, 1), 0)
    ridx = jnp.where(d == 0, pos + 1, c - pos).astype(F32)
    dl = dl_ref[0]
    log_gamma = -_softplus(-dl)

    def rope(x):
        hw = RET_DK // 2
        return jnp.concatenate([_rope_half(x[:, :hw], cos[:, :hw], sin[:, :hw]),
                                _rope_half(x[:, hw:], cos[:, hw:], sin[:, hw:])], axis=1)

    for h in range(RET_HEADS):
        lg = log_gamma[:, h:h + 1]
        q = rope(q_ref[:, h * RET_DK:(h + 1) * RET_DK]) * (RET_DK ** -0.5)
        k = rope(k_ref[:, h * RET_DK:(h + 1) * RET_DK])
        vb = _bf(v_ref[:, h * RET_DV:(h + 1) * RET_DV])
        gcum = lg * ridx
        dec = jnp.where(incl, jnp.exp(jnp.where(incl, lg * tf, 0.0)), 0.0)
        qk = _dot_nt(_bf(q), _bf(k)) * dec
        st = s_ref[h]
        o_ref[0, :, h * RET_DV:(h + 1) * RET_DV] = (
            _dot(_bf(q * jnp.exp(gcum)), _bf(st)) + _dot(_bf(qk), vb))
        gtot = lg * float(c)
        kd = k * jnp.exp(gtot - gcum)
        s_ref[h] = st * jnp.exp(gtot) + _dot_tn(_bf(kd), vb)


def _ret_core(rw, proj, cos_tab, sin_tab, decay_logit):
    c = CHUNK
    rb = lambda b, d, s: rw.chunk_block(b, d, s, c)
    dl = jnp.zeros((2, 1, 128), F32).at[:, 0, :RET_HEADS].set(decay_logit)
    return pl.pallas_call(
        functools.partial(_ret_core_kernel, ctx_steps=rw.ctx_steps),
        grid=(rw.batch, 2, rw.n_steps),
        in_specs=[pl.BlockSpec((c, RET_KEY), lambda b, d, s: (rb(b, d, s), 0)),
                  pl.BlockSpec((c, RET_KEY), lambda b, d, s: (rb(b, d, s), 1)),
                  pl.BlockSpec((c, RET_VAL), lambda b, d, s: (rb(b, d, s), 1)),
                  pl.BlockSpec((c, RET_DK), lambda b, d, s: (rw.lat_chunk(d, s, c), 0)),
                  pl.BlockSpec((c, RET_DK), lambda b, d, s: (rw.lat_chunk(d, s, c), 0)),
                  pl.BlockSpec((1, 1, 128), lambda b, d, s: (d, 0, 0))],
        out_specs=pl.BlockSpec((1, c, RET_VAL), lambda b, d, s: (d, rb(b, d, s), 0)),
        out_shape=jax.ShapeDtypeStruct((2, rw.rows, RET_VAL), F32),
        scratch_shapes=[pltpu.VMEM((RET_HEADS, RET_DK, RET_DV), F32)],
        compiler_params=_cparams(("parallel", "parallel", "arbitrary")),
    )(proj, proj, proj, cos_tab, sin_tab, dl)


def _lru_scan_kernel(xs_ref, wg_ref, bg_ref, lam_ref, h_ref, carry_ref, *, rev):
    @pl.when(pl.program_id(1) == 0)
    def _():
        carry_ref[...] = jnp.zeros_like(carry_ref)

    t = xs_ref.shape[0]
    row = lax.broadcasted_iota(jnp.int32, (t, LRU_BW), 0)
    sp = _softplus(-lam_ref[...])
    for n in range(LRU_BLOCKS):
        sl = slice(n * LRU_BW, (n + 1) * LRU_BW)
        xn = xs_ref[:, sl]
        gates = jax.nn.sigmoid(_dot(_bf(xn), wg_ref[n]) + bg_ref[n])
        log_a = -LRU_C * gates[:, :LRU_BW] * sp[:, sl]
        a = jnp.exp(log_a)
        b = jnp.sqrt(-jnp.tanh(log_a) * (a * a + 1.0)) * gates[:, LRU_BW:] * xn
        k = 1
        while k < t:
            if rev:
                keep = row < t - k
                sa = jnp.where(keep, pltpu.roll(a, t - k, 0), 1.0)
                sb = jnp.where(keep, pltpu.roll(b, t - k, 0), 0.0)
            else:
                keep = row >= k
                sa = jnp.where(keep, pltpu.roll(a, k, 0), 1.0)
                sb = jnp.where(keep, pltpu.roll(b, k, 0), 0.0)
            b = b + a * sb
            a = a * sa
            k *= 2
        h = a * carry_ref[:, sl] + b
        h_ref[:, sl] = h
        carry_ref[:, sl] = h[0:1] if rev else h[t - 1:t]


def _lru_scan(rw, xs, w_gate, b_gate, lam, rev):
    t = TM_CONV
    d = 1 if rev else 0
    rb = lambda b, s: rw.chunk_block(b, d, s, t)
    return pl.pallas_call(
        functools.partial(_lru_scan_kernel, rev=rev),
        grid=(rw.batch, (rw.n_lat + rw.n_ctx) // t),
        in_specs=[pl.BlockSpec((t, LRU_W), lambda b, s: (rb(b, s), 0)),
                  pl.BlockSpec((LRU_BLOCKS, LRU_BW, 2 * LRU_BW), lambda b, s: (0, 0, 0)),
                  pl.BlockSpec((LRU_BLOCKS, 1, 2 * LRU_BW), lambda b, s: (0, 0, 0)),
                  pl.BlockSpec((1, LRU_W), lambda b, s: (0, 0))],
        out_specs=pl.BlockSpec((t, LRU_W), lambda b, s: (rb(b, s), 0)),
        out_shape=jax.ShapeDtypeStruct((rw.rows, LRU_W), F32),
        scratch_shapes=[pltpu.VMEM((1, LRU_W), F32)],
        compiler_params=_cparams(("parallel", "arbitrary")),
    )(xs, _bf(w_gate), b_gate.reshape(LRU_BLOCKS, 1, 2 * LRU_BW), lam.reshape(1, LRU_W))


def _rope16(x, tab):
    cos, s1, s2 = tab[:, :128], tab[:, 128:256], tab[:, 256:]
    cols = []
    for cidx in range(x.shape[1] // 128):
        xc = x[:, cidx * 128:(cidx + 1) * 128]
        cols.append(xc * cos + pltpu.roll(xc, 112, 1) * s1 + pltpu.roll(xc, 16, 1) * s2)
    return cols[0] if len(cols) == 1 else jnp.concatenate(cols, axis=1)


def _swa_kernel(sink_ref, q_ref, kp_ref, kc_ref, kn_ref, vp_ref, vc_ref, vn_ref, kx_ref, vx_ref,
                tp_ref, tc_ref, tn_ref, o_ref):
    qb = pl.program_id(1)
    nb = pl.num_programs(1)
    blk = SWA_BLOCK
    n_loc = 3 * blk
    n_keys = n_loc + kx_ref.shape[0]
    tab_c = tc_ref[...]
    q = _rope16(q_ref[...], tab_c) * (SWA_HD ** -0.5)
    k_all = jnp.concatenate([_rope16(kp_ref[...], tp_ref[...]), _rope16(kc_ref[...], tab_c),
                             _rope16(kn_ref[...], tn_ref[...]), kx_ref[...]], axis=0)
    v_all = jnp.concatenate([vp_ref[...], vc_ref[...], vn_ref[...], vx_ref[...]], axis=0)
    qi = lax.broadcasted_iota(jnp.int32, (blk, n_keys), 0)
    kj = lax.broadcasted_iota(jnp.int32, (blk, n_keys), 1)
    rel = kj - blk - qi
    in_window = jnp.logical_and(rel <= SWA_WINDOW, rel >= -SWA_WINDOW)
    has_prev = jnp.logical_or(kj >= blk, qb > 0)
    has_next = jnp.logical_or(kj < 2 * blk, qb < nb - 1)
    valid = jnp.logical_or(kj >= n_loc, jnp.logical_and(in_window, jnp.logical_and(has_prev, has_next)))
    low = lax.broadcasted_iota(jnp.int32, (n_keys, 128), 1) < SWA_HD

    def probs(scores, sink):
        sc = jnp.where(valid, scores, -jnp.inf)
        mx = jnp.maximum(jnp.max(sc, axis=-1, keepdims=True), sink)
        e = jnp.exp(sc - mx)
        den = jnp.sum(e, axis=-1, keepdims=True) + jnp.exp(sink - mx)
        return _bf(e * (1.0 / den))

    group = SWA_QH // SWA_KVH
    for g in range(SWA_KVH):
        col = slice((g // 2) * 128, (g // 2 + 1) * 128)
        native_low = g % 2 == 0
        sel = low if native_low else jnp.logical_not(low)
        k_nat = jnp.where(sel, k_all[:, col], 0.0)
        v_nat = jnp.where(sel, v_all[:, col], 0.0)
        k_oth = pltpu.roll(k_nat, SWA_HD, 1)
        v_oth = pltpu.roll(v_nat, SWA_HD, 1)
        k_lo, k_hi = (k_nat, k_oth) if native_low else (k_oth, k_nat)
        v_lo, v_hi = (v_nat, v_oth) if native_low else (v_oth, v_nat)
        k_lo, k_hi, v_lo, v_hi = _bf(k_lo), _bf(k_hi), _bf(v_lo), _bf(v_hi)
        for cidx in range(group // 2):
            qc = (g * group) // 2 + cidx
            qv = _bf(q[:, qc * 128:(qc + 1) * 128])
            p_lo = probs(_dot_nt(qv, k_lo), sink_ref[2 * qc])
            p_hi = probs(_dot_nt(qv, k_hi), sink_ref[2 * qc + 1])
            o_ref[:, qc * 128:(qc + 1) * 128] = _dot(p_lo, v_lo) + _dot(p_hi, v_hi)


def _swa_core(rw, proj, sink, tab):
    blk = SWA_BLOCK
    nb = rw.n_lat // blk
    kcol = SWA_QD // SWA_KVD
    vcol = kcol + 1
    ctx_blk = lambda b: rw.lat_rows // rw.n_ctx + b
    prev = lambda q: jnp.maximum(q - 1, 0)
    nxt = lambda q: jnp.minimum(q + 1, nb - 1)
    kv = lambda col, f: pl.BlockSpec((blk, SWA_KVD), lambda b, q: (b * nb + f(q), col))
    same = lambda q: q
    return pl.pallas_call(
        _swa_kernel,
        grid=(rw.batch, nb),
        in_specs=[pl.BlockSpec(memory_space=pltpu.SMEM),
                  pl.BlockSpec((blk, SWA_QD), lambda b, q: (b * nb + q, 0)),
                  kv(kcol, prev), kv(kcol, same), kv(kcol, nxt),
                  kv(vcol, prev), kv(vcol, same), kv(vcol, nxt),
                  pl.BlockSpec((rw.n_ctx, SWA_KVD), lambda b, q: (ctx_blk(b), kcol)),
                  pl.BlockSpec((rw.n_ctx, SWA_KVD), lambda b, q: (ctx_blk(b), vcol)),
                  pl.BlockSpec((blk, 384), lambda b, q: (prev(q), 0)),
                  pl.BlockSpec((blk, 384), lambda b, q: (q, 0)),
                  pl.BlockSpec((blk, 384), lambda b, q: (nxt(q), 0))],
        out_specs=pl.BlockSpec((blk, SWA_QD), lambda b, q: (b * nb + q, 0)),
        out_shape=jax.ShapeDtypeStruct((rw.lat_rows, SWA_QD), F32),
        compiler_params=_cparams(("parallel", "parallel")),
    )(sink, proj, proj, proj, proj, proj, proj, proj, proj, proj, tab, tab, tab)


def _axial_angles(n_lat, dh):
    t = jnp.arange(n_lat)
    row = (t // GRID_W).astype(F32)
    col = (t % GRID_W).astype(F32)
    nf = dh // 4
    inv = ROPE_THETA ** (-jnp.arange(nf, dtype=F32) / nf)
    return row[:, None] * inv, col[:, None] * inv


def _ret_tables(n_lat):
    ar, ac = _axial_angles(n_lat, RET_DK)
    cos = jnp.concatenate([jnp.cos(ar), jnp.cos(ar), jnp.cos(ac), jnp.cos(ac)], axis=1)
    sin = jnp.concatenate([-jnp.sin(ar), jnp.sin(ar), -jnp.sin(ac), jnp.sin(ac)], axis=1)
    return cos, sin


def _swa_table(n_lat):
    ar, ac = _axial_angles(n_lat, SWA_HD)
    z = jnp.zeros_like(ar)
    cos = jnp.concatenate([jnp.cos(ar), jnp.cos(ar), jnp.cos(ac), jnp.cos(ac)], axis=1)
    s1 = jnp.concatenate([-jnp.sin(ar), z, -jnp.sin(ac), z], axis=1)
    s2 = jnp.concatenate([z, jnp.sin(ar), z, jnp.sin(ac)], axis=1)
    return jnp.concatenate([jnp.tile(cos, (1, 2)), jnp.tile(s1, (1, 2)), jnp.tile(s2, (1, 2))], axis=1)


def _gdn_layer(rw, layer, x, mod3, gain, w_in, conv_w, a_log, dt_bias, norm_g, w_out):
    w_main = _bf(w_in[:, :GDN_MAIN])
    n_ab = w_in.shape[1] - GDN_MAIN
    w_ab = _bf(jnp.pad(w_in[:, GDN_MAIN:], ((0, 0), (0, 128 - n_ab))))
    proj, ab = _inproj(rw, x, gain, mod3, layer, w_main, w_ab)
    hv = GDN_V_HEADS
    alog_row = jnp.zeros((1, 128), F32).at[0, 0:hv].set(a_log[0]).at[0, 2 * hv:3 * hv].set(a_log[1])
    dtb_row = jnp.zeros((1, 128), F32).at[0, 0:hv].set(dt_bias[0]).at[0, 2 * hv:3 * hv].set(dt_bias[1])
    cb = 512
    row_spec = pl.BlockSpec((TM_CONV, 128), lambda i, j: (i, 0))
    vec_spec = pl.BlockSpec((1, 128), lambda i, j: (0, 0))
    qkv, g_all, b_all = pl.pallas_call(
        functools.partial(_gdn_conv_kernel, rw=rw),
        grid=(rw.rows // TM_CONV, GDN_CONV // cb),
        in_specs=_conv_specs(rw, cb) + [row_spec, vec_spec, vec_spec],
        out_specs=[pl.BlockSpec((TM_CONV, cb), lambda i, j: (i, j)), row_spec, row_spec],
        out_shape=[jax.ShapeDtypeStruct((rw.rows, GDN_CONV), F32),
                   jax.ShapeDtypeStruct((rw.rows, 128), F32),
                   jax.ShapeDtypeStruct((rw.rows, 128), F32)],
        compiler_params=_cparams(("parallel", "arbitrary")),
    )(proj, proj, proj, conv_w, ab, alog_row, dtb_row)
    g_dir = jnp.stack([g_all[:, 0:hv], g_all[:, 2 * hv:3 * hv]])
    b_dir = jnp.stack([b_all[:, hv:2 * hv], b_all[:, 3 * hv:4 * hv]])
    o = _gdn_core(rw, qkv, g_dir, b_dir)
    return _two_dir_out(_gdn_out_kernel, rw, layer, x, mod3, o, proj, GDN_CONV // GDN_VAL, norm_g, _bf(w_out))


def _ret_layer(rw, layer, x, mod3, gain, w_in, decay_logit, norm_g, w_out):
    proj = _inproj(rw, x, gain, mod3, layer, _bf(w_in))
    cos_tab, sin_tab = _ret_tables(rw.n_lat)
    o = _ret_core(rw, proj, cos_tab, sin_tab, decay_logit)
    return _two_dir_out(_ret_out_kernel, rw, layer, x, mod3, o, proj, (2 * RET_KEY + RET_VAL) // RET_VAL,
                        norm_g, _bf(w_out))


def _lru_layer(rw, layer, x, mod3, gain, w_in, conv_w, conv_b, w_gate, b_gate, lam, w_out):
    proj = _inproj(rw, x, gain, mod3, layer, _bf(w_in))
    xs = pl.pallas_call(
        functools.partial(_lru_conv_kernel, rw=rw),
        grid=(rw.rows // TM_CONV, 1),
        in_specs=_conv_specs(rw, LRU_W) + [pl.BlockSpec((1, LRU_W), lambda i, j: (0, 0))],
        out_specs=pl.BlockSpec((TM_CONV, LRU_W), lambda i, j: (i, 0)),
        out_shape=jax.ShapeDtypeStruct((rw.rows, LRU_W), F32),
        compiler_params=_cparams(("parallel", "arbitrary")),
    )(proj, proj, proj, conv_w, conv_b.reshape(1, LRU_W))
    h0 = _lru_scan(rw, xs, w_gate[0], b_gate[0], lam[0], rev=False)
    h1 = _lru_scan(rw, xs, w_gate[1], b_gate[1], lam[1], rev=True)
    tm = TM_OUT
    specs = [pl.BlockSpec((tm, LRU_W), lambda i: (i, 0)),
             pl.BlockSpec((tm, LRU_W), lambda i: (i, 0)),
             pl.BlockSpec((tm, LRU_W), lambda i: (i, 1))]
    return _outproj_call(_lru_out_kernel, rw, rw.rows, layer, x, mod3, specs, [h0, h1, proj], _bf(w_out), [])


def _swa_layer(rw, layer, x, mod3, gain, w_in, sink, w_out):
    proj = _inproj(rw, x, gain, mod3, layer, _bf(w_in))
    o = _swa_core(rw, proj, sink, _swa_table(rw.n_lat))
    specs = [pl.BlockSpec((TM_OUT, SWA_QD), lambda i: (i, 0))]
    return _outproj_call(_swa_out_kernel, rw, rw.lat_rows, layer, x, mod3, specs, [o], _bf(w_out), [])


def kernel(x, c, ctx, c_ctx, norm_mix_g, norm_ffn_g, w_mod, b_mod, w_ff1, w_ff2, norm_out_g, gdn_w_in, gdn_conv_w, gdn_a_log, gdn_dt_bias, gdn_norm_g, gdn_w_out, ret_w_in, ret_decay_logit, ret_norm_g, ret_w_out, lru_w_in, lru_conv_w, lru_conv_b, lru_w_gate, lru_b_gate, lru_lambda, lru_w_out, swa_w_in, swa_sink, swa_w_out):
    batch, n_lat, _ = x.shape
    n_ctx = ctx.shape[1]
    assert batch + 1 <= MOD_ROWS and n_lat % TM_IN == 0 and (batch * n_ctx) % TM_IN == 0
    assert w_mod.shape[0] == DEPTH and gdn_w_in.shape[0] == 1 and swa_w_in.shape[0] == 1
    rw = _Rows(batch, n_lat, n_ctx)
    xf = jnp.concatenate([x.reshape(rw.lat_rows, D), ctx.reshape(batch * n_ctx, D)], axis=0)
    cc = jnp.zeros((MOD_ROWS, D), F32).at[:batch].set(c).at[batch].set(c_ctx)
    mod3 = _adaln(cc, w_mod, b_mod)

    def mlp(layer, xin, n_rows, final):
        return _mlp(rw, xin, n_rows, norm_ffn_g[layer], mod3, layer, _bf(w_ff1[layer]), _bf(w_ff2[layer]),
                    norm_out_g, final)

    xf = _gdn_layer(rw, 0, xf, mod3, norm_mix_g[0], gdn_w_in[0], gdn_conv_w[0], gdn_a_log[0], gdn_dt_bias[0],
                    gdn_norm_g[0], gdn_w_out[0])
    xf = mlp(0, xf, rw.rows, False)
    xf = _ret_layer(rw, 1, xf, mod3, norm_mix_g[1], ret_w_in[0], ret_decay_logit[0], ret_norm_g[0], ret_w_out[0])
    xf = mlp(1, xf, rw.rows, False)
    xf = _lru_layer(rw, 2, xf, mod3, norm_mix_g[2], lru_w_in[0], lru_conv_w[0], lru_conv_b[0], lru_w_gate[0],
                    lru_b_gate[0], lru_lambda[0], lru_w_out[0])
    xf = mlp(2, xf, rw.rows, False)
    xl = _swa_layer(rw, 3, xf, mod3, norm_mix_g[3], swa_w_in[0], swa_sink[0], swa_w_out[0])
    out = mlp(3, xl, rw.lat_rows, True)
    return out.reshape(batch, n_lat, D)
```

```python
import functools
import math

import jax
import jax.numpy as jnp
from jax import lax
from jax.experimental import pallas as pl
from jax.experimental.pallas import tpu as pltpu

F32 = jnp.float32
BF16 = jnp.bfloat16

D = 1024
D_FF = 4 * D
EPS = 1e-6
DEPTH = 4
ROPE_THETA = 10000.0
GRID_W = 64
CHUNK = 64
MOD_ROWS = 16

GDN_QK_HEADS = 8
GDN_V_HEADS = 16
GDN_HD = 128
GDN_KEY = GDN_QK_HEADS * GDN_HD
GDN_VAL = GDN_V_HEADS * GDN_HD
GDN_CONV = 2 * GDN_KEY + GDN_VAL
GDN_MAIN = GDN_CONV + GDN_VAL

RET_HEADS = 4
RET_DK = 256
RET_DV = 512
RET_KEY = RET_HEADS * RET_DK
RET_VAL = RET_HEADS * RET_DV

LRU_W = 1280
LRU_BLOCKS = 10
LRU_BW = LRU_W // LRU_BLOCKS
LRU_C = 8.0

SWA_QH = 16
SWA_KVH = 4
SWA_HD = 64
SWA_BLOCK = 128
SWA_WINDOW = 128
SWA_QD = SWA_QH * SWA_HD
SWA_KVD = SWA_KVH * SWA_HD

TM_IN = 1024
TM_OUT = 512
TM_CONV = 256
TN_IN_MAX = 1536
FF_CHUNK = 1024
RET_CHUNK = 256
VMEM_LIMIT = 48 * 1024 * 1024


def _dot(a, b):
    return jnp.dot(a, b, preferred_element_type=F32)


def _dot_nt(a, b):
    return lax.dot_general(a, b, (((1,), (1,)), ((), ())), preferred_element_type=F32)


def _dot_tn(a, b):
    return lax.dot_general(a, b, (((0,), (0,)), ((), ())), preferred_element_type=F32)


def _bf(x):
    return x.astype(BF16)


def _split3(x):
    hi = _bf(x)
    r = x - hi.astype(F32)
    mid = _bf(r)
    lo = _bf(r - mid.astype(F32))
    return hi, mid, lo


def _silu(x):
    return x * jax.nn.sigmoid(x)


def _softplus(x):
    return jnp.maximum(x, 0.0) + jnp.log1p(jnp.exp(-jnp.abs(x)))


def _gelu_tanh(x):
    cdf = 0.5 * (1.0 + jnp.tanh(math.sqrt(2.0 / math.pi) * (x + 0.044715 * (x * x * x))))
    return x * cdf


def _cparams(sem):
    return pltpu.CompilerParams(dimension_semantics=sem, vmem_limit_bytes=VMEM_LIMIT)


class _Rows:
    def __init__(self, batch, n_lat, n_ctx):
        self.batch, self.n_lat, self.n_ctx = batch, n_lat, n_ctx
        self.lat_rows = batch * n_lat
        self.rows = self.lat_rows + batch * n_ctx
        self.n_steps = (n_lat + n_ctx) // CHUNK
        self.ctx_steps = n_ctx // CHUNK
        self.lat_steps = n_lat // CHUNK

    def mod_row(self, i, tm):
        return jnp.where(i < self.lat_rows // tm, i // (self.n_lat // tm), self.batch)

    def chunk_block(self, b, d, s, blk):
        nc, nl = self.n_ctx // blk, self.n_lat // blk
        c_ctx = jnp.where(d == 0, s, nc - 1 - s)
        c_lat = jnp.where(d == 0, s - nc, nl - 1 - (s - nc))
        return jnp.where(s < nc, self.lat_rows // blk + b * nc + c_ctx, b * nl + c_lat)

    def lat_chunk(self, d, s, blk):
        nc, nl = self.n_ctx // blk, self.n_lat // blk
        return jnp.where(s < nc, 0, jnp.where(d == 0, s - nc, nl - 1 - (s - nc)))


def _adaln_kernel(c_ref, w_ref, b_ref, o_ref):
    s = _silu(c_ref[...])
    w = w_ref[0]
    s_hi = _bf(s)
    s_lo = _bf(s - s_hi.astype(F32))
    w_hi = _bf(w)
    w_lo = _bf(w - w_hi.astype(F32))
    y = _dot(s_hi, w_hi) + (_dot(s_lo, w_hi) + _dot(s_hi, w_lo))
    o_ref[0] = y + b_ref[0]


def _adaln(cc, w_mod, b_mod):
    depth, _, n = w_mod.shape
    tn = 1024
    out = pl.pallas_call(
        _adaln_kernel,
        grid=(depth, n // tn),
        in_specs=[pl.BlockSpec((MOD_ROWS, D), lambda l, j: (0, 0)),
                  pl.BlockSpec((1, D, tn), lambda l, j: (l, 0, j)),
                  pl.BlockSpec((1, 1, tn), lambda l, j: (l, 0, j))],
        out_specs=pl.BlockSpec((1, MOD_ROWS, tn), lambda l, j: (l, 0, j)),
        out_shape=jax.ShapeDtypeStruct((depth, MOD_ROWS, n), F32),
        compiler_params=_cparams(("parallel", "parallel")),
    )(cc, w_mod, b_mod.reshape(depth, 1, n))
    return out.reshape(depth * MOD_ROWS * 6, 1, D)


def _mod_spec(rw, layer, which, tm):
    base = layer * MOD_ROWS * 6
    return pl.BlockSpec((1, 1, D), lambda i, *_: (base + rw.mod_row(i, tm) * 6 + which, 0, 0))


def _norm_mod(x, gain, shift, scale):
    y = x * lax.rsqrt(jnp.mean(x * x, axis=-1, keepdims=True) + EPS) * gain
    return y * (1.0 + scale) + shift


def _inproj_kernel(x_ref, g_ref, sh_ref, sc_ref, w_ref, *rest, has_extra):
    if has_extra:
        w2_ref, o_ref, o2_ref, h_ref = rest
    else:
        o_ref, h_ref = rest

    @pl.when(pl.program_id(1) == 0)
    def _():
        h = _bf(_norm_mod(x_ref[...], g_ref[...], sh_ref[0], sc_ref[0]))
        h_ref[...] = h
        if has_extra:
            o2_ref[...] = _dot(h, w2_ref[...])

    o_ref[...] = _dot(h_ref[...], w_ref[...])


def _inproj(rw, x, gain, mod3, layer, w, w_extra=None):
    n = w.shape[1]
    tm = TM_IN
    tn = max(t for t in range(256, TN_IN_MAX + 1, 256) if n % t == 0)
    has_extra = w_extra is not None
    in_specs = [pl.BlockSpec((tm, D), lambda i, j: (i, 0)),
                pl.BlockSpec((1, D), lambda i, j: (0, 0)),
                _mod_spec(rw, layer, 0, tm),
                _mod_spec(rw, layer, 1, tm),
                pl.BlockSpec((D, tn), lambda i, j: (0, j))]
    out_specs = [pl.BlockSpec((tm, tn), lambda i, j: (i, j))]
    out_shape = [jax.ShapeDtypeStruct((rw.rows, n), F32)]
    args = [x, gain.reshape(1, D), mod3, mod3, w]
    if has_extra:
        ne = w_extra.shape[1]
        in_specs.append(pl.BlockSpec((D, ne), lambda i, j: (0, 0)))
        out_specs.append(pl.BlockSpec((tm, ne), lambda i, j: (i, 0)))
        out_shape.append(jax.ShapeDtypeStruct((rw.rows, ne), F32))
        args.append(w_extra)
    res = pl.pallas_call(
        functools.partial(_inproj_kernel, has_extra=has_extra),
        grid=(rw.rows // tm, n // tn),
        in_specs=in_specs, out_specs=out_specs, out_shape=out_shape,
        scratch_shapes=[pltpu.VMEM((tm, D), BF16)],
        compiler_params=_cparams(("parallel", "arbitrary")),
    )(*args)
    return res if has_extra else res[0]


def _mlp_kernel(x_ref, g_ref, sh_ref, sc_ref, gt_ref, w1_ref, w2_ref, go_ref, o_ref, h_ref, acc_ref,
                *, final_norm):
    k = pl.program_id(1)

    @pl.when(k == 0)
    def _():
        h_ref[...] = _bf(_norm_mod(x_ref[...], g_ref[...], sh_ref[0], sc_ref[0]))
        acc_ref[...] = jnp.zeros_like(acc_ref)

    a = jnp.square(jnp.maximum(_dot(h_ref[...], w1_ref[...]), 0.0))
    acc_ref[...] += _dot(_bf(a), w2_ref[...])

    @pl.when(k == pl.num_programs(1) - 1)
    def _():
        y = x_ref[...] + gt_ref[0] * acc_ref[...]
        if final_norm:
            y = y * lax.rsqrt(jnp.mean(y * y, axis=-1, keepdims=True) + EPS) * go_ref[...]
        o_ref[...] = y


def _mlp(rw, x, n_rows, gain, mod3, layer, w1, w2, out_gain, final_norm):
    tm, ck = TM_IN, FF_CHUNK
    return pl.pallas_call(
        functools.partial(_mlp_kernel, final_norm=final_norm),
        grid=(n_rows // tm, D_FF // ck),
        in_specs=[pl.BlockSpec((tm, D), lambda i, k: (i, 0)),
                  pl.BlockSpec((1, D), lambda i, k: (0, 0)),
                  _mod_spec(rw, layer, 3, tm),
                  _mod_spec(rw, layer, 4, tm),
                  _mod_spec(rw, layer, 5, tm),
                  pl.BlockSpec((D, ck), lambda i, k: (0, k)),
                  pl.BlockSpec((ck, D), lambda i, k: (k, 0)),
                  pl.BlockSpec((1, D), lambda i, k: (0, 0))],
        out_specs=pl.BlockSpec((tm, D), lambda i, k: (i, 0)),
        out_shape=jax.ShapeDtypeStruct((n_rows, D), F32),
        scratch_shapes=[pltpu.VMEM((tm, D), BF16), pltpu.VMEM((tm, D), F32)],
        compiler_params=_cparams(("parallel", "arbitrary")),
    )(x, gain.reshape(1, D), mod3, mod3, mod3, w1, w2, out_gain.reshape(1, D))


def _gdn_out_kernel(x_ref, gt_ref, o0_ref, o1_ref, z_ref, ng_ref, w_ref, out_ref, a_ref):
    ng = ng_ref[...]
    for h in range(GDN_V_HEADS):
        sl = slice(h * GDN_HD, (h + 1) * GDN_HD)
        o = o0_ref[0, :, sl] + o1_ref[0, :, sl]
        y = o * lax.rsqrt(jnp.mean(o * o, axis=-1, keepdims=True) + EPS) * ng
        a_ref[:, sl] = _bf(y * _silu(z_ref[:, sl]))
    out_ref[...] = x_ref[...] + gt_ref[0] * _dot(a_ref[...], w_ref[...])


def _ret_out_kernel(x_ref, gt_ref, o0_ref, o1_ref, z_ref, ng_ref, w_ref, out_ref, a_ref):
    for h in range(RET_HEADS):
        sl = slice(h * RET_DV, (h + 1) * RET_DV)
        o = o0_ref[0, :, sl] + o1_ref[0, :, sl]
        mu = jnp.mean(o, axis=-1, keepdims=True)
        oc = o - mu
        var = jnp.mean(oc * oc, axis=-1, keepdims=True)
        y = oc * lax.rsqrt(var + EPS) * ng_ref[:, sl]
        a_ref[:, sl] = _bf(y * _silu(z_ref[:, sl]))
    out_ref[...] = x_ref[...] + gt_ref[0] * _dot(a_ref[...], w_ref[...])


def _lru_out_kernel(x_ref, gt_ref, h0_ref, h1_ref, gb_ref, w_ref, out_ref):
    a = (h0_ref[...] + h1_ref[...]) * _gelu_tanh(gb_ref[...])
    out_ref[...] = x_ref[...] + gt_ref[0] * _dot(_bf(a), w_ref[...])


def _swa_out_kernel(x_ref, gt_ref, o_ref, w_ref, out_ref):
    out_ref[...] = x_ref[...] + gt_ref[0] * _dot(_bf(o_ref[...]), w_ref[...])


def _outproj_call(kern, rw, n_rows, layer, x, mod3, extra_specs, extra_args, w, scratch):
    tm = TM_OUT
    k = w.shape[0]
    return pl.pallas_call(
        kern,
        grid=(n_rows // tm,),
        in_specs=[pl.BlockSpec((tm, D), lambda i: (i, 0)), _mod_spec(rw, layer, 2, tm)] + extra_specs
                 + [pl.BlockSpec((k, D), lambda i: (0, 0))],
        out_specs=pl.BlockSpec((tm, D), lambda i: (i, 0)),
        out_shape=jax.ShapeDtypeStruct((n_rows, D), F32),
        scratch_shapes=scratch,
        compiler_params=_cparams(("parallel",)),
    )(x, mod3, *extra_args, w)


def _two_dir_out(kern, rw, layer, x, mod3, o, proj, z_block, norm_g, w):
    tm, k = TM_OUT, w.shape[0]
    ng = norm_g.reshape(1, -1)
    specs = [pl.BlockSpec((1, tm, k), lambda i: (0, i, 0)),
             pl.BlockSpec((1, tm, k), lambda i: (1, i, 0)),
             pl.BlockSpec((tm, k), lambda i: (i, z_block)),
             pl.BlockSpec(ng.shape, lambda i: (0, 0))]
    return _outproj_call(kern, rw, rw.rows, layer, x, mod3, specs, [o, o, proj, ng], w,
                         [pltpu.VMEM((tm, k), BF16)])


def _conv_core(x_ref, p_ref, n_ref, w_ref, first, last):
    x = x_ref[...]
    t = x.shape[0]
    prev = p_ref[...] * jnp.where(first, 0.0, 1.0)
    nxt = n_ref[...] * jnp.where(last, 0.0, 1.0)
    row8 = lax.broadcasted_iota(jnp.int32, prev.shape, 0)
    r2 = pltpu.roll(x, 2, 0)
    r1 = pltpu.roll(x, 1, 0)
    rp = pltpu.roll(x, t - 1, 0)
    f2 = jnp.where(row8 < 2, pltpu.roll(prev, 2, 0), r2[:8])
    f1 = jnp.where(row8 < 1, pltpu.roll(prev, 1, 0), r1[:8])
    l1 = jnp.where(row8 == 7, pltpu.roll(nxt, 7, 0), rp[t - 8:])
    xm2 = jnp.concatenate([f2, r2[8:]], axis=0)
    xm1 = jnp.concatenate([f1, r1[8:]], axis=0)
    xp1 = jnp.concatenate([rp[:t - 8], l1], axis=0)
    w = w_ref[...]
    return w[0:1] * xm2 + w[1:2] * xm1 + w[2:3] * x + w[3:4] * xp1


def _seg_flags(rw, i):
    per = rw.n_lat // TM_CONV
    is_ctx = i >= rw.lat_rows // TM_CONV
    assert rw.n_ctx == TM_CONV
    first = jnp.logical_or(is_ctx, i % per == 0)
    last = jnp.logical_or(is_ctx, i % per == per - 1)
    return first, last


def _gdn_conv_kernel(x_ref, p_ref, n_ref, w_ref, ab_ref, alog_ref, dtb_ref, o_ref, g_ref, b_ref, *, rw):
    i, j = pl.program_id(0), pl.program_id(1)
    first, last = _seg_flags(rw, i)
    s = _silu(_conv_core(x_ref, p_ref, n_ref, w_ref, first, last))
    width = s.shape[1]
    qk_blocks = 2 * GDN_KEY // width

    @pl.when(j < qk_blocks)
    def _():
        for h in range(width // GDN_HD):
            sl = slice(h * GDN_HD, (h + 1) * GDN_HD)
            sh = s[:, sl]
            is_q = j * width + h * GDN_HD < GDN_KEY
            scale = jnp.where(is_q, GDN_HD ** -0.5, 1.0)
            o_ref[:, sl] = sh * (lax.rsqrt(jnp.sum(sh * sh, axis=-1, keepdims=True) + EPS) * scale)

    @pl.when(j >= qk_blocks)
    def _():
        o_ref[...] = s

    @pl.when(j == 0)
    def _():
        ab = ab_ref[...]
        g_ref[...] = -jnp.exp(alog_ref[...]) * _softplus(ab + dtb_ref[...])
        b_ref[...] = jax.nn.sigmoid(ab)


def _lru_conv_kernel(x_ref, p_ref, n_ref, w_ref, bias_ref, o_ref, *, rw):
    first, last = _seg_flags(rw, pl.program_id(0))
    o_ref[...] = _conv_core(x_ref, p_ref, n_ref, w_ref, first, last) + bias_ref[...]


def _conv_specs(rw, cb):
    t8 = TM_CONV // 8
    n8 = rw.rows // 8
    return [pl.BlockSpec((TM_CONV, cb), lambda i, j: (i, j)),
            pl.BlockSpec((8, cb), lambda i, j: (jnp.maximum(i * t8 - 1, 0), j)),
            pl.BlockSpec((8, cb), lambda i, j: (jnp.minimum((i + 1) * t8, n8 - 1), j)),
            pl.BlockSpec((4, cb), lambda i, j: (0, j))]


def _dir_masks(d, n):
    ii = lax.broadcasted_iota(jnp.int32, (n, n), 0)
    jj = lax.broadcasted_iota(jnp.int32, (n, n), 1)
    t = (ii - jj) * jnp.where(d == 0, 1, -1)
    return t, t >= 0, t > 0


def _gdn_core_kernel(qkv_ref, g_ref, b_ref, o_ref, s_ref):
    d, s = pl.program_id(1), pl.program_id(2)

    @pl.when(s == 0)
    def _():
        s_ref[...] = jnp.zeros_like(s_ref)

    c, hd = CHUNK, GDN_HD
    rep = GDN_V_HEADS // GDN_QK_HEADS
    heads = range(GDN_V_HEADS)
    t, incl, strict = _dir_masks(d, c)
    g = g_ref[0]
    beta = b_ref[0]
    cum = _bf(jnp.where(incl, 1.0, 0.0))
    cum_t = _bf(jnp.where(t <= 0, 1.0, 0.0))
    parts = _split3(g)
    gc = sum(_dot(cum, p) for p in parts)
    gr = sum(_dot_tn(p, cum_t) for p in parts)
    gtot = jnp.sum(g, axis=0, keepdims=True)

    q_l = [qkv_ref[:, hq * hd:(hq + 1) * hd] for hq in range(GDN_QK_HEADS)]
    k_l = [qkv_ref[:, GDN_KEY + hq * hd:GDN_KEY + (hq + 1) * hd] for hq in range(GDN_QK_HEADS)]
    v_l = [qkv_ref[:, 2 * GDN_KEY + h * hd:2 * GDN_KEY + (h + 1) * hd] for h in heads]
    gram = [_dot_nt(_bf(jnp.concatenate([k, q], axis=0)), _bf(k)) for q, k in zip(q_l, k_l)]
    gcol = [gc[:, h:h + 1] for h in heads]
    bcol = [beta[:, h:h + 1] for h in heads]
    dec = [jnp.where(incl, jnp.exp(jnp.where(incl, gcol[h] - gr[h:h + 1, :], 0.0)), 0.0) for h in heads]
    eg = [jnp.exp(gcol[h]) for h in heads]
    m = [jnp.where(strict, -(gram[h // rep][:c] * dec[h] * bcol[h]), 0.0) for h in heads]
    rhs = [jnp.concatenate([k_l[h // rep] * (bcol[h] * eg[h]), v_l[h] * bcol[h]], axis=1) for h in heads]

    m_hi = [_bf(x) for x in m]
    n = m
    p = [_dot(x, x) for x in m_hi]
    for _ in range(4):
        pb = [_bf(x) for x in p]
        r = [_dot(_bf(jnp.concatenate([p[h], n[h]], axis=0)), pb[h]) for h in heads]
        n = [n[h] + p[h] + r[h][c:] for h in heads]
        p = [r[h][:c] for h in heads]
    r = [_dot(_bf(n[h]), _bf(p[h])) for h in heads]
    n = [n[h] + p[h] + r[h] for h in heads]
    nb = [_bf(x) for x in n]

    x = [rhs[h] + _dot(nb[h], _bf(rhs[h])) for h in heads]
    m_lo = [_bf(m[h] - m_hi[h].astype(F32)) for h in heads]
    x_hi = [_bf(x[h]) for h in heads]
    x_lo = [_bf(x[h] - x_hi[h].astype(F32)) for h in heads]
    resid = [(rhs[h] - x[h]) + (_dot(m_hi[h], x_hi[h]) + (_dot(m_hi[h], x_lo[h]) + _dot(m_lo[h], x_hi[h])))
             for h in heads]
    x = [x[h] + (resid[h] + _dot(nb[h], _bf(resid[h]))) for h in heads]

    st = [s_ref[h] for h in heads]
    wq = [_bf(jnp.concatenate([x[h][:, :hd], q_l[h // rep] * eg[h]], axis=0)) for h in heads]
    r = [_dot(wq[h], _bf(st[h])) for h in heads]
    unb = [_bf(x[h][:, hd:] - r[h][:c]) for h in heads]
    qk = [_bf(gram[h // rep][c:] * dec[h]) for h in heads]
    for h in heads:
        o_ref[0, :, h * hd:(h + 1) * hd] = r[h][c:] + _dot(qk[h], unb[h])
    gt = [gtot[:, h:h + 1] for h in heads]
    kd = [_bf(k_l[h // rep] * jnp.exp(gt[h] - gcol[h])) for h in heads]
    for h in heads:
        s_ref[h] = st[h] * jnp.exp(gt[h]) + _dot_tn(kd[h], unb[h])


def _gdn_core(rw, qkv, g_dir, b_dir):
    c = CHUNK
    rb = lambda b, d, s: rw.chunk_block(b, d, s, c)
    return pl.pallas_call(
        _gdn_core_kernel,
        grid=(rw.batch, 2, rw.n_steps),
        in_specs=[pl.BlockSpec((c, GDN_CONV), lambda b, d, s: (rb(b, d, s), 0)),
                  pl.BlockSpec((1, c, GDN_V_HEADS), lambda b, d, s: (d, rb(b, d, s), 0)),
                  pl.BlockSpec((1, c, GDN_V_HEADS), lambda b, d, s: (d, rb(b, d, s), 0))],
        out_specs=pl.BlockSpec((1, c, GDN_VAL), lambda b, d, s: (d, rb(b, d, s), 0)),
        out_shape=jax.ShapeDtypeStruct((2, rw.rows, GDN_VAL), F32),
        scratch_shapes=[pltpu.VMEM((GDN_V_HEADS, GDN_HD, GDN_HD), F32)],
        compiler_params=_cparams(("parallel", "parallel", "arbitrary")),
    )(qkv, g_dir, b_dir)


def _rope_half(x, cos, sin_signed):
    return x * cos + pltpu.roll(x, x.shape[1] // 2, 1) * sin_signed


def _ret_core_kernel(q_ref, k_ref, v_ref, cos_ref, sin_ref, dl_ref, o_ref, s_ref, *, ctx_steps):
    d, s = pl.program_id(1), pl.program_id(2)

    @pl.when(s == 0)
    def _():
        s_ref[...] = jnp.zeros_like(s_ref)

    c = q_ref.shape[0]
    heads = range(RET_HEADS)
    is_lat = s >= ctx_steps
    cos = jnp.where(is_lat, cos_ref[...], 1.0)
    sin = jnp.where(is_lat, sin_ref[...], 0.0)
    t, incl, _ = _dir_masks(d, c)
    tf = t.astype(F32)
    pos = lax.broadcasted_iota(jnp.int32, (c, 1), 0)
    ridx = jnp.where(d == 0, pos + 1, c - pos).astype(F32)
    log_gamma = -_softplus(-dl_ref[0])

    def rope(x):
        hw = RET_DK // 2
        return jnp.concatenate([_rope_half(x[:, :hw], cos[:, :hw], sin[:, :hw]),
                                _rope_half(x[:, hw:], cos[:, hw:], sin[:, hw:])], axis=1)

    lg = [log_gamma[:, h:h + 1] for h in heads]
    q = [rope(q_ref[:, h * RET_DK:(h + 1) * RET_DK]) * (RET_DK ** -0.5) for h in heads]
    k = [rope(k_ref[:, h * RET_DK:(h + 1) * RET_DK]) for h in heads]
    vb = [_bf(v_ref[:, h * RET_DV:(h + 1) * RET_DV]) for h in heads]
    gcum = [lg[h] * ridx for h in heads]
    dec = [jnp.where(incl, jnp.exp(jnp.where(incl, lg[h] * tf, 0.0)), 0.0) for h in heads]
    qk = [_bf(_dot_nt(_bf(q[h]), _bf(k[h])) * dec[h]) for h in heads]
    st = [s_ref[h] for h in heads]
    for h in heads:
        o_ref[0, :, h * RET_DV:(h + 1) * RET_DV] = (
            _dot(_bf(q[h] * jnp.exp(gcum[h])), _bf(st[h])) + _dot(qk[h], vb[h]))
    gtot = [lg[h] * float(c) for h in heads]
    kd = [_bf(k[h] * jnp.exp(gtot[h] - gcum[h])) for h in heads]
    for h in heads:
        s_ref[h] = st[h] * jnp.exp(gtot[h]) + _dot_tn(kd[h], vb[h])


def _ret_core(rw, proj, cos_tab, sin_tab, decay_logit):
    c = RET_CHUNK
    rb = lambda b, d, s: rw.chunk_block(b, d, s, c)
    dl = jnp.zeros((2, 1, 128), F32).at[:, 0, :RET_HEADS].set(decay_logit)
    return pl.pallas_call(
        functools.partial(_ret_core_kernel, ctx_steps=rw.n_ctx // c),
        grid=(rw.batch, 2, (rw.n_lat + rw.n_ctx) // c),
        in_specs=[pl.BlockSpec((c, RET_KEY), lambda b, d, s: (rb(b, d, s), 0)),
                  pl.BlockSpec((c, RET_KEY), lambda b, d, s: (rb(b, d, s), 1)),
                  pl.BlockSpec((c, RET_VAL), lambda b, d, s: (rb(b, d, s), 1)),
                  pl.BlockSpec((c, RET_DK), lambda b, d, s: (rw.lat_chunk(d, s, c), 0)),
                  pl.BlockSpec((c, RET_DK), lambda b, d, s: (rw.lat_chunk(d, s, c), 0)),
                  pl.BlockSpec((1, 1, 128), lambda b, d, s: (d, 0, 0))],
        out_specs=pl.BlockSpec((1, c, RET_VAL), lambda b, d, s: (d, rb(b, d, s), 0)),
        out_shape=jax.ShapeDtypeStruct((2, rw.rows, RET_VAL), F32),
        scratch_shapes=[pltpu.VMEM((RET_HEADS, RET_DK, RET_DV), F32)],
        compiler_params=_cparams(("parallel", "parallel", "arbitrary")),
    )(proj, proj, proj, cos_tab, sin_tab, dl)


def _lru_scan_kernel(xs_ref, wg_ref, bg_ref, lam_ref, h_ref, carry_ref, *, rev):
    @pl.when(pl.program_id(1) == 0)
    def _():
        carry_ref[...] = jnp.zeros_like(carry_ref)

    t = xs_ref.shape[0]
    row8 = lax.broadcasted_iota(jnp.int32, (t // 8, 8, LRU_BW), 1)
    sp = _softplus(-lam_ref[...])
    for n in range(LRU_BLOCKS):
        sl = slice(n * LRU_BW, (n + 1) * LRU_BW)
        xn = xs_ref[:, sl]
        gates = jax.nn.sigmoid(_dot(_bf(xn), wg_ref[n]) + bg_ref[n])
        log_a = -LRU_C * gates[:, :LRU_BW] * sp[:, sl]
        a = jnp.exp(log_a)
        b = jnp.sqrt(-jnp.tanh(log_a) * (a * a + 1.0)) * gates[:, LRU_BW:] * xn
        groups = t // 8
        a3 = a.reshape(groups, 8, LRU_BW)
        b3 = b.reshape(groups, 8, LRU_BW)
        for k in (1, 2, 4):
            if rev:
                keep = row8 < 8 - k
                shift = 8 - k
            else:
                keep = row8 >= k
                shift = k
            sa = jnp.where(keep, pltpu.roll(a3, shift, 1), 1.0)
            sb = jnp.where(keep, pltpu.roll(b3, shift, 1), 0.0)
            b3 = b3 + a3 * sb
            a3 = a3 * sa
        carry = carry_ref[:, sl]
        for gi in (reversed(range(groups)) if rev else range(groups)):
            hg = a3[gi] * carry + b3[gi]
            carry = hg[0:1] if rev else hg[7:8]
            h_ref[gi * 8:(gi + 1) * 8, sl] = hg
        carry_ref[:, sl] = carry


def _lru_scan(rw, xs, w_gate, b_gate, lam, rev):
    t = TM_CONV
    d = 1 if rev else 0
    rb = lambda b, s: rw.chunk_block(b, d, s, t)
    return pl.pallas_call(
        functools.partial(_lru_scan_kernel, rev=rev),
        grid=(rw.batch, (rw.n_lat + rw.n_ctx) // t),
        in_specs=[pl.BlockSpec((t, LRU_W), lambda b, s: (rb(b, s), 0)),
                  pl.BlockSpec((LRU_BLOCKS, LRU_BW, 2 * LRU_BW), lambda b, s: (0, 0, 0)),
                  pl.BlockSpec((LRU_BLOCKS, 1, 2 * LRU_BW), lambda b, s: (0, 0, 0)),
                  pl.BlockSpec((1, LRU_W), lambda b, s: (0, 0))],
        out_specs=pl.BlockSpec((t, LRU_W), lambda b, s: (rb(b, s), 0)),
        out_shape=jax.ShapeDtypeStruct((rw.rows, LRU_W), F32),
        scratch_shapes=[pltpu.VMEM((1, LRU_W), F32)],
        compiler_params=_cparams(("parallel", "arbitrary")),
    )(xs, _bf(w_gate), b_gate.reshape(LRU_BLOCKS, 1, 2 * LRU_BW), lam.reshape(1, LRU_W))


def _rope16(x, tab):
    cos, s1, s2 = tab[:, :128], tab[:, 128:256], tab[:, 256:]
    cols = []
    for cidx in range(x.shape[1] // 128):
        xc = x[:, cidx * 128:(cidx + 1) * 128]
        cols.append(xc * cos + pltpu.roll(xc, 112, 1) * s1 + pltpu.roll(xc, 16, 1) * s2)
    return cols[0] if len(cols) == 1 else jnp.concatenate(cols, axis=1)


def _swa_kernel(sink_ref, q_ref, kp_ref, kc_ref, kn_ref, vp_ref, vc_ref, vn_ref, kx_ref, vx_ref,
                tp_ref, tc_ref, tn_ref, o_ref):
    qb = pl.program_id(1)
    nb = pl.num_programs(1)
    blk = SWA_BLOCK
    n_loc = 3 * blk
    n_keys = n_loc + kx_ref.shape[0]
    tab_c = tc_ref[...]
    q = _rope16(q_ref[...], tab_c) * (SWA_HD ** -0.5)
    k_all = jnp.concatenate([_rope16(kp_ref[...], tp_ref[...]), _rope16(kc_ref[...], tab_c),
                             _rope16(kn_ref[...], tn_ref[...]), kx_ref[...]], axis=0)
    v_all = jnp.concatenate([vp_ref[...], vc_ref[...], vn_ref[...], vx_ref[...]], axis=0)
    qi = lax.broadcasted_iota(jnp.int32, (blk, n_keys), 0)
    kj = lax.broadcasted_iota(jnp.int32, (blk, n_keys), 1)
    rel = kj - blk - qi
    in_window = jnp.logical_and(rel <= SWA_WINDOW, rel >= -SWA_WINDOW)
    has_prev = jnp.logical_or(kj >= blk, qb > 0)
    has_next = jnp.logical_or(kj < 2 * blk, qb < nb - 1)
    valid = jnp.logical_or(kj >= n_loc, jnp.logical_and(in_window, jnp.logical_and(has_prev, has_next)))
    low = lax.broadcasted_iota(jnp.int32, (n_keys, 128), 1) < SWA_HD

    def probs(scores, sink):
        sc = jnp.where(valid, scores, -jnp.inf)
        mx = jnp.maximum(jnp.max(sc, axis=-1, keepdims=True), sink)
        e = jnp.exp(sc - mx)
        den = jnp.sum(e, axis=-1, keepdims=True) + jnp.exp(sink - mx)
        return _bf(e * (1.0 / den))

    group = SWA_QH // SWA_KVH
    for g in range(SWA_KVH):
        col = slice((g // 2) * 128, (g // 2 + 1) * 128)
        native_low = g % 2 == 0
        sel = low if native_low else jnp.logical_not(low)
        k_nat = jnp.where(sel, k_all[:, col], 0.0)
        v_nat = jnp.where(sel, v_all[:, col], 0.0)
        k_oth = pltpu.roll(k_nat, SWA_HD, 1)
        v_oth = pltpu.roll(v_nat, SWA_HD, 1)
        k_lo, k_hi = (k_nat, k_oth) if native_low else (k_oth, k_nat)
        v_lo, v_hi = (v_nat, v_oth) if native_low else (v_oth, v_nat)
        k_lo, k_hi, v_lo, v_hi = _bf(k_lo), _bf(k_hi), _bf(v_lo), _bf(v_hi)
        for cidx in range(group // 2):
            qc = (g * group) // 2 + cidx
            qv = _bf(q[:, qc * 128:(qc + 1) * 128])
            p_lo = probs(_dot_nt(qv, k_lo), sink_ref[2 * qc])
            p_hi = probs(_dot_nt(qv, k_hi), sink_ref[2 * qc + 1])
            o_ref[:, qc * 128:(qc + 1) * 128] = _dot(p_lo, v_lo) + _dot(p_hi, v_hi)


def _swa_core(rw, proj, sink, tab):
    blk = SWA_BLOCK
    nb = rw.n_lat // blk
    kcol = SWA_QD // SWA_KVD
    vcol = kcol + 1
    ctx_blk = lambda b: rw.lat_rows // rw.n_ctx + b
    prev = lambda q: jnp.maximum(q - 1, 0)
    nxt = lambda q: jnp.minimum(q + 1, nb - 1)
    kv = lambda col, f: pl.BlockSpec((blk, SWA_KVD), lambda b, q: (b * nb + f(q), col))
    same = lambda q: q
    return pl.pallas_call(
        _swa_kernel,
        grid=(rw.batch, nb),
        in_specs=[pl.BlockSpec(memory_space=pltpu.SMEM),
                  pl.BlockSpec((blk, SWA_QD), lambda b, q: (b * nb + q, 0)),
                  kv(kcol, prev), kv(kcol, same), kv(kcol, nxt),
                  kv(vcol, prev), kv(vcol, same), kv(vcol, nxt),
                  pl.BlockSpec((rw.n_ctx, SWA_KVD), lambda b, q: (ctx_blk(b), kcol)),
                  pl.BlockSpec((rw.n_ctx, SWA_KVD), lambda b, q: (ctx_blk(b), vcol)),
                  pl.BlockSpec((blk, 384), lambda b, q: (prev(q), 0)),
                  pl.BlockSpec((blk, 384), lambda b, q: (q, 0)),
                  pl.BlockSpec((blk, 384), lambda b, q: (nxt(q), 0))],
        out_specs=pl.BlockSpec((blk, SWA_QD), lambda b, q: (b * nb + q, 0)),
        out_shape=jax.ShapeDtypeStruct((rw.lat_rows, SWA_QD), F32),
        compiler_params=_cparams(("parallel", "parallel")),
    )(sink, proj, proj, proj, proj, proj, proj, proj, proj, proj, tab, tab, tab)


def _axial_angles(n_lat, dh):
    t = jnp.arange(n_lat)
    row = (t // GRID_W).astype(F32)
    col = (t % GRID_W).astype(F32)
    nf = dh // 4
    inv = ROPE_THETA ** (-jnp.arange(nf, dtype=F32) / nf)
    return row[:, None] * inv, col[:, None] * inv


def _ret_tables(n_lat):
    ar, ac = _axial_angles(n_lat, RET_DK)
    cos = jnp.concatenate([jnp.cos(ar), jnp.cos(ar), jnp.cos(ac), jnp.cos(ac)], axis=1)
    sin = jnp.concatenate([-jnp.sin(ar), jnp.sin(ar), -jnp.sin(ac), jnp.sin(ac)], axis=1)
    return cos, sin


def _swa_table(n_lat):
    ar, ac = _axial_angles(n_lat, SWA_HD)
    z = jnp.zeros_like(ar)
    cos = jnp.concatenate([jnp.cos(ar), jnp.cos(ar), jnp.cos(ac), jnp.cos(ac)], axis=1)
    s1 = jnp.concatenate([-jnp.sin(ar), z, -jnp.sin(ac), z], axis=1)
    s2 = jnp.concatenate([z, jnp.sin(ar), z, jnp.sin(ac)], axis=1)
    return jnp.concatenate([jnp.tile(cos, (1, 2)), jnp.tile(s1, (1, 2)), jnp.tile(s2, (1, 2))], axis=1)


def _gdn_layer(rw, layer, x, mod3, gain, w_in, conv_w, a_log, dt_bias, norm_g, w_out):
    w_main = _bf(w_in[:, :GDN_MAIN])
    n_ab = w_in.shape[1] - GDN_MAIN
    w_ab = _bf(jnp.pad(w_in[:, GDN_MAIN:], ((0, 0), (0, 128 - n_ab))))
    proj, ab = _inproj(rw, x, gain, mod3, layer, w_main, w_ab)
    hv = GDN_V_HEADS
    alog_row = jnp.zeros((1, 128), F32).at[0, 0:hv].set(a_log[0]).at[0, 2 * hv:3 * hv].set(a_log[1])
    dtb_row = jnp.zeros((1, 128), F32).at[0, 0:hv].set(dt_bias[0]).at[0, 2 * hv:3 * hv].set(dt_bias[1])
    cb = 2048
    row_spec = pl.BlockSpec((TM_CONV, 128), lambda i, j: (i, 0))
    vec_spec = pl.BlockSpec((1, 128), lambda i, j: (0, 0))
    qkv, g_all, b_all = pl.pallas_call(
        functools.partial(_gdn_conv_kernel, rw=rw),
        grid=(rw.rows // TM_CONV, GDN_CONV // cb),
        in_specs=_conv_specs(rw, cb) + [row_spec, vec_spec, vec_spec],
        out_specs=[pl.BlockSpec((TM_CONV, cb), lambda i, j: (i, j)), row_spec, row_spec],
        out_shape=[jax.ShapeDtypeStruct((rw.rows, GDN_CONV), F32),
                   jax.ShapeDtypeStruct((rw.rows, 128), F32),
                   jax.ShapeDtypeStruct((rw.rows, 128), F32)],
        compiler_params=_cparams(("parallel", "arbitrary")),
    )(proj, proj, proj, conv_w, ab, alog_row, dtb_row)
    g_dir = jnp.stack([g_all[:, 0:hv], g_all[:, 2 * hv:3 * hv]])
    b_dir = jnp.stack([b_all[:, hv:2 * hv], b_all[:, 3 * hv:4 * hv]])
    o = _gdn_core(rw, qkv, g_dir, b_dir)
    return _two_dir_out(_gdn_out_kernel, rw, layer, x, mod3, o, proj, GDN_CONV // GDN_VAL, norm_g, _bf(w_out))


def _ret_layer(rw, layer, x, mod3, gain, w_in, decay_logit, norm_g, w_out):
    proj = _inproj(rw, x, gain, mod3, layer, _bf(w_in))
    cos_tab, sin_tab = _ret_tables(rw.n_lat)
    o = _ret_core(rw, proj, cos_tab, sin_tab, decay_logit)
    return _two_dir_out(_ret_out_kernel, rw, layer, x, mod3, o, proj, (2 * RET_KEY + RET_VAL) // RET_VAL,
                        norm_g, _bf(w_out))


def _lru_layer(rw, layer, x, mod3, gain, w_in, conv_w, conv_b, w_gate, b_gate, lam, w_out):
    proj = _inproj(rw, x, gain, mod3, layer, _bf(w_in))
    xs = pl.pallas_call(
        functools.partial(_lru_conv_kernel, rw=rw),
        grid=(rw.rows // TM_CONV, 1),
        in_specs=_conv_specs(rw, LRU_W) + [pl.BlockSpec((1, LRU_W), lambda i, j: (0, 0))],
        out_specs=pl.BlockSpec((TM_CONV, LRU_W), lambda i, j: (i, 0)),
        out_shape=jax.ShapeDtypeStruct((rw.rows, LRU_W), F32),
        compiler_params=_cparams(("parallel", "arbitrary")),
    )(proj, proj, proj, conv_w, conv_b.reshape(1, LRU_W))
    h0 = _lru_scan(rw, xs, w_gate[0], b_gate[0], lam[0], rev=False)
    h1 = _lru_scan(rw, xs, w_gate[1], b_gate[1], lam[1], rev=True)
    tm = TM_OUT
    specs = [pl.BlockSpec((tm, LRU_W), lambda i: (i, 0)),
             pl.BlockSpec((tm, LRU_W), lambda i: (i, 0)),
             pl.BlockSpec((tm, LRU_W), lambda i: (i, 1))]
    return _outproj_call(_lru_out_kernel, rw, rw.rows, layer, x, mod3, specs, [h0, h1, proj], _bf(w_out), [])


def _swa_layer(rw, layer, x, mod3, gain, w_in, sink, w_out):
    proj = _inproj(rw, x, gain, mod3, layer, _bf(w_in))
    o = _swa_core(rw, proj, sink, _swa_table(rw.n_lat))
    specs = [pl.BlockSpec((TM_OUT, SWA_QD), lambda i: (i, 0))]
    return _outproj_call(_swa_out_kernel, rw, rw.lat_rows, layer, x, mod3, specs, [o], _bf(w_out), [])


def kernel(x, c, ctx, c_ctx, norm_mix_g, norm_ffn_g, w_mod, b_mod, w_ff1, w_ff2, norm_out_g, gdn_w_in, gdn_conv_w, gdn_a_log, gdn_dt_bias, gdn_norm_g, gdn_w_out, ret_w_in, ret_decay_logit, ret_norm_g, ret_w_out, lru_w_in, lru_conv_w, lru_conv_b, lru_w_gate, lru_b_gate, lru_lambda, lru_w_out, swa_w_in, swa_sink, swa_w_out):
    batch, n_lat, _ = x.shape
    n_ctx = ctx.shape[1]
    assert batch + 1 <= MOD_ROWS and n_lat % TM_IN == 0 and (batch * n_ctx) % TM_IN == 0
    assert w_mod.shape[0] == DEPTH and gdn_w_in.shape[0] == 1 and swa_w_in.shape[0] == 1
    rw = _Rows(batch, n_lat, n_ctx)
    xf = jnp.concatenate([x.reshape(rw.lat_rows, D), ctx.reshape(batch * n_ctx, D)], axis=0)
    cc = jnp.zeros((MOD_ROWS, D), F32).at[:batch].set(c).at[batch].set(c_ctx)
    mod3 = _adaln(cc, w_mod, b_mod)

    def mlp(layer, xin, n_rows, final):
        return _mlp(rw, xin, n_rows, norm_ffn_g[layer], mod3, layer, _bf(w_ff1[layer]), _bf(w_ff2[layer]),
                    norm_out_g, final)

    xf = _gdn_layer(rw, 0, xf, mod3, norm_mix_g[0], gdn_w_in[0], gdn_conv_w[0], gdn_a_log[0], gdn_dt_bias[0],
                    gdn_norm_g[0], gdn_w_out[0])
    xf = mlp(0, xf, rw.rows, False)
    xf = _ret_layer(rw, 1, xf, mod3, norm_mix_g[1], ret_w_in[0], ret_decay_logit[0], ret_norm_g[0], ret_w_out[0])
    xf = mlp(1, xf, rw.rows, False)
    xf = _lru_layer(rw, 2, xf, mod3, norm_mix_g[2], lru_w_in[0], lru_conv_w[0], lru_conv_b[0], lru_w_gate[0],
                    lru_b_gate[0], lru_lambda[0], lru_w_out[0])
    xf = mlp(2, xf, rw.rows, False)
    xl = _swa_layer(rw, 3, xf, mod3, norm_mix_g[3], swa_w_in[0], swa_sink[0], swa_w_out[0])
    out = mlp(3, xl, rw.lat_rows, True)
    return out.reshape(batch, n_lat, D)
```

```python
import functools
import math

import jax
import jax.numpy as jnp
from jax import lax
from jax.experimental import pallas as pl
from jax.experimental.pallas import tpu as pltpu

F32 = jnp.float32
BF16 = jnp.bfloat16

D = 1024
D_FF = 4 * D
EPS = 1e-6
DEPTH = 4
ROPE_THETA = 10000.0
GRID_W = 64
CHUNK = 64
MOD_ROWS = 16

GDN_QK_HEADS = 8
GDN_V_HEADS = 16
GDN_HD = 128
GDN_KEY = GDN_QK_HEADS * GDN_HD
GDN_VAL = GDN_V_HEADS * GDN_HD
GDN_CONV = 2 * GDN_KEY + GDN_VAL
GDN_MAIN = GDN_CONV + GDN_VAL

RET_HEADS = 4
RET_DK = 256
RET_DV = 512
RET_KEY = RET_HEADS * RET_DK
RET_VAL = RET_HEADS * RET_DV

LRU_W = 1280
LRU_BLOCKS = 10
LRU_BW = LRU_W // LRU_BLOCKS
LRU_C = 8.0

SWA_QH = 16
SWA_KVH = 4
SWA_HD = 64
SWA_BLOCK = 128
SWA_WINDOW = 128
SWA_QD = SWA_QH * SWA_HD
SWA_KVD = SWA_KVH * SWA_HD

TM_IN = 1024
TM_OUT = 512
TM_CONV = 256
TN_IN_MAX = 1536
FF_CHUNK = 1024
RET_CHUNK = 256
VMEM_LIMIT = 48 * 1024 * 1024


def _dot(a, b):
    return jnp.dot(a, b, preferred_element_type=F32)


def _dot_nt(a, b):
    return lax.dot_general(a, b, (((1,), (1,)), ((), ())), preferred_element_type=F32)


def _dot_tn(a, b):
    return lax.dot_general(a, b, (((0,), (0,)), ((), ())), preferred_element_type=F32)


def _bf(x):
    return x.astype(BF16)


def _split3(x):
    hi = _bf(x)
    r = x - hi.astype(F32)
    mid = _bf(r)
    lo = _bf(r - mid.astype(F32))
    return hi, mid, lo


def _silu(x):
    return x * jax.nn.sigmoid(x)


def _softplus(x):
    return jnp.maximum(x, 0.0) + jnp.log1p(jnp.exp(-jnp.abs(x)))


def _gelu_tanh(x):
    cdf = 0.5 * (1.0 + jnp.tanh(math.sqrt(2.0 / math.pi) * (x + 0.044715 * (x * x * x))))
    return x * cdf


def _cparams(sem):
    return pltpu.CompilerParams(dimension_semantics=sem, vmem_limit_bytes=VMEM_LIMIT)


class _Rows:
    def __init__(self, batch, n_lat, n_ctx):
        self.batch, self.n_lat, self.n_ctx = batch, n_lat, n_ctx
        self.lat_rows = batch * n_lat
        self.rows = self.lat_rows + batch * n_ctx
        self.n_steps = (n_lat + n_ctx) // CHUNK
        self.ctx_steps = n_ctx // CHUNK
        self.lat_steps = n_lat // CHUNK

    def mod_row(self, i, tm):
        return jnp.where(i < self.lat_rows // tm, i // (self.n_lat // tm), self.batch)

    def chunk_block(self, b, d, s, blk):
        nc, nl = self.n_ctx // blk, self.n_lat // blk
        c_ctx = jnp.where(d == 0, s, nc - 1 - s)
        c_lat = jnp.where(d == 0, s - nc, nl - 1 - (s - nc))
        return jnp.where(s < nc, self.lat_rows // blk + b * nc + c_ctx, b * nl + c_lat)

    def lat_chunk(self, d, s, blk):
        nc, nl = self.n_ctx // blk, self.n_lat // blk
        return jnp.where(s < nc, 0, jnp.where(d == 0, s - nc, nl - 1 - (s - nc)))


def _adaln_kernel(c_ref, w_ref, b_ref, o_ref):
    s = _silu(c_ref[...])
    w = w_ref[0]
    s_hi = _bf(s)
    s_lo = _bf(s - s_hi.astype(F32))
    w_hi = _bf(w)
    w_lo = _bf(w - w_hi.astype(F32))
    y = _dot(s_hi, w_hi) + (_dot(s_lo, w_hi) + _dot(s_hi, w_lo))
    o_ref[0] = y + b_ref[0]


def _adaln(cc, w_mod, b_mod):
    depth, _, n = w_mod.shape
    tn = 1024
    out = pl.pallas_call(
        _adaln_kernel,
        grid=(depth, n // tn),
        in_specs=[pl.BlockSpec((MOD_ROWS, D), lambda l, j: (0, 0)),
                  pl.BlockSpec((1, D, tn), lambda l, j: (l, 0, j)),
                  pl.BlockSpec((1, 1, tn), lambda l, j: (l, 0, j))],
        out_specs=pl.BlockSpec((1, MOD_ROWS, tn), lambda l, j: (l, 0, j)),
        out_shape=jax.ShapeDtypeStruct((depth, MOD_ROWS, n), F32),
        compiler_params=_cparams(("parallel", "parallel")),
    )(cc, w_mod, b_mod.reshape(depth, 1, n))
    return out.reshape(depth * MOD_ROWS * 6, 1, D)


def _mod_spec(rw, layer, which, tm):
    base = layer * MOD_ROWS * 6
    return pl.BlockSpec((1, 1, D), lambda i, *_: (base + rw.mod_row(i, tm) * 6 + which, 0, 0))


def _norm_mod(x, gain, shift, scale):
    y = x * lax.rsqrt(jnp.mean(x * x, axis=-1, keepdims=True) + EPS) * gain
    return y * (1.0 + scale) + shift


def _inproj_kernel(x_ref, g_ref, sh_ref, sc_ref, w_ref, *rest, has_extra):
    if has_extra:
        w2_ref, o_ref, o2_ref, h_ref = rest
    else:
        o_ref, h_ref = rest

    @pl.when(pl.program_id(1) == 0)
    def _():
        h = _bf(_norm_mod(x_ref[...], g_ref[...], sh_ref[0], sc_ref[0]))
        h_ref[...] = h
        if has_extra:
            o2_ref[...] = _dot(h, w2_ref[...])

    o_ref[...] = _dot(h_ref[...], w_ref[...])


def _inproj(rw, x, gain, mod3, layer, w, w_extra=None):
    n = w.shape[1]
    tm = TM_IN
    tn = max(t for t in range(256, TN_IN_MAX + 1, 256) if n % t == 0)
    has_extra = w_extra is not None
    in_specs = [pl.BlockSpec((tm, D), lambda i, j: (i, 0)),
                pl.BlockSpec((1, D), lambda i, j: (0, 0)),
                _mod_spec(rw, layer, 0, tm),
                _mod_spec(rw, layer, 1, tm),
                pl.BlockSpec((D, tn), lambda i, j: (0, j))]
    out_specs = [pl.BlockSpec((tm, tn), lambda i, j: (i, j))]
    out_shape = [jax.ShapeDtypeStruct((rw.rows, n), F32)]
    args = [x, gain.reshape(1, D), mod3, mod3, w]
    if has_extra:
        ne = w_extra.shape[1]
        in_specs.append(pl.BlockSpec((D, ne), lambda i, j: (0, 0)))
        out_specs.append(pl.BlockSpec((tm, ne), lambda i, j: (i, 0)))
        out_shape.append(jax.ShapeDtypeStruct((rw.rows, ne), F32))
        args.append(w_extra)
    res = pl.pallas_call(
        functools.partial(_inproj_kernel, has_extra=has_extra),
        grid=(rw.rows // tm, n // tn),
        in_specs=in_specs, out_specs=out_specs, out_shape=out_shape,
        scratch_shapes=[pltpu.VMEM((tm, D), BF16)],
        compiler_params=_cparams(("parallel", "arbitrary")),
    )(*args)
    return res if has_extra else res[0]


def _mlp_kernel(x_ref, g_ref, sh_ref, sc_ref, gt_ref, w1_ref, w2_ref, go_ref, o_ref, h_ref, acc_ref,
                *, final_norm):
    k = pl.program_id(1)

    @pl.when(k == 0)
    def _():
        h_ref[...] = _bf(_norm_mod(x_ref[...], g_ref[...], sh_ref[0], sc_ref[0]))
        acc_ref[...] = jnp.zeros_like(acc_ref)

    a = jnp.square(jnp.maximum(_dot(h_ref[...], w1_ref[...]), 0.0))
    acc_ref[...] += _dot(_bf(a), w2_ref[...])

    @pl.when(k == pl.num_programs(1) - 1)
    def _():
        y = x_ref[...] + gt_ref[0] * acc_ref[...]
        if final_norm:
            y = y * lax.rsqrt(jnp.mean(y * y, axis=-1, keepdims=True) + EPS) * go_ref[...]
        o_ref[...] = y


def _mlp(rw, x, n_rows, gain, mod3, layer, w1, w2, out_gain, final_norm):
    tm, ck = TM_IN, FF_CHUNK
    return pl.pallas_call(
        functools.partial(_mlp_kernel, final_norm=final_norm),
        grid=(n_rows // tm, D_FF // ck),
        in_specs=[pl.BlockSpec((tm, D), lambda i, k: (i, 0)),
                  pl.BlockSpec((1, D), lambda i, k: (0, 0)),
                  _mod_spec(rw, layer, 3, tm),
                  _mod_spec(rw, layer, 4, tm),
                  _mod_spec(rw, layer, 5, tm),
                  pl.BlockSpec((D, ck), lambda i, k: (0, k)),
                  pl.BlockSpec((ck, D), lambda i, k: (k, 0)),
                  pl.BlockSpec((1, D), lambda i, k: (0, 0))],
        out_specs=pl.BlockSpec((tm, D), lambda i, k: (i, 0)),
        out_shape=jax.ShapeDtypeStruct((n_rows, D), F32),
        scratch_shapes=[pltpu.VMEM((tm, D), BF16), pltpu.VMEM((tm, D), F32)],
        compiler_params=_cparams(("parallel", "arbitrary")),
    )(x, gain.reshape(1, D), mod3, mod3, mod3, w1, w2, out_gain.reshape(1, D))


def _gdn_out_kernel(x_ref, gt_ref, o0_ref, o1_ref, z_ref, ng_ref, w_ref, out_ref, a_ref):
    ng = ng_ref[...]
    for h in range(GDN_V_HEADS):
        sl = slice(h * GDN_HD, (h + 1) * GDN_HD)
        o = o0_ref[0, :, sl] + o1_ref[0, :, sl]
        y = o * lax.rsqrt(jnp.mean(o * o, axis=-1, keepdims=True) + EPS) * ng
        a_ref[:, sl] = _bf(y * _silu(z_ref[:, sl]))
    out_ref[...] = x_ref[...] + gt_ref[0] * _dot(a_ref[...], w_ref[...])


def _ret_out_kernel(x_ref, gt_ref, o0_ref, o1_ref, z_ref, ng_ref, w_ref, out_ref, a_ref):
    for h in range(RET_HEADS):
        sl = slice(h * RET_DV, (h + 1) * RET_DV)
        o = o0_ref[0, :, sl] + o1_ref[0, :, sl]
        mu = jnp.mean(o, axis=-1, keepdims=True)
        oc = o - mu
        var = jnp.mean(oc * oc, axis=-1, keepdims=True)
        y = oc * lax.rsqrt(var + EPS) * ng_ref[:, sl]
        a_ref[:, sl] = _bf(y * _silu(z_ref[:, sl]))
    out_ref[...] = x_ref[...] + gt_ref[0] * _dot(a_ref[...], w_ref[...])


def _lru_out_kernel(x_ref, gt_ref, h0_ref, h1_ref, gb_ref, w_ref, out_ref):
    a = (h0_ref[...] + h1_ref[...]) * _gelu_tanh(gb_ref[...])
    out_ref[...] = x_ref[...] + gt_ref[0] * _dot(_bf(a), w_ref[...])


def _swa_out_kernel(x_ref, gt_ref, o_ref, w_ref, out_ref):
    out_ref[...] = x_ref[...] + gt_ref[0] * _dot(_bf(o_ref[...]), w_ref[...])


def _outproj_call(kern, rw, n_rows, layer, x, mod3, extra_specs, extra_args, w, scratch):
    tm = TM_OUT
    k = w.shape[0]
    return pl.pallas_call(
        kern,
        grid=(n_rows // tm,),
        in_specs=[pl.BlockSpec((tm, D), lambda i: (i, 0)), _mod_spec(rw, layer, 2, tm)] + extra_specs
                 + [pl.BlockSpec((k, D), lambda i: (0, 0))],
        out_specs=pl.BlockSpec((tm, D), lambda i: (i, 0)),
        out_shape=jax.ShapeDtypeStruct((n_rows, D), F32),
        scratch_shapes=scratch,
        compiler_params=_cparams(("parallel",)),
    )(x, mod3, *extra_args, w)


def _two_dir_out(kern, rw, layer, x, mod3, o, proj, z_block, norm_g, w):
    tm, k = TM_OUT, w.shape[0]
    ng = norm_g.reshape(1, -1)
    specs = [pl.BlockSpec((1, tm, k), lambda i: (0, i, 0)),
             pl.BlockSpec((1, tm, k), lambda i: (1, i, 0)),
             pl.BlockSpec((tm, k), lambda i: (i, z_block)),
             pl.BlockSpec(ng.shape, lambda i: (0, 0))]
    return _outproj_call(kern, rw, rw.rows, layer, x, mod3, specs, [o, o, proj, ng], w,
                         [pltpu.VMEM((tm, k), BF16)])


def _conv_core(x_ref, p_ref, n_ref, w_ref, first, last):
    x = x_ref[...]
    t = x.shape[0]
    prev = p_ref[...] * jnp.where(first, 0.0, 1.0)
    nxt = n_ref[...] * jnp.where(last, 0.0, 1.0)
    row8 = lax.broadcasted_iota(jnp.int32, prev.shape, 0)
    r2 = pltpu.roll(x, 2, 0)
    r1 = pltpu.roll(x, 1, 0)
    rp = pltpu.roll(x, t - 1, 0)
    f2 = jnp.where(row8 < 2, pltpu.roll(prev, 2, 0), r2[:8])
    f1 = jnp.where(row8 < 1, pltpu.roll(prev, 1, 0), r1[:8])
    l1 = jnp.where(row8 == 7, pltpu.roll(nxt, 7, 0), rp[t - 8:])
    xm2 = jnp.concatenate([f2, r2[8:]], axis=0)
    xm1 = jnp.concatenate([f1, r1[8:]], axis=0)
    xp1 = jnp.concatenate([rp[:t - 8], l1], axis=0)
    w = w_ref[...]
    return w[0:1] * xm2 + w[1:2] * xm1 + w[2:3] * x + w[3:4] * xp1


def _seg_flags(rw, i):
    per = rw.n_lat // TM_CONV
    is_ctx = i >= rw.lat_rows // TM_CONV
    assert rw.n_ctx == TM_CONV
    first = jnp.logical_or(is_ctx, i % per == 0)
    last = jnp.logical_or(is_ctx, i % per == per - 1)
    return first, last


def _gdn_conv_kernel(x_ref, p_ref, n_ref, w_ref, ab_ref, alog_ref, dtb_ref, o_ref, g_ref, b_ref, *, rw):
    i, j = pl.program_id(0), pl.program_id(1)
    first, last = _seg_flags(rw, i)
    s = _silu(_conv_core(x_ref, p_ref, n_ref, w_ref, first, last))
    width = s.shape[1]
    qk_blocks = 2 * GDN_KEY // width

    @pl.when(j < qk_blocks)
    def _():
        for h in range(width // GDN_HD):
            sl = slice(h * GDN_HD, (h + 1) * GDN_HD)
            sh = s[:, sl]
            is_q = j * width + h * GDN_HD < GDN_KEY
            scale = jnp.where(is_q, GDN_HD ** -0.5, 1.0)
            o_ref[:, sl] = sh * (lax.rsqrt(jnp.sum(sh * sh, axis=-1, keepdims=True) + EPS) * scale)

    @pl.when(j >= qk_blocks)
    def _():
        o_ref[...] = s

    @pl.when(j == 0)
    def _():
        ab = ab_ref[...]
        g_ref[...] = -jnp.exp(alog_ref[...]) * _softplus(ab + dtb_ref[...])
        b_ref[...] = jax.nn.sigmoid(ab)


def _conv_specs(rw, cb):
    t8 = TM_CONV // 8
    n8 = rw.rows // 8
    return [pl.BlockSpec((TM_CONV, cb), lambda i, j: (i, j)),
            pl.BlockSpec((8, cb), lambda i, j: (jnp.maximum(i * t8 - 1, 0), j)),
            pl.BlockSpec((8, cb), lambda i, j: (jnp.minimum((i + 1) * t8, n8 - 1), j)),
            pl.BlockSpec((4, cb), lambda i, j: (0, j))]


def _dir_masks(d, n):
    ii = lax.broadcasted_iota(jnp.int32, (n, n), 0)
    jj = lax.broadcasted_iota(jnp.int32, (n, n), 1)
    t = (ii - jj) * jnp.where(d == 0, 1, -1)
    return t, t >= 0, t > 0


def _gdn_core_kernel(qkv_ref, g_ref, b_ref, o_ref, s_ref):
    d, s = pl.program_id(1), pl.program_id(2)

    @pl.when(s == 0)
    def _():
        s_ref[...] = jnp.zeros_like(s_ref)

    c, hd = CHUNK, GDN_HD
    rep = GDN_V_HEADS // GDN_QK_HEADS
    heads = range(GDN_V_HEADS)
    t, incl, strict = _dir_masks(d, c)
    g = g_ref[0]
    beta = b_ref[0]
    cum = _bf(jnp.where(incl, 1.0, 0.0))
    cum_t = _bf(jnp.where(t <= 0, 1.0, 0.0))
    parts = _split3(g)
    gc = sum(_dot(cum, p) for p in parts)
    gr = sum(_dot_tn(p, cum_t) for p in parts)
    gtot = jnp.sum(g, axis=0, keepdims=True)

    q_l = [qkv_ref[:, hq * hd:(hq + 1) * hd] for hq in range(GDN_QK_HEADS)]
    k_l = [qkv_ref[:, GDN_KEY + hq * hd:GDN_KEY + (hq + 1) * hd] for hq in range(GDN_QK_HEADS)]
    v_l = [qkv_ref[:, 2 * GDN_KEY + h * hd:2 * GDN_KEY + (h + 1) * hd] for h in heads]
    gram = [_dot_nt(_bf(jnp.concatenate([k, q], axis=0)), _bf(k)) for q, k in zip(q_l, k_l)]
    gcol = [gc[:, h:h + 1] for h in heads]
    bcol = [beta[:, h:h + 1] for h in heads]
    dec = [jnp.where(incl, jnp.exp(jnp.where(incl, gcol[h] - gr[h:h + 1, :], 0.0)), 0.0) for h in heads]
    eg = [jnp.exp(gcol[h]) for h in heads]
    m = [jnp.where(strict, -(gram[h // rep][:c] * dec[h] * bcol[h]), 0.0) for h in heads]
    rhs = [jnp.concatenate([k_l[h // rep] * (bcol[h] * eg[h]), v_l[h] * bcol[h]], axis=1) for h in heads]

    m_hi = [_bf(x) for x in m]
    n = m
    p = [_dot(x, x) for x in m_hi]
    for _ in range(4):
        pb = [_bf(x) for x in p]
        r = [_dot(_bf(jnp.concatenate([p[h], n[h]], axis=0)), pb[h]) for h in heads]
        n = [n[h] + p[h] + r[h][c:] for h in heads]
        p = [r[h][:c] for h in heads]
    r = [_dot(_bf(n[h]), _bf(p[h])) for h in heads]
    n = [n[h] + p[h] + r[h] for h in heads]
    nb = [_bf(x) for x in n]

    x = [rhs[h] + _dot(nb[h], _bf(rhs[h])) for h in heads]
    resid = [(rhs[h] - x[h]) + _dot(m_hi[h], _bf(x[h])) for h in heads]
    x = [x[h] + (resid[h] + _dot(nb[h], _bf(resid[h]))) for h in heads]

    st = [s_ref[h] for h in heads]
    wq = [_bf(jnp.concatenate([x[h][:, :hd], q_l[h // rep] * eg[h]], axis=0)) for h in heads]
    r = [_dot(wq[h], _bf(st[h])) for h in heads]
    unb = [_bf(x[h][:, hd:] - r[h][:c]) for h in heads]
    qk = [_bf(gram[h // rep][c:] * dec[h]) for h in heads]
    for h in heads:
        o_ref[0, :, h * hd:(h + 1) * hd] = r[h][c:] + _dot(qk[h], unb[h])
    gt = [gtot[:, h:h + 1] for h in heads]
    kd = [_bf(k_l[h // rep] * jnp.exp(gt[h] - gcol[h])) for h in heads]
    for h in heads:
        s_ref[h] = st[h] * jnp.exp(gt[h]) + _dot_tn(kd[h], unb[h])


def _gdn_core(rw, qkv, g_dir, b_dir):
    c = CHUNK
    rb = lambda b, d, s: rw.chunk_block(b, d, s, c)
    return pl.pallas_call(
        _gdn_core_kernel,
        grid=(rw.batch, 2, rw.n_steps),
        in_specs=[pl.BlockSpec((c, GDN_CONV), lambda b, d, s: (rb(b, d, s), 0)),
                  pl.BlockSpec((1, c, GDN_V_HEADS), lambda b, d, s: (d, rb(b, d, s), 0)),
                  pl.BlockSpec((1, c, GDN_V_HEADS), lambda b, d, s: (d, rb(b, d, s), 0))],
        out_specs=pl.BlockSpec((1, c, GDN_VAL), lambda b, d, s: (d, rb(b, d, s), 0)),
        out_shape=jax.ShapeDtypeStruct((2, rw.rows, GDN_VAL), F32),
        scratch_shapes=[pltpu.VMEM((GDN_V_HEADS, GDN_HD, GDN_HD), F32)],
        compiler_params=_cparams(("parallel", "parallel", "arbitrary")),
    )(qkv, g_dir, b_dir)


def _rope_half(x, cos, sin_signed):
    return x * cos + pltpu.roll(x, x.shape[1] // 2, 1) * sin_signed


def _ret_core_kernel(q_ref, k_ref, v_ref, cos_ref, sin_ref, dl_ref, o_ref, s_ref, *, ctx_steps):
    d, s = pl.program_id(1), pl.program_id(2)

    @pl.when(s == 0)
    def _():
        s_ref[...] = jnp.zeros_like(s_ref)

    c = q_ref.shape[0]
    heads = range(RET_HEADS)
    is_lat = s >= ctx_steps
    cos = jnp.where(is_lat, cos_ref[...], 1.0)
    sin = jnp.where(is_lat, sin_ref[...], 0.0)
    t, incl, _ = _dir_masks(d, c)
    tf = t.astype(F32)
    pos = lax.broadcasted_iota(jnp.int32, (c, 1), 0)
    ridx = jnp.where(d == 0, pos + 1, c - pos).astype(F32)
    log_gamma = -_softplus(-dl_ref[0])

    def rope(x):
        hw = RET_DK // 2
        return jnp.concatenate([_rope_half(x[:, :hw], cos[:, :hw], sin[:, :hw]),
                                _rope_half(x[:, hw:], cos[:, hw:], sin[:, hw:])], axis=1)

    lg = [log_gamma[:, h:h + 1] for h in heads]
    q = [rope(q_ref[:, h * RET_DK:(h + 1) * RET_DK]) * (RET_DK ** -0.5) for h in heads]
    k = [rope(k_ref[:, h * RET_DK:(h + 1) * RET_DK]) for h in heads]
    vb = [_bf(v_ref[:, h * RET_DV:(h + 1) * RET_DV]) for h in heads]
    gcum = [lg[h] * ridx for h in heads]
    dec = [jnp.where(incl, jnp.exp(jnp.where(incl, lg[h] * tf, 0.0)), 0.0) for h in heads]
    qk = [_bf(_dot_nt(_bf(q[h]), _bf(k[h])) * dec[h]) for h in heads]
    st = [s_ref[h] for h in heads]
    for h in heads:
        o_ref[0, :, h * RET_DV:(h + 1) * RET_DV] = (
            _dot(_bf(q[h] * jnp.exp(gcum[h])), _bf(st[h])) + _dot(qk[h], vb[h]))
    gtot = [lg[h] * float(c) for h in heads]
    kd = [_bf(k[h] * jnp.exp(gtot[h] - gcum[h])) for h in heads]
    for h in heads:
        s_ref[h] = st[h] * jnp.exp(gtot[h]) + _dot_tn(kd[h], vb[h])


def _ret_core(rw, proj, cos_tab, sin_tab, decay_logit):
    c = RET_CHUNK
    rb = lambda b, d, s: rw.chunk_block(b, d, s, c)
    dl = jnp.zeros((2, 1, 128), F32).at[:, 0, :RET_HEADS].set(decay_logit)
    return pl.pallas_call(
        functools.partial(_ret_core_kernel, ctx_steps=rw.n_ctx // c),
        grid=(rw.batch, 2, (rw.n_lat + rw.n_ctx) // c),
        in_specs=[pl.BlockSpec((c, RET_KEY), lambda b, d, s: (rb(b, d, s), 0)),
                  pl.BlockSpec((c, RET_KEY), lambda b, d, s: (rb(b, d, s), 1)),
                  pl.BlockSpec((c, RET_VAL), lambda b, d, s: (rb(b, d, s), 1)),
                  pl.BlockSpec((c, RET_DK), lambda b, d, s: (rw.lat_chunk(d, s, c), 0)),
                  pl.BlockSpec((c, RET_DK), lambda b, d, s: (rw.lat_chunk(d, s, c), 0)),
                  pl.BlockSpec((1, 1, 128), lambda b, d, s: (d, 0, 0))],
        out_specs=pl.BlockSpec((1, c, RET_VAL), lambda b, d, s: (d, rb(b, d, s), 0)),
        out_shape=jax.ShapeDtypeStruct((2, rw.rows, RET_VAL), F32),
        scratch_shapes=[pltpu.VMEM((RET_HEADS, RET_DK, RET_DV), F32)],
        compiler_params=_cparams(("parallel", "parallel", "arbitrary")),
    )(proj, proj, proj, cos_tab, sin_tab, dl)


def _lru_scan_kernel(x_ref, p_ref, n_ref, cw_ref, cb_ref, wg_ref, bg_ref, lam_ref, h_ref, carry_ref, *, rw, rev):
    @pl.when(pl.program_id(1) == 0)
    def _():
        carry_ref[...] = jnp.zeros_like(carry_ref)

    t = x_ref.shape[0]
    tile = rw.chunk_block(pl.program_id(0), 1 if rev else 0, pl.program_id(1), t)
    first, last = _seg_flags(rw, tile)
    xs = _conv_core(x_ref, p_ref, n_ref, cw_ref, first, last) + cb_ref[...]
    row8 = lax.broadcasted_iota(jnp.int32, (t // 8, 8, LRU_BW), 1)
    sp = _softplus(-lam_ref[...])
    for n in range(LRU_BLOCKS):
        sl = slice(n * LRU_BW, (n + 1) * LRU_BW)
        xn = xs[:, sl]
        gates = jax.nn.sigmoid(_dot(_bf(xn), wg_ref[n]) + bg_ref[n])
        log_a = -LRU_C * gates[:, :LRU_BW] * sp[:, sl]
        a = jnp.exp(log_a)
        b = jnp.sqrt(-jnp.tanh(log_a) * (a * a + 1.0)) * gates[:, LRU_BW:] * xn
        groups = t // 8
        a3 = a.reshape(groups, 8, LRU_BW)
        b3 = b.reshape(groups, 8, LRU_BW)
        for k in (1, 2, 4):
            if rev:
                keep = row8 < 8 - k
                shift = 8 - k
            else:
                keep = row8 >= k
                shift = k
            sa = jnp.where(keep, pltpu.roll(a3, shift, 1), 1.0)
            sb = jnp.where(keep, pltpu.roll(b3, shift, 1), 0.0)
            b3 = b3 + a3 * sb
            a3 = a3 * sa
        carry = carry_ref[:, sl]
        for gi in (reversed(range(groups)) if rev else range(groups)):
            hg = a3[gi] * carry + b3[gi]
            carry = hg[0:1] if rev else hg[7:8]
            h_ref[gi * 8:(gi + 1) * 8, sl] = hg
        carry_ref[:, sl] = carry


def _lru_scan(rw, proj, conv_w, conv_b, w_gate, b_gate, lam, rev):
    t = TM_CONV
    t8 = t // 8
    n8 = rw.rows // 8
    d = 1 if rev else 0
    rb = lambda b, s: rw.chunk_block(b, d, s, t)
    return pl.pallas_call(
        functools.partial(_lru_scan_kernel, rw=rw, rev=rev),
        grid=(rw.batch, (rw.n_lat + rw.n_ctx) // t),
        in_specs=[pl.BlockSpec((t, LRU_W), lambda b, s: (rb(b, s), 0)),
                  pl.BlockSpec((8, LRU_W), lambda b, s: (jnp.maximum(rb(b, s) * t8 - 1, 0), 0)),
                  pl.BlockSpec((8, LRU_W), lambda b, s: (jnp.minimum((rb(b, s) + 1) * t8, n8 - 1), 0)),
                  pl.BlockSpec((4, LRU_W), lambda b, s: (0, 0)),
                  pl.BlockSpec((1, LRU_W), lambda b, s: (0, 0)),
                  pl.BlockSpec((LRU_BLOCKS, LRU_BW, 2 * LRU_BW), lambda b, s: (0, 0, 0)),
                  pl.BlockSpec((LRU_BLOCKS, 1, 2 * LRU_BW), lambda b, s: (0, 0, 0)),
                  pl.BlockSpec((1, LRU_W), lambda b, s: (0, 0))],
        out_specs=pl.BlockSpec((t, LRU_W), lambda b, s: (rb(b, s), 0)),
        out_shape=jax.ShapeDtypeStruct((rw.rows, LRU_W), F32),
        scratch_shapes=[pltpu.VMEM((1, LRU_W), F32)],
        compiler_params=_cparams(("parallel", "arbitrary")),
    )(proj, proj, proj, conv_w, conv_b.reshape(1, LRU_W), _bf(w_gate),
      b_gate.reshape(LRU_BLOCKS, 1, 2 * LRU_BW), lam.reshape(1, LRU_W))


def _rope16(x, tab):
    cos, s1, s2 = tab[:, :128], tab[:, 128:256], tab[:, 256:]
    cols = []
    for cidx in range(x.shape[1] // 128):
        xc = x[:, cidx * 128:(cidx + 1) * 128]
        cols.append(xc * cos + pltpu.roll(xc, 112, 1) * s1 + pltpu.roll(xc, 16, 1) * s2)
    return cols[0] if len(cols) == 1 else jnp.concatenate(cols, axis=1)


def _swa_kernel(sink_ref, q_ref, kp_ref, kc_ref, kn_ref, vp_ref, vc_ref, vn_ref, kx_ref, vx_ref,
                tp_ref, tc_ref, tn_ref, o_ref):
    qb = pl.program_id(1)
    nb = pl.num_programs(1)
    blk = SWA_BLOCK
    n_loc = 3 * blk
    n_keys = n_loc + kx_ref.shape[0]
    tab_c = tc_ref[...]
    q = _rope16(q_ref[...], tab_c) * (SWA_HD ** -0.5)
    k_all = jnp.concatenate([_rope16(kp_ref[...], tp_ref[...]), _rope16(kc_ref[...], tab_c),
                             _rope16(kn_ref[...], tn_ref[...]), kx_ref[...]], axis=0)
    v_all = jnp.concatenate([vp_ref[...], vc_ref[...], vn_ref[...], vx_ref[...]], axis=0)
    assert SWA_WINDOW >= blk - 1
    qi = lax.broadcasted_iota(jnp.int32, (blk, blk), 0)
    kj = lax.broadcasted_iota(jnp.int32, (blk, blk), 1)
    valid_prev = jnp.logical_and(kj - blk - qi >= -SWA_WINDOW, qb > 0)
    valid_next = jnp.logical_and(kj + blk - qi <= SWA_WINDOW, qb < nb - 1)
    lane = lax.broadcasted_iota(jnp.int32, (n_keys, 128), 1)
    low = lane < SWA_HD
    out_low = lax.broadcasted_iota(jnp.int32, (blk, 128), 1) < SWA_HD

    def expo(scores, sink):
        sc = jnp.concatenate([jnp.where(valid_prev, scores[:, :blk], -jnp.inf), scores[:, blk:2 * blk],
                              jnp.where(valid_next, scores[:, 2 * blk:n_loc], -jnp.inf), scores[:, n_loc:]], axis=1)
        mx = jnp.maximum(jnp.max(sc, axis=-1, keepdims=True), sink)
        return _bf(jnp.exp(sc - mx)), jnp.exp(sink - mx)

    group = SWA_QH // SWA_KVH
    for g in range(SWA_KVH):
        col = slice((g // 2) * 128, (g // 2 + 1) * 128)
        native_low = g % 2 == 0
        sel = low if native_low else jnp.logical_not(low)
        k_nat = jnp.where(sel, k_all[:, col], 0.0)
        v_nat = jnp.where(sel, v_all[:, col], 0.0)
        k_oth = pltpu.roll(k_nat, SWA_HD, 1)
        v_oth = pltpu.roll(v_nat, SWA_HD, 1)
        k_lo, k_hi = (k_nat, k_oth) if native_low else (k_oth, k_nat)
        v_lo, v_hi = (v_nat, v_oth) if native_low else (v_oth, v_nat)
        v_lo = jnp.where(lane == SWA_HD, 1.0, v_lo)
        v_hi = jnp.where(lane == 0, 1.0, v_hi)
        k_lo, k_hi, v_lo, v_hi = _bf(k_lo), _bf(k_hi), _bf(v_lo), _bf(v_hi)
        for cidx in range(group // 2):
            qc = (g * group) // 2 + cidx
            qv = _bf(q[:, qc * 128:(qc + 1) * 128])
            e_lo, sink_lo = expo(_dot_nt(qv, k_lo), sink_ref[2 * qc])
            e_hi, sink_hi = expo(_dot_nt(qv, k_hi), sink_ref[2 * qc + 1])
            a_lo = _dot(e_lo, v_lo)
            a_hi = _dot(e_hi, v_hi)
            inv_lo = 1.0 / (a_lo[:, SWA_HD:SWA_HD + 1] + sink_lo)
            inv_hi = 1.0 / (a_hi[:, 0:1] + sink_hi)
            o_ref[:, qc * 128:(qc + 1) * 128] = jnp.where(out_low, a_lo * inv_lo, a_hi * inv_hi)


def _swa_core(rw, proj, sink, tab):
    blk = SWA_BLOCK
    nb = rw.n_lat // blk
    kcol = SWA_QD // SWA_KVD
    vcol = kcol + 1
    ctx_blk = lambda b: rw.lat_rows // rw.n_ctx + b
    prev = lambda q: jnp.maximum(q - 1, 0)
    nxt = lambda q: jnp.minimum(q + 1, nb - 1)
    kv = lambda col, f: pl.BlockSpec((blk, SWA_KVD), lambda b, q: (b * nb + f(q), col))
    same = lambda q: q
    return pl.pallas_call(
        _swa_kernel,
        grid=(rw.batch, nb),
        in_specs=[pl.BlockSpec(memory_space=pltpu.SMEM),
                  pl.BlockSpec((blk, SWA_QD), lambda b, q: (b * nb + q, 0)),
                  kv(kcol, prev), kv(kcol, same), kv(kcol, nxt),
                  kv(vcol, prev), kv(vcol, same), kv(vcol, nxt),
                  pl.BlockSpec((rw.n_ctx, SWA_KVD), lambda b, q: (ctx_blk(b), kcol)),
                  pl.BlockSpec((rw.n_ctx, SWA_KVD), lambda b, q: (ctx_blk(b), vcol)),
                  pl.BlockSpec((blk, 384), lambda b, q: (prev(q), 0)),
                  pl.BlockSpec((blk, 384), lambda b, q: (q, 0)),
                  pl.BlockSpec((blk, 384), lambda b, q: (nxt(q), 0))],
        out_specs=pl.BlockSpec((blk, SWA_QD), lambda b, q: (b * nb + q, 0)),
        out_shape=jax.ShapeDtypeStruct((rw.lat_rows, SWA_QD), F32),
        compiler_params=_cparams(("parallel", "parallel")),
    )(sink, proj, proj, proj, proj, proj, proj, proj, proj, proj, tab, tab, tab)


def _axial_angles(n_lat, dh):
    t = jnp.arange(n_lat)
    row = (t // GRID_W).astype(F32)
    col = (t % GRID_W).astype(F32)
    nf = dh // 4
    inv = ROPE_THETA ** (-jnp.arange(nf, dtype=F32) / nf)
    return row[:, None] * inv, col[:, None] * inv


def _ret_tables(n_lat):
    ar, ac = _axial_angles(n_lat, RET_DK)
    cos = jnp.concatenate([jnp.cos(ar), jnp.cos(ar), jnp.cos(ac), jnp.cos(ac)], axis=1)
    sin = jnp.concatenate([-jnp.sin(ar), jnp.sin(ar), -jnp.sin(ac), jnp.sin(ac)], axis=1)
    return cos, sin


def _swa_table(n_lat):
    ar, ac = _axial_angles(n_lat, SWA_HD)
    z = jnp.zeros_like(ar)
    cos = jnp.concatenate([jnp.cos(ar), jnp.cos(ar), jnp.cos(ac), jnp.cos(ac)], axis=1)
    s1 = jnp.concatenate([-jnp.sin(ar), z, -jnp.sin(ac), z], axis=1)
    s2 = jnp.concatenate([z, jnp.sin(ar), z, jnp.sin(ac)], axis=1)
    return jnp.concatenate([jnp.tile(cos, (1, 2)), jnp.tile(s1, (1, 2)), jnp.tile(s2, (1, 2))], axis=1)


def _gdn_layer(rw, layer, x, mod3, gain, w_in, conv_w, a_log, dt_bias, norm_g, w_out):
    w_main = _bf(w_in[:, :GDN_MAIN])
    n_ab = w_in.shape[1] - GDN_MAIN
    w_ab = _bf(jnp.pad(w_in[:, GDN_MAIN:], ((0, 0), (0, 128 - n_ab))))
    proj, ab = _inproj(rw, x, gain, mod3, layer, w_main, w_ab)
    hv = GDN_V_HEADS
    alog_row = jnp.zeros((1, 128), F32).at[0, 0:hv].set(a_log[0]).at[0, 2 * hv:3 * hv].set(a_log[1])
    dtb_row = jnp.zeros((1, 128), F32).at[0, 0:hv].set(dt_bias[0]).at[0, 2 * hv:3 * hv].set(dt_bias[1])
    cb = 2048
    row_spec = pl.BlockSpec((TM_CONV, 128), lambda i, j: (i, 0))
    vec_spec = pl.BlockSpec((1, 128), lambda i, j: (0, 0))
    qkv, g_all, b_all = pl.pallas_call(
        functools.partial(_gdn_conv_kernel, rw=rw),
        grid=(rw.rows // TM_CONV, GDN_CONV // cb),
        in_specs=_conv_specs(rw, cb) + [row_spec, vec_spec, vec_spec],
        out_specs=[pl.BlockSpec((TM_CONV, cb), lambda i, j: (i, j)), row_spec, row_spec],
        out_shape=[jax.ShapeDtypeStruct((rw.rows, GDN_CONV), F32),
                   jax.ShapeDtypeStruct((rw.rows, 128), F32),
                   jax.ShapeDtypeStruct((rw.rows, 128), F32)],
        compiler_params=_cparams(("parallel", "arbitrary")),
    )(proj, proj, proj, conv_w, ab, alog_row, dtb_row)
    g_dir = jnp.stack([g_all[:, 0:hv], g_all[:, 2 * hv:3 * hv]])
    b_dir = jnp.stack([b_all[:, hv:2 * hv], b_all[:, 3 * hv:4 * hv]])
    o = _gdn_core(rw, qkv, g_dir, b_dir)
    return _two_dir_out(_gdn_out_kernel, rw, layer, x, mod3, o, proj, GDN_CONV // GDN_VAL, norm_g, _bf(w_out))


def _ret_layer(rw, layer, x, mod3, gain, w_in, decay_logit, norm_g, w_out):
    proj = _inproj(rw, x, gain, mod3, layer, _bf(w_in))
    cos_tab, sin_tab = _ret_tables(rw.n_lat)
    o = _ret_core(rw, proj, cos_tab, sin_tab, decay_logit)
    return _two_dir_out(_ret_out_kernel, rw, layer, x, mod3, o, proj, (2 * RET_KEY + RET_VAL) // RET_VAL,
                        norm_g, _bf(w_out))


def _lru_layer(rw, layer, x, mod3, gain, w_in, conv_w, conv_b, w_gate, b_gate, lam, w_out):
    proj = _inproj(rw, x, gain, mod3, layer, _bf(w_in))
    h0 = _lru_scan(rw, proj, conv_w, conv_b, w_gate[0], b_gate[0], lam[0], rev=False)
    h1 = _lru_scan(rw, proj, conv_w, conv_b, w_gate[1], b_gate[1], lam[1], rev=True)
    tm = TM_OUT
    specs = [pl.BlockSpec((tm, LRU_W), lambda i: (i, 0)),
             pl.BlockSpec((tm, LRU_W), lambda i: (i, 0)),
             pl.BlockSpec((tm, LRU_W), lambda i: (i, 1))]
    return _outproj_call(_lru_out_kernel, rw, rw.rows, layer, x, mod3, specs, [h0, h1, proj], _bf(w_out), [])


def _swa_layer(rw, layer, x, mod3, gain, w_in, sink, w_out):
    proj = _inproj(rw, x, gain, mod3, layer, _bf(w_in))
    o = _swa_core(rw, proj, sink, _swa_table(rw.n_lat))
    specs = [pl.BlockSpec((TM_OUT, SWA_QD), lambda i: (i, 0))]
    return _outproj_call(_swa_out_kernel, rw, rw.lat_rows, layer, x, mod3, specs, [o], _bf(w_out), [])


def kernel(x, c, ctx, c_ctx, norm_mix_g, norm_ffn_g, w_mod, b_mod, w_ff1, w_ff2, norm_out_g, gdn_w_in, gdn_conv_w, gdn_a_log, gdn_dt_bias, gdn_norm_g, gdn_w_out, ret_w_in, ret_decay_logit, ret_norm_g, ret_w_out, lru_w_in, lru_conv_w, lru_conv_b, lru_w_gate, lru_b_gate, lru_lambda, lru_w_out, swa_w_in, swa_sink, swa_w_out):
    batch, n_lat, _ = x.shape
    n_ctx = ctx.shape[1]
    assert batch + 1 <= MOD_ROWS and n_lat % TM_IN == 0 and (batch * n_ctx) % TM_IN == 0
    assert w_mod.shape[0] == DEPTH and gdn_w_in.shape[0] == 1 and swa_w_in.shape[0] == 1
    rw = _Rows(batch, n_lat, n_ctx)
    xf = jnp.concatenate([x.reshape(rw.lat_rows, D), ctx.reshape(batch * n_ctx, D)], axis=0)
    cc = jnp.zeros((MOD_ROWS, D), F32).at[:batch].set(c).at[batch].set(c_ctx)
    mod3 = _adaln(cc, w_mod, b_mod)

    def mlp(layer, xin, n_rows, final):
        return _mlp(rw, xin, n_rows, norm_ffn_g[layer], mod3, layer, _bf(w_ff1[layer]), _bf(w_ff2[layer]),
                    norm_out_g, final)

    xf = _gdn_layer(rw, 0, xf, mod3, norm_mix_g[0], gdn_w_in[0], gdn_conv_w[0], gdn_a_log[0], gdn_dt_bias[0],
                    gdn_norm_g[0], gdn_w_out[0])
    xf = mlp(0, xf, rw.rows, False)
    xf = _ret_layer(rw, 1, xf, mod3, norm_mix_g[1], ret_w_in[0], ret_decay_logit[0], ret_norm_g[0], ret_w_out[0])
    xf = mlp(1, xf, rw.rows, False)
    xf = _lru_layer(rw, 2, xf, mod3, norm_mix_g[2], lru_w_in[0], lru_conv_w[0], lru_conv_b[0], lru_w_gate[0],
                    lru_b_gate[0], lru_lambda[0], lru_w_out[0])
    xf = mlp(2, xf, rw.rows, False)
    xl = _swa_layer(rw, 3, xf, mod3, norm_mix_g[3], swa_w_in[0], swa_sink[0], swa_w_out[0])
    out = mlp(3, xl, rw.lat_rows, True)
    return out.reshape(batch, n_lat, D)
```

```python
import functools
import math

import jax
import jax.numpy as jnp
from jax import lax
from jax.experimental import pallas as pl
from jax.experimental.pallas import tpu as pltpu

F32 = jnp.float32
BF16 = jnp.bfloat16

D = 1024
D_FF = 4 * D
EPS = 1e-6
DEPTH = 4
ROPE_THETA = 10000.0
GRID_W = 64
CHUNK = 64
MOD_ROWS = 16

GDN_QK_HEADS = 8
GDN_V_HEADS = 16
GDN_HD = 128
GDN_KEY = GDN_QK_HEADS * GDN_HD
GDN_VAL = GDN_V_HEADS * GDN_HD
GDN_CONV = 2 * GDN_KEY + GDN_VAL
GDN_MAIN = GDN_CONV + GDN_VAL

RET_HEADS = 4
RET_DK = 256
RET_DV = 512
RET_KEY = RET_HEADS * RET_DK
RET_VAL = RET_HEADS * RET_DV

LRU_W = 1280
LRU_BLOCKS = 10
LRU_BW = LRU_W // LRU_BLOCKS
LRU_C = 8.0

SWA_QH = 16
SWA_KVH = 4
SWA_HD = 64
SWA_BLOCK = 128
SWA_WINDOW = 128
SWA_QD = SWA_QH * SWA_HD
SWA_KVD = SWA_KVH * SWA_HD

TM_IN = 1024
TM_OUT = 512
TM_CONV = 256
TN_IN_MAX = 1536
FF_CHUNK = 1024
RET_CHUNK = 256
GDN_BLOCK = 2 * CHUNK
VMEM_LIMIT = 48 * 1024 * 1024


def _dot(a, b):
    return jnp.dot(a, b, preferred_element_type=F32)


def _dot_nt(a, b):
    return lax.dot_general(a, b, (((1,), (1,)), ((), ())), preferred_element_type=F32)


def _dot_tn(a, b):
    return lax.dot_general(a, b, (((0,), (0,)), ((), ())), preferred_element_type=F32)


def _bf(x):
    return x.astype(BF16)


def _split3(x):
    hi = _bf(x)
    r = x - hi.astype(F32)
    mid = _bf(r)
    lo = _bf(r - mid.astype(F32))
    return hi, mid, lo


def _silu(x):
    return x * jax.nn.sigmoid(x)


def _softplus(x):
    return jnp.maximum(x, 0.0) + jnp.log1p(jnp.exp(-jnp.abs(x)))


def _gelu_tanh(x):
    cdf = 0.5 * (1.0 + jnp.tanh(math.sqrt(2.0 / math.pi) * (x + 0.044715 * (x * x * x))))
    return x * cdf


def _cparams(sem):
    return pltpu.CompilerParams(dimension_semantics=sem, vmem_limit_bytes=VMEM_LIMIT)


class _Rows:
    def __init__(self, batch, n_lat, n_ctx):
        self.batch, self.n_lat, self.n_ctx = batch, n_lat, n_ctx
        self.lat_rows = batch * n_lat
        self.rows = self.lat_rows + batch * n_ctx
        self.n_steps = (n_lat + n_ctx) // CHUNK
        self.ctx_steps = n_ctx // CHUNK
        self.lat_steps = n_lat // CHUNK

    def mod_row(self, i, tm):
        return jnp.where(i < self.lat_rows // tm, i // (self.n_lat // tm), self.batch)

    def chunk_block(self, b, d, s, blk):
        nc, nl = self.n_ctx // blk, self.n_lat // blk
        c_ctx = jnp.where(d == 0, s, nc - 1 - s)
        c_lat = jnp.where(d == 0, s - nc, nl - 1 - (s - nc))
        return jnp.where(s < nc, self.lat_rows // blk + b * nc + c_ctx, b * nl + c_lat)

    def lat_chunk(self, d, s, blk):
        nc, nl = self.n_ctx // blk, self.n_lat // blk
        return jnp.where(s < nc, 0, jnp.where(d == 0, s - nc, nl - 1 - (s - nc)))


def _adaln_kernel(c_ref, w_ref, b_ref, o_ref):
    s = _silu(c_ref[...])
    w = w_ref[0]
    s_hi = _bf(s)
    s_lo = _bf(s - s_hi.astype(F32))
    w_hi = _bf(w)
    w_lo = _bf(w - w_hi.astype(F32))
    y = _dot(s_hi, w_hi) + (_dot(s_lo, w_hi) + _dot(s_hi, w_lo))
    o_ref[0] = y + b_ref[0]


def _adaln(cc, w_mod, b_mod):
    depth, _, n = w_mod.shape
    tn = 1024
    out = pl.pallas_call(
        _adaln_kernel,
        grid=(depth, n // tn),
        in_specs=[pl.BlockSpec((MOD_ROWS, D), lambda l, j: (0, 0)),
                  pl.BlockSpec((1, D, tn), lambda l, j: (l, 0, j)),
                  pl.BlockSpec((1, 1, tn), lambda l, j: (l, 0, j))],
        out_specs=pl.BlockSpec((1, MOD_ROWS, tn), lambda l, j: (l, 0, j)),
        out_shape=jax.ShapeDtypeStruct((depth, MOD_ROWS, n), F32),
        compiler_params=_cparams(("parallel", "parallel")),
    )(cc, w_mod, b_mod.reshape(depth, 1, n))
    return out.reshape(depth * MOD_ROWS * 6, 1, D)


def _mod_spec(rw, layer, which, tm):
    base = layer * MOD_ROWS * 6
    return pl.BlockSpec((1, 1, D), lambda i, *_: (base + rw.mod_row(i, tm) * 6 + which, 0, 0))


def _norm_mod(x, gain, shift, scale):
    y = x * lax.rsqrt(jnp.mean(x * x, axis=-1, keepdims=True) + EPS) * gain
    return y * (1.0 + scale) + shift


def _inproj_kernel(x_ref, g_ref, sh_ref, sc_ref, w_ref, *rest, has_extra):
    if has_extra:
        w2_ref, o_ref, o2_ref, h_ref = rest
    else:
        o_ref, h_ref = rest

    @pl.when(pl.program_id(1) == 0)
    def _():
        h = _bf(_norm_mod(x_ref[...], g_ref[...], sh_ref[0], sc_ref[0]))
        h_ref[...] = h
        if has_extra:
            o2_ref[...] = _dot(h, w2_ref[...])

    o_ref[...] = _dot(h_ref[...], w_ref[...])


def _inproj(rw, x, gain, mod3, layer, w, w_extra=None):
    n = w.shape[1]
    tm = TM_IN
    tn = max(t for t in range(256, TN_IN_MAX + 1, 256) if n % t == 0)
    has_extra = w_extra is not None
    in_specs = [pl.BlockSpec((tm, D), lambda i, j: (i, 0)),
                pl.BlockSpec((1, D), lambda i, j: (0, 0)),
                _mod_spec(rw, layer, 0, tm),
                _mod_spec(rw, layer, 1, tm),
                pl.BlockSpec((D, tn), lambda i, j: (0, j))]
    out_specs = [pl.BlockSpec((tm, tn), lambda i, j: (i, j))]
    out_shape = [jax.ShapeDtypeStruct((rw.rows, n), F32)]
    args = [x, gain.reshape(1, D), mod3, mod3, w]
    if has_extra:
        ne = w_extra.shape[1]
        in_specs.append(pl.BlockSpec((D, ne), lambda i, j: (0, 0)))
        out_specs.append(pl.BlockSpec((tm, ne), lambda i, j: (i, 0)))
        out_shape.append(jax.ShapeDtypeStruct((rw.rows, ne), F32))
        args.append(w_extra)
    res = pl.pallas_call(
        functools.partial(_inproj_kernel, has_extra=has_extra),
        grid=(rw.rows // tm, n // tn),
        in_specs=in_specs, out_specs=out_specs, out_shape=out_shape,
        scratch_shapes=[pltpu.VMEM((tm, D), BF16)],
        compiler_params=_cparams(("parallel", "arbitrary")),
    )(*args)
    return res if has_extra else res[0]


def _mlp_kernel(x_ref, g_ref, sh_ref, sc_ref, gt_ref, w1_ref, w2_ref, go_ref, o_ref, h_ref, acc_ref,
                *, final_norm):
    k = pl.program_id(1)

    @pl.when(k == 0)
    def _():
        h_ref[...] = _bf(_norm_mod(x_ref[...], g_ref[...], sh_ref[0], sc_ref[0]))
        acc_ref[...] = jnp.zeros_like(acc_ref)

    a = jnp.square(jnp.maximum(_dot(h_ref[...], w1_ref[...]), 0.0))
    acc_ref[...] += _dot(_bf(a), w2_ref[...])

    @pl.when(k == pl.num_programs(1) - 1)
    def _():
        y = x_ref[...] + gt_ref[0] * acc_ref[...]
        if final_norm:
            y = y * lax.rsqrt(jnp.mean(y * y, axis=-1, keepdims=True) + EPS) * go_ref[...]
        o_ref[...] = y


def _mlp(rw, x, n_rows, gain, mod3, layer, w1, w2, out_gain, final_norm):
    tm, ck = TM_IN, FF_CHUNK
    return pl.pallas_call(
        functools.partial(_mlp_kernel, final_norm=final_norm),
        grid=(n_rows // tm, D_FF // ck),
        in_specs=[pl.BlockSpec((tm, D), lambda i, k: (i, 0)),
                  pl.BlockSpec((1, D), lambda i, k: (0, 0)),
                  _mod_spec(rw, layer, 3, tm),
                  _mod_spec(rw, layer, 4, tm),
                  _mod_spec(rw, layer, 5, tm),
                  pl.BlockSpec((D, ck), lambda i, k: (0, k)),
                  pl.BlockSpec((ck, D), lambda i, k: (k, 0)),
                  pl.BlockSpec((1, D), lambda i, k: (0, 0))],
        out_specs=pl.BlockSpec((tm, D), lambda i, k: (i, 0)),
        out_shape=jax.ShapeDtypeStruct((n_rows, D), F32),
        scratch_shapes=[pltpu.VMEM((tm, D), BF16), pltpu.VMEM((tm, D), F32)],
        compiler_params=_cparams(("parallel", "arbitrary")),
    )(x, gain.reshape(1, D), mod3, mod3, mod3, w1, w2, out_gain.reshape(1, D))


def _gdn_out_kernel(x_ref, gt_ref, o0_ref, o1_ref, z_ref, ng_ref, w_ref, out_ref, a_ref):
    ng = ng_ref[...]
    for h in range(GDN_V_HEADS):
        sl = slice(h * GDN_HD, (h + 1) * GDN_HD)
        o = o0_ref[0, :, sl].astype(F32) + o1_ref[0, :, sl].astype(F32)
        y = o * lax.rsqrt(jnp.mean(o * o, axis=-1, keepdims=True) + EPS) * ng
        a_ref[:, sl] = _bf(y * _silu(z_ref[:, sl]))
    out_ref[...] = x_ref[...] + gt_ref[0] * _dot(a_ref[...], w_ref[...])


def _ret_out_kernel(x_ref, gt_ref, o0_ref, o1_ref, z_ref, ng_ref, w_ref, out_ref, a_ref):
    for h in range(RET_HEADS):
        sl = slice(h * RET_DV, (h + 1) * RET_DV)
        o = o0_ref[0, :, sl].astype(F32) + o1_ref[0, :, sl].astype(F32)
        mu = jnp.mean(o, axis=-1, keepdims=True)
        oc = o - mu
        var = jnp.mean(oc * oc, axis=-1, keepdims=True)
        y = oc * lax.rsqrt(var + EPS) * ng_ref[:, sl]
        a_ref[:, sl] = _bf(y * _silu(z_ref[:, sl]))
    out_ref[...] = x_ref[...] + gt_ref[0] * _dot(a_ref[...], w_ref[...])


def _lru_out_kernel(x_ref, gt_ref, h0_ref, h1_ref, gb_ref, w_ref, out_ref):
    a = (h0_ref[...] + h1_ref[...]) * _gelu_tanh(gb_ref[...])
    out_ref[...] = x_ref[...] + gt_ref[0] * _dot(_bf(a), w_ref[...])


def _swa_out_kernel(x_ref, gt_ref, o_ref, w_ref, out_ref):
    out_ref[...] = x_ref[...] + gt_ref[0] * _dot(_bf(o_ref[...]), w_ref[...])


def _outproj_call(kern, rw, n_rows, layer, x, mod3, extra_specs, extra_args, w, scratch):
    tm = TM_OUT
    k = w.shape[0]
    return pl.pallas_call(
        kern,
        grid=(n_rows // tm,),
        in_specs=[pl.BlockSpec((tm, D), lambda i: (i, 0)), _mod_spec(rw, layer, 2, tm)] + extra_specs
                 + [pl.BlockSpec((k, D), lambda i: (0, 0))],
        out_specs=pl.BlockSpec((tm, D), lambda i: (i, 0)),
        out_shape=jax.ShapeDtypeStruct((n_rows, D), F32),
        scratch_shapes=scratch,
        compiler_params=_cparams(("parallel",)),
    )(x, mod3, *extra_args, w)


def _two_dir_out(kern, rw, layer, x, mod3, o, proj, z_block, norm_g, w):
    tm, k = TM_OUT, w.shape[0]
    ng = norm_g.reshape(1, -1)
    specs = [pl.BlockSpec((1, tm, k), lambda i: (0, i, 0)),
             pl.BlockSpec((1, tm, k), lambda i: (1, i, 0)),
             pl.BlockSpec((tm, k), lambda i: (i, z_block)),
             pl.BlockSpec(ng.shape, lambda i: (0, 0))]
    return _outproj_call(kern, rw, rw.rows, layer, x, mod3, specs, [o, o, proj, ng], w,
                         [pltpu.VMEM((tm, k), BF16)])


def _conv_core(x_ref, p_ref, n_ref, w_ref, first, last, pad_ref=None):
    t = x_ref.shape[0]
    x = x_ref[...]
    prev = p_ref[...] * jnp.where(first, 0.0, 1.0)
    nxt = n_ref[...] * jnp.where(last, 0.0, 1.0)
    if pad_ref is not None:
        pad_ref[0:8] = prev
        pad_ref[8:t + 8] = x
        pad_ref[t + 8:t + 16] = nxt
        xm2, xm1, xp1 = pad_ref[6:t + 6], pad_ref[7:t + 7], pad_ref[9:t + 9]
    else:
        row8 = lax.broadcasted_iota(jnp.int32, prev.shape, 0)
        r2 = pltpu.roll(x, 2, 0)
        r1 = pltpu.roll(x, 1, 0)
        rp = pltpu.roll(x, t - 1, 0)
        f2 = jnp.where(row8 < 2, pltpu.roll(prev, 2, 0), r2[:8])
        f1 = jnp.where(row8 < 1, pltpu.roll(prev, 1, 0), r1[:8])
        l1 = jnp.where(row8 == 7, pltpu.roll(nxt, 7, 0), rp[t - 8:])
        xm2 = jnp.concatenate([f2, r2[8:]], axis=0)
        xm1 = jnp.concatenate([f1, r1[8:]], axis=0)
        xp1 = jnp.concatenate([rp[:t - 8], l1], axis=0)
    w = w_ref[...]
    return w[0:1] * xm2 + w[1:2] * xm1 + w[2:3] * x + w[3:4] * xp1


def _seg_flags(rw, i):
    per = rw.n_lat // TM_CONV
    is_ctx = i >= rw.lat_rows // TM_CONV
    assert rw.n_ctx == TM_CONV
    first = jnp.logical_or(is_ctx, i % per == 0)
    last = jnp.logical_or(is_ctx, i % per == per - 1)
    return first, last


def _gdn_conv_kernel(x_ref, p_ref, n_ref, w_ref, ab_ref, alog_ref, dtb_ref, o_ref, g_ref, b_ref, pad_ref, *, rw):
    i, j = pl.program_id(0), pl.program_id(1)
    first, last = _seg_flags(rw, i)
    s = _silu(_conv_core(x_ref, p_ref, n_ref, w_ref, first, last, pad_ref))
    width = s.shape[1]
    qk_blocks = 2 * GDN_KEY // width

    @pl.when(j < qk_blocks)
    def _():
        for h in range(width // GDN_HD):
            sl = slice(h * GDN_HD, (h + 1) * GDN_HD)
            sh = s[:, sl]
            is_q = j * width + h * GDN_HD < GDN_KEY
            scale = jnp.where(is_q, GDN_HD ** -0.5, 1.0)
            o_ref[:, sl] = sh * (lax.rsqrt(jnp.sum(sh * sh, axis=-1, keepdims=True) + EPS) * scale)

    @pl.when(j >= qk_blocks)
    def _():
        o_ref[...] = s

    @pl.when(j == 0)
    def _():
        ab = ab_ref[...]
        g_ref[...] = -jnp.exp(alog_ref[...]) * _softplus(ab + dtb_ref[...])
        b_ref[...] = jax.nn.sigmoid(ab)


def _conv_specs(rw, cb):
    t8 = TM_CONV // 8
    n8 = rw.rows // 8
    return [pl.BlockSpec((TM_CONV, cb), lambda i, j: (i, j)),
            pl.BlockSpec((8, cb), lambda i, j: (jnp.maximum(i * t8 - 1, 0), j)),
            pl.BlockSpec((8, cb), lambda i, j: (jnp.minimum((i + 1) * t8, n8 - 1), j)),
            pl.BlockSpec((4, cb), lambda i, j: (0, j))]


def _dir_masks(d, n):
    ii = lax.broadcasted_iota(jnp.int32, (n, n), 0)
    jj = lax.broadcasted_iota(jnp.int32, (n, n), 1)
    t = (ii - jj) * jnp.where(d == 0, 1, -1)
    return t, t >= 0, t > 0


def _gdn_core_kernel(qkv_ref, g_ref, b_ref, o_ref, s_ref):
    d, s = pl.program_id(1), pl.program_id(2)

    @pl.when(s == 0)
    def _():
        s_ref[...] = jnp.zeros_like(s_ref)

    c, hd = CHUNK, GDN_HD
    rep = GDN_V_HEADS // GDN_QK_HEADS
    heads = range(GDN_V_HEADS)
    n_sub = qkv_ref.shape[0] // c
    subs = range(n_sub)
    offs = [pl.multiple_of(jnp.where(d == 0, i, n_sub - 1 - i) * c, c) for i in subs]
    chains = [(i, h) for i in subs for h in heads]
    t, incl, strict = _dir_masks(d, c)
    cum = _bf(jnp.where(incl, 1.0, 0.0))
    cum_t = _bf(jnp.where(t <= 0, 1.0, 0.0))
    g = [g_ref[0, pl.ds(offs[i], c), :] for i in subs]
    beta = [b_ref[0, pl.ds(offs[i], c), :] for i in subs]
    parts = [_split3(g[i]) for i in subs]
    gc = [sum(_dot(cum, p) for p in parts[i]) for i in subs]
    gr = [sum(_dot_tn(p, cum_t) for p in parts[i]) for i in subs]
    gtot = [jnp.sum(g[i], axis=0, keepdims=True) for i in subs]

    def rows(i, lo):
        return qkv_ref[pl.ds(offs[i], c), lo:lo + hd]

    q_l = [[rows(i, hq * hd) for hq in range(GDN_QK_HEADS)] for i in subs]
    k_l = [[rows(i, GDN_KEY + hq * hd) for hq in range(GDN_QK_HEADS)] for i in subs]
    gram = [[_dot_nt(_bf(jnp.concatenate([k, q], axis=0)), _bf(k)) for q, k in zip(q_l[i], k_l[i])]
            for i in subs]
    gcol = [gc[i][:, h:h + 1] for i, h in chains]
    bcol = [beta[i][:, h:h + 1] for i, h in chains]
    dec = [jnp.where(incl, jnp.exp(jnp.where(incl, gcol[n] - gr[i][h:h + 1, :], 0.0)), 0.0)
           for n, (i, h) in enumerate(chains)]
    eg = [jnp.exp(x) for x in gcol]
    m = [jnp.where(strict, -(gram[i][h // rep][:c] * dec[n] * bcol[n]), 0.0) for n, (i, h) in enumerate(chains)]
    rhs = [jnp.concatenate([k_l[i][h // rep] * (bcol[n] * eg[n]), rows(i, 2 * GDN_KEY + h * hd) * bcol[n]], axis=1)
           for n, (i, h) in enumerate(chains)]
    ids = range(len(chains))

    m_hi = [_bf(x) for x in m]
    zeros_cc = jnp.zeros((c, 2 * c), BF16)
    lane_cc = lax.broadcasted_iota(jnp.int32, (c, 2 * c), 1)
    row_cc = lax.broadcasted_iota(jnp.int32, (c, 2 * c), 0)
    eye_hi = jnp.where(lane_cc == row_cc + c, 1.0, 0.0)
    is_hi = lane_cc >= c
    cat = [jnp.concatenate([m[n], jnp.zeros((c, c), F32)], axis=1) + eye_hi for n in ids]
    for _ in range(6):
        cb = [_bf(x) for x in cat]
        r = [_dot(cb[n], jnp.concatenate([cb[n], zeros_cc], axis=0)) for n in ids]
        cat = [r[n] + jnp.where(is_hi, cat[n], 0.0) for n in ids]
    nb = [_bf(cat[n] - eye_hi) for n in ids]
    zeros_rhs = jnp.zeros((c, 2 * hd), BF16)

    def apply_n(y):
        return [_dot(nb[n], jnp.concatenate([zeros_rhs, _bf(y[n])], axis=0)) for n in ids]

    nr = apply_n(rhs)
    x = [rhs[n] + nr[n] for n in ids]
    resid = [(rhs[n] - x[n]) + _dot(m_hi[n], _bf(x[n])) for n in ids]
    nr = apply_n(resid)
    x = [x[n] + (resid[n] + nr[n]) for n in ids]
    qk = [_bf(gram[i][h // rep][c:] * dec[n]) for n, (i, h) in enumerate(chains)]
    gt = [gtot[i][:, h:h + 1] for i, h in chains]
    kd = [_bf(k_l[i][h // rep] * jnp.exp(gt[n] - gcol[n])) for n, (i, h) in enumerate(chains)]
    wq = [_bf(jnp.concatenate([x[n][:, :hd], q_l[i][h // rep] * eg[n]], axis=0)) for n, (i, h) in enumerate(chains)]
    decay = [jnp.exp(x) for x in gt]

    st = [s_ref[h] for h in heads]
    for i in subs:
        base = i * GDN_V_HEADS
        r = [_dot(wq[base + h], _bf(st[h])) for h in heads]
        unb = [_bf(x[base + h][:, hd:] - r[h][:c]) for h in heads]
        for h in heads:
            o_ref[0, pl.ds(offs[i], c), h * hd:(h + 1) * hd] = _bf(r[h][c:] + _dot(qk[base + h], unb[h]))
        st = [st[h] * decay[base + h] + _dot_tn(kd[base + h], unb[h]) for h in heads]
    for h in heads:
        s_ref[h] = st[h]


def _gdn_core(rw, qkv, g_dir, b_dir):
    blk = GDN_BLOCK
    rb = lambda b, d, s: rw.chunk_block(b, d, s, blk)
    return pl.pallas_call(
        _gdn_core_kernel,
        grid=(rw.batch, 2, (rw.n_lat + rw.n_ctx) // blk),
        in_specs=[pl.BlockSpec((blk, GDN_CONV), lambda b, d, s: (rb(b, d, s), 0)),
                  pl.BlockSpec((1, blk, GDN_V_HEADS), lambda b, d, s: (d, rb(b, d, s), 0)),
                  pl.BlockSpec((1, blk, GDN_V_HEADS), lambda b, d, s: (d, rb(b, d, s), 0))],
        out_specs=pl.BlockSpec((1, blk, GDN_VAL), lambda b, d, s: (d, rb(b, d, s), 0)),
        out_shape=jax.ShapeDtypeStruct((2, rw.rows, GDN_VAL), BF16),
        scratch_shapes=[pltpu.VMEM((GDN_V_HEADS, GDN_HD, GDN_HD), F32)],
        compiler_params=_cparams(("parallel", "parallel", "arbitrary")),
    )(qkv, g_dir, b_dir)


def _rope_half(x, cos, sin_signed):
    return x * cos + pltpu.roll(x, x.shape[1] // 2, 1) * sin_signed


def _ret_core_kernel(q_ref, k_ref, v_ref, cos_ref, sin_ref, dl_ref, o_ref, s_ref, *, ctx_steps):
    d, s = pl.program_id(1), pl.program_id(2)

    @pl.when(s == 0)
    def _():
        s_ref[...] = jnp.zeros_like(s_ref)

    c = q_ref.shape[0]
    heads = range(RET_HEADS)
    is_lat = s >= ctx_steps
    cos = jnp.where(is_lat, cos_ref[...], 1.0)
    sin = jnp.where(is_lat, sin_ref[...], 0.0)
    t, incl, _ = _dir_masks(d, c)
    tf = t.astype(F32)
    pos = lax.broadcasted_iota(jnp.int32, (c, 1), 0)
    ridx = jnp.where(d == 0, pos + 1, c - pos).astype(F32)
    log_gamma = -_softplus(-dl_ref[0])

    def rope(x):
        hw = RET_DK // 2
        return jnp.concatenate([_rope_half(x[:, :hw], cos[:, :hw], sin[:, :hw]),
                                _rope_half(x[:, hw:], cos[:, hw:], sin[:, hw:])], axis=1)

    lg = [log_gamma[:, h:h + 1] for h in heads]
    q = [rope(q_ref[:, h * RET_DK:(h + 1) * RET_DK]) * (RET_DK ** -0.5) for h in heads]
    k = [rope(k_ref[:, h * RET_DK:(h + 1) * RET_DK]) for h in heads]
    vb = [_bf(v_ref[:, h * RET_DV:(h + 1) * RET_DV]) for h in heads]
    gcum = [lg[h] * ridx for h in heads]
    dec = [jnp.where(incl, jnp.exp(jnp.where(incl, lg[h] * tf, 0.0)), 0.0) for h in heads]
    qk = [_bf(_dot_nt(_bf(q[h]), _bf(k[h])) * dec[h]) for h in heads]
    st = [s_ref[h] for h in heads]
    for h in heads:
        o_ref[0, :, h * RET_DV:(h + 1) * RET_DV] = _bf(
            _dot(_bf(q[h] * jnp.exp(gcum[h])), _bf(st[h])) + _dot(qk[h], vb[h]))
    gtot = [lg[h] * float(c) for h in heads]
    kd = [_bf(k[h] * jnp.exp(gtot[h] - gcum[h])) for h in heads]
    for h in heads:
        s_ref[h] = st[h] * jnp.exp(gtot[h]) + _dot_tn(kd[h], vb[h])


def _ret_core(rw, proj, cos_tab, sin_tab, decay_logit):
    c = RET_CHUNK
    rb = lambda b, d, s: rw.chunk_block(b, d, s, c)
    dl = jnp.zeros((2, 1, 128), F32).at[:, 0, :RET_HEADS].set(decay_logit)
    return pl.pallas_call(
        functools.partial(_ret_core_kernel, ctx_steps=rw.n_ctx // c),
        grid=(rw.batch, 2, (rw.n_lat + rw.n_ctx) // c),
        in_specs=[pl.BlockSpec((c, RET_KEY), lambda b, d, s: (rb(b, d, s), 0)),
                  pl.BlockSpec((c, RET_KEY), lambda b, d, s: (rb(b, d, s), 1)),
                  pl.BlockSpec((c, RET_VAL), lambda b, d, s: (rb(b, d, s), 1)),
                  pl.BlockSpec((c, RET_DK), lambda b, d, s: (rw.lat_chunk(d, s, c), 0)),
                  pl.BlockSpec((c, RET_DK), lambda b, d, s: (rw.lat_chunk(d, s, c), 0)),
                  pl.BlockSpec((1, 1, 128), lambda b, d, s: (d, 0, 0))],
        out_specs=pl.BlockSpec((1, c, RET_VAL), lambda b, d, s: (d, rb(b, d, s), 0)),
        out_shape=jax.ShapeDtypeStruct((2, rw.rows, RET_VAL), BF16),
        scratch_shapes=[pltpu.VMEM((RET_HEADS, RET_DK, RET_DV), F32)],
        compiler_params=_cparams(("parallel", "parallel", "arbitrary")),
    )(proj, proj, proj, cos_tab, sin_tab, dl)


def _lru_scan_kernel(x_ref, p_ref, n_ref, cw_ref, cb_ref, wg_ref, bg_ref, lam_ref, h_ref, carry_ref, *, rw, rev):
    @pl.when(pl.program_id(1) == 0)
    def _():
        carry_ref[...] = jnp.zeros_like(carry_ref)

    t = x_ref.shape[0]
    tile = rw.chunk_block(pl.program_id(0), 1 if rev else 0, pl.program_id(1), t)
    first, last = _seg_flags(rw, tile)
    xs = _conv_core(x_ref, p_ref, n_ref, cw_ref, first, last) + cb_ref[...]
    row8 = lax.broadcasted_iota(jnp.int32, (t // 8, 8, LRU_BW), 1)
    sp = _softplus(-lam_ref[...])
    for n in range(LRU_BLOCKS):
        sl = slice(n * LRU_BW, (n + 1) * LRU_BW)
        xn = xs[:, sl]
        gates = jax.nn.sigmoid(_dot(_bf(xn), wg_ref[n]) + bg_ref[n])
        log_a = -LRU_C * gates[:, :LRU_BW] * sp[:, sl]
        a = jnp.exp(log_a)
        b = jnp.sqrt(-jnp.tanh(log_a) * (a * a + 1.0)) * gates[:, LRU_BW:] * xn
        groups = t // 8
        a3 = a.reshape(groups, 8, LRU_BW)
        b3 = b.reshape(groups, 8, LRU_BW)
        for k in (1, 2, 4):
            if rev:
                keep = row8 < 8 - k
                shift = 8 - k
            else:
                keep = row8 >= k
                shift = k
            sa = jnp.where(keep, pltpu.roll(a3, shift, 1), 1.0)
            sb = jnp.where(keep, pltpu.roll(b3, shift, 1), 0.0)
            b3 = b3 + a3 * sb
            a3 = a3 * sa
        carry = carry_ref[:, sl]
        for gi in (reversed(range(groups)) if rev else range(groups)):
            hg = a3[gi] * carry + b3[gi]
            carry = hg[0:1] if rev else hg[7:8]
            h_ref[gi * 8:(gi + 1) * 8, sl] = hg
        carry_ref[:, sl] = carry


def _lru_scan(rw, proj, conv_w, conv_b, w_gate, b_gate, lam, rev):
    t = TM_CONV
    t8 = t // 8
    n8 = rw.rows // 8
    d = 1 if rev else 0
    rb = lambda b, s: rw.chunk_block(b, d, s, t)
    return pl.pallas_call(
        functools.partial(_lru_scan_kernel, rw=rw, rev=rev),
        grid=(rw.batch, (rw.n_lat + rw.n_ctx) // t),
        in_specs=[pl.BlockSpec((t, LRU_W), lambda b, s: (rb(b, s), 0)),
                  pl.BlockSpec((8, LRU_W), lambda b, s: (jnp.maximum(rb(b, s) * t8 - 1, 0), 0)),
                  pl.BlockSpec((8, LRU_W), lambda b, s: (jnp.minimum((rb(b, s) + 1) * t8, n8 - 1), 0)),
                  pl.BlockSpec((4, LRU_W), lambda b, s: (0, 0)),
                  pl.BlockSpec((1, LRU_W), lambda b, s: (0, 0)),
                  pl.BlockSpec((LRU_BLOCKS, LRU_BW, 2 * LRU_BW), lambda b, s: (0, 0, 0)),
                  pl.BlockSpec((LRU_BLOCKS, 1, 2 * LRU_BW), lambda b, s: (0, 0, 0)),
                  pl.BlockSpec((1, LRU_W), lambda b, s: (0, 0))],
        out_specs=pl.BlockSpec((t, LRU_W), lambda b, s: (rb(b, s), 0)),
        out_shape=jax.ShapeDtypeStruct((rw.rows, LRU_W), F32),
        scratch_shapes=[pltpu.VMEM((1, LRU_W), F32)],
        compiler_params=_cparams(("parallel", "arbitrary")),
    )(proj, proj, proj, conv_w, conv_b.reshape(1, LRU_W), _bf(w_gate),
      b_gate.reshape(LRU_BLOCKS, 1, 2 * LRU_BW), lam.reshape(1, LRU_W))


def _rope16(x, tab):
    cos, s1, s2 = tab[:, :128], tab[:, 128:256], tab[:, 256:]
    cols = []
    for cidx in range(x.shape[1] // 128):
        xc = x[:, cidx * 128:(cidx + 1) * 128]
        cols.append(xc * cos + pltpu.roll(xc, 112, 1) * s1 + pltpu.roll(xc, 16, 1) * s2)
    return cols[0] if len(cols) == 1 else jnp.concatenate(cols, axis=1)


def _swa_kernel(sink_ref, q_ref, kp_ref, kc_ref, kn_ref, vp_ref, vc_ref, vn_ref, kx_ref, vx_ref,
                tp_ref, tc_ref, tn_ref, o_ref):
    qb = pl.program_id(1)
    nb = pl.num_programs(1)
    blk = SWA_BLOCK
    n_loc = 3 * blk
    n_keys = n_loc + kx_ref.shape[0]
    tab_c = tc_ref[...]
    q = _rope16(q_ref[...], tab_c) * (SWA_HD ** -0.5)
    k_all = jnp.concatenate([_rope16(kp_ref[...], tp_ref[...]), _rope16(kc_ref[...], tab_c),
                             _rope16(kn_ref[...], tn_ref[...]), kx_ref[...]], axis=0)
    v_all = jnp.concatenate([vp_ref[...], vc_ref[...], vn_ref[...], vx_ref[...]], axis=0)
    assert SWA_WINDOW >= blk - 1
    qi = lax.broadcasted_iota(jnp.int32, (blk, blk), 0)
    kj = lax.broadcasted_iota(jnp.int32, (blk, blk), 1)
    valid_prev = jnp.logical_and(kj - blk - qi >= -SWA_WINDOW, qb > 0)
    valid_next = jnp.logical_and(kj + blk - qi <= SWA_WINDOW, qb < nb - 1)
    lane = lax.broadcasted_iota(jnp.int32, (n_keys, 128), 1)
    low = lane < SWA_HD
    out_low = lax.broadcasted_iota(jnp.int32, (blk, 128), 1) < SWA_HD

    def expo(scores, sink):
        sc = jnp.concatenate([jnp.where(valid_prev, scores[:, :blk], -jnp.inf), scores[:, blk:2 * blk],
                              jnp.where(valid_next, scores[:, 2 * blk:n_loc], -jnp.inf), scores[:, n_loc:]], axis=1)
        mx = jnp.maximum(jnp.max(sc, axis=-1, keepdims=True), sink)
        return _bf(jnp.exp(sc - mx)), jnp.exp(sink - mx)

    group = SWA_QH // SWA_KVH
    for g in range(SWA_KVH):
        col = slice((g // 2) * 128, (g // 2 + 1) * 128)
        native_low = g % 2 == 0
        sel = low if native_low else jnp.logical_not(low)
        k_nat = jnp.where(sel, k_all[:, col], 0.0)
        v_nat = jnp.where(sel, v_all[:, col], 0.0)
        k_oth = pltpu.roll(k_nat, SWA_HD, 1)
        v_oth = pltpu.roll(v_nat, SWA_HD, 1)
        k_lo, k_hi = (k_nat, k_oth) if native_low else (k_oth, k_nat)
        v_lo, v_hi = (v_nat, v_oth) if native_low else (v_oth, v_nat)
        v_lo = jnp.where(lane == SWA_HD, 1.0, v_lo)
        v_hi = jnp.where(lane == 0, 1.0, v_hi)
        k_lo, k_hi, v_lo, v_hi = _bf(k_lo), _bf(k_hi), _bf(v_lo), _bf(v_hi)
        for cidx in range(group // 2):
            qc = (g * group) // 2 + cidx
            qv = _bf(q[:, qc * 128:(qc + 1) * 128])
            e_lo, sink_lo = expo(_dot_nt(qv, k_lo), sink_ref[2 * qc])
            e_hi, sink_hi = expo(_dot_nt(qv, k_hi), sink_ref[2 * qc + 1])
            a_lo = _dot(e_lo, v_lo)
            a_hi = _dot(e_hi, v_hi)
            inv_lo = 1.0 / (a_lo[:, SWA_HD:SWA_HD + 1] + sink_lo)
            inv_hi = 1.0 / (a_hi[:, 0:1] + sink_hi)
            o_ref[:, qc * 128:(qc + 1) * 128] = jnp.where(out_low, a_lo * inv_lo, a_hi * inv_hi)


def _swa_core(rw, proj, sink, tab):
    blk = SWA_BLOCK
    nb = rw.n_lat // blk
    kcol = SWA_QD // SWA_KVD
    vcol = kcol + 1
    ctx_blk = lambda b: rw.lat_rows // rw.n_ctx + b
    prev = lambda q: jnp.maximum(q - 1, 0)
    nxt = lambda q: jnp.minimum(q + 1, nb - 1)
    kv = lambda col, f: pl.BlockSpec((blk, SWA_KVD), lambda b, q: (b * nb + f(q), col))
    same = lambda q: q
    return pl.pallas_call(
        _swa_kernel,
        grid=(rw.batch, nb),
        in_specs=[pl.BlockSpec(memory_space=pltpu.SMEM),
                  pl.BlockSpec((blk, SWA_QD), lambda b, q: (b * nb + q, 0)),
                  kv(kcol, prev), kv(kcol, same), kv(kcol, nxt),
                  kv(vcol, prev), kv(vcol, same), kv(vcol, nxt),
                  pl.BlockSpec((rw.n_ctx, SWA_KVD), lambda b, q: (ctx_blk(b), kcol)),
                  pl.BlockSpec((rw.n_ctx, SWA_KVD), lambda b, q: (ctx_blk(b), vcol)),
                  pl.BlockSpec((blk, 384), lambda b, q: (prev(q), 0)),
                  pl.BlockSpec((blk, 384), lambda b, q: (q, 0)),
                  pl.BlockSpec((blk, 384), lambda b, q: (nxt(q), 0))],
        out_specs=pl.BlockSpec((blk, SWA_QD), lambda b, q: (b * nb + q, 0)),
        out_shape=jax.ShapeDtypeStruct((rw.lat_rows, SWA_QD), F32),
        compiler_params=_cparams(("parallel", "parallel")),
    )(sink, proj, proj, proj, proj, proj, proj, proj, proj, proj, tab, tab, tab)


def _axial_angles(n_lat, dh):
    t = jnp.arange(n_lat)
    row = (t // GRID_W).astype(F32)
    col = (t % GRID_W).astype(F32)
    nf = dh // 4
    inv = ROPE_THETA ** (-jnp.arange(nf, dtype=F32) / nf)
    return row[:, None] * inv, col[:, None] * inv


def _ret_tables(n_lat):
    ar, ac = _axial_angles(n_lat, RET_DK)
    cos = jnp.concatenate([jnp.cos(ar), jnp.cos(ar), jnp.cos(ac), jnp.cos(ac)], axis=1)
    sin = jnp.concatenate([-jnp.sin(ar), jnp.sin(ar), -jnp.sin(ac), jnp.sin(ac)], axis=1)
    return cos, sin


def _swa_table(n_lat):
    ar, ac = _axial_angles(n_lat, SWA_HD)
    z = jnp.zeros_like(ar)
    cos = jnp.concatenate([jnp.cos(ar), jnp.cos(ar), jnp.cos(ac), jnp.cos(ac)], axis=1)
    s1 = jnp.concatenate([-jnp.sin(ar), z, -jnp.sin(ac), z], axis=1)
    s2 = jnp.concatenate([z, jnp.sin(ar), z, jnp.sin(ac)], axis=1)
    return jnp.concatenate([jnp.tile(cos, (1, 2)), jnp.tile(s1, (1, 2)), jnp.tile(s2, (1, 2))], axis=1)


def _gdn_layer(rw, layer, x, mod3, gain, w_in, conv_w, a_log, dt_bias, norm_g, w_out):
    w_main = _bf(w_in[:, :GDN_MAIN])
    n_ab = w_in.shape[1] - GDN_MAIN
    w_ab = _bf(jnp.pad(w_in[:, GDN_MAIN:], ((0, 0), (0, 128 - n_ab))))
    proj, ab = _inproj(rw, x, gain, mod3, layer, w_main, w_ab)
    hv = GDN_V_HEADS
    alog_row = jnp.zeros((1, 128), F32).at[0, 0:hv].set(a_log[0]).at[0, 2 * hv:3 * hv].set(a_log[1])
    dtb_row = jnp.zeros((1, 128), F32).at[0, 0:hv].set(dt_bias[0]).at[0, 2 * hv:3 * hv].set(dt_bias[1])
    cb = 2048
    row_spec = pl.BlockSpec((TM_CONV, 128), lambda i, j: (i, 0))
    vec_spec = pl.BlockSpec((1, 128), lambda i, j: (0, 0))
    qkv, g_all, b_all = pl.pallas_call(
        functools.partial(_gdn_conv_kernel, rw=rw),
        grid=(rw.rows // TM_CONV, GDN_CONV // cb),
        in_specs=_conv_specs(rw, cb) + [row_spec, vec_spec, vec_spec],
        out_specs=[pl.BlockSpec((TM_CONV, cb), lambda i, j: (i, j)), row_spec, row_spec],
        out_shape=[jax.ShapeDtypeStruct((rw.rows, GDN_CONV), F32),
                   jax.ShapeDtypeStruct((rw.rows, 128), F32),
                   jax.ShapeDtypeStruct((rw.rows, 128), F32)],
        scratch_shapes=[pltpu.VMEM((TM_CONV + 16, cb), F32)],
        compiler_params=_cparams(("parallel", "arbitrary")),
    )(proj, proj, proj, conv_w, ab, alog_row, dtb_row)
    g_dir = jnp.stack([g_all[:, 0:hv], g_all[:, 2 * hv:3 * hv]])
    b_dir = jnp.stack([b_all[:, hv:2 * hv], b_all[:, 3 * hv:4 * hv]])
    o = _gdn_core(rw, qkv, g_dir, b_dir)
    return _two_dir_out(_gdn_out_kernel, rw, layer, x, mod3, o, proj, GDN_CONV // GDN_VAL, norm_g, _bf(w_out))


def _ret_layer(rw, layer, x, mod3, gain, w_in, decay_logit, norm_g, w_out):
    proj = _inproj(rw, x, gain, mod3, layer, _bf(w_in))
    cos_tab, sin_tab = _ret_tables(rw.n_lat)
    o = _ret_core(rw, proj, cos_tab, sin_tab, decay_logit)
    return _two_dir_out(_ret_out_kernel, rw, layer, x, mod3, o, proj, (2 * RET_KEY + RET_VAL) // RET_VAL,
                        norm_g, _bf(w_out))


def _lru_layer(rw, layer, x, mod3, gain, w_in, conv_w, conv_b, w_gate, b_gate, lam, w_out):
    proj = _inproj(rw, x, gain, mod3, layer, _bf(w_in))
    h0 = _lru_scan(rw, proj, conv_w, conv_b, w_gate[0], b_gate[0], lam[0], rev=False)
    h1 = _lru_scan(rw, proj, conv_w, conv_b, w_gate[1], b_gate[1], lam[1], rev=True)
    tm = TM_OUT
    specs = [pl.BlockSpec((tm, LRU_W), lambda i: (i, 0)),
             pl.BlockSpec((tm, LRU_W), lambda i: (i, 0)),
             pl.BlockSpec((tm, LRU_W), lambda i: (i, 1))]
    return _outproj_call(_lru_out_kernel, rw, rw.rows, layer, x, mod3, specs, [h0, h1, proj], _bf(w_out), [])


def _swa_layer(rw, layer, x, mod3, gain, w_in, sink, w_out):
    proj = _inproj(rw, x, gain, mod3, layer, _bf(w_in))
    o = _swa_core(rw, proj, sink, _swa_table(rw.n_lat))
    specs = [pl.BlockSpec((TM_OUT, SWA_QD), lambda i: (i, 0))]
    return _outproj_call(_swa_out_kernel, rw, rw.lat_rows, layer, x, mod3, specs, [o], _bf(w_out), [])


def kernel(x, c, ctx, c_ctx, norm_mix_g, norm_ffn_g, w_mod, b_mod, w_ff1, w_ff2, norm_out_g, gdn_w_in, gdn_conv_w, gdn_a_log, gdn_dt_bias, gdn_norm_g, gdn_w_out, ret_w_in, ret_decay_logit, ret_norm_g, ret_w_out, lru_w_in, lru_conv_w, lru_conv_b, lru_w_gate, lru_b_gate, lru_lambda, lru_w_out, swa_w_in, swa_sink, swa_w_out):
    batch, n_lat, _ = x.shape
    n_ctx = ctx.shape[1]
    assert batch + 1 <= MOD_ROWS and n_lat % TM_IN == 0 and (batch * n_ctx) % TM_IN == 0
    assert w_mod.shape[0] == DEPTH and gdn_w_in.shape[0] == 1 and swa_w_in.shape[0] == 1
    rw = _Rows(batch, n_lat, n_ctx)
    xf = jnp.concatenate([x.reshape(rw.lat_rows, D), ctx.reshape(batch * n_ctx, D)], axis=0)
    cc = jnp.zeros((MOD_ROWS, D), F32).at[:batch].set(c).at[batch].set(c_ctx)
    mod3 = _adaln(cc, w_mod, b_mod)

    def mlp(layer, xin, n_rows, final):
        return _mlp(rw, xin, n_rows, norm_ffn_g[layer], mod3, layer, _bf(w_ff1[layer]), _bf(w_ff2[layer]),
                    norm_out_g, final)

    xf = _gdn_layer(rw, 0, xf, mod3, norm_mix_g[0], gdn_w_in[0], gdn_conv_w[0], gdn_a_log[0], gdn_dt_bias[0],
                    gdn_norm_g[0], gdn_w_out[0])
    xf = mlp(0, xf, rw.rows, False)
    xf = _ret_layer(rw, 1, xf, mod3, norm_mix_g[1], ret_w_in[0], ret_decay_logit[0], ret_norm_g[0], ret_w_out[0])
    xf = mlp(1, xf, rw.rows, False)
    xf = _lru_layer(rw, 2, xf, mod3, norm_mix_g[2], lru_w_in[0], lru_conv_w[0], lru_conv_b[0], lru_w_gate[0],
                    lru_b_gate[0], lru_lambda[0], lru_w_out[0])
    xf = mlp(2, xf, rw.rows, False)
    xl = _swa_layer(rw, 3, xf, mod3, norm_mix_g[3], swa_w_in[0], swa_sink[0], swa_w_out[0])
    out = mlp(3, xl, rw.lat_rows, True)
    return out.reshape(batch, n_lat, D)
```

```python
import functools
import math

import jax
import jax.numpy as jnp
from jax import lax
from jax.experimental import pallas as pl
from jax.experimental.pallas import tpu as pltpu

F32 = jnp.float32
BF16 = jnp.bfloat16

D = 1024
D_FF = 4 * D
EPS = 1e-6
DEPTH = 4
ROPE_THETA = 10000.0
GRID_W = 64
CHUNK = 64
MOD_ROWS = 16

GDN_QK_HEADS = 8
GDN_V_HEADS = 16
GDN_HD = 128
GDN_KEY = GDN_QK_HEADS * GDN_HD
GDN_VAL = GDN_V_HEADS * GDN_HD
GDN_CONV = 2 * GDN_KEY + GDN_VAL
GDN_MAIN = GDN_CONV + GDN_VAL

RET_HEADS = 4
RET_DK = 256
RET_DV = 512
RET_KEY = RET_HEADS * RET_DK
RET_VAL = RET_HEADS * RET_DV

LRU_W = 1280
LRU_BLOCKS = 10
LRU_BW = LRU_W // LRU_BLOCKS
LRU_C = 8.0

SWA_QH = 16
SWA_KVH = 4
SWA_HD = 64
SWA_BLOCK = 128
SWA_WINDOW = 128
SWA_QD = SWA_QH * SWA_HD
SWA_KVD = SWA_KVH * SWA_HD

TM_IN = 1024
TM_OUT = 512
TM_CONV = 256
TN_IN_MAX = 1536
FF_CHUNK = 1024
RET_CHUNK = 256
GDN_BLOCK = 4 * CHUNK
VMEM_LIMIT = 48 * 1024 * 1024
MLP_VMEM_LIMIT = 56 * 1024 * 1024


def _dot(a, b):
    return jnp.dot(a, b, preferred_element_type=F32)


def _dot_nt(a, b):
    return lax.dot_general(a, b, (((1,), (1,)), ((), ())), preferred_element_type=F32)


def _dot_tn(a, b):
    return lax.dot_general(a, b, (((0,), (0,)), ((), ())), preferred_element_type=F32)


def _bf(x):
    return x.astype(BF16)


def _split3(x):
    hi = _bf(x)
    r = x - hi.astype(F32)
    mid = _bf(r)
    lo = _bf(r - mid.astype(F32))
    return hi, mid, lo


def _silu(x):
    return x * jax.nn.sigmoid(x)


def _softplus(x):
    return jnp.maximum(x, 0.0) + jnp.log1p(jnp.exp(-jnp.abs(x)))


def _gelu_tanh(x):
    cdf = 0.5 * (1.0 + jnp.tanh(math.sqrt(2.0 / math.pi) * (x + 0.044715 * (x * x * x))))
    return x * cdf


def _cparams(sem, vmem=None):
    return pltpu.CompilerParams(dimension_semantics=sem, vmem_limit_bytes=VMEM_LIMIT if vmem is None else vmem)


class _Rows:
    def __init__(self, batch, n_lat, n_ctx):
        self.batch, self.n_lat, self.n_ctx = batch, n_lat, n_ctx
        self.lat_rows = batch * n_lat
        self.rows = self.lat_rows + batch * n_ctx
        self.n_steps = (n_lat + n_ctx) // CHUNK
        self.ctx_steps = n_ctx // CHUNK
        self.lat_steps = n_lat // CHUNK

    def mod_row(self, i, tm):
        return jnp.where(i < self.lat_rows // tm, i // (self.n_lat // tm), self.batch)

    def chunk_block(self, b, d, s, blk):
        nc, nl = self.n_ctx // blk, self.n_lat // blk
        c_ctx = jnp.where(d == 0, s, nc - 1 - s)
        c_lat = jnp.where(d == 0, s - nc, nl - 1 - (s - nc))
        return jnp.where(s < nc, self.lat_rows // blk + b * nc + c_ctx, b * nl + c_lat)

    def lat_chunk(self, d, s, blk):
        nc, nl = self.n_ctx // blk, self.n_lat // blk
        return jnp.where(s < nc, 0, jnp.where(d == 0, s - nc, nl - 1 - (s - nc)))


def _adaln_kernel(c_ref, w_ref, b_ref, o_ref):
    s = _silu(c_ref[...])
    w = w_ref[0]
    s_hi = _bf(s)
    s_lo = _bf(s - s_hi.astype(F32))
    w_hi = _bf(w)
    w_lo = _bf(w - w_hi.astype(F32))
    y = _dot(s_hi, w_hi) + (_dot(s_lo, w_hi) + _dot(s_hi, w_lo))
    o_ref[0] = y + b_ref[0]


def _adaln(cc, w_mod, b_mod):
    depth, _, n = w_mod.shape
    tn = 1024
    out = pl.pallas_call(
        _adaln_kernel,
        grid=(depth, n // tn),
        in_specs=[pl.BlockSpec((MOD_ROWS, D), lambda l, j: (0, 0)),
                  pl.BlockSpec((1, D, tn), lambda l, j: (l, 0, j)),
                  pl.BlockSpec((1, 1, tn), lambda l, j: (l, 0, j))],
        out_specs=pl.BlockSpec((1, MOD_ROWS, tn), lambda l, j: (l, 0, j)),
        out_shape=jax.ShapeDtypeStruct((depth, MOD_ROWS, n), F32),
        compiler_params=_cparams(("parallel", "parallel")),
    )(cc, w_mod, b_mod.reshape(depth, 1, n))
    return out.reshape(depth * MOD_ROWS * 6, 1, D)


def _mod_spec(rw, layer, which, tm):
    base = layer * MOD_ROWS * 6
    return pl.BlockSpec((1, 1, D), lambda i, *_: (base + rw.mod_row(i, tm) * 6 + which, 0, 0))


def _norm_mod(x, gain, shift, scale):
    y = x * lax.rsqrt(jnp.mean(x * x, axis=-1, keepdims=True) + EPS) * gain
    return y * (1.0 + scale) + shift


def _inproj_kernel(x_ref, g_ref, sh_ref, sc_ref, w_ref, *rest, has_extra):
    if has_extra:
        w2_ref, o_ref, o2_ref, h_ref = rest
    else:
        o_ref, h_ref = rest

    @pl.when(pl.program_id(1) == 0)
    def _():
        h = _bf(_norm_mod(x_ref[...], g_ref[...], sh_ref[0], sc_ref[0]))
        h_ref[...] = h
        if has_extra:
            o2_ref[...] = _dot(h, _bf(w2_ref[...]))

    o_ref[...] = _dot(h_ref[...], _bf(w_ref[...]))


def _inproj(rw, x, gain, mod3, layer, w, w_extra=None, n_cols=None):
    n = w.shape[1] if n_cols is None else n_cols
    tm = TM_IN
    tn = max(t for t in range(256, TN_IN_MAX + 1, 256) if n % t == 0)
    has_extra = w_extra is not None
    in_specs = [pl.BlockSpec((tm, D), lambda i, j: (i, 0)),
                pl.BlockSpec((1, D), lambda i, j: (0, 0)),
                _mod_spec(rw, layer, 0, tm),
                _mod_spec(rw, layer, 1, tm),
                pl.BlockSpec((D, tn), lambda i, j: (0, j))]
    out_specs = [pl.BlockSpec((tm, tn), lambda i, j: (i, j))]
    out_shape = [jax.ShapeDtypeStruct((rw.rows, n), F32)]
    args = [x, gain.reshape(1, D), mod3, mod3, w]
    if has_extra:
        ne = w_extra.shape[1]
        in_specs.append(pl.BlockSpec((D, ne), lambda i, j: (0, 0)))
        out_specs.append(pl.BlockSpec((tm, ne), lambda i, j: (i, 0)))
        out_shape.append(jax.ShapeDtypeStruct((rw.rows, ne), F32))
        args.append(w_extra)
    res = pl.pallas_call(
        functools.partial(_inproj_kernel, has_extra=has_extra),
        grid=(rw.rows // tm, n // tn),
        in_specs=in_specs, out_specs=out_specs, out_shape=out_shape,
        scratch_shapes=[pltpu.VMEM((tm, D), BF16)],
        compiler_params=_cparams(("parallel", "arbitrary")),
    )(*args)
    return res if has_extra else res[0]


def _mlp_kernel(x_ref, g_ref, sh_ref, sc_ref, gt_ref, w1_ref, w2_ref, go_ref, o_ref, h_ref, acc_ref,
                *, final_norm):
    k = pl.program_id(1)

    @pl.when(k == 0)
    def _():
        h_ref[...] = _bf(_norm_mod(x_ref[...], g_ref[...], sh_ref[0], sc_ref[0]))
        acc_ref[...] = jnp.zeros_like(acc_ref)

    a = jnp.square(jnp.maximum(_dot(h_ref[...], _bf(w1_ref[...])), 0.0))
    acc_ref[...] += _dot(_bf(a), _bf(w2_ref[...]))

    @pl.when(k == pl.num_programs(1) - 1)
    def _():
        y = x_ref[...] + gt_ref[0] * acc_ref[...]
        if final_norm:
            y = y * lax.rsqrt(jnp.mean(y * y, axis=-1, keepdims=True) + EPS) * go_ref[...]
        o_ref[...] = y


def _mlp(rw, x, n_rows, gain, mod3, layer, w1, w2, out_gain, final_norm):
    tm, ck = TM_IN, FF_CHUNK
    return pl.pallas_call(
        functools.partial(_mlp_kernel, final_norm=final_norm),
        grid=(n_rows // tm, D_FF // ck),
        in_specs=[pl.BlockSpec((tm, D), lambda i, k: (i, 0)),
                  pl.BlockSpec((1, D), lambda i, k: (0, 0)),
                  _mod_spec(rw, layer, 3, tm),
                  _mod_spec(rw, layer, 4, tm),
                  _mod_spec(rw, layer, 5, tm),
                  pl.BlockSpec((None, D, ck), lambda i, k: (layer, 0, k)),
                  pl.BlockSpec((None, ck, D), lambda i, k: (layer, k, 0)),
                  pl.BlockSpec((1, D), lambda i, k: (0, 0))],
        out_specs=pl.BlockSpec((tm, D), lambda i, k: (i, 0)),
        out_shape=jax.ShapeDtypeStruct((n_rows, D), F32),
        scratch_shapes=[pltpu.VMEM((tm, D), BF16), pltpu.VMEM((tm, D), F32)],
        compiler_params=_cparams(("parallel", "arbitrary"), MLP_VMEM_LIMIT),
    )(x, gain.reshape(1, D), mod3, mod3, mod3, w1, w2, out_gain.reshape(1, D))


def _gdn_out_kernel(x_ref, gt_ref, o0_ref, o1_ref, z_ref, ng_ref, w_ref, out_ref, a_ref):
    ng = ng_ref[...]
    for h in range(GDN_V_HEADS):
        sl = slice(h * GDN_HD, (h + 1) * GDN_HD)
        o = o0_ref[0, :, sl].astype(F32) + o1_ref[0, :, sl].astype(F32)
        y = o * lax.rsqrt(jnp.mean(o * o, axis=-1, keepdims=True) + EPS) * ng
        a_ref[:, sl] = _bf(y * _silu(z_ref[:, sl]))
    out_ref[...] = x_ref[...] + gt_ref[0] * _dot(a_ref[...], w_ref[...])


def _ret_out_kernel(x_ref, gt_ref, o0_ref, o1_ref, z_ref, ng_ref, w_ref, out_ref, a_ref):
    for h in range(RET_HEADS):
        sl = slice(h * RET_DV, (h + 1) * RET_DV)
        o = o0_ref[0, :, sl].astype(F32) + o1_ref[0, :, sl].astype(F32)
        mu = jnp.mean(o, axis=-1, keepdims=True)
        oc = o - mu
        var = jnp.mean(oc * oc, axis=-1, keepdims=True)
        y = oc * lax.rsqrt(var + EPS) * ng_ref[:, sl]
        a_ref[:, sl] = _bf(y * _silu(z_ref[:, sl]))
    out_ref[...] = x_ref[...] + gt_ref[0] * _dot(a_ref[...], w_ref[...])


def _lru_out_kernel(x_ref, gt_ref, h0_ref, h1_ref, gb_ref, w_ref, out_ref):
    a = (h0_ref[...] + h1_ref[...]) * _gelu_tanh(gb_ref[...])
    out_ref[...] = x_ref[...] + gt_ref[0] * _dot(_bf(a), w_ref[...])


def _swa_out_kernel(x_ref, gt_ref, o_ref, w_ref, out_ref):
    out_ref[...] = x_ref[...] + gt_ref[0] * _dot(_bf(o_ref[...]), w_ref[...])


def _outproj_call(kern, rw, n_rows, layer, x, mod3, extra_specs, extra_args, w, scratch):
    tm = TM_OUT
    k = w.shape[0]
    return pl.pallas_call(
        kern,
        grid=(n_rows // tm,),
        in_specs=[pl.BlockSpec((tm, D), lambda i: (i, 0)), _mod_spec(rw, layer, 2, tm)] + extra_specs
                 + [pl.BlockSpec((k, D), lambda i: (0, 0))],
        out_specs=pl.BlockSpec((tm, D), lambda i: (i, 0)),
        out_shape=jax.ShapeDtypeStruct((n_rows, D), F32),
        scratch_shapes=scratch,
        compiler_params=_cparams(("parallel",)),
    )(x, mod3, *extra_args, w)


def _two_dir_out(kern, rw, layer, x, mod3, o, proj, z_block, norm_g, w):
    tm, k = TM_OUT, w.shape[0]
    ng = norm_g.reshape(1, -1)
    specs = [pl.BlockSpec((1, tm, k), lambda i: (0, i, 0)),
             pl.BlockSpec((1, tm, k), lambda i: (1, i, 0)),
             pl.BlockSpec((tm, k), lambda i: (i, z_block)),
             pl.BlockSpec(ng.shape, lambda i: (0, 0))]
    return _outproj_call(kern, rw, rw.rows, layer, x, mod3, specs, [o, o, proj, ng], w,
                         [pltpu.VMEM((tm, k), BF16)])


def _conv_core(x_ref, p_ref, n_ref, w_ref, first, last, pad_ref=None):
    t = x_ref.shape[0]
    x = x_ref[...]
    prev = p_ref[...] * jnp.where(first, 0.0, 1.0)
    nxt = n_ref[...] * jnp.where(last, 0.0, 1.0)
    if pad_ref is not None:
        pad_ref[0:8] = prev
        pad_ref[8:t + 8] = x
        pad_ref[t + 8:t + 16] = nxt
        xm2, xm1, xp1 = pad_ref[6:t + 6], pad_ref[7:t + 7], pad_ref[9:t + 9]
    else:
        row8 = lax.broadcasted_iota(jnp.int32, prev.shape, 0)
        r2 = pltpu.roll(x, 2, 0)
        r1 = pltpu.roll(x, 1, 0)
        rp = pltpu.roll(x, t - 1, 0)
        f2 = jnp.where(row8 < 2, pltpu.roll(prev, 2, 0), r2[:8])
        f1 = jnp.where(row8 < 1, pltpu.roll(prev, 1, 0), r1[:8])
        l1 = jnp.where(row8 == 7, pltpu.roll(nxt, 7, 0), rp[t - 8:])
        xm2 = jnp.concatenate([f2, r2[8:]], axis=0)
        xm1 = jnp.concatenate([f1, r1[8:]], axis=0)
        xp1 = jnp.concatenate([rp[:t - 8], l1], axis=0)
    w = w_ref[...]
    return w[0:1] * xm2 + w[1:2] * xm1 + w[2:3] * x + w[3:4] * xp1


def _seg_flags(rw, i):
    per = rw.n_lat // TM_CONV
    is_ctx = i >= rw.lat_rows // TM_CONV
    assert rw.n_ctx == TM_CONV
    first = jnp.logical_or(is_ctx, i % per == 0)
    last = jnp.logical_or(is_ctx, i % per == per - 1)
    return first, last


def _gdn_conv_kernel(x_ref, p_ref, n_ref, w_ref, ab_ref, alog_ref, dtb_ref, o_ref, g_ref, b_ref, pad_ref, *, rw):
    i, j = pl.program_id(0), pl.program_id(1)
    first, last = _seg_flags(rw, i)
    s = _silu(_conv_core(x_ref, p_ref, n_ref, w_ref, first, last, pad_ref))
    width = s.shape[1]
    qk_blocks = 2 * GDN_KEY // width

    @pl.when(j < qk_blocks)
    def _():
        for h in range(width // GDN_HD):
            sl = slice(h * GDN_HD, (h + 1) * GDN_HD)
            sh = s[:, sl]
            is_q = j * width + h * GDN_HD < GDN_KEY
            scale = jnp.where(is_q, GDN_HD ** -0.5, 1.0)
            o_ref[:, sl] = sh * (lax.rsqrt(jnp.sum(sh * sh, axis=-1, keepdims=True) + EPS) * scale)

    @pl.when(j >= qk_blocks)
    def _():
        o_ref[...] = s

    @pl.when(j == 0)
    def _():
        ab = ab_ref[...]
        g_ref[...] = -jnp.exp(alog_ref[...]) * _softplus(ab + dtb_ref[...])
        b_ref[...] = jax.nn.sigmoid(ab)


def _conv_specs(rw, cb):
    t8 = TM_CONV // 8
    n8 = rw.rows // 8
    return [pl.BlockSpec((TM_CONV, cb), lambda i, j: (i, j)),
            pl.BlockSpec((8, cb), lambda i, j: (jnp.maximum(i * t8 - 1, 0), j)),
            pl.BlockSpec((8, cb), lambda i, j: (jnp.minimum((i + 1) * t8, n8 - 1), j)),
            pl.BlockSpec((4, cb), lambda i, j: (0, j))]


def _dir_masks(d, n):
    ii = lax.broadcasted_iota(jnp.int32, (n, n), 0)
    jj = lax.broadcasted_iota(jnp.int32, (n, n), 1)
    t = (ii - jj) * jnp.where(d == 0, 1, -1)
    return t, t >= 0, t > 0


def _gdn_core_kernel(qkv_ref, g_ref, b_ref, o_ref, s_ref):
    d, s = pl.program_id(1), pl.program_id(2)

    @pl.when(s == 0)
    def _():
        s_ref[...] = jnp.zeros_like(s_ref)

    c, hd = CHUNK, GDN_HD
    rep = GDN_V_HEADS // GDN_QK_HEADS
    heads = range(GDN_V_HEADS)
    n_sub = qkv_ref.shape[0] // c
    subs = range(n_sub)
    offs = [pl.multiple_of(jnp.where(d == 0, i, n_sub - 1 - i) * c, c) for i in subs]
    chains = [(i, h) for i in subs for h in heads]
    t, incl, strict = _dir_masks(d, c)
    cum = _bf(jnp.where(incl, 1.0, 0.0))
    cum_t = _bf(jnp.where(t <= 0, 1.0, 0.0))
    g = [g_ref[0, pl.ds(offs[i], c), :] for i in subs]
    beta = [b_ref[0, pl.ds(offs[i], c), :] for i in subs]
    parts = [_split3(g[i]) for i in subs]
    gc = [sum(_dot(cum, p) for p in parts[i]) for i in subs]
    gr = [sum(_dot_tn(p, cum_t) for p in parts[i]) for i in subs]
    gtot = [jnp.sum(g[i], axis=0, keepdims=True) for i in subs]

    def rows(i, lo):
        return qkv_ref[pl.ds(offs[i], c), lo:lo + hd]

    q_l = [[rows(i, hq * hd) for hq in range(GDN_QK_HEADS)] for i in subs]
    k_l = [[rows(i, GDN_KEY + hq * hd) for hq in range(GDN_QK_HEADS)] for i in subs]
    gram = [[_dot_nt(_bf(jnp.concatenate([k, q], axis=0)), _bf(k)) for q, k in zip(q_l[i], k_l[i])]
            for i in subs]
    gcol = [gc[i][:, h:h + 1] for i, h in chains]
    bcol = [beta[i][:, h:h + 1] for i, h in chains]
    dec = [jnp.where(incl, jnp.exp(jnp.where(incl, gcol[n] - gr[i][h:h + 1, :], 0.0)), 0.0)
           for n, (i, h) in enumerate(chains)]
    eg = [jnp.exp(x) for x in gcol]
    m = [jnp.where(strict, -(gram[i][h // rep][:c] * dec[n] * bcol[n]), 0.0) for n, (i, h) in enumerate(chains)]
    rhs = [jnp.concatenate([k_l[i][h // rep] * (bcol[n] * eg[n]), rows(i, 2 * GDN_KEY + h * hd) * bcol[n]], axis=1)
           for n, (i, h) in enumerate(chains)]
    ids = range(len(chains))

    m_hi = [_bf(x) for x in m]
    zeros_cc = jnp.zeros((c, 2 * c), BF16)
    lane_cc = lax.broadcasted_iota(jnp.int32, (c, 2 * c), 1)
    row_cc = lax.broadcasted_iota(jnp.int32, (c, 2 * c), 0)
    eye_hi = jnp.where(lane_cc == row_cc + c, 1.0, 0.0)
    is_hi = lane_cc >= c
    cat = [jnp.concatenate([m[n], jnp.zeros((c, c), F32)], axis=1) + eye_hi for n in ids]
    for _ in range(6):
        cb = [_bf(x) for x in cat]
        r = [_dot(cb[n], jnp.concatenate([cb[n], zeros_cc], axis=0)) for n in ids]
        cat = [r[n] + jnp.where(is_hi, cat[n], 0.0) for n in ids]
    nb = [_bf(cat[n] - eye_hi) for n in ids]
    zeros_rhs = jnp.zeros((c, 2 * hd), BF16)

    def apply_n(y):
        return [_dot(nb[n], jnp.concatenate([zeros_rhs, _bf(y[n])], axis=0)) for n in ids]

    nr = apply_n(rhs)
    x = [rhs[n] + nr[n] for n in ids]
    resid = [(rhs[n] - x[n]) + _dot(m_hi[n], _bf(x[n])) for n in ids]
    nr = apply_n(resid)
    x = [x[n] + (resid[n] + nr[n]) for n in ids]
    qk = [_bf(gram[i][h // rep][c:] * dec[n]) for n, (i, h) in enumerate(chains)]
    gt = [gtot[i][:, h:h + 1] for i, h in chains]
    kd = [_bf(k_l[i][h // rep] * jnp.exp(gt[n] - gcol[n])) for n, (i, h) in enumerate(chains)]
    wq = [_bf(jnp.concatenate([x[n][:, :hd], q_l[i][h // rep] * eg[n]], axis=0)) for n, (i, h) in enumerate(chains)]
    decay = [jnp.exp(x) for x in gt]

    st = [s_ref[h] for h in heads]
    for i in subs:
        base = i * GDN_V_HEADS
        r = [_dot(wq[base + h], _bf(st[h])) for h in heads]
        unb = [_bf(x[base + h][:, hd:] - r[h][:c]) for h in heads]
        for h in heads:
            o_ref[0, pl.ds(offs[i], c), h * hd:(h + 1) * hd] = _bf(r[h][c:] + _dot(qk[base + h], unb[h]))
        st = [st[h] * decay[base + h] + _dot_tn(kd[base + h], unb[h]) for h in heads]
    for h in heads:
        s_ref[h] = st[h]


def _gdn_core(rw, qkv, g_dir, b_dir):
    blk = GDN_BLOCK
    rb = lambda b, d, s: rw.chunk_block(b, d, s, blk)
    return pl.pallas_call(
        _gdn_core_kernel,
        grid=(rw.batch, 2, (rw.n_lat + rw.n_ctx) // blk),
        in_specs=[pl.BlockSpec((blk, GDN_CONV), lambda b, d, s: (rb(b, d, s), 0)),
                  pl.BlockSpec((1, blk, GDN_V_HEADS), lambda b, d, s: (d, rb(b, d, s), 0)),
                  pl.BlockSpec((1, blk, GDN_V_HEADS), lambda b, d, s: (d, rb(b, d, s), 0))],
        out_specs=pl.BlockSpec((1, blk, GDN_VAL), lambda b, d, s: (d, rb(b, d, s), 0)),
        out_shape=jax.ShapeDtypeStruct((2, rw.rows, GDN_VAL), BF16),
        scratch_shapes=[pltpu.VMEM((GDN_V_HEADS, GDN_HD, GDN_HD), F32)],
        compiler_params=_cparams(("parallel", "parallel", "arbitrary")),
    )(qkv, g_dir, b_dir)


def _rope_half(x, cos, sin_signed):
    return x * cos + pltpu.roll(x, x.shape[1] // 2, 1) * sin_signed


def _ret_core_kernel(q_ref, k_ref, v_ref, cos_ref, sin_ref, dl_ref, o_ref, s_ref, *, ctx_steps):
    d, s = pl.program_id(1), pl.program_id(2)

    @pl.when(s == 0)
    def _():
        s_ref[...] = jnp.zeros_like(s_ref)

    c = q_ref.shape[0]
    heads = range(RET_HEADS)
    is_lat = s >= ctx_steps
    cos = jnp.where(is_lat, cos_ref[...], 1.0)
    sin = jnp.where(is_lat, sin_ref[...], 0.0)
    t, incl, _ = _dir_masks(d, c)
    tf = t.astype(F32)
    pos = lax.broadcasted_iota(jnp.int32, (c, 1), 0)
    ridx = jnp.where(d == 0, pos + 1, c - pos).astype(F32)
    log_gamma = -_softplus(-dl_ref[0])

    def rope(x):
        hw = RET_DK // 2
        return jnp.concatenate([_rope_half(x[:, :hw], cos[:, :hw], sin[:, :hw]),
                                _rope_half(x[:, hw:], cos[:, hw:], sin[:, hw:])], axis=1)

    lg = [log_gamma[:, h:h + 1] for h in heads]
    q = [rope(q_ref[:, h * RET_DK:(h + 1) * RET_DK]) * (RET_DK ** -0.5) for h in heads]
    k = [rope(k_ref[:, h * RET_DK:(h + 1) * RET_DK]) for h in heads]
    vb = [_bf(v_ref[:, h * RET_DV:(h + 1) * RET_DV]) for h in heads]
    gcum = [lg[h] * ridx for h in heads]
    dec = [jnp.where(incl, jnp.exp(jnp.where(incl, lg[h] * tf, 0.0)), 0.0) for h in heads]
    qk = [_bf(_dot_nt(_bf(q[h]), _bf(k[h])) * dec[h]) for h in heads]
    st = [s_ref[h] for h in heads]
    for h in heads:
        o_ref[0, :, h * RET_DV:(h + 1) * RET_DV] = _bf(
            _dot(_bf(q[h] * jnp.exp(gcum[h])), _bf(st[h])) + _dot(qk[h], vb[h]))
    gtot = [lg[h] * float(c) for h in heads]
    kd = [_bf(k[h] * jnp.exp(gtot[h] - gcum[h])) for h in heads]
    for h in heads:
        s_ref[h] = st[h] * jnp.exp(gtot[h]) + _dot_tn(kd[h], vb[h])


def _ret_core(rw, proj, cos_tab, sin_tab, decay_logit):
    c = RET_CHUNK
    rb = lambda b, d, s: rw.chunk_block(b, d, s, c)
    dl = jnp.zeros((2, 1, 128), F32).at[:, 0, :RET_HEADS].set(decay_logit)
    return pl.pallas_call(
        functools.partial(_ret_core_kernel, ctx_steps=rw.n_ctx // c),
        grid=(rw.batch, 2, (rw.n_lat + rw.n_ctx) // c),
        in_specs=[pl.BlockSpec((c, RET_KEY), lambda b, d, s: (rb(b, d, s), 0)),
                  pl.BlockSpec((c, RET_KEY), lambda b, d, s: (rb(b, d, s), 1)),
                  pl.BlockSpec((c, RET_VAL), lambda b, d, s: (rb(b, d, s), 1)),
                  pl.BlockSpec((c, RET_DK), lambda b, d, s: (rw.lat_chunk(d, s, c), 0)),
                  pl.BlockSpec((c, RET_DK), lambda b, d, s: (rw.lat_chunk(d, s, c), 0)),
                  pl.BlockSpec((1, 1, 128), lambda b, d, s: (d, 0, 0))],
        out_specs=pl.BlockSpec((1, c, RET_VAL), lambda b, d, s: (d, rb(b, d, s), 0)),
        out_shape=jax.ShapeDtypeStruct((2, rw.rows, RET_VAL), BF16),
        scratch_shapes=[pltpu.VMEM((RET_HEADS, RET_DK, RET_DV), F32)],
        compiler_params=_cparams(("parallel", "parallel", "arbitrary")),
    )(proj, proj, proj, cos_tab, sin_tab, dl)


def _lru_scan_kernel(x_ref, p_ref, n_ref, cw_ref, cb_ref, wg_ref, bg_ref, lam_ref, h_ref, carry_ref, *, rw, rev):
    @pl.when(pl.program_id(1) == 0)
    def _():
        carry_ref[...] = jnp.zeros_like(carry_ref)

    t = x_ref.shape[0]
    tile = rw.chunk_block(pl.program_id(0), 1 if rev else 0, pl.program_id(1), t)
    first, last = _seg_flags(rw, tile)
    xs = _conv_core(x_ref, p_ref, n_ref, cw_ref, first, last) + cb_ref[...]
    row8 = lax.broadcasted_iota(jnp.int32, (t // 8, 8, LRU_BW), 1)
    sp = _softplus(-lam_ref[...])
    for n in range(LRU_BLOCKS):
        sl = slice(n * LRU_BW, (n + 1) * LRU_BW)
        xn = xs[:, sl]
        gates = jax.nn.sigmoid(_dot(_bf(xn), wg_ref[n]) + bg_ref[n])
        log_a = -LRU_C * gates[:, :LRU_BW] * sp[:, sl]
        a = jnp.exp(log_a)
        b = jnp.sqrt(-jnp.tanh(log_a) * (a * a + 1.0)) * gates[:, LRU_BW:] * xn
        groups = t // 8
        a3 = a.reshape(groups, 8, LRU_BW)
        b3 = b.reshape(groups, 8, LRU_BW)
        for k in (1, 2, 4):
            if rev:
                keep = row8 < 8 - k
                shift = 8 - k
            else:
                keep = row8 >= k
                shift = k
            sa = jnp.where(keep, pltpu.roll(a3, shift, 1), 1.0)
            sb = jnp.where(keep, pltpu.roll(b3, shift, 1), 0.0)
            b3 = b3 + a3 * sb
            a3 = a3 * sa
        carry = carry_ref[:, sl]
        for gi in (reversed(range(groups)) if rev else range(groups)):
            hg = a3[gi] * carry + b3[gi]
            carry = hg[0:1] if rev else hg[7:8]
            h_ref[gi * 8:(gi + 1) * 8, sl] = hg
        carry_ref[:, sl] = carry


def _lru_scan(rw, proj, conv_w, conv_b, w_gate, b_gate, lam, rev):
    t = TM_CONV
    t8 = t // 8
    n8 = rw.rows // 8
    d = 1 if rev else 0
    rb = lambda b, s: rw.chunk_block(b, d, s, t)
    return pl.pallas_call(
        functools.partial(_lru_scan_kernel, rw=rw, rev=rev),
        grid=(rw.batch, (rw.n_lat + rw.n_ctx) // t),
        in_specs=[pl.BlockSpec((t, LRU_W), lambda b, s: (rb(b, s), 0)),
                  pl.BlockSpec((8, LRU_W), lambda b, s: (jnp.maximum(rb(b, s) * t8 - 1, 0), 0)),
                  pl.BlockSpec((8, LRU_W), lambda b, s: (jnp.minimum((rb(b, s) + 1) * t8, n8 - 1), 0)),
                  pl.BlockSpec((4, LRU_W), lambda b, s: (0, 0)),
                  pl.BlockSpec((1, LRU_W), lambda b, s: (0, 0)),
                  pl.BlockSpec((LRU_BLOCKS, LRU_BW, 2 * LRU_BW), lambda b, s: (0, 0, 0)),
                  pl.BlockSpec((LRU_BLOCKS, 1, 2 * LRU_BW), lambda b, s: (0, 0, 0)),
                  pl.BlockSpec((1, LRU_W), lambda b, s: (0, 0))],
        out_specs=pl.BlockSpec((t, LRU_W), lambda b, s: (rb(b, s), 0)),
        out_shape=jax.ShapeDtypeStruct((rw.rows, LRU_W), F32),
        scratch_shapes=[pltpu.VMEM((1, LRU_W), F32)],
        compiler_params=_cparams(("parallel", "arbitrary")),
    )(proj, proj, proj, conv_w, conv_b.reshape(1, LRU_W), _bf(w_gate),
      b_gate.reshape(LRU_BLOCKS, 1, 2 * LRU_BW), lam.reshape(1, LRU_W))


def _rope16(x, tab):
    cos, s1, s2 = tab[:, :128], tab[:, 128:256], tab[:, 256:]
    cols = []
    for cidx in range(x.shape[1] // 128):
        xc = x[:, cidx * 128:(cidx + 1) * 128]
        cols.append(xc * cos + pltpu.roll(xc, 112, 1) * s1 + pltpu.roll(xc, 16, 1) * s2)
    return cols[0] if len(cols) == 1 else jnp.concatenate(cols, axis=1)


def _swa_kernel(sink_ref, q_ref, kp_ref, kc_ref, kn_ref, vp_ref, vc_ref, vn_ref, kx_ref, vx_ref,
                tp_ref, tc_ref, tn_ref, o_ref):
    qb = pl.program_id(1)
    nb = pl.num_programs(1)
    blk = SWA_BLOCK
    n_loc = 3 * blk
    n_keys = n_loc + kx_ref.shape[0]
    tab_c = tc_ref[...]
    q = _rope16(q_ref[...], tab_c) * (SWA_HD ** -0.5)
    k_all = jnp.concatenate([_rope16(kp_ref[...], tp_ref[...]), _rope16(kc_ref[...], tab_c),
                             _rope16(kn_ref[...], tn_ref[...]), kx_ref[...]], axis=0)
    v_all = jnp.concatenate([vp_ref[...], vc_ref[...], vn_ref[...], vx_ref[...]], axis=0)
    assert SWA_WINDOW >= blk - 1
    qi = lax.broadcasted_iota(jnp.int32, (blk, blk), 0)
    kj = lax.broadcasted_iota(jnp.int32, (blk, blk), 1)
    valid_prev = jnp.logical_and(kj - blk - qi >= -SWA_WINDOW, qb > 0)
    valid_next = jnp.logical_and(kj + blk - qi <= SWA_WINDOW, qb < nb - 1)
    lane = lax.broadcasted_iota(jnp.int32, (n_keys, 128), 1)
    low = lane < SWA_HD
    out_low = lax.broadcasted_iota(jnp.int32, (blk, 128), 1) < SWA_HD

    def expo(scores, sink):
        sc = jnp.concatenate([jnp.where(valid_prev, scores[:, :blk], -jnp.inf), scores[:, blk:2 * blk],
                              jnp.where(valid_next, scores[:, 2 * blk:n_loc], -jnp.inf), scores[:, n_loc:]], axis=1)
        mx = jnp.maximum(jnp.max(sc, axis=-1, keepdims=True), sink)
        return _bf(jnp.exp(sc - mx)), jnp.exp(sink - mx)

    group = SWA_QH // SWA_KVH
    for g in range(SWA_KVH):
        col = slice((g // 2) * 128, (g // 2 + 1) * 128)
        native_low = g % 2 == 0
        sel = low if native_low else jnp.logical_not(low)
        k_nat = jnp.where(sel, k_all[:, col], 0.0)
        v_nat = jnp.where(sel, v_all[:, col], 0.0)
        k_oth = pltpu.roll(k_nat, SWA_HD, 1)
        v_oth = pltpu.roll(v_nat, SWA_HD, 1)
        k_lo, k_hi = (k_nat, k_oth) if native_low else (k_oth, k_nat)
        v_lo, v_hi = (v_nat, v_oth) if native_low else (v_oth, v_nat)
        v_lo = jnp.where(lane == SWA_HD, 1.0, v_lo)
        v_hi = jnp.where(lane == 0, 1.0, v_hi)
        k_lo, k_hi, v_lo, v_hi = _bf(k_lo), _bf(k_hi), _bf(v_lo), _bf(v_hi)
        for cidx in range(group // 2):
            qc = (g * group) // 2 + cidx
            qv = _bf(q[:, qc * 128:(qc + 1) * 128])
            e_lo, sink_lo = expo(_dot_nt(qv, k_lo), sink_ref[2 * qc])
            e_hi, sink_hi = expo(_dot_nt(qv, k_hi), sink_ref[2 * qc + 1])
            a_lo = _dot(e_lo, v_lo)
            a_hi = _dot(e_hi, v_hi)
            inv_lo = 1.0 / (a_lo[:, SWA_HD:SWA_HD + 1] + sink_lo)
            inv_hi = 1.0 / (a_hi[:, 0:1] + sink_hi)
            o_ref[:, qc * 128:(qc + 1) * 128] = jnp.where(out_low, a_lo * inv_lo, a_hi * inv_hi)


def _swa_core(rw, proj, sink, tab):
    blk = SWA_BLOCK
    nb = rw.n_lat // blk
    kcol = SWA_QD // SWA_KVD
    vcol = kcol + 1
    ctx_blk = lambda b: rw.lat_rows // rw.n_ctx + b
    prev = lambda q: jnp.maximum(q - 1, 0)
    nxt = lambda q: jnp.minimum(q + 1, nb - 1)
    kv = lambda col, f: pl.BlockSpec((blk, SWA_KVD), lambda b, q: (b * nb + f(q), col))
    same = lambda q: q
    return pl.pallas_call(
        _swa_kernel,
        grid=(rw.batch, nb),
        in_specs=[pl.BlockSpec(memory_space=pltpu.SMEM),
                  pl.BlockSpec((blk, SWA_QD), lambda b, q: (b * nb + q, 0)),
                  kv(kcol, prev), kv(kcol, same), kv(kcol, nxt),
                  kv(vcol, prev), kv(vcol, same), kv(vcol, nxt),
                  pl.BlockSpec((rw.n_ctx, SWA_KVD), lambda b, q: (ctx_blk(b), kcol)),
                  pl.BlockSpec((rw.n_ctx, SWA_KVD), lambda b, q: (ctx_blk(b), vcol)),
                  pl.BlockSpec((blk, 384), lambda b, q: (prev(q), 0)),
                  pl.BlockSpec((blk, 384), lambda b, q: (q, 0)),
                  pl.BlockSpec((blk, 384), lambda b, q: (nxt(q), 0))],
        out_specs=pl.BlockSpec((blk, SWA_QD), lambda b, q: (b * nb + q, 0)),
        out_shape=jax.ShapeDtypeStruct((rw.lat_rows, SWA_QD), F32),
        compiler_params=_cparams(("parallel", "parallel")),
    )(sink, proj, proj, proj, proj, proj, proj, proj, proj, proj, tab, tab, tab)


def _axial_angles(n_lat, dh):
    t = jnp.arange(n_lat)
    row = (t // GRID_W).astype(F32)
    col = (t % GRID_W).astype(F32)
    nf = dh // 4
    inv = ROPE_THETA ** (-jnp.arange(nf, dtype=F32) / nf)
    return row[:, None] * inv, col[:, None] * inv


def _ret_tables(n_lat):
    ar, ac = _axial_angles(n_lat, RET_DK)
    cos = jnp.concatenate([jnp.cos(ar), jnp.cos(ar), jnp.cos(ac), jnp.cos(ac)], axis=1)
    sin = jnp.concatenate([-jnp.sin(ar), jnp.sin(ar), -jnp.sin(ac), jnp.sin(ac)], axis=1)
    return cos, sin


def _swa_table(n_lat):
    ar, ac = _axial_angles(n_lat, SWA_HD)
    z = jnp.zeros_like(ar)
    cos = jnp.concatenate([jnp.cos(ar), jnp.cos(ar), jnp.cos(ac), jnp.cos(ac)], axis=1)
    s1 = jnp.concatenate([-jnp.sin(ar), z, -jnp.sin(ac), z], axis=1)
    s2 = jnp.concatenate([z, jnp.sin(ar), z, jnp.sin(ac)], axis=1)
    return jnp.concatenate([jnp.tile(cos, (1, 2)), jnp.tile(s1, (1, 2)), jnp.tile(s2, (1, 2))], axis=1)


def _gdn_layer(rw, layer, x, mod3, gain, w_in, conv_w, a_log, dt_bias, norm_g, w_out):
    n_ab = w_in.shape[1] - GDN_MAIN
    w_ab = _bf(jnp.pad(w_in[:, GDN_MAIN:], ((0, 0), (0, 128 - n_ab))))
    proj, ab = _inproj(rw, x, gain, mod3, layer, w_in, w_ab, n_cols=GDN_MAIN)
    hv = GDN_V_HEADS
    alog_row = jnp.zeros((1, 128), F32).at[0, 0:hv].set(a_log[0]).at[0, 2 * hv:3 * hv].set(a_log[1])
    dtb_row = jnp.zeros((1, 128), F32).at[0, 0:hv].set(dt_bias[0]).at[0, 2 * hv:3 * hv].set(dt_bias[1])
    cb = 2048
    row_spec = pl.BlockSpec((TM_CONV, 128), lambda i, j: (i, 0))
    vec_spec = pl.BlockSpec((1, 128), lambda i, j: (0, 0))
    qkv, g_all, b_all = pl.pallas_call(
        functools.partial(_gdn_conv_kernel, rw=rw),
        grid=(rw.rows // TM_CONV, GDN_CONV // cb),
        in_specs=_conv_specs(rw, cb) + [row_spec, vec_spec, vec_spec],
        out_specs=[pl.BlockSpec((TM_CONV, cb), lambda i, j: (i, j)), row_spec, row_spec],
        out_shape=[jax.ShapeDtypeStruct((rw.rows, GDN_CONV), F32),
                   jax.ShapeDtypeStruct((rw.rows, 128), F32),
                   jax.ShapeDtypeStruct((rw.rows, 128), F32)],
        scratch_shapes=[pltpu.VMEM((TM_CONV + 16, cb), F32)],
        compiler_params=_cparams(("parallel", "arbitrary")),
    )(proj, proj, proj, conv_w, ab, alog_row, dtb_row)
    g_dir = jnp.stack([g_all[:, 0:hv], g_all[:, 2 * hv:3 * hv]])
    b_dir = jnp.stack([b_all[:, hv:2 * hv], b_all[:, 3 * hv:4 * hv]])
    o = _gdn_core(rw, qkv, g_dir, b_dir)
    return _two_dir_out(_gdn_out_kernel, rw, layer, x, mod3, o, proj, GDN_CONV // GDN_VAL, norm_g, _bf(w_out))


def _ret_layer(rw, layer, x, mod3, gain, w_in, decay_logit, norm_g, w_out):
    proj = _inproj(rw, x, gain, mod3, layer, w_in)
    cos_tab, sin_tab = _ret_tables(rw.n_lat)
    o = _ret_core(rw, proj, cos_tab, sin_tab, decay_logit)
    return _two_dir_out(_ret_out_kernel, rw, layer, x, mod3, o, proj, (2 * RET_KEY + RET_VAL) // RET_VAL,
                        norm_g, _bf(w_out))


def _lru_layer(rw, layer, x, mod3, gain, w_in, conv_w, conv_b, w_gate, b_gate, lam, w_out):
    proj = _inproj(rw, x, gain, mod3, layer, w_in)
    h0 = _lru_scan(rw, proj, conv_w, conv_b, w_gate[0], b_gate[0], lam[0], rev=False)
    h1 = _lru_scan(rw, proj, conv_w, conv_b, w_gate[1], b_gate[1], lam[1], rev=True)
    tm = TM_OUT
    specs = [pl.BlockSpec((tm, LRU_W), lambda i: (i, 0)),
             pl.BlockSpec((tm, LRU_W), lambda i: (i, 0)),
             pl.BlockSpec((tm, LRU_W), lambda i: (i, 1))]
    return _outproj_call(_lru_out_kernel, rw, rw.rows, layer, x, mod3, specs, [h0, h1, proj], _bf(w_out), [])


def _swa_layer(rw, layer, x, mod3, gain, w_in, sink, w_out):
    proj = _inproj(rw, x, gain, mod3, layer, w_in)
    o = _swa_core(rw, proj, sink, _swa_table(rw.n_lat))
    specs = [pl.BlockSpec((TM_OUT, SWA_QD), lambda i: (i, 0))]
    return _outproj_call(_swa_out_kernel, rw, rw.lat_rows, layer, x, mod3, specs, [o], _bf(w_out), [])


def kernel(x, c, ctx, c_ctx, norm_mix_g, norm_ffn_g, w_mod, b_mod, w_ff1, w_ff2, norm_out_g, gdn_w_in, gdn_conv_w, gdn_a_log, gdn_dt_bias, gdn_norm_g, gdn_w_out, ret_w_in, ret_decay_logit, ret_norm_g, ret_w_out, lru_w_in, lru_conv_w, lru_conv_b, lru_w_gate, lru_b_gate, lru_lambda, lru_w_out, swa_w_in, swa_sink, swa_w_out):
    batch, n_lat, _ = x.shape
    n_ctx = ctx.shape[1]
    assert batch + 1 <= MOD_ROWS and n_lat % TM_IN == 0 and (batch * n_ctx) % TM_IN == 0
    assert w_mod.shape[0] == DEPTH and gdn_w_in.shape[0] == 1 and swa_w_in.shape[0] == 1
    rw = _Rows(batch, n_lat, n_ctx)
    xf = jnp.concatenate([x.reshape(rw.lat_rows, D), ctx.reshape(batch * n_ctx, D)], axis=0)
    cc = jnp.zeros((MOD_ROWS, D), F32).at[:batch].set(c).at[batch].set(c_ctx)
    mod3 = _adaln(cc, w_mod, b_mod)

    def mlp(layer, xin, n_rows, final):
        return _mlp(rw, xin, n_rows, norm_ffn_g[layer], mod3, layer, w_ff1, w_ff2,
                    norm_out_g, final)

    xf = _gdn_layer(rw, 0, xf, mod3, norm_mix_g[0], gdn_w_in[0], gdn_conv_w[0], gdn_a_log[0], gdn_dt_bias[0],
                    gdn_norm_g[0], gdn_w_out[0])
    xf = mlp(0, xf, rw.rows, False)
    xf = _ret_layer(rw, 1, xf, mod3, norm_mix_g[1], ret_w_in[0], ret_decay_logit[0], ret_norm_g[0], ret_w_out[0])
    xf = mlp(1, xf, rw.rows, False)
    xf = _lru_layer(rw, 2, xf, mod3, norm_mix_g[2], lru_w_in[0], lru_conv_w[0], lru_conv_b[0], lru_w_gate[0],
                    lru_b_gate[0], lru_lambda[0], lru_w_out[0])
    xf = mlp(2, xf, rw.rows, False)
    xl = _swa_layer(rw, 3, xf, mod3, norm_mix_g[3], swa_w_in[0], swa_sink[0], swa_w_out[0])
    out = mlp(3, xl, rw.lat_rows, True)
    return out.reshape(batch, n_lat, D)
```

```python
import functools
import math

import jax
import jax.numpy as jnp
from jax import lax
from jax.experimental import pallas as pl
from jax.experimental.pallas import tpu as pltpu

F32 = jnp.float32
BF16 = jnp.bfloat16

D = 1024
D_FF = 4 * D
EPS = 1e-6
DEPTH = 4
ROPE_THETA = 10000.0
GRID_W = 64
CHUNK = 64
MOD_ROWS = 16

GDN_QK_HEADS = 8
GDN_V_HEADS = 16
GDN_HD = 128
GDN_KEY = GDN_QK_HEADS * GDN_HD
GDN_VAL = GDN_V_HEADS * GDN_HD
GDN_CONV = 2 * GDN_KEY + GDN_VAL
GDN_MAIN = GDN_CONV + GDN_VAL

RET_HEADS = 4
RET_DK = 256
RET_DV = 512
RET_KEY = RET_HEADS * RET_DK
RET_VAL = RET_HEADS * RET_DV

LRU_W = 1280
LRU_BLOCKS = 10
LRU_BW = LRU_W // LRU_BLOCKS
LRU_C = 8.0

SWA_QH = 16
SWA_KVH = 4
SWA_HD = 64
SWA_BLOCK = 128
SWA_WINDOW = 128
SWA_QD = SWA_QH * SWA_HD
SWA_KVD = SWA_KVH * SWA_HD

TM_IN = 1024
TM_OUT = 512
TM_CONV = 256
HALO = 16
TN_IN_MAX = 1536
FF_CHUNK = 1024
RET_CHUNK = 256
GDN_BLOCK = 4 * CHUNK
VMEM_LIMIT = 48 * 1024 * 1024
MLP_VMEM_LIMIT = 56 * 1024 * 1024


def _dot(a, b):
    return jnp.dot(a, b, preferred_element_type=F32)


def _dot_nt(a, b):
    return lax.dot_general(a, b, (((1,), (1,)), ((), ())), preferred_element_type=F32)


def _dot_tn(a, b):
    return lax.dot_general(a, b, (((0,), (0,)), ((), ())), preferred_element_type=F32)


def _bf(x):
    return x.astype(BF16)


def _split3(x):
    hi = _bf(x)
    r = x - hi.astype(F32)
    mid = _bf(r)
    lo = _bf(r - mid.astype(F32))
    return hi, mid, lo


def _silu(x):
    return x * jax.nn.sigmoid(x)


def _softplus(x):
    return jnp.maximum(x, 0.0) + jnp.log1p(jnp.exp(-jnp.abs(x)))


def _gelu_tanh(x):
    cdf = 0.5 * (1.0 + jnp.tanh(math.sqrt(2.0 / math.pi) * (x + 0.044715 * (x * x * x))))
    return x * cdf


def _cparams(sem, vmem=None):
    return pltpu.CompilerParams(dimension_semantics=sem, vmem_limit_bytes=VMEM_LIMIT if vmem is None else vmem)


class _Rows:
    def __init__(self, batch, n_lat, n_ctx):
        self.batch, self.n_lat, self.n_ctx = batch, n_lat, n_ctx
        self.lat_rows = batch * n_lat
        self.rows = self.lat_rows + batch * n_ctx
        self.n_steps = (n_lat + n_ctx) // CHUNK
        self.ctx_steps = n_ctx // CHUNK
        self.lat_steps = n_lat // CHUNK

    def mod_row(self, i, tm):
        return jnp.where(i < self.lat_rows // tm, i // (self.n_lat // tm), self.batch)

    def chunk_block(self, b, d, s, blk):
        nc, nl = self.n_ctx // blk, self.n_lat // blk
        c_ctx = jnp.where(d == 0, s, nc - 1 - s)
        c_lat = jnp.where(d == 0, s - nc, nl - 1 - (s - nc))
        return jnp.where(s < nc, self.lat_rows // blk + b * nc + c_ctx, b * nl + c_lat)

    def lat_chunk(self, d, s, blk):
        nc, nl = self.n_ctx // blk, self.n_lat // blk
        return jnp.where(s < nc, 0, jnp.where(d == 0, s - nc, nl - 1 - (s - nc)))


def _adaln_kernel(c_ref, w_ref, b_ref, o_ref):
    s = _silu(c_ref[...])
    w = w_ref[0]
    s_hi = _bf(s)
    s_lo = _bf(s - s_hi.astype(F32))
    w_hi = _bf(w)
    w_lo = _bf(w - w_hi.astype(F32))
    y = _dot(s_hi, w_hi) + (_dot(s_lo, w_hi) + _dot(s_hi, w_lo))
    o_ref[0] = y + b_ref[0]


def _adaln(cc, w_mod, b_mod):
    depth, _, n = w_mod.shape
    tn = 1024
    out = pl.pallas_call(
        _adaln_kernel,
        grid=(depth, n // tn),
        in_specs=[pl.BlockSpec((MOD_ROWS, D), lambda l, j: (0, 0)),
                  pl.BlockSpec((1, D, tn), lambda l, j: (l, 0, j)),
                  pl.BlockSpec((1, 1, tn), lambda l, j: (l, 0, j))],
        out_specs=pl.BlockSpec((1, MOD_ROWS, tn), lambda l, j: (l, 0, j)),
        out_shape=jax.ShapeDtypeStruct((depth, MOD_ROWS, n), F32),
        compiler_params=_cparams(("parallel", "parallel")),
    )(cc, w_mod, b_mod.reshape(depth, 1, n))
    return out.reshape(depth * MOD_ROWS * 6, 1, D)


def _mod_spec(rw, layer, which, tm):
    base = layer * MOD_ROWS * 6
    return pl.BlockSpec((1, 1, D), lambda i, *_: (base + rw.mod_row(i, tm) * 6 + which, 0, 0))


def _norm_mod(x, gain, shift, scale):
    y = x * lax.rsqrt(jnp.mean(x * x, axis=-1, keepdims=True) + EPS) * gain
    return y * (1.0 + scale) + shift


def _inproj_kernel(x_ref, g_ref, sh_ref, sc_ref, w_ref, *rest, has_extra):
    if has_extra:
        w2_ref, o_ref, o2_ref, h_ref = rest
    else:
        o_ref, h_ref = rest

    @pl.when(pl.program_id(1) == 0)
    def _():
        h = _bf(_norm_mod(x_ref[...], g_ref[...], sh_ref[0], sc_ref[0]))
        h_ref[...] = h
        if has_extra:
            o2_ref[...] = _dot(h, w2_ref[...])

    o_ref[...] = _bf(_dot(h_ref[...], w_ref[...]))


def _inproj(rw, x, gain, mod3, layer, w, w_extra=None, n_cols=None):
    n = w.shape[1] if n_cols is None else n_cols
    tm = TM_IN
    tn = max(t for t in range(256, TN_IN_MAX + 1, 256) if n % t == 0)
    has_extra = w_extra is not None
    in_specs = [pl.BlockSpec((tm, D), lambda i, j: (i, 0)),
                pl.BlockSpec((1, D), lambda i, j: (0, 0)),
                _mod_spec(rw, layer, 0, tm),
                _mod_spec(rw, layer, 1, tm),
                pl.BlockSpec((D, tn), lambda i, j: (0, j))]
    out_specs = [pl.BlockSpec((tm, tn), lambda i, j: (i, j))]
    out_shape = [jax.ShapeDtypeStruct((rw.rows, n), BF16)]
    args = [x, gain.reshape(1, D), mod3, mod3, w]
    if has_extra:
        ne = w_extra.shape[1]
        in_specs.append(pl.BlockSpec((D, ne), lambda i, j: (0, 0)))
        out_specs.append(pl.BlockSpec((tm, ne), lambda i, j: (i, 0)))
        out_shape.append(jax.ShapeDtypeStruct((rw.rows, ne), F32))
        args.append(w_extra)
    res = pl.pallas_call(
        functools.partial(_inproj_kernel, has_extra=has_extra),
        grid=(rw.rows // tm, n // tn),
        in_specs=in_specs, out_specs=out_specs, out_shape=out_shape,
        scratch_shapes=[pltpu.VMEM((tm, D), BF16)],
        compiler_params=_cparams(("parallel", "arbitrary")),
    )(*args)
    return res if has_extra else res[0]


def _mlp_kernel(x_ref, g_ref, sh_ref, sc_ref, gt_ref, w1_ref, w2_ref, go_ref, o_ref, h_ref, acc_ref,
                *, final_norm):
    k = pl.program_id(1)

    @pl.when(k == 0)
    def _():
        h_ref[...] = _bf(_norm_mod(x_ref[...], g_ref[...], sh_ref[0], sc_ref[0]))
        acc_ref[...] = jnp.zeros_like(acc_ref)

    a = jnp.square(jnp.maximum(_dot(h_ref[...], _bf(w1_ref[...])), 0.0))
    acc_ref[...] += _dot(_bf(a), _bf(w2_ref[...]))

    @pl.when(k == pl.num_programs(1) - 1)
    def _():
        y = x_ref[...] + gt_ref[0] * acc_ref[...]
        if final_norm:
            y = y * lax.rsqrt(jnp.mean(y * y, axis=-1, keepdims=True) + EPS) * go_ref[...]
        o_ref[...] = y


def _mlp(rw, x, n_rows, gain, mod3, layer, w1, w2, out_gain, final_norm):
    tm, ck = TM_IN, FF_CHUNK
    return pl.pallas_call(
        functools.partial(_mlp_kernel, final_norm=final_norm),
        grid=(n_rows // tm, D_FF // ck),
        in_specs=[pl.BlockSpec((tm, D), lambda i, k: (i, 0)),
                  pl.BlockSpec((1, D), lambda i, k: (0, 0)),
                  _mod_spec(rw, layer, 3, tm),
                  _mod_spec(rw, layer, 4, tm),
                  _mod_spec(rw, layer, 5, tm),
                  pl.BlockSpec((None, D, ck), lambda i, k: (layer, 0, k)),
                  pl.BlockSpec((None, ck, D), lambda i, k: (layer, k, 0)),
                  pl.BlockSpec((1, D), lambda i, k: (0, 0))],
        out_specs=pl.BlockSpec((tm, D), lambda i, k: (i, 0)),
        out_shape=jax.ShapeDtypeStruct((n_rows, D), F32),
        scratch_shapes=[pltpu.VMEM((tm, D), BF16), pltpu.VMEM((tm, D), F32)],
        compiler_params=_cparams(("parallel", "arbitrary"), MLP_VMEM_LIMIT),
    )(x, gain.reshape(1, D), mod3, mod3, mod3, w1, w2, out_gain.reshape(1, D))


def _gdn_out_kernel(x_ref, gt_ref, o0_ref, o1_ref, z_ref, ng_ref, w_ref, out_ref, a_ref):
    ng = ng_ref[...]
    for h in range(GDN_V_HEADS):
        sl = slice(h * GDN_HD, (h + 1) * GDN_HD)
        o = o0_ref[0, :, sl].astype(F32) + o1_ref[0, :, sl].astype(F32)
        y = o * lax.rsqrt(jnp.mean(o * o, axis=-1, keepdims=True) + EPS) * ng
        a_ref[:, sl] = _bf(y * _silu(z_ref[:, sl].astype(F32)))
    out_ref[...] = x_ref[...] + gt_ref[0] * _dot(a_ref[...], w_ref[...])


def _ret_out_kernel(x_ref, gt_ref, o0_ref, o1_ref, z_ref, ng_ref, w_ref, out_ref, a_ref):
    for h in range(RET_HEADS):
        sl = slice(h * RET_DV, (h + 1) * RET_DV)
        o = o0_ref[0, :, sl].astype(F32) + o1_ref[0, :, sl].astype(F32)
        mu = jnp.mean(o, axis=-1, keepdims=True)
        oc = o - mu
        var = jnp.mean(oc * oc, axis=-1, keepdims=True)
        y = oc * lax.rsqrt(var + EPS) * ng_ref[:, sl]
        a_ref[:, sl] = _bf(y * _silu(z_ref[:, sl].astype(F32)))
    out_ref[...] = x_ref[...] + gt_ref[0] * _dot(a_ref[...], w_ref[...])


def _lru_out_kernel(x_ref, gt_ref, h0_ref, h1_ref, gb_ref, w_ref, out_ref):
    a = (h0_ref[...] + h1_ref[...]) * _gelu_tanh(gb_ref[...].astype(F32))
    out_ref[...] = x_ref[...] + gt_ref[0] * _dot(_bf(a), w_ref[...])


def _swa_out_kernel(x_ref, gt_ref, o_ref, w_ref, out_ref):
    out_ref[...] = x_ref[...] + gt_ref[0] * _dot(_bf(o_ref[...]), w_ref[...])


def _outproj_call(kern, rw, n_rows, layer, x, mod3, extra_specs, extra_args, w, scratch):
    tm = TM_OUT
    k = w.shape[0]
    return pl.pallas_call(
        kern,
        grid=(n_rows // tm,),
        in_specs=[pl.BlockSpec((tm, D), lambda i: (i, 0)), _mod_spec(rw, layer, 2, tm)] + extra_specs
                 + [pl.BlockSpec((k, D), lambda i: (0, 0))],
        out_specs=pl.BlockSpec((tm, D), lambda i: (i, 0)),
        out_shape=jax.ShapeDtypeStruct((n_rows, D), F32),
        scratch_shapes=scratch,
        compiler_params=_cparams(("parallel",)),
    )(x, mod3, *extra_args, w)


def _two_dir_out(kern, rw, layer, x, mod3, o, proj, z_block, norm_g, w):
    tm, k = TM_OUT, w.shape[0]
    ng = norm_g.reshape(1, -1)
    specs = [pl.BlockSpec((1, tm, k), lambda i: (0, i, 0)),
             pl.BlockSpec((1, tm, k), lambda i: (1, i, 0)),
             pl.BlockSpec((tm, k), lambda i: (i, z_block)),
             pl.BlockSpec(ng.shape, lambda i: (0, 0))]
    return _outproj_call(kern, rw, rw.rows, layer, x, mod3, specs, [o, o, proj, ng], w,
                         [pltpu.VMEM((tm, k), BF16)])


def _conv_core(x_ref, p_ref, n_ref, w_ref, first, last, pad_ref=None):
    t = x_ref.shape[0]
    x = x_ref[...].astype(F32)
    prev = p_ref[...].astype(F32)[HALO - 8:] * jnp.where(first, 0.0, 1.0)
    nxt = n_ref[...].astype(F32)[:8] * jnp.where(last, 0.0, 1.0)
    if pad_ref is not None:
        pad_ref[0:8] = prev
        pad_ref[8:t + 8] = x
        pad_ref[t + 8:t + 16] = nxt
        xm2, xm1, xp1 = pad_ref[6:t + 6], pad_ref[7:t + 7], pad_ref[9:t + 9]
    else:
        row8 = lax.broadcasted_iota(jnp.int32, prev.shape, 0)
        r2 = pltpu.roll(x, 2, 0)
        r1 = pltpu.roll(x, 1, 0)
        rp = pltpu.roll(x, t - 1, 0)
        f2 = jnp.where(row8 < 2, pltpu.roll(prev, 2, 0), r2[:8])
        f1 = jnp.where(row8 < 1, pltpu.roll(prev, 1, 0), r1[:8])
        l1 = jnp.where(row8 == 7, pltpu.roll(nxt, 7, 0), rp[t - 8:])
        xm2 = jnp.concatenate([f2, r2[8:]], axis=0)
        xm1 = jnp.concatenate([f1, r1[8:]], axis=0)
        xp1 = jnp.concatenate([rp[:t - 8], l1], axis=0)
    w = w_ref[...]
    return w[0:1] * xm2 + w[1:2] * xm1 + w[2:3] * x + w[3:4] * xp1


def _seg_flags(rw, i):
    per = rw.n_lat // TM_CONV
    is_ctx = i >= rw.lat_rows // TM_CONV
    assert rw.n_ctx == TM_CONV
    first = jnp.logical_or(is_ctx, i % per == 0)
    last = jnp.logical_or(is_ctx, i % per == per - 1)
    return first, last


def _gdn_conv_kernel(x_ref, p_ref, n_ref, w_ref, ab_ref, alog_ref, dtb_ref, o_ref, g_ref, b_ref, pad_ref, *, rw):
    i, j = pl.program_id(0), pl.program_id(1)
    first, last = _seg_flags(rw, i)
    s = _silu(_conv_core(x_ref, p_ref, n_ref, w_ref, first, last, pad_ref))
    width = s.shape[1]
    qk_blocks = 2 * GDN_KEY // width

    @pl.when(j < qk_blocks)
    def _():
        for h in range(width // GDN_HD):
            sl = slice(h * GDN_HD, (h + 1) * GDN_HD)
            sh = s[:, sl]
            is_q = j * width + h * GDN_HD < GDN_KEY
            scale = jnp.where(is_q, GDN_HD ** -0.5, 1.0)
            o_ref[:, sl] = sh * (lax.rsqrt(jnp.sum(sh * sh, axis=-1, keepdims=True) + EPS) * scale)

    @pl.when(j >= qk_blocks)
    def _():
        o_ref[...] = s

    @pl.when(j == 0)
    def _():
        ab = ab_ref[...]
        g_ref[...] = -jnp.exp(alog_ref[...]) * _softplus(ab + dtb_ref[...])
        b_ref[...] = jax.nn.sigmoid(ab)


def _conv_specs(rw, cb):
    th = TM_CONV // HALO
    nh = rw.rows // HALO
    return [pl.BlockSpec((TM_CONV, cb), lambda i, j: (i, j)),
            pl.BlockSpec((HALO, cb), lambda i, j: (jnp.maximum(i * th - 1, 0), j)),
            pl.BlockSpec((HALO, cb), lambda i, j: (jnp.minimum((i + 1) * th, nh - 1), j)),
            pl.BlockSpec((4, cb), lambda i, j: (0, j))]


def _dir_masks(d, n):
    ii = lax.broadcasted_iota(jnp.int32, (n, n), 0)
    jj = lax.broadcasted_iota(jnp.int32, (n, n), 1)
    t = (ii - jj) * jnp.where(d == 0, 1, -1)
    return t, t >= 0, t > 0


def _gdn_core_kernel(qkv_ref, g_ref, b_ref, o_ref, s_ref):
    d, s = pl.program_id(1), pl.program_id(2)

    @pl.when(s == 0)
    def _():
        s_ref[...] = jnp.zeros_like(s_ref)

    c, hd = CHUNK, GDN_HD
    rep = GDN_V_HEADS // GDN_QK_HEADS
    heads = range(GDN_V_HEADS)
    n_sub = qkv_ref.shape[0] // c
    subs = range(n_sub)
    offs = [pl.multiple_of(jnp.where(d == 0, i, n_sub - 1 - i) * c, c) for i in subs]
    chains = [(i, h) for i in subs for h in heads]
    t, incl, strict = _dir_masks(d, c)
    cum = _bf(jnp.where(incl, 1.0, 0.0))
    cum_t = _bf(jnp.where(t <= 0, 1.0, 0.0))
    g = [g_ref[0, pl.ds(offs[i], c), :] for i in subs]
    beta = [b_ref[0, pl.ds(offs[i], c), :] for i in subs]
    parts = [_split3(g[i]) for i in subs]
    gc = [sum(_dot(cum, p) for p in parts[i]) for i in subs]
    gr = [sum(_dot_tn(p, cum_t) for p in parts[i]) for i in subs]
    gtot = [jnp.sum(g[i], axis=0, keepdims=True) for i in subs]

    def rows(i, lo):
        return qkv_ref[pl.ds(offs[i], c), lo:lo + hd]

    q_l = [[rows(i, hq * hd) for hq in range(GDN_QK_HEADS)] for i in subs]
    k_l = [[rows(i, GDN_KEY + hq * hd) for hq in range(GDN_QK_HEADS)] for i in subs]
    gram = [[_dot_nt(_bf(jnp.concatenate([k, q], axis=0)), _bf(k)) for q, k in zip(q_l[i], k_l[i])]
            for i in subs]
    gcol = [gc[i][:, h:h + 1] for i, h in chains]
    bcol = [beta[i][:, h:h + 1] for i, h in chains]
    dec = [jnp.where(incl, jnp.exp(jnp.where(incl, gcol[n] - gr[i][h:h + 1, :], 0.0)), 0.0)
           for n, (i, h) in enumerate(chains)]
    eg = [jnp.exp(x) for x in gcol]
    m = [jnp.where(strict, -(gram[i][h // rep][:c] * dec[n] * bcol[n]), 0.0) for n, (i, h) in enumerate(chains)]
    rhs = [jnp.concatenate([k_l[i][h // rep] * (bcol[n] * eg[n]), rows(i, 2 * GDN_KEY + h * hd) * bcol[n]], axis=1)
           for n, (i, h) in enumerate(chains)]
    ids = range(len(chains))

    m_hi = [_bf(x) for x in m]
    zeros_cc = jnp.zeros((c, 2 * c), BF16)
    lane_cc = lax.broadcasted_iota(jnp.int32, (c, 2 * c), 1)
    row_cc = lax.broadcasted_iota(jnp.int32, (c, 2 * c), 0)
    eye_hi = jnp.where(lane_cc == row_cc + c, 1.0, 0.0)
    is_hi = lane_cc >= c
    cat = [jnp.concatenate([m[n], jnp.zeros((c, c), F32)], axis=1) + eye_hi for n in ids]
    for _ in range(6):
        cb = [_bf(x) for x in cat]
        r = [_dot(cb[n], jnp.concatenate([cb[n], zeros_cc], axis=0)) for n in ids]
        cat = [r[n] + jnp.where(is_hi, cat[n], 0.0) for n in ids]
    nb = [_bf(cat[n] - eye_hi) for n in ids]
    zeros_rhs = jnp.zeros((c, 2 * hd), BF16)

    def apply_n(y):
        return [_dot(nb[n], jnp.concatenate([zeros_rhs, _bf(y[n])], axis=0)) for n in ids]

    nr = apply_n(rhs)
    x = [rhs[n] + nr[n] for n in ids]
    resid = [(rhs[n] - x[n]) + _dot(m_hi[n], _bf(x[n])) for n in ids]
    nr = apply_n(resid)
    x = [x[n] + (resid[n] + nr[n]) for n in ids]
    qk = [_bf(gram[i][h // rep][c:] * dec[n]) for n, (i, h) in enumerate(chains)]
    gt = [gtot[i][:, h:h + 1] for i, h in chains]
    kd = [_bf(k_l[i][h // rep] * jnp.exp(gt[n] - gcol[n])) for n, (i, h) in enumerate(chains)]
    wq = [_bf(jnp.concatenate([x[n][:, :hd], q_l[i][h // rep] * eg[n]], axis=0)) for n, (i, h) in enumerate(chains)]
    decay = [jnp.exp(x) for x in gt]

    st = [s_ref[h] for h in heads]
    for i in subs:
        base = i * GDN_V_HEADS
        r = [_dot(wq[base + h], _bf(st[h])) for h in heads]
        unb = [_bf(x[base + h][:, hd:] - r[h][:c]) for h in heads]
        for h in heads:
            o_ref[0, pl.ds(offs[i], c), h * hd:(h + 1) * hd] = _bf(r[h][c:] + _dot(qk[base + h], unb[h]))
        st = [st[h] * decay[base + h] + _dot_tn(kd[base + h], unb[h]) for h in heads]
    for h in heads:
        s_ref[h] = st[h]


def _gdn_core(rw, qkv, g_dir, b_dir):
    blk = GDN_BLOCK
    rb = lambda b, d, s: rw.chunk_block(b, d, s, blk)
    return pl.pallas_call(
        _gdn_core_kernel,
        grid=(rw.batch, 2, (rw.n_lat + rw.n_ctx) // blk),
        in_specs=[pl.BlockSpec((blk, GDN_CONV), lambda b, d, s: (rb(b, d, s), 0)),
                  pl.BlockSpec((1, blk, GDN_V_HEADS), lambda b, d, s: (d, rb(b, d, s), 0)),
                  pl.BlockSpec((1, blk, GDN_V_HEADS), lambda b, d, s: (d, rb(b, d, s), 0))],
        out_specs=pl.BlockSpec((1, blk, GDN_VAL), lambda b, d, s: (d, rb(b, d, s), 0)),
        out_shape=jax.ShapeDtypeStruct((2, rw.rows, GDN_VAL), BF16),
        scratch_shapes=[pltpu.VMEM((GDN_V_HEADS, GDN_HD, GDN_HD), F32)],
        compiler_params=_cparams(("parallel", "parallel", "arbitrary")),
    )(qkv, g_dir, b_dir)


def _rope_half(x, cos, sin_signed):
    return x * cos + pltpu.roll(x, x.shape[1] // 2, 1) * sin_signed


def _ret_core_kernel(q_ref, k_ref, v_ref, cos_ref, sin_ref, dl_ref, o_ref, s_ref, *, ctx_steps):
    d, s = pl.program_id(1), pl.program_id(2)

    @pl.when(s == 0)
    def _():
        s_ref[...] = jnp.zeros_like(s_ref)

    c = q_ref.shape[0]
    heads = range(RET_HEADS)
    is_lat = s >= ctx_steps
    cos = jnp.where(is_lat, cos_ref[...], 1.0)
    sin = jnp.where(is_lat, sin_ref[...], 0.0)
    t, incl, _ = _dir_masks(d, c)
    tf = t.astype(F32)
    pos = lax.broadcasted_iota(jnp.int32, (c, 1), 0)
    ridx = jnp.where(d == 0, pos + 1, c - pos).astype(F32)
    log_gamma = -_softplus(-dl_ref[0])

    def rope(x):
        hw = RET_DK // 2
        return jnp.concatenate([_rope_half(x[:, :hw], cos[:, :hw], sin[:, :hw]),
                                _rope_half(x[:, hw:], cos[:, hw:], sin[:, hw:])], axis=1)

    lg = [log_gamma[:, h:h + 1] for h in heads]
    q = [rope(q_ref[:, h * RET_DK:(h + 1) * RET_DK].astype(F32)) * (RET_DK ** -0.5) for h in heads]
    k = [rope(k_ref[:, h * RET_DK:(h + 1) * RET_DK].astype(F32)) for h in heads]
    vb = [v_ref[:, h * RET_DV:(h + 1) * RET_DV] for h in heads]
    gcum = [lg[h] * ridx for h in heads]
    dec = [jnp.where(incl, jnp.exp(jnp.where(incl, lg[h] * tf, 0.0)), 0.0) for h in heads]
    qk = [_bf(_dot_nt(_bf(q[h]), _bf(k[h])) * dec[h]) for h in heads]
    st = [s_ref[h] for h in heads]
    for h in heads:
        o_ref[0, :, h * RET_DV:(h + 1) * RET_DV] = _bf(
            _dot(_bf(q[h] * jnp.exp(gcum[h])), _bf(st[h])) + _dot(qk[h], vb[h]))
    gtot = [lg[h] * float(c) for h in heads]
    kd = [_bf(k[h] * jnp.exp(gtot[h] - gcum[h])) for h in heads]
    for h in heads:
        s_ref[h] = st[h] * jnp.exp(gtot[h]) + _dot_tn(kd[h], vb[h])


def _ret_core(rw, proj, cos_tab, sin_tab, decay_logit):
    c = RET_CHUNK
    rb = lambda b, d, s: rw.chunk_block(b, d, s, c)
    dl = jnp.zeros((2, 1, 128), F32).at[:, 0, :RET_HEADS].set(decay_logit)
    return pl.pallas_call(
        functools.partial(_ret_core_kernel, ctx_steps=rw.n_ctx // c),
        grid=(rw.batch, 2, (rw.n_lat + rw.n_ctx) // c),
        in_specs=[pl.BlockSpec((c, RET_KEY), lambda b, d, s: (rb(b, d, s), 0)),
                  pl.BlockSpec((c, RET_KEY), lambda b, d, s: (rb(b, d, s), 1)),
                  pl.BlockSpec((c, RET_VAL), lambda b, d, s: (rb(b, d, s), 1)),
                  pl.BlockSpec((c, RET_DK), lambda b, d, s: (rw.lat_chunk(d, s, c), 0)),
                  pl.BlockSpec((c, RET_DK), lambda b, d, s: (rw.lat_chunk(d, s, c), 0)),
                  pl.BlockSpec((1, 1, 128), lambda b, d, s: (d, 0, 0))],
        out_specs=pl.BlockSpec((1, c, RET_VAL), lambda b, d, s: (d, rb(b, d, s), 0)),
        out_shape=jax.ShapeDtypeStruct((2, rw.rows, RET_VAL), BF16),
        scratch_shapes=[pltpu.VMEM((RET_HEADS, RET_DK, RET_DV), F32)],
        compiler_params=_cparams(("parallel", "parallel", "arbitrary")),
    )(proj, proj, proj, cos_tab, sin_tab, dl)


def _lru_scan_kernel(x_ref, p_ref, n_ref, cw_ref, cb_ref, wg_ref, bg_ref, lam_ref, h_ref, carry_ref, *, rw, rev):
    @pl.when(pl.program_id(1) == 0)
    def _():
        carry_ref[...] = jnp.zeros_like(carry_ref)

    t = x_ref.shape[0]
    tile = rw.chunk_block(pl.program_id(0), 1 if rev else 0, pl.program_id(1), t)
    first, last = _seg_flags(rw, tile)
    xs = _conv_core(x_ref, p_ref, n_ref, cw_ref, first, last) + cb_ref[...]
    row8 = lax.broadcasted_iota(jnp.int32, (t // 8, 8, LRU_BW), 1)
    sp = _softplus(-lam_ref[...])
    for n in range(LRU_BLOCKS):
        sl = slice(n * LRU_BW, (n + 1) * LRU_BW)
        xn = xs[:, sl]
        gates = jax.nn.sigmoid(_dot(_bf(xn), wg_ref[n]) + bg_ref[n])
        log_a = -LRU_C * gates[:, :LRU_BW] * sp[:, sl]
        a = jnp.exp(log_a)
        b = jnp.sqrt(-jnp.tanh(log_a) * (a * a + 1.0)) * gates[:, LRU_BW:] * xn
        groups = t // 8
        a3 = a.reshape(groups, 8, LRU_BW)
        b3 = b.reshape(groups, 8, LRU_BW)
        for k in (1, 2, 4):
            if rev:
                keep = row8 < 8 - k
                shift = 8 - k
            else:
                keep = row8 >= k
                shift = k
            sa = jnp.where(keep, pltpu.roll(a3, shift, 1), 1.0)
            sb = jnp.where(keep, pltpu.roll(b3, shift, 1), 0.0)
            b3 = b3 + a3 * sb
            a3 = a3 * sa
        carry = carry_ref[:, sl]
        for gi in (reversed(range(groups)) if rev else range(groups)):
            hg = a3[gi] * carry + b3[gi]
            carry = hg[0:1] if rev else hg[7:8]
            h_ref[gi * 8:(gi + 1) * 8, sl] = hg
        carry_ref[:, sl] = carry


def _lru_scan(rw, proj, conv_w, conv_b, w_gate, b_gate, lam, rev):
    t = TM_CONV
    th = t // HALO
    nh = rw.rows // HALO
    d = 1 if rev else 0
    rb = lambda b, s: rw.chunk_block(b, d, s, t)
    return pl.pallas_call(
        functools.partial(_lru_scan_kernel, rw=rw, rev=rev),
        grid=(rw.batch, (rw.n_lat + rw.n_ctx) // t),
        in_specs=[pl.BlockSpec((t, LRU_W), lambda b, s: (rb(b, s), 0)),
                  pl.BlockSpec((HALO, LRU_W), lambda b, s: (jnp.maximum(rb(b, s) * th - 1, 0), 0)),
                  pl.BlockSpec((HALO, LRU_W), lambda b, s: (jnp.minimum((rb(b, s) + 1) * th, nh - 1), 0)),
                  pl.BlockSpec((4, LRU_W), lambda b, s: (0, 0)),
                  pl.BlockSpec((1, LRU_W), lambda b, s: (0, 0)),
                  pl.BlockSpec((LRU_BLOCKS, LRU_BW, 2 * LRU_BW), lambda b, s: (0, 0, 0)),
                  pl.BlockSpec((LRU_BLOCKS, 1, 2 * LRU_BW), lambda b, s: (0, 0, 0)),
                  pl.BlockSpec((1, LRU_W), lambda b, s: (0, 0))],
        out_specs=pl.BlockSpec((t, LRU_W), lambda b, s: (rb(b, s), 0)),
        out_shape=jax.ShapeDtypeStruct((rw.rows, LRU_W), F32),
        scratch_shapes=[pltpu.VMEM((1, LRU_W), F32)],
        compiler_params=_cparams(("parallel", "arbitrary")),
    )(proj, proj, proj, conv_w, conv_b.reshape(1, LRU_W), _bf(w_gate),
      b_gate.reshape(LRU_BLOCKS, 1, 2 * LRU_BW), lam.reshape(1, LRU_W))


def _rope16(x, tab):
    cos, s1, s2 = tab[:, :128], tab[:, 128:256], tab[:, 256:]
    cols = []
    for cidx in range(x.shape[1] // 128):
        xc = x[:, cidx * 128:(cidx + 1) * 128]
        cols.append(xc * cos + pltpu.roll(xc, 112, 1) * s1 + pltpu.roll(xc, 16, 1) * s2)
    return cols[0] if len(cols) == 1 else jnp.concatenate(cols, axis=1)


def _swa_kernel(sink_ref, q_ref, kp_ref, kc_ref, kn_ref, vp_ref, vc_ref, vn_ref, kx_ref, vx_ref,
                tp_ref, tc_ref, tn_ref, o_ref):
    qb = pl.program_id(1)
    nb = pl.num_programs(1)
    blk = SWA_BLOCK
    n_loc = 3 * blk
    n_keys = n_loc + kx_ref.shape[0]
    tab_c = tc_ref[...]
    q = _rope16(q_ref[...].astype(F32), tab_c) * (SWA_HD ** -0.5)
    k_all = jnp.concatenate([_rope16(kp_ref[...].astype(F32), tp_ref[...]), _rope16(kc_ref[...].astype(F32), tab_c),
                             _rope16(kn_ref[...].astype(F32), tn_ref[...]), kx_ref[...].astype(F32)], axis=0)
    v_all = jnp.concatenate([vp_ref[...], vc_ref[...], vn_ref[...], vx_ref[...]], axis=0).astype(F32)
    assert SWA_WINDOW >= blk - 1
    qi = lax.broadcasted_iota(jnp.int32, (blk, blk), 0)
    kj = lax.broadcasted_iota(jnp.int32, (blk, blk), 1)
    valid_prev = jnp.logical_and(kj - blk - qi >= -SWA_WINDOW, qb > 0)
    valid_next = jnp.logical_and(kj + blk - qi <= SWA_WINDOW, qb < nb - 1)
    lane = lax.broadcasted_iota(jnp.int32, (n_keys, 128), 1)
    low = lane < SWA_HD
    out_low = lax.broadcasted_iota(jnp.int32, (blk, 128), 1) < SWA_HD

    def expo(scores, sink):
        sc = jnp.concatenate([jnp.where(valid_prev, scores[:, :blk], -jnp.inf), scores[:, blk:2 * blk],
                              jnp.where(valid_next, scores[:, 2 * blk:n_loc], -jnp.inf), scores[:, n_loc:]], axis=1)
        mx = jnp.maximum(jnp.max(sc, axis=-1, keepdims=True), sink)
        return _bf(jnp.exp(sc - mx)), jnp.exp(sink - mx)

    group = SWA_QH // SWA_KVH
    for g in range(SWA_KVH):
        col = slice((g // 2) * 128, (g // 2 + 1) * 128)
        native_low = g % 2 == 0
        sel = low if native_low else jnp.logical_not(low)
        k_nat = jnp.where(sel, k_all[:, col], 0.0)
        v_nat = jnp.where(sel, v_all[:, col], 0.0)
        k_oth = pltpu.roll(k_nat, SWA_HD, 1)
        v_oth = pltpu.roll(v_nat, SWA_HD, 1)
        k_lo, k_hi = (k_nat, k_oth) if native_low else (k_oth, k_nat)
        v_lo, v_hi = (v_nat, v_oth) if native_low else (v_oth, v_nat)
        v_lo = jnp.where(lane == SWA_HD, 1.0, v_lo)
        v_hi = jnp.where(lane == 0, 1.0, v_hi)
        k_lo, k_hi, v_lo, v_hi = _bf(k_lo), _bf(k_hi), _bf(v_lo), _bf(v_hi)
        for cidx in range(group // 2):
            qc = (g * group) // 2 + cidx
            qv = _bf(q[:, qc * 128:(qc + 1) * 128])
            e_lo, sink_lo = expo(_dot_nt(qv, k_lo), sink_ref[2 * qc])
            e_hi, sink_hi = expo(_dot_nt(qv, k_hi), sink_ref[2 * qc + 1])
            a_lo = _dot(e_lo, v_lo)
            a_hi = _dot(e_hi, v_hi)
            inv_lo = 1.0 / (a_lo[:, SWA_HD:SWA_HD + 1] + sink_lo)
            inv_hi = 1.0 / (a_hi[:, 0:1] + sink_hi)
            o_ref[:, qc * 128:(qc + 1) * 128] = jnp.where(out_low, a_lo * inv_lo, a_hi * inv_hi)


def _swa_core(rw, proj, sink, tab):
    blk = SWA_BLOCK
    nb = rw.n_lat // blk
    kcol = SWA_QD // SWA_KVD
    vcol = kcol + 1
    ctx_blk = lambda b: rw.lat_rows // rw.n_ctx + b
    prev = lambda q: jnp.maximum(q - 1, 0)
    nxt = lambda q: jnp.minimum(q + 1, nb - 1)
    kv = lambda col, f: pl.BlockSpec((blk, SWA_KVD), lambda b, q: (b * nb + f(q), col))
    same = lambda q: q
    return pl.pallas_call(
        _swa_kernel,
        grid=(rw.batch, nb),
        in_specs=[pl.BlockSpec(memory_space=pltpu.SMEM),
                  pl.BlockSpec((blk, SWA_QD), lambda b, q: (b * nb + q, 0)),
                  kv(kcol, prev), kv(kcol, same), kv(kcol, nxt),
                  kv(vcol, prev), kv(vcol, same), kv(vcol, nxt),
                  pl.BlockSpec((rw.n_ctx, SWA_KVD), lambda b, q: (ctx_blk(b), kcol)),
                  pl.BlockSpec((rw.n_ctx, SWA_KVD), lambda b, q: (ctx_blk(b), vcol)),
                  pl.BlockSpec((blk, 384), lambda b, q: (prev(q), 0)),
                  pl.BlockSpec((blk, 384), lambda b, q: (q, 0)),
                  pl.BlockSpec((blk, 384), lambda b, q: (nxt(q), 0))],
        out_specs=pl.BlockSpec((blk, SWA_QD), lambda b, q: (b * nb + q, 0)),
        out_shape=jax.ShapeDtypeStruct((rw.lat_rows, SWA_QD), F32),
        compiler_params=_cparams(("parallel", "parallel")),
    )(sink, proj, proj, proj, proj, proj, proj, proj, proj, proj, tab, tab, tab)


def _axial_angles(n_lat, dh):
    t = jnp.arange(n_lat)
    row = (t // GRID_W).astype(F32)
    col = (t % GRID_W).astype(F32)
    nf = dh // 4
    inv = ROPE_THETA ** (-jnp.arange(nf, dtype=F32) / nf)
    return row[:, None] * inv, col[:, None] * inv


def _ret_tables(n_lat):
    ar, ac = _axial_angles(n_lat, RET_DK)
    cos = jnp.concatenate([jnp.cos(ar), jnp.cos(ar), jnp.cos(ac), jnp.cos(ac)], axis=1)
    sin = jnp.concatenate([-jnp.sin(ar), jnp.sin(ar), -jnp.sin(ac), jnp.sin(ac)], axis=1)
    return cos, sin


def _swa_table(n_lat):
    ar, ac = _axial_angles(n_lat, SWA_HD)
    z = jnp.zeros_like(ar)
    cos = jnp.concatenate([jnp.cos(ar), jnp.cos(ar), jnp.cos(ac), jnp.cos(ac)], axis=1)
    s1 = jnp.concatenate([-jnp.sin(ar), z, -jnp.sin(ac), z], axis=1)
    s2 = jnp.concatenate([z, jnp.sin(ar), z, jnp.sin(ac)], axis=1)
    return jnp.concatenate([jnp.tile(cos, (1, 2)), jnp.tile(s1, (1, 2)), jnp.tile(s2, (1, 2))], axis=1)


def _gdn_layer(rw, layer, x, mod3, gain, w_in, conv_w, a_log, dt_bias, norm_g, w_out):
    n_ab = w_in.shape[1] - GDN_MAIN
    w_ab = _bf(jnp.pad(w_in[:, GDN_MAIN:], ((0, 0), (0, 128 - n_ab))))
    proj, ab = _inproj(rw, x, gain, mod3, layer, _bf(w_in), w_ab, n_cols=GDN_MAIN)
    hv = GDN_V_HEADS
    alog_row = jnp.zeros((1, 128), F32).at[0, 0:hv].set(a_log[0]).at[0, 2 * hv:3 * hv].set(a_log[1])
    dtb_row = jnp.zeros((1, 128), F32).at[0, 0:hv].set(dt_bias[0]).at[0, 2 * hv:3 * hv].set(dt_bias[1])
    cb = 2048
    row_spec = pl.BlockSpec((TM_CONV, 128), lambda i, j: (i, 0))
    vec_spec = pl.BlockSpec((1, 128), lambda i, j: (0, 0))
    qkv, g_all, b_all = pl.pallas_call(
        functools.partial(_gdn_conv_kernel, rw=rw),
        grid=(rw.rows // TM_CONV, GDN_CONV // cb),
        in_specs=_conv_specs(rw, cb) + [row_spec, vec_spec, vec_spec],
        out_specs=[pl.BlockSpec((TM_CONV, cb), lambda i, j: (i, j)), row_spec, row_spec],
        out_shape=[jax.ShapeDtypeStruct((rw.rows, GDN_CONV), F32),
                   jax.ShapeDtypeStruct((rw.rows, 128), F32),
                   jax.ShapeDtypeStruct((rw.rows, 128), F32)],
        scratch_shapes=[pltpu.VMEM((TM_CONV + 16, cb), F32)],
        compiler_params=_cparams(("parallel", "arbitrary")),
    )(proj, proj, proj, conv_w, ab, alog_row, dtb_row)
    g_dir = jnp.stack([g_all[:, 0:hv], g_all[:, 2 * hv:3 * hv]])
    b_dir = jnp.stack([b_all[:, hv:2 * hv], b_all[:, 3 * hv:4 * hv]])
    o = _gdn_core(rw, qkv, g_dir, b_dir)
    return _two_dir_out(_gdn_out_kernel, rw, layer, x, mod3, o, proj, GDN_CONV // GDN_VAL, norm_g, _bf(w_out))


def _ret_layer(rw, layer, x, mod3, gain, w_in, decay_logit, norm_g, w_out):
    proj = _inproj(rw, x, gain, mod3, layer, _bf(w_in))
    cos_tab, sin_tab = _ret_tables(rw.n_lat)
    o = _ret_core(rw, proj, cos_tab, sin_tab, decay_logit)
    return _two_dir_out(_ret_out_kernel, rw, layer, x, mod3, o, proj, (2 * RET_KEY + RET_VAL) // RET_VAL,
                        norm_g, _bf(w_out))


def _lru_layer(rw, layer, x, mod3, gain, w_in, conv_w, conv_b, w_gate, b_gate, lam, w_out):
    proj = _inproj(rw, x, gain, mod3, layer, _bf(w_in))
    h0 = _lru_scan(rw, proj, conv_w, conv_b, w_gate[0], b_gate[0], lam[0], rev=False)
    h1 = _lru_scan(rw, proj, conv_w, conv_b, w_gate[1], b_gate[1], lam[1], rev=True)
    tm = TM_OUT
    specs = [pl.BlockSpec((tm, LRU_W), lambda i: (i, 0)),
             pl.BlockSpec((tm, LRU_W), lambda i: (i, 0)),
             pl.BlockSpec((tm, LRU_W), lambda i: (i, 1))]
    return _outproj_call(_lru_out_kernel, rw, rw.rows, layer, x, mod3, specs, [h0, h1, proj], _bf(w_out), [])


def _swa_layer(rw, layer, x, mod3, gain, w_in, sink, w_out):
    proj = _inproj(rw, x, gain, mod3, layer, _bf(w_in))
    o = _swa_core(rw, proj, sink, _swa_table(rw.n_lat))
    specs = [pl.BlockSpec((TM_OUT, SWA_QD), lambda i: (i, 0))]
    return _outproj_call(_swa_out_kernel, rw, rw.lat_rows, layer, x, mod3, specs, [o], _bf(w_out), [])


def kernel(x, c, ctx, c_ctx, norm_mix_g, norm_ffn_g, w_mod, b_mod, w_ff1, w_ff2, norm_out_g, gdn_w_in, gdn_conv_w, gdn_a_log, gdn_dt_bias, gdn_norm_g, gdn_w_out, ret_w_in, ret_decay_logit, ret_norm_g, ret_w_out, lru_w_in, lru_conv_w, lru_conv_b, lru_w_gate, lru_b_gate, lru_lambda, lru_w_out, swa_w_in, swa_sink, swa_w_out):
    batch, n_lat, _ = x.shape
    n_ctx = ctx.shape[1]
    assert batch + 1 <= MOD_ROWS and n_lat % TM_IN == 0 and (batch * n_ctx) % TM_IN == 0
    assert w_mod.shape[0] == DEPTH and gdn_w_in.shape[0] == 1 and swa_w_in.shape[0] == 1
    rw = _Rows(batch, n_lat, n_ctx)
    xf = jnp.concatenate([x.reshape(rw.lat_rows, D), ctx.reshape(batch * n_ctx, D)], axis=0)
    cc = jnp.zeros((MOD_ROWS, D), F32).at[:batch].set(c).at[batch].set(c_ctx)
    mod3 = _adaln(cc, w_mod, b_mod)

    def mlp(layer, xin, n_rows, final):
        return _mlp(rw, xin, n_rows, norm_ffn_g[layer], mod3, layer, w_ff1, w_ff2,
                    norm_out_g, final)

    xf = _gdn_layer(rw, 0, xf, mod3, norm_mix_g[0], gdn_w_in[0], gdn_conv_w[0], gdn_a_log[0], gdn_dt_bias[0],
                    gdn_norm_g[0], gdn_w_out[0])
    xf = mlp(0, xf, rw.rows, False)
    xf = _ret_layer(rw, 1, xf, mod3, norm_mix_g[1], ret_w_in[0], ret_decay_logit[0], ret_norm_g[0], ret_w_out[0])
    xf = mlp(1, xf, rw.rows, False)
    xf = _lru_layer(rw, 2, xf, mod3, norm_mix_g[2], lru_w_in[0], lru_conv_w[0], lru_conv_b[0], lru_w_gate[0],
                    lru_b_gate[0], lru_lambda[0], lru_w_out[0])
    xf = mlp(2, xf, rw.rows, False)
    xl = _swa_layer(rw, 3, xf, mod3, norm_mix_g[3], swa_w_in[0], swa_sink[0], swa_w_out[0])
    out = mlp(3, xl, rw.lat_rows, True)
    return out.reshape(batch, n_lat, D)
```

```python
import functools
import math

import jax
import jax.numpy as jnp
from jax import lax
from jax.experimental import pallas as pl
from jax.experimental.pallas import tpu as pltpu

F32 = jnp.float32
BF16 = jnp.bfloat16

D = 1024
D_FF = 4 * D
EPS = 1e-6
DEPTH = 4
ROPE_THETA = 10000.0
GRID_W = 64
CHUNK = 64
MOD_ROWS = 16

GDN_QK_HEADS = 8
GDN_V_HEADS = 16
GDN_HD = 128
GDN_KEY = GDN_QK_HEADS * GDN_HD
GDN_VAL = GDN_V_HEADS * GDN_HD
GDN_CONV = 2 * GDN_KEY + GDN_VAL
GDN_MAIN = GDN_CONV + GDN_VAL

RET_HEADS = 4
RET_DK = 256
RET_DV = 512
RET_KEY = RET_HEADS * RET_DK
RET_VAL = RET_HEADS * RET_DV

LRU_W = 1280
LRU_BLOCKS = 10
LRU_BW = LRU_W // LRU_BLOCKS
LRU_C = 8.0

SWA_QH = 16
SWA_KVH = 4
SWA_HD = 64
SWA_BLOCK = 128
SWA_WINDOW = 128
SWA_QD = SWA_QH * SWA_HD
SWA_KVD = SWA_KVH * SWA_HD

TM_IN = 1024
TM_OUT = 512
TM_CONV = 256
HALO = 16
TN_IN_MAX = 1536
FF_CHUNK = 1024
RET_CHUNK = 256
GDN_BLOCK = 4 * CHUNK
VMEM_LIMIT = 48 * 1024 * 1024
MLP_VMEM_LIMIT = 56 * 1024 * 1024


def _dot(a, b):
    return jnp.dot(a, b, preferred_element_type=F32)


def _dot_nt(a, b):
    return lax.dot_general(a, b, (((1,), (1,)), ((), ())), preferred_element_type=F32)


def _dot_tn(a, b):
    return lax.dot_general(a, b, (((0,), (0,)), ((), ())), preferred_element_type=F32)


def _bf(x):
    return x.astype(BF16)


def _split3(x):
    hi = _bf(x)
    r = x - hi.astype(F32)
    mid = _bf(r)
    lo = _bf(r - mid.astype(F32))
    return hi, mid, lo


def _silu(x):
    return x * jax.nn.sigmoid(x)


def _softplus(x):
    return jnp.maximum(x, 0.0) + jnp.log1p(jnp.exp(-jnp.abs(x)))


def _gelu_tanh(x):
    cdf = 0.5 * (1.0 + jnp.tanh(math.sqrt(2.0 / math.pi) * (x + 0.044715 * (x * x * x))))
    return x * cdf


def _cparams(sem, vmem=None):
    return pltpu.CompilerParams(dimension_semantics=sem, vmem_limit_bytes=VMEM_LIMIT if vmem is None else vmem)


class _Rows:
    def __init__(self, batch, n_lat, n_ctx):
        self.batch, self.n_lat, self.n_ctx = batch, n_lat, n_ctx
        self.lat_rows = batch * n_lat
        self.rows = self.lat_rows + batch * n_ctx
        self.n_steps = (n_lat + n_ctx) // CHUNK
        self.ctx_steps = n_ctx // CHUNK
        self.lat_steps = n_lat // CHUNK

    def mod_row(self, i, tm):
        return jnp.where(i < self.lat_rows // tm, i // (self.n_lat // tm), self.batch)

    def chunk_block(self, b, d, s, blk):
        nc, nl = self.n_ctx // blk, self.n_lat // blk
        c_ctx = jnp.where(d == 0, s, nc - 1 - s)
        c_lat = jnp.where(d == 0, s - nc, nl - 1 - (s - nc))
        return jnp.where(s < nc, self.lat_rows // blk + b * nc + c_ctx, b * nl + c_lat)

    def lat_chunk(self, d, s, blk):
        nc, nl = self.n_ctx // blk, self.n_lat // blk
        return jnp.where(s < nc, 0, jnp.where(d == 0, s - nc, nl - 1 - (s - nc)))


def _adaln_kernel(c_ref, w_ref, b_ref, o_ref):
    s = _silu(c_ref[...])
    w = w_ref[0]
    s_hi = _bf(s)
    s_lo = _bf(s - s_hi.astype(F32))
    w_hi = _bf(w)
    w_lo = _bf(w - w_hi.astype(F32))
    y = _dot(s_hi, w_hi) + (_dot(s_lo, w_hi) + _dot(s_hi, w_lo))
    o_ref[0] = y + b_ref[0]


def _adaln(cc, w_mod, b_mod):
    depth, _, n = w_mod.shape
    tn = 1024
    out = pl.pallas_call(
        _adaln_kernel,
        grid=(depth, n // tn),
        in_specs=[pl.BlockSpec((MOD_ROWS, D), lambda l, j: (0, 0)),
                  pl.BlockSpec((1, D, tn), lambda l, j: (l, 0, j)),
                  pl.BlockSpec((1, 1, tn), lambda l, j: (l, 0, j))],
        out_specs=pl.BlockSpec((1, MOD_ROWS, tn), lambda l, j: (l, 0, j)),
        out_shape=jax.ShapeDtypeStruct((depth, MOD_ROWS, n), F32),
        compiler_params=_cparams(("parallel", "parallel")),
    )(cc, w_mod, b_mod.reshape(depth, 1, n))
    return out.reshape(depth * MOD_ROWS * 6, 1, D)


def _mod_spec(rw, layer, which, tm):
    base = layer * MOD_ROWS * 6
    return pl.BlockSpec((1, 1, D), lambda i, *_: (base + rw.mod_row(i, tm) * 6 + which, 0, 0))


def _norm_mod(x, gain, shift, scale):
    y = x * lax.rsqrt(jnp.mean(x * x, axis=-1, keepdims=True) + EPS) * gain
    return y * (1.0 + scale) + shift


def _inproj_kernel(x_ref, g_ref, sh_ref, sc_ref, w_ref, *rest, has_extra):
    if has_extra:
        w2_ref, o_ref, o2_ref, h_ref = rest
    else:
        o_ref, h_ref = rest

    @pl.when(pl.program_id(1) == 0)
    def _():
        h = _bf(_norm_mod(x_ref[...], g_ref[...], sh_ref[0], sc_ref[0]))
        h_ref[...] = h
        if has_extra:
            o2_ref[...] = _dot(h, w2_ref[...])

    o_ref[...] = _bf(_dot(h_ref[...], w_ref[...]))


def _inproj(rw, x, gain, mod3, layer, w, w_extra=None, n_cols=None):
    n = w.shape[1] if n_cols is None else n_cols
    tm = TM_IN
    tn = max(t for t in range(256, TN_IN_MAX + 1, 256) if n % t == 0)
    has_extra = w_extra is not None
    in_specs = [pl.BlockSpec((tm, D), lambda i, j: (i, 0)),
                pl.BlockSpec((1, D), lambda i, j: (0, 0)),
                _mod_spec(rw, layer, 0, tm),
                _mod_spec(rw, layer, 1, tm),
                pl.BlockSpec((D, tn), lambda i, j: (0, j))]
    out_specs = [pl.BlockSpec((tm, tn), lambda i, j: (i, j))]
    out_shape = [jax.ShapeDtypeStruct((rw.rows, n), BF16)]
    args = [x, gain.reshape(1, D), mod3, mod3, w]
    if has_extra:
        ne = w_extra.shape[1]
        in_specs.append(pl.BlockSpec((D, ne), lambda i, j: (0, 0)))
        out_specs.append(pl.BlockSpec((tm, ne), lambda i, j: (i, 0)))
        out_shape.append(jax.ShapeDtypeStruct((rw.rows, ne), F32))
        args.append(w_extra)
    res = pl.pallas_call(
        functools.partial(_inproj_kernel, has_extra=has_extra),
        grid=(rw.rows // tm, n // tn),
        in_specs=in_specs, out_specs=out_specs, out_shape=out_shape,
        scratch_shapes=[pltpu.VMEM((tm, D), BF16)],
        compiler_params=_cparams(("parallel", "arbitrary")),
    )(*args)
    return res if has_extra else res[0]


def _mlp_kernel(x_ref, g_ref, sh_ref, sc_ref, gt_ref, w1_ref, w2_ref, go_ref, o_ref, h_ref, acc_ref,
                *, final_norm):
    k = pl.program_id(1)

    @pl.when(k == 0)
    def _():
        h_ref[...] = _bf(_norm_mod(x_ref[...], g_ref[...], sh_ref[0], sc_ref[0]))
        acc_ref[...] = jnp.zeros_like(acc_ref)

    a = jnp.square(jnp.maximum(_dot(h_ref[...], _bf(w1_ref[...])), 0.0))
    acc_ref[...] += _dot(_bf(a), _bf(w2_ref[...]))

    @pl.when(k == pl.num_programs(1) - 1)
    def _():
        y = x_ref[...] + gt_ref[0] * acc_ref[...]
        if final_norm:
            y = y * lax.rsqrt(jnp.mean(y * y, axis=-1, keepdims=True) + EPS) * go_ref[...]
        o_ref[...] = y


def _mlp(rw, x, n_rows, gain, mod3, layer, w1, w2, out_gain, final_norm):
    tm, ck = TM_IN, FF_CHUNK
    return pl.pallas_call(
        functools.partial(_mlp_kernel, final_norm=final_norm),
        grid=(n_rows // tm, D_FF // ck),
        in_specs=[pl.BlockSpec((tm, D), lambda i, k: (i, 0)),
                  pl.BlockSpec((1, D), lambda i, k: (0, 0)),
                  _mod_spec(rw, layer, 3, tm),
                  _mod_spec(rw, layer, 4, tm),
                  _mod_spec(rw, layer, 5, tm),
                  pl.BlockSpec((None, D, ck), lambda i, k: (layer, 0, k)),
                  pl.BlockSpec((None, ck, D), lambda i, k: (layer, k, 0)),
                  pl.BlockSpec((1, D), lambda i, k: (0, 0))],
        out_specs=pl.BlockSpec((tm, D), lambda i, k: (i, 0)),
        out_shape=jax.ShapeDtypeStruct((n_rows, D), F32),
        scratch_shapes=[pltpu.VMEM((tm, D), BF16), pltpu.VMEM((tm, D), F32)],
        compiler_params=_cparams(("parallel", "arbitrary"), MLP_VMEM_LIMIT),
    )(x, gain.reshape(1, D), mod3, mod3, mod3, w1, w2, out_gain.reshape(1, D))


OUT_SUB = 256


def _row_blocks(n_rows):
    return [slice(r, r + OUT_SUB) for r in range(0, n_rows, OUT_SUB)]


def _gdn_out_kernel(x_ref, gt_ref, o0_ref, o1_ref, z_ref, ng_ref, w_ref, out_ref, a_ref):
    ng = ng_ref[...]
    for rows in _row_blocks(x_ref.shape[0]):
        for h in range(GDN_V_HEADS):
            sl = slice(h * GDN_HD, (h + 1) * GDN_HD)
            o = o0_ref[0, rows, sl].astype(F32) + o1_ref[0, rows, sl].astype(F32)
            y = o * lax.rsqrt(jnp.mean(o * o, axis=-1, keepdims=True) + EPS) * ng
            a_ref[rows, sl] = _bf(y * _silu(z_ref[rows, sl].astype(F32)))
        out_ref[rows] = x_ref[rows] + gt_ref[0] * _dot(a_ref[rows], w_ref[...])


def _ret_out_kernel(x_ref, gt_ref, o0_ref, o1_ref, z_ref, ng_ref, w_ref, out_ref, a_ref):
    for rows in _row_blocks(x_ref.shape[0]):
        for h in range(RET_HEADS):
            sl = slice(h * RET_DV, (h + 1) * RET_DV)
            o = o0_ref[0, rows, sl].astype(F32) + o1_ref[0, rows, sl].astype(F32)
            mu = jnp.mean(o, axis=-1, keepdims=True)
            oc = o - mu
            var = jnp.mean(oc * oc, axis=-1, keepdims=True)
            y = oc * lax.rsqrt(var + EPS) * ng_ref[:, sl]
            a_ref[rows, sl] = _bf(y * _silu(z_ref[rows, sl].astype(F32)))
        out_ref[rows] = x_ref[rows] + gt_ref[0] * _dot(a_ref[rows], w_ref[...])


def _lru_out_kernel(x_ref, gt_ref, h0_ref, h1_ref, gb_ref, w_ref, out_ref):
    for rows in _row_blocks(x_ref.shape[0]):
        a = (h0_ref[rows] + h1_ref[rows]) * _gelu_tanh(gb_ref[rows].astype(F32))
        out_ref[rows] = x_ref[rows] + gt_ref[0] * _dot(_bf(a), w_ref[...])


def _swa_out_kernel(x_ref, gt_ref, o_ref, w_ref, out_ref):
    out_ref[...] = x_ref[...] + gt_ref[0] * _dot(_bf(o_ref[...]), w_ref[...])


def _outproj_call(kern, rw, n_rows, layer, x, mod3, extra_specs, extra_args, w, scratch):
    tm = TM_OUT
    k = w.shape[0]
    return pl.pallas_call(
        kern,
        grid=(n_rows // tm,),
        in_specs=[pl.BlockSpec((tm, D), lambda i: (i, 0)), _mod_spec(rw, layer, 2, tm)] + extra_specs
                 + [pl.BlockSpec((k, D), lambda i: (0, 0))],
        out_specs=pl.BlockSpec((tm, D), lambda i: (i, 0)),
        out_shape=jax.ShapeDtypeStruct((n_rows, D), F32),
        scratch_shapes=scratch,
        compiler_params=_cparams(("parallel",)),
    )(x, mod3, *extra_args, w)


def _two_dir_out(kern, rw, layer, x, mod3, o, proj, z_block, norm_g, w):
    tm, k = TM_OUT, w.shape[0]
    ng = norm_g.reshape(1, -1)
    specs = [pl.BlockSpec((1, tm, k), lambda i: (0, i, 0)),
             pl.BlockSpec((1, tm, k), lambda i: (1, i, 0)),
             pl.BlockSpec((tm, k), lambda i: (i, z_block)),
             pl.BlockSpec(ng.shape, lambda i: (0, 0))]
    return _outproj_call(kern, rw, rw.rows, layer, x, mod3, specs, [o, o, proj, ng], w,
                         [pltpu.VMEM((tm, k), BF16)])


def _conv_core(x_ref, p_ref, n_ref, w_ref, first, last, pad_ref=None):
    t = x_ref.shape[0]
    x = x_ref[...].astype(F32)
    prev = p_ref[...].astype(F32)[HALO - 8:] * jnp.where(first, 0.0, 1.0)
    nxt = n_ref[...].astype(F32)[:8] * jnp.where(last, 0.0, 1.0)
    if pad_ref is not None:
        pad_ref[0:8] = prev
        pad_ref[8:t + 8] = x
        pad_ref[t + 8:t + 16] = nxt
        xm2, xm1, xp1 = pad_ref[6:t + 6], pad_ref[7:t + 7], pad_ref[9:t + 9]
    else:
        row8 = lax.broadcasted_iota(jnp.int32, prev.shape, 0)
        r2 = pltpu.roll(x, 2, 0)
        r1 = pltpu.roll(x, 1, 0)
        rp = pltpu.roll(x, t - 1, 0)
        f2 = jnp.where(row8 < 2, pltpu.roll(prev, 2, 0), r2[:8])
        f1 = jnp.where(row8 < 1, pltpu.roll(prev, 1, 0), r1[:8])
        l1 = jnp.where(row8 == 7, pltpu.roll(nxt, 7, 0), rp[t - 8:])
        xm2 = jnp.concatenate([f2, r2[8:]], axis=0)
        xm1 = jnp.concatenate([f1, r1[8:]], axis=0)
        xp1 = jnp.concatenate([rp[:t - 8], l1], axis=0)
    w = w_ref[...]
    return w[0:1] * xm2 + w[1:2] * xm1 + w[2:3] * x + w[3:4] * xp1


def _seg_flags(rw, i):
    per = rw.n_lat // TM_CONV
    is_ctx = i >= rw.lat_rows // TM_CONV
    assert rw.n_ctx == TM_CONV
    first = jnp.logical_or(is_ctx, i % per == 0)
    last = jnp.logical_or(is_ctx, i % per == per - 1)
    return first, last


def _gdn_conv_kernel(x_ref, p_ref, n_ref, w_ref, ab_ref, alog_ref, dtb_ref, o_ref, g_ref, b_ref, pad_ref, *, rw):
    i, j = pl.program_id(0), pl.program_id(1)
    first, last = _seg_flags(rw, i)
    s = _silu(_conv_core(x_ref, p_ref, n_ref, w_ref, first, last, pad_ref))
    width = s.shape[1]
    qk_blocks = 2 * GDN_KEY // width

    @pl.when(j < qk_blocks)
    def _():
        for h in range(width // GDN_HD):
            sl = slice(h * GDN_HD, (h + 1) * GDN_HD)
            sh = s[:, sl]
            is_q = j * width + h * GDN_HD < GDN_KEY
            scale = jnp.where(is_q, GDN_HD ** -0.5, 1.0)
            o_ref[:, sl] = sh * (lax.rsqrt(jnp.sum(sh * sh, axis=-1, keepdims=True) + EPS) * scale)

    @pl.when(j >= qk_blocks)
    def _():
        o_ref[...] = s

    @pl.when(j == 0)
    def _():
        ab = ab_ref[...]
        g_ref[...] = -jnp.exp(alog_ref[...]) * _softplus(ab + dtb_ref[...])
        b_ref[...] = jax.nn.sigmoid(ab)


def _conv_specs(rw, cb):
    th = TM_CONV // HALO
    nh = rw.rows // HALO
    return [pl.BlockSpec((TM_CONV, cb), lambda i, j: (i, j)),
            pl.BlockSpec((HALO, cb), lambda i, j: (jnp.maximum(i * th - 1, 0), j)),
            pl.BlockSpec((HALO, cb), lambda i, j: (jnp.minimum((i + 1) * th, nh - 1), j)),
            pl.BlockSpec((4, cb), lambda i, j: (0, j))]


def _dir_masks(d, n):
    ii = lax.broadcasted_iota(jnp.int32, (n, n), 0)
    jj = lax.broadcasted_iota(jnp.int32, (n, n), 1)
    t = (ii - jj) * jnp.where(d == 0, 1, -1)
    return t, t >= 0, t > 0


def _gdn_core_kernel(qkv_ref, g_ref, b_ref, o_ref, s_ref):
    d, s = pl.program_id(1), pl.program_id(2)

    @pl.when(s == 0)
    def _():
        s_ref[...] = jnp.zeros_like(s_ref)

    c, hd = CHUNK, GDN_HD
    rep = GDN_V_HEADS // GDN_QK_HEADS
    heads = range(GDN_V_HEADS)
    n_sub = qkv_ref.shape[0] // c
    subs = range(n_sub)
    offs = [pl.multiple_of(jnp.where(d == 0, i, n_sub - 1 - i) * c, c) for i in subs]
    chains = [(i, h) for i in subs for h in heads]
    t, incl, strict = _dir_masks(d, c)
    cum = _bf(jnp.where(incl, 1.0, 0.0))
    cum_t = _bf(jnp.where(t <= 0, 1.0, 0.0))
    g = [g_ref[0, pl.ds(offs[i], c), :] for i in subs]
    beta = [b_ref[0, pl.ds(offs[i], c), :] for i in subs]
    parts = [_split3(g[i]) for i in subs]
    gc = [sum(_dot(cum, p) for p in parts[i]) for i in subs]
    gr = [sum(_dot_tn(p, cum_t) for p in parts[i]) for i in subs]
    gtot = [jnp.sum(g[i], axis=0, keepdims=True) for i in subs]

    def rows(i, lo):
        return qkv_ref[pl.ds(offs[i], c), lo:lo + hd]

    q_l = [[rows(i, hq * hd) for hq in range(GDN_QK_HEADS)] for i in subs]
    k_l = [[rows(i, GDN_KEY + hq * hd) for hq in range(GDN_QK_HEADS)] for i in subs]
    gram = [[_dot_nt(_bf(jnp.concatenate([k, q], axis=0)), _bf(k)) for q, k in zip(q_l[i], k_l[i])]
            for i in subs]
    gcol = [gc[i][:, h:h + 1] for i, h in chains]
    bcol = [beta[i][:, h:h + 1] for i, h in chains]
    dec = [jnp.where(incl, jnp.exp(jnp.where(incl, gcol[n] - gr[i][h:h + 1, :], 0.0)), 0.0)
           for n, (i, h) in enumerate(chains)]
    eg = [jnp.exp(x) for x in gcol]
    m = [jnp.where(strict, -(gram[i][h // rep][:c] * dec[n] * bcol[n]), 0.0) for n, (i, h) in enumerate(chains)]
    rhs = [jnp.concatenate([k_l[i][h // rep] * (bcol[n] * eg[n]), rows(i, 2 * GDN_KEY + h * hd) * bcol[n]], axis=1)
           for n, (i, h) in enumerate(chains)]
    ids = range(len(chains))

    m_hi = [_bf(x) for x in m]
    zeros_cc = jnp.zeros((c, 2 * c), BF16)
    lane_cc = lax.broadcasted_iota(jnp.int32, (c, 2 * c), 1)
    row_cc = lax.broadcasted_iota(jnp.int32, (c, 2 * c), 0)
    eye_hi = _bf(jnp.where(lane_cc == row_cc + c, 1.0, 0.0))
    is_hi = lane_cc >= c
    cb = [jnp.concatenate([m_hi[n], jnp.zeros((c, c), BF16)], axis=1) + eye_hi for n in ids]
    for _ in range(6):
        r = [_dot(cb[n], jnp.concatenate([cb[n], zeros_cc], axis=0)) for n in ids]
        cb = [_bf(r[n]) + jnp.where(is_hi, cb[n], jnp.zeros_like(cb[n])) for n in ids]
    nb = [cb[n] - eye_hi for n in ids]
    zeros_rhs = jnp.zeros((c, 2 * hd), BF16)

    def apply_n(y):
        return [_dot(nb[n], jnp.concatenate([zeros_rhs, _bf(y[n])], axis=0)) for n in ids]

    nr = apply_n(rhs)
    x = [rhs[n] + nr[n] for n in ids]
    resid = [_dot(m_hi[n], _bf(x[n])) - nr[n] for n in ids]
    nr = apply_n(resid)
    x = [x[n] + (resid[n] + nr[n]) for n in ids]
    qk = [_bf(gram[i][h // rep][c:] * dec[n]) for n, (i, h) in enumerate(chains)]
    gt = [gtot[i][:, h:h + 1] for i, h in chains]
    kd = [_bf(k_l[i][h // rep] * jnp.exp(gt[n] - gcol[n])) for n, (i, h) in enumerate(chains)]
    wq = [_bf(jnp.concatenate([x[n][:, :hd], q_l[i][h // rep] * eg[n]], axis=0)) for n, (i, h) in enumerate(chains)]
    decay = [jnp.exp(x) for x in gt]

    st = [s_ref[h] for h in heads]
    for i in subs:
        base = i * GDN_V_HEADS
        r = [_dot(wq[base + h], _bf(st[h])) for h in heads]
        unb = [_bf(x[base + h][:, hd:] - r[h][:c]) for h in heads]
        for h in heads:
            o_ref[0, pl.ds(offs[i], c), h * hd:(h + 1) * hd] = _bf(r[h][c:] + _dot(qk[base + h], unb[h]))
        st = [st[h] * decay[base + h] + _dot_tn(kd[base + h], unb[h]) for h in heads]
    for h in heads:
        s_ref[h] = st[h]


def _gdn_core(rw, qkv, g_dir, b_dir):
    blk = GDN_BLOCK
    rb = lambda b, d, s: rw.chunk_block(b, d, s, blk)
    return pl.pallas_call(
        _gdn_core_kernel,
        grid=(rw.batch, 2, (rw.n_lat + rw.n_ctx) // blk),
        in_specs=[pl.BlockSpec((blk, GDN_CONV), lambda b, d, s: (rb(b, d, s), 0)),
                  pl.BlockSpec((1, blk, GDN_V_HEADS), lambda b, d, s: (d, rb(b, d, s), 0)),
                  pl.BlockSpec((1, blk, GDN_V_HEADS), lambda b, d, s: (d, rb(b, d, s), 0))],
        out_specs=pl.BlockSpec((1, blk, GDN_VAL), lambda b, d, s: (d, rb(b, d, s), 0)),
        out_shape=jax.ShapeDtypeStruct((2, rw.rows, GDN_VAL), BF16),
        scratch_shapes=[pltpu.VMEM((GDN_V_HEADS, GDN_HD, GDN_HD), F32)],
        compiler_params=_cparams(("parallel", "parallel", "arbitrary")),
    )(qkv, g_dir, b_dir)


def _rope_half(x, cos, sin_signed):
    return x * cos + pltpu.roll(x, x.shape[1] // 2, 1) * sin_signed


def _ret_core_kernel(q_ref, k_ref, v_ref, cos_ref, sin_ref, dl_ref, o_ref, s_ref, *, ctx_steps):
    d, s = pl.program_id(1), pl.program_id(2)

    @pl.when(s == 0)
    def _():
        s_ref[...] = jnp.zeros_like(s_ref)

    c = q_ref.shape[0]
    heads = range(RET_HEADS)
    is_lat = s >= ctx_steps
    cos = jnp.where(is_lat, cos_ref[...], 1.0)
    sin = jnp.where(is_lat, sin_ref[...], 0.0)
    t, incl, _ = _dir_masks(d, c)
    tf = t.astype(F32)
    pos = lax.broadcasted_iota(jnp.int32, (c, 1), 0)
    ridx = jnp.where(d == 0, pos + 1, c - pos).astype(F32)
    log_gamma = -_softplus(-dl_ref[0])

    def rope(x):
        hw = RET_DK // 2
        return jnp.concatenate([_rope_half(x[:, :hw], cos[:, :hw], sin[:, :hw]),
                                _rope_half(x[:, hw:], cos[:, hw:], sin[:, hw:])], axis=1)

    lg = [log_gamma[:, h:h + 1] for h in heads]
    q = [rope(q_ref[:, h * RET_DK:(h + 1) * RET_DK].astype(F32)) * (RET_DK ** -0.5) for h in heads]
    k = [rope(k_ref[:, h * RET_DK:(h + 1) * RET_DK].astype(F32)) for h in heads]
    vb = [v_ref[:, h * RET_DV:(h + 1) * RET_DV] for h in heads]
    gcum = [lg[h] * ridx for h in heads]
    dec = [jnp.where(incl, jnp.exp(jnp.where(incl, lg[h] * tf, 0.0)), 0.0) for h in heads]
    qk = [_bf(_dot_nt(_bf(q[h]), _bf(k[h])) * dec[h]) for h in heads]
    st = [s_ref[h] for h in heads]
    for h in heads:
        o_ref[0, :, h * RET_DV:(h + 1) * RET_DV] = _bf(
            _dot(_bf(q[h] * jnp.exp(gcum[h])), _bf(st[h])) + _dot(qk[h], vb[h]))
    gtot = [lg[h] * float(c) for h in heads]
    kd = [_bf(k[h] * jnp.exp(gtot[h] - gcum[h])) for h in heads]
    for h in heads:
        s_ref[h] = st[h] * jnp.exp(gtot[h]) + _dot_tn(kd[h], vb[h])


def _ret_core(rw, proj, cos_tab, sin_tab, decay_logit):
    c = RET_CHUNK
    rb = lambda b, d, s: rw.chunk_block(b, d, s, c)
    dl = jnp.zeros((2, 1, 128), F32).at[:, 0, :RET_HEADS].set(decay_logit)
    return pl.pallas_call(
        functools.partial(_ret_core_kernel, ctx_steps=rw.n_ctx // c),
        grid=(rw.batch, 2, (rw.n_lat + rw.n_ctx) // c),
        in_specs=[pl.BlockSpec((c, RET_KEY), lambda b, d, s: (rb(b, d, s), 0)),
                  pl.BlockSpec((c, RET_KEY), lambda b, d, s: (rb(b, d, s), 1)),
                  pl.BlockSpec((c, RET_VAL), lambda b, d, s: (rb(b, d, s), 1)),
                  pl.BlockSpec((c, RET_DK), lambda b, d, s: (rw.lat_chunk(d, s, c), 0)),
                  pl.BlockSpec((c, RET_DK), lambda b, d, s: (rw.lat_chunk(d, s, c), 0)),
                  pl.BlockSpec((1, 1, 128), lambda b, d, s: (d, 0, 0))],
        out_specs=pl.BlockSpec((1, c, RET_VAL), lambda b, d, s: (d, rb(b, d, s), 0)),
        out_shape=jax.ShapeDtypeStruct((2, rw.rows, RET_VAL), BF16),
        scratch_shapes=[pltpu.VMEM((RET_HEADS, RET_DK, RET_DV), F32)],
        compiler_params=_cparams(("parallel", "parallel", "arbitrary")),
    )(proj, proj, proj, cos_tab, sin_tab, dl)


def _lru_scan_kernel(x_ref, p_ref, n_ref, cw_ref, cb_ref, wg_ref, bg_ref, lam_ref, h_ref, carry_ref, *, rw, rev):
    @pl.when(pl.program_id(1) == 0)
    def _():
        carry_ref[...] = jnp.zeros_like(carry_ref)

    t = x_ref.shape[0]
    tile = rw.chunk_block(pl.program_id(0), 1 if rev else 0, pl.program_id(1), t)
    first, last = _seg_flags(rw, tile)
    xs = _conv_core(x_ref, p_ref, n_ref, cw_ref, first, last) + cb_ref[...]
    row8 = lax.broadcasted_iota(jnp.int32, (t // 8, 8, LRU_BW), 1)
    sp = _softplus(-lam_ref[...])
    for n in range(LRU_BLOCKS):
        sl = slice(n * LRU_BW, (n + 1) * LRU_BW)
        xn = xs[:, sl]
        gates = jax.nn.sigmoid(_dot(_bf(xn), wg_ref[n]) + bg_ref[n])
        log_a = -LRU_C * gates[:, :LRU_BW] * sp[:, sl]
        a = jnp.exp(log_a)
        b = jnp.sqrt(-jnp.tanh(log_a) * (a * a + 1.0)) * gates[:, LRU_BW:] * xn
        groups = t // 8
        a3 = a.reshape(groups, 8, LRU_BW)
        b3 = b.reshape(groups, 8, LRU_BW)
        for k in (1, 2, 4):
            if rev:
                keep = row8 < 8 - k
                shift = 8 - k
            else:
                keep = row8 >= k
                shift = k
            sa = jnp.where(keep, pltpu.roll(a3, shift, 1), 1.0)
            sb = jnp.where(keep, pltpu.roll(b3, shift, 1), 0.0)
            b3 = b3 + a3 * sb
            a3 = a3 * sa
        carry = carry_ref[:, sl]
        for gi in (reversed(range(groups)) if rev else range(groups)):
            hg = a3[gi] * carry + b3[gi]
            carry = hg[0:1] if rev else hg[7:8]
            h_ref[gi * 8:(gi + 1) * 8, sl] = hg
        carry_ref[:, sl] = carry


def _lru_scan(rw, proj, conv_w, conv_b, w_gate, b_gate, lam, rev):
    t = TM_CONV
    th = t // HALO
    nh = rw.rows // HALO
    d = 1 if rev else 0
    rb = lambda b, s: rw.chunk_block(b, d, s, t)
    return pl.pallas_call(
        functools.partial(_lru_scan_kernel, rw=rw, rev=rev),
        grid=(rw.batch, (rw.n_lat + rw.n_ctx) // t),
        in_specs=[pl.BlockSpec((t, LRU_W), lambda b, s: (rb(b, s), 0)),
                  pl.BlockSpec((HALO, LRU_W), lambda b, s: (jnp.maximum(rb(b, s) * th - 1, 0), 0)),
                  pl.BlockSpec((HALO, LRU_W), lambda b, s: (jnp.minimum((rb(b, s) + 1) * th, nh - 1), 0)),
                  pl.BlockSpec((4, LRU_W), lambda b, s: (0, 0)),
                  pl.BlockSpec((1, LRU_W), lambda b, s: (0, 0)),
                  pl.BlockSpec((LRU_BLOCKS, LRU_BW, 2 * LRU_BW), lambda b, s: (0, 0, 0)),
                  pl.BlockSpec((LRU_BLOCKS, 1, 2 * LRU_BW), lambda b, s: (0, 0, 0)),
                  pl.BlockSpec((1, LRU_W), lambda b, s: (0, 0))],
        out_specs=pl.BlockSpec((t, LRU_W), lambda b, s: (rb(b, s), 0)),
        out_shape=jax.ShapeDtypeStruct((rw.rows, LRU_W), F32),
        scratch_shapes=[pltpu.VMEM((1, LRU_W), F32)],
        compiler_params=_cparams(("parallel", "arbitrary")),
    )(proj, proj, proj, conv_w, conv_b.reshape(1, LRU_W), _bf(w_gate),
      b_gate.reshape(LRU_BLOCKS, 1, 2 * LRU_BW), lam.reshape(1, LRU_W))


def _rope16(x, tab):
    cos, s1, s2 = tab[:, :128], tab[:, 128:256], tab[:, 256:]
    cols = []
    for cidx in range(x.shape[1] // 128):
        xc = x[:, cidx * 128:(cidx + 1) * 128]
        cols.append(xc * cos + pltpu.roll(xc, 112, 1) * s1 + pltpu.roll(xc, 16, 1) * s2)
    return cols[0] if len(cols) == 1 else jnp.concatenate(cols, axis=1)


def _swa_kernel(sink_ref, q_ref, kp_ref, kc_ref, kn_ref, vp_ref, vc_ref, vn_ref, kx_ref, vx_ref,
                tp_ref, tc_ref, tn_ref, o_ref):
    qb = pl.program_id(1)
    nb = pl.num_programs(1)
    blk = SWA_BLOCK
    n_loc = 3 * blk
    n_keys = n_loc + kx_ref.shape[0]
    tab_c = tc_ref[...]
    q = _rope16(q_ref[...].astype(F32), tab_c) * (SWA_HD ** -0.5)
    k_all = jnp.concatenate([_rope16(kp_ref[...].astype(F32), tp_ref[...]), _rope16(kc_ref[...].astype(F32), tab_c),
                             _rope16(kn_ref[...].astype(F32), tn_ref[...]), kx_ref[...].astype(F32)], axis=0)
    v_all = jnp.concatenate([vp_ref[...], vc_ref[...], vn_ref[...], vx_ref[...]], axis=0).astype(F32)
    assert SWA_WINDOW >= blk - 1
    qi = lax.broadcasted_iota(jnp.int32, (blk, blk), 0)
    kj = lax.broadcasted_iota(jnp.int32, (blk, blk), 1)
    valid_prev = jnp.logical_and(kj - blk - qi >= -SWA_WINDOW, qb > 0)
    valid_next = jnp.logical_and(kj + blk - qi <= SWA_WINDOW, qb < nb - 1)
    lane = lax.broadcasted_iota(jnp.int32, (n_keys, 128), 1)
    low = lane < SWA_HD
    out_low = lax.broadcasted_iota(jnp.int32, (blk, 128), 1) < SWA_HD

    def expo(scores, sink):
        sc = jnp.concatenate([jnp.where(valid_prev, scores[:, :blk], -jnp.inf), scores[:, blk:2 * blk],
                              jnp.where(valid_next, scores[:, 2 * blk:n_loc], -jnp.inf), scores[:, n_loc:]], axis=1)
        mx = jnp.maximum(jnp.max(sc, axis=-1, keepdims=True), sink)
        return _bf(jnp.exp(sc - mx)), jnp.exp(sink - mx)

    group = SWA_QH // SWA_KVH
    for g in range(SWA_KVH):
        col = slice((g // 2) * 128, (g // 2 + 1) * 128)
        native_low = g % 2 == 0
        sel = low if native_low else jnp.logical_not(low)
        k_nat = jnp.where(sel, k_all[:, col], 0.0)
        v_nat = jnp.where(sel, v_all[:, col], 0.0)
        k_oth = pltpu.roll(k_nat, SWA_HD, 1)
        v_oth = pltpu.roll(v_nat, SWA_HD, 1)
        k_lo, k_hi = (k_nat, k_oth) if native_low else (k_oth, k_nat)
        v_lo, v_hi = (v_nat, v_oth) if native_low else (v_oth, v_nat)
        v_lo = jnp.where(lane == SWA_HD, 1.0, v_lo)
        v_hi = jnp.where(lane == 0, 1.0, v_hi)
        k_lo, k_hi, v_lo, v_hi = _bf(k_lo), _bf(k_hi), _bf(v_lo), _bf(v_hi)
        for cidx in range(group // 2):
            qc = (g * group) // 2 + cidx
            qv = _bf(q[:, qc * 128:(qc + 1) * 128])
            e_lo, sink_lo = expo(_dot_nt(qv, k_lo), sink_ref[2 * qc])
            e_hi, sink_hi = expo(_dot_nt(qv, k_hi), sink_ref[2 * qc + 1])
            a_lo = _dot(e_lo, v_lo)
            a_hi = _dot(e_hi, v_hi)
            inv_lo = 1.0 / (a_lo[:, SWA_HD:SWA_HD + 1] + sink_lo)
            inv_hi = 1.0 / (a_hi[:, 0:1] + sink_hi)
            o_ref[:, qc * 128:(qc + 1) * 128] = jnp.where(out_low, a_lo * inv_lo, a_hi * inv_hi)


def _swa_core(rw, proj, sink, tab):
    blk = SWA_BLOCK
    nb = rw.n_lat // blk
    kcol = SWA_QD // SWA_KVD
    vcol = kcol + 1
    ctx_blk = lambda b: rw.lat_rows // rw.n_ctx + b
    prev = lambda q: jnp.maximum(q - 1, 0)
    nxt = lambda q: jnp.minimum(q + 1, nb - 1)
    kv = lambda col, f: pl.BlockSpec((blk, SWA_KVD), lambda b, q: (b * nb + f(q), col))
    same = lambda q: q
    return pl.pallas_call(
        _swa_kernel,
        grid=(rw.batch, nb),
        in_specs=[pl.BlockSpec(memory_space=pltpu.SMEM),
                  pl.BlockSpec((blk, SWA_QD), lambda b, q: (b * nb + q, 0)),
                  kv(kcol, prev), kv(kcol, same), kv(kcol, nxt),
                  kv(vcol, prev), kv(vcol, same), kv(vcol, nxt),
                  pl.BlockSpec((rw.n_ctx, SWA_KVD), lambda b, q: (ctx_blk(b), kcol)),
                  pl.BlockSpec((rw.n_ctx, SWA_KVD), lambda b, q: (ctx_blk(b), vcol)),
                  pl.BlockSpec((blk, 384), lambda b, q: (prev(q), 0)),
                  pl.BlockSpec((blk, 384), lambda b, q: (q, 0)),
                  pl.BlockSpec((blk, 384), lambda b, q: (nxt(q), 0))],
        out_specs=pl.BlockSpec((blk, SWA_QD), lambda b, q: (b * nb + q, 0)),
        out_shape=jax.ShapeDtypeStruct((rw.lat_rows, SWA_QD), F32),
        compiler_params=_cparams(("parallel", "parallel")),
    )(sink, proj, proj, proj, proj, proj, proj, proj, proj, proj, tab, tab, tab)


def _axial_angles(n_lat, dh):
    t = jnp.arange(n_lat)
    row = (t // GRID_W).astype(F32)
    col = (t % GRID_W).astype(F32)
    nf = dh // 4
    inv = ROPE_THETA ** (-jnp.arange(nf, dtype=F32) / nf)
    return row[:, None] * inv, col[:, None] * inv


def _ret_tables(n_lat):
    ar, ac = _axial_angles(n_lat, RET_DK)
    cos = jnp.concatenate([jnp.cos(ar), jnp.cos(ar), jnp.cos(ac), jnp.cos(ac)], axis=1)
    sin = jnp.concatenate([-jnp.sin(ar), jnp.sin(ar), -jnp.sin(ac), jnp.sin(ac)], axis=1)
    return cos, sin


def _swa_table(n_lat):
    ar, ac = _axial_angles(n_lat, SWA_HD)
    z = jnp.zeros_like(ar)
    cos = jnp.concatenate([jnp.cos(ar), jnp.cos(ar), jnp.cos(ac), jnp.cos(ac)], axis=1)
    s1 = jnp.concatenate([-jnp.sin(ar), z, -jnp.sin(ac), z], axis=1)
    s2 = jnp.concatenate([z, jnp.sin(ar), z, jnp.sin(ac)], axis=1)
    return jnp.concatenate([jnp.tile(cos, (1, 2)), jnp.tile(s1, (1, 2)), jnp.tile(s2, (1, 2))], axis=1)


def _gdn_layer(rw, layer, x, mod3, gain, w_in, conv_w, a_log, dt_bias, norm_g, w_out):
    n_ab = w_in.shape[1] - GDN_MAIN
    w_ab = _bf(jnp.pad(w_in[:, GDN_MAIN:], ((0, 0), (0, 128 - n_ab))))
    proj, ab = _inproj(rw, x, gain, mod3, layer, _bf(w_in), w_ab, n_cols=GDN_MAIN)
    hv = GDN_V_HEADS
    alog_row = jnp.zeros((1, 128), F32).at[0, 0:hv].set(a_log[0]).at[0, 2 * hv:3 * hv].set(a_log[1])
    dtb_row = jnp.zeros((1, 128), F32).at[0, 0:hv].set(dt_bias[0]).at[0, 2 * hv:3 * hv].set(dt_bias[1])
    cb = 2048
    row_spec = pl.BlockSpec((TM_CONV, 128), lambda i, j: (i, 0))
    vec_spec = pl.BlockSpec((1, 128), lambda i, j: (0, 0))
    qkv, g_all, b_all = pl.pallas_call(
        functools.partial(_gdn_conv_kernel, rw=rw),
        grid=(rw.rows // TM_CONV, GDN_CONV // cb),
        in_specs=_conv_specs(rw, cb) + [row_spec, vec_spec, vec_spec],
        out_specs=[pl.BlockSpec((TM_CONV, cb), lambda i, j: (i, j)), row_spec, row_spec],
        out_shape=[jax.ShapeDtypeStruct((rw.rows, GDN_CONV), F32),
                   jax.ShapeDtypeStruct((rw.rows, 128), F32),
                   jax.ShapeDtypeStruct((rw.rows, 128), F32)],
        scratch_shapes=[pltpu.VMEM((TM_CONV + 16, cb), F32)],
        compiler_params=_cparams(("parallel", "arbitrary")),
    )(proj, proj, proj, conv_w, ab, alog_row, dtb_row)
    g_dir = jnp.stack([g_all[:, 0:hv], g_all[:, 2 * hv:3 * hv]])
    b_dir = jnp.stack([b_all[:, hv:2 * hv], b_all[:, 3 * hv:4 * hv]])
    o = _gdn_core(rw, qkv, g_dir, b_dir)
    return _two_dir_out(_gdn_out_kernel, rw, layer, x, mod3, o, proj, GDN_CONV // GDN_VAL, norm_g, _bf(w_out))


def _ret_layer(rw, layer, x, mod3, gain, w_in, decay_logit, norm_g, w_out):
    proj = _inproj(rw, x, gain, mod3, layer, _bf(w_in))
    cos_tab, sin_tab = _ret_tables(rw.n_lat)
    o = _ret_core(rw, proj, cos_tab, sin_tab, decay_logit)
    return _two_dir_out(_ret_out_kernel, rw, layer, x, mod3, o, proj, (2 * RET_KEY + RET_VAL) // RET_VAL,
                        norm_g, _bf(w_out))


def _lru_layer(rw, layer, x, mod3, gain, w_in, conv_w, conv_b, w_gate, b_gate, lam, w_out):
    proj = _inproj(rw, x, gain, mod3, layer, _bf(w_in))
    h0 = _lru_scan(rw, proj, conv_w, conv_b, w_gate[0], b_gate[0], lam[0], rev=False)
    h1 = _lru_scan(rw, proj, conv_w, conv_b, w_gate[1], b_gate[1], lam[1], rev=True)
    tm = TM_OUT
    specs = [pl.BlockSpec((tm, LRU_W), lambda i: (i, 0)),
             pl.BlockSpec((tm, LRU_W), lambda i: (i, 0)),
             pl.BlockSpec((tm, LRU_W), lambda i: (i, 1))]
    return _outproj_call(_lru_out_kernel, rw, rw.rows, layer, x, mod3, specs, [h0, h1, proj], _bf(w_out), [])


def _swa_layer(rw, layer, x, mod3, gain, w_in, sink, w_out):
    proj = _inproj(rw, x, gain, mod3, layer, _bf(w_in))
    o = _swa_core(rw, proj, sink, _swa_table(rw.n_lat))
    specs = [pl.BlockSpec((TM_OUT, SWA_QD), lambda i: (i, 0))]
    return _outproj_call(_swa_out_kernel, rw, rw.lat_rows, layer, x, mod3, specs, [o], _bf(w_out), [])


def kernel(x, c, ctx, c_ctx, norm_mix_g, norm_ffn_g, w_mod, b_mod, w_ff1, w_ff2, norm_out_g, gdn_w_in, gdn_conv_w, gdn_a_log, gdn_dt_bias, gdn_norm_g, gdn_w_out, ret_w_in, ret_decay_logit, ret_norm_g, ret_w_out, lru_w_in, lru_conv_w, lru_conv_b, lru_w_gate, lru_b_gate, lru_lambda, lru_w_out, swa_w_in, swa_sink, swa_w_out):
    batch, n_lat, _ = x.shape
    n_ctx = ctx.shape[1]
    assert batch + 1 <= MOD_ROWS and n_lat % TM_IN == 0 and (batch * n_ctx) % TM_IN == 0
    assert w_mod.shape[0] == DEPTH and gdn_w_in.shape[0] == 1 and swa_w_in.shape[0] == 1
    rw = _Rows(batch, n_lat, n_ctx)
    xf = jnp.concatenate([x.reshape(rw.lat_rows, D), ctx.reshape(batch * n_ctx, D)], axis=0)
    cc = jnp.zeros((MOD_ROWS, D), F32).at[:batch].set(c).at[batch].set(c_ctx)
    mod3 = _adaln(cc, w_mod, b_mod)

    def mlp(layer, xin, n_rows, final):
        return _mlp(rw, xin, n_rows, norm_ffn_g[layer], mod3, layer, w_ff1, w_ff2,
                    norm_out_g, final)

    xf = _gdn_layer(rw, 0, xf, mod3, norm_mix_g[0], gdn_w_in[0], gdn_conv_w[0], gdn_a_log[0], gdn_dt_bias[0],
                    gdn_norm_g[0], gdn_w_out[0])
    xf = mlp(0, xf, rw.rows, False)
    xf = _ret_layer(rw, 1, xf, mod3, norm_mix_g[1], ret_w_in[0], ret_decay_logit[0], ret_norm_g[0], ret_w_out[0])
    xf = mlp(1, xf, rw.rows, False)
    xf = _lru_layer(rw, 2, xf, mod3, norm_mix_g[2], lru_w_in[0], lru_conv_w[0], lru_conv_b[0], lru_w_gate[0],
                    lru_b_gate[0], lru_lambda[0], lru_w_out[0])
    xf = mlp(2, xf, rw.rows, False)
    xl = _swa_layer(rw, 3, xf, mod3, norm_mix_g[3], swa_w_in[0], swa_sink[0], swa_w_out[0])
    out = mlp(3, xl, rw.lat_rows, True)
    return out.reshape(batch, n_lat, D)
```

```python
import functools
import math

import jax
import jax.numpy as jnp
from jax import lax
from jax.experimental import pallas as pl
from jax.experimental.pallas import tpu as pltpu

F32 = jnp.float32
BF16 = jnp.bfloat16

D = 1024
D_FF = 4 * D
EPS = 1e-6
DEPTH = 4
ROPE_THETA = 10000.0
GRID_W = 64
CHUNK = 64
MOD_ROWS = 16

GDN_QK_HEADS = 8
GDN_V_HEADS = 16
GDN_HD = 128
GDN_KEY = GDN_QK_HEADS * GDN_HD
GDN_VAL = GDN_V_HEADS * GDN_HD
GDN_CONV = 2 * GDN_KEY + GDN_VAL
GDN_MAIN = GDN_CONV + GDN_VAL

RET_HEADS = 4
RET_DK = 256
RET_DV = 512
RET_KEY = RET_HEADS * RET_DK
RET_VAL = RET_HEADS * RET_DV

LRU_W = 1280
LRU_BLOCKS = 10
LRU_BW = LRU_W // LRU_BLOCKS
LRU_C = 8.0

SWA_QH = 16
SWA_KVH = 4
SWA_HD = 64
SWA_BLOCK = 128
SWA_WINDOW = 128
SWA_QD = SWA_QH * SWA_HD
SWA_KVD = SWA_KVH * SWA_HD

TM_IN = 1024
TM_OUT = 512
TM_CONV = 256
HALO = 16
TN_IN_MAX = 1536
FF_CHUNK = 1024
RET_CHUNK = 256
GDN_BLOCK = 4 * CHUNK
VMEM_LIMIT = 48 * 1024 * 1024
MLP_VMEM_LIMIT = 56 * 1024 * 1024


def _dot(a, b):
    return jnp.dot(a, b, preferred_element_type=F32)


def _dot_nt(a, b):
    return lax.dot_general(a, b, (((1,), (1,)), ((), ())), preferred_element_type=F32)


def _dot_tn(a, b):
    return lax.dot_general(a, b, (((0,), (0,)), ((), ())), preferred_element_type=F32)


def _bf(x):
    return x.astype(BF16)


def _split3(x):
    hi = _bf(x)
    r = x - hi.astype(F32)
    mid = _bf(r)
    lo = _bf(r - mid.astype(F32))
    return hi, mid, lo


def _silu(x):
    return x * jax.nn.sigmoid(x)


def _softplus(x):
    return jnp.maximum(x, 0.0) + jnp.log1p(jnp.exp(-jnp.abs(x)))


def _gelu_tanh(x):
    cdf = 0.5 * (1.0 + jnp.tanh(math.sqrt(2.0 / math.pi) * (x + 0.044715 * (x * x * x))))
    return x * cdf


def _cparams(sem, vmem=None):
    return pltpu.CompilerParams(dimension_semantics=sem, vmem_limit_bytes=VMEM_LIMIT if vmem is None else vmem)


class _Rows:
    def __init__(self, batch, n_lat, n_ctx):
        self.batch, self.n_lat, self.n_ctx = batch, n_lat, n_ctx
        self.lat_rows = batch * n_lat
        self.rows = self.lat_rows + batch * n_ctx
        self.n_steps = (n_lat + n_ctx) // CHUNK
        self.ctx_steps = n_ctx // CHUNK
        self.lat_steps = n_lat // CHUNK

    def mod_row(self, i, tm):
        return jnp.where(i < self.lat_rows // tm, i // (self.n_lat // tm), self.batch)

    def chunk_block(self, b, d, s, blk):
        nc, nl = self.n_ctx // blk, self.n_lat // blk
        c_ctx = jnp.where(d == 0, s, nc - 1 - s)
        c_lat = jnp.where(d == 0, s - nc, nl - 1 - (s - nc))
        return jnp.where(s < nc, self.lat_rows // blk + b * nc + c_ctx, b * nl + c_lat)

    def lat_chunk(self, d, s, blk):
        nc, nl = self.n_ctx // blk, self.n_lat // blk
        return jnp.where(s < nc, 0, jnp.where(d == 0, s - nc, nl - 1 - (s - nc)))


def _adaln_kernel(c_ref, w_ref, b_ref, o_ref):
    s = _silu(c_ref[...])
    w = w_ref[0]
    s_hi = _bf(s)
    s_lo = _bf(s - s_hi.astype(F32))
    w_hi = _bf(w)
    w_lo = _bf(w - w_hi.astype(F32))
    y = _dot(s_hi, w_hi) + (_dot(s_lo, w_hi) + _dot(s_hi, w_lo))
    o_ref[0] = y + b_ref[0]


def _adaln(cc, w_mod, b_mod):
    depth, _, n = w_mod.shape
    tn = 1024
    out = pl.pallas_call(
        _adaln_kernel,
        grid=(depth, n // tn),
        in_specs=[pl.BlockSpec((MOD_ROWS, D), lambda l, j: (0, 0)),
                  pl.BlockSpec((1, D, tn), lambda l, j: (l, 0, j)),
                  pl.BlockSpec((1, 1, tn), lambda l, j: (l, 0, j))],
        out_specs=pl.BlockSpec((1, MOD_ROWS, tn), lambda l, j: (l, 0, j)),
        out_shape=jax.ShapeDtypeStruct((depth, MOD_ROWS, n), F32),
        compiler_params=_cparams(("parallel", "parallel")),
    )(cc, w_mod, b_mod.reshape(depth, 1, n))
    return out.reshape(depth * MOD_ROWS * 6, 1, D)


def _mod_spec(rw, layer, which, tm):
    base = layer * MOD_ROWS * 6
    return pl.BlockSpec((1, 1, D), lambda i, *_: (base + rw.mod_row(i, tm) * 6 + which, 0, 0))


def _norm_mod(x, gain, shift, scale):
    y = x * lax.rsqrt(jnp.mean(x * x, axis=-1, keepdims=True) + EPS) * gain
    return y * (1.0 + scale) + shift


def _inproj_kernel(x_ref, g_ref, sh_ref, sc_ref, w_ref, *rest, has_extra):
    if has_extra:
        w2_ref, o_ref, o2_ref, h_ref = rest
    else:
        o_ref, h_ref = rest

    @pl.when(pl.program_id(1) == 0)
    def _():
        h = _bf(_norm_mod(x_ref[...], g_ref[...], sh_ref[0], sc_ref[0]))
        h_ref[...] = h
        if has_extra:
            o2_ref[...] = _dot(h, w2_ref[...])

    o_ref[...] = _bf(_dot(h_ref[...], w_ref[...]))


def _inproj(rw, x, gain, mod3, layer, w, w_extra=None, n_cols=None):
    n = w.shape[1] if n_cols is None else n_cols
    tm = TM_IN
    tn = max(t for t in range(256, TN_IN_MAX + 1, 256) if n % t == 0)
    has_extra = w_extra is not None
    in_specs = [pl.BlockSpec((tm, D), lambda i, j: (i, 0)),
                pl.BlockSpec((1, D), lambda i, j: (0, 0)),
                _mod_spec(rw, layer, 0, tm),
                _mod_spec(rw, layer, 1, tm),
                pl.BlockSpec((D, tn), lambda i, j: (0, j))]
    out_specs = [pl.BlockSpec((tm, tn), lambda i, j: (i, j))]
    out_shape = [jax.ShapeDtypeStruct((rw.rows, n), BF16)]
    args = [x, gain.reshape(1, D), mod3, mod3, w]
    if has_extra:
        ne = w_extra.shape[1]
        in_specs.append(pl.BlockSpec((D, ne), lambda i, j: (0, 0)))
        out_specs.append(pl.BlockSpec((tm, ne), lambda i, j: (i, 0)))
        out_shape.append(jax.ShapeDtypeStruct((rw.rows, ne), F32))
        args.append(w_extra)
    res = pl.pallas_call(
        functools.partial(_inproj_kernel, has_extra=has_extra),
        grid=(rw.rows // tm, n // tn),
        in_specs=in_specs, out_specs=out_specs, out_shape=out_shape,
        scratch_shapes=[pltpu.VMEM((tm, D), BF16)],
        compiler_params=_cparams(("parallel", "arbitrary")),
    )(*args)
    return res if has_extra else res[0]


def _mlp_kernel(x_ref, g_ref, sh_ref, sc_ref, gt_ref, w1_ref, w2_ref, go_ref, o_ref, h_ref, acc_ref,
                *, final_norm):
    k = pl.program_id(1)

    @pl.when(k == 0)
    def _():
        h_ref[...] = _bf(_norm_mod(x_ref[...], g_ref[...], sh_ref[0], sc_ref[0]))
        acc_ref[...] = jnp.zeros_like(acc_ref)

    a = jnp.square(jnp.maximum(_dot(h_ref[...], _bf(w1_ref[...])), 0.0))
    acc_ref[...] += _dot(_bf(a), _bf(w2_ref[...]))

    @pl.when(k == pl.num_programs(1) - 1)
    def _():
        y = x_ref[...] + gt_ref[0] * acc_ref[...]
        if final_norm:
            y = y * lax.rsqrt(jnp.mean(y * y, axis=-1, keepdims=True) + EPS) * go_ref[...]
        o_ref[...] = y


def _mlp(rw, x, n_rows, gain, mod3, layer, w1, w2, out_gain, final_norm):
    tm, ck = TM_IN, FF_CHUNK
    return pl.pallas_call(
        functools.partial(_mlp_kernel, final_norm=final_norm),
        grid=(n_rows // tm, D_FF // ck),
        in_specs=[pl.BlockSpec((tm, D), lambda i, k: (i, 0)),
                  pl.BlockSpec((1, D), lambda i, k: (0, 0)),
                  _mod_spec(rw, layer, 3, tm),
                  _mod_spec(rw, layer, 4, tm),
                  _mod_spec(rw, layer, 5, tm),
                  pl.BlockSpec((None, D, ck), lambda i, k: (layer, 0, k)),
                  pl.BlockSpec((None, ck, D), lambda i, k: (layer, k, 0)),
                  pl.BlockSpec((1, D), lambda i, k: (0, 0))],
        out_specs=pl.BlockSpec((tm, D), lambda i, k: (i, 0)),
        out_shape=jax.ShapeDtypeStruct((n_rows, D), F32),
        scratch_shapes=[pltpu.VMEM((tm, D), BF16), pltpu.VMEM((tm, D), F32)],
        compiler_params=_cparams(("parallel", "arbitrary"), MLP_VMEM_LIMIT),
    )(x, gain.reshape(1, D), mod3, mod3, mod3, w1, w2, out_gain.reshape(1, D))


OUT_SUB = 256


def _row_blocks(n_rows):
    return [slice(r, r + OUT_SUB) for r in range(0, n_rows, OUT_SUB)]


def _gdn_out_kernel(x_ref, gt_ref, o0_ref, o1_ref, z_ref, ng_ref, w_ref, out_ref, a_ref):
    ng = ng_ref[...]
    for rows in _row_blocks(x_ref.shape[0]):
        for h in range(GDN_V_HEADS):
            sl = slice(h * GDN_HD, (h + 1) * GDN_HD)
            o = o0_ref[0, rows, sl].astype(F32) + o1_ref[0, rows, sl].astype(F32)
            y = o * lax.rsqrt(jnp.mean(o * o, axis=-1, keepdims=True) + EPS) * ng
            a_ref[rows, sl] = _bf(y * _silu(z_ref[rows, sl].astype(F32)))
        out_ref[rows] = x_ref[rows] + gt_ref[0] * _dot(a_ref[rows], w_ref[...])


def _ret_out_kernel(x_ref, gt_ref, o0_ref, o1_ref, z_ref, ng_ref, w_ref, out_ref, a_ref):
    for rows in _row_blocks(x_ref.shape[0]):
        for h in range(RET_HEADS):
            sl = slice(h * RET_DV, (h + 1) * RET_DV)
            o = o0_ref[0, rows, sl].astype(F32) + o1_ref[0, rows, sl].astype(F32)
            mu = jnp.mean(o, axis=-1, keepdims=True)
            oc = o - mu
            var = jnp.mean(oc * oc, axis=-1, keepdims=True)
            y = oc * lax.rsqrt(var + EPS) * ng_ref[:, sl]
            a_ref[rows, sl] = _bf(y * _silu(z_ref[rows, sl].astype(F32)))
        out_ref[rows] = x_ref[rows] + gt_ref[0] * _dot(a_ref[rows], w_ref[...])


def _lru_out_kernel(x_ref, gt_ref, h0_ref, h1_ref, gb_ref, w_ref, out_ref):
    for rows in _row_blocks(x_ref.shape[0]):
        a = (h0_ref[rows] + h1_ref[rows]) * _gelu_tanh(gb_ref[rows].astype(F32))
        out_ref[rows] = x_ref[rows] + gt_ref[0] * _dot(_bf(a), w_ref[...])


def _swa_out_kernel(x_ref, gt_ref, o_ref, w_ref, out_ref):
    out_ref[...] = x_ref[...] + gt_ref[0] * _dot(_bf(o_ref[...]), w_ref[...])


def _outproj_call(kern, rw, n_rows, layer, x, mod3, extra_specs, extra_args, w, scratch):
    tm = TM_OUT
    k = w.shape[0]
    return pl.pallas_call(
        kern,
        grid=(n_rows // tm,),
        in_specs=[pl.BlockSpec((tm, D), lambda i: (i, 0)), _mod_spec(rw, layer, 2, tm)] + extra_specs
                 + [pl.BlockSpec((k, D), lambda i: (0, 0))],
        out_specs=pl.BlockSpec((tm, D), lambda i: (i, 0)),
        out_shape=jax.ShapeDtypeStruct((n_rows, D), F32),
        scratch_shapes=scratch,
        compiler_params=_cparams(("parallel",)),
    )(x, mod3, *extra_args, w)


def _two_dir_out(kern, rw, layer, x, mod3, o, proj, z_block, norm_g, w):
    tm, k = TM_OUT, w.shape[0]
    ng = norm_g.reshape(1, -1)
    specs = [pl.BlockSpec((1, tm, k), lambda i: (0, i, 0)),
             pl.BlockSpec((1, tm, k), lambda i: (1, i, 0)),
             pl.BlockSpec((tm, k), lambda i: (i, z_block)),
             pl.BlockSpec(ng.shape, lambda i: (0, 0))]
    return _outproj_call(kern, rw, rw.rows, layer, x, mod3, specs, [o, o, proj, ng], w,
                         [pltpu.VMEM((tm, k), BF16)])


def _shift_matrices(t):
    delta = lax.broadcasted_iota(jnp.int32, (t, t), 0) - lax.broadcasted_iota(jnp.int32, (t, t), 1)
    return {k: _bf(jnp.where(jnp.logical_or(delta == k, delta == k - t), 1.0, 0.0)) for k in (2, 1, t - 1)}


def _conv_core(x_ref, p_ref, n_ref, w_ref, first, last, shifts=None):
    t = x_ref.shape[0]
    xb = x_ref[...]
    x = xb.astype(F32)
    prev = p_ref[...].astype(F32)[HALO - 8:] * jnp.where(first, 0.0, 1.0)
    nxt = n_ref[...].astype(F32)[:8] * jnp.where(last, 0.0, 1.0)
    if shifts is not None:
        assert xb.dtype == BF16
        r2, r1, rp = _dot(shifts[2], xb), _dot(shifts[1], xb), _dot(shifts[t - 1], xb)
    else:
        r2, r1, rp = pltpu.roll(x, 2, 0), pltpu.roll(x, 1, 0), pltpu.roll(x, t - 1, 0)
    row8 = lax.broadcasted_iota(jnp.int32, prev.shape, 0)
    f2 = jnp.where(row8 < 2, pltpu.roll(prev, 2, 0), r2[:8])
    f1 = jnp.where(row8 < 1, pltpu.roll(prev, 1, 0), r1[:8])
    l1 = jnp.where(row8 == 7, pltpu.roll(nxt, 7, 0), rp[t - 8:])
    xm2 = jnp.concatenate([f2, r2[8:]], axis=0)
    xm1 = jnp.concatenate([f1, r1[8:]], axis=0)
    xp1 = jnp.concatenate([rp[:t - 8], l1], axis=0)
    w = w_ref[...]
    return w[0:1] * xm2 + w[1:2] * xm1 + w[2:3] * x + w[3:4] * xp1


def _seg_flags(rw, i):
    per = rw.n_lat // TM_CONV
    is_ctx = i >= rw.lat_rows // TM_CONV
    assert rw.n_ctx == TM_CONV
    first = jnp.logical_or(is_ctx, i % per == 0)
    last = jnp.logical_or(is_ctx, i % per == per - 1)
    return first, last


def _gdn_conv_kernel(x_ref, p_ref, n_ref, w_ref, ab_ref, alog_ref, dtb_ref, o_ref, g_ref, b_ref, *, rw):
    i, j = pl.program_id(0), pl.program_id(1)
    first, last = _seg_flags(rw, i)
    s = _silu(_conv_core(x_ref, p_ref, n_ref, w_ref, first, last, _shift_matrices(x_ref.shape[0])))
    width = s.shape[1]
    qk_blocks = 2 * GDN_KEY // width

    @pl.when(j < qk_blocks)
    def _():
        for h in range(width // GDN_HD):
            sl = slice(h * GDN_HD, (h + 1) * GDN_HD)
            sh = s[:, sl]
            is_q = j * width + h * GDN_HD < GDN_KEY
            scale = jnp.where(is_q, GDN_HD ** -0.5, 1.0)
            o_ref[:, sl] = sh * (lax.rsqrt(jnp.sum(sh * sh, axis=-1, keepdims=True) + EPS) * scale)

    @pl.when(j >= qk_blocks)
    def _():
        o_ref[...] = s

    @pl.when(j == 0)
    def _():
        ab = ab_ref[...]
        g_ref[...] = -jnp.exp(alog_ref[...]) * _softplus(ab + dtb_ref[...])
        b_ref[...] = jax.nn.sigmoid(ab)


def _conv_specs(rw, cb):
    th = TM_CONV // HALO
    nh = rw.rows // HALO
    return [pl.BlockSpec((TM_CONV, cb), lambda i, j: (i, j)),
            pl.BlockSpec((HALO, cb), lambda i, j: (jnp.maximum(i * th - 1, 0), j)),
            pl.BlockSpec((HALO, cb), lambda i, j: (jnp.minimum((i + 1) * th, nh - 1), j)),
            pl.BlockSpec((4, cb), lambda i, j: (0, j))]


def _dir_masks(d, n):
    ii = lax.broadcasted_iota(jnp.int32, (n, n), 0)
    jj = lax.broadcasted_iota(jnp.int32, (n, n), 1)
    t = (ii - jj) * jnp.where(d == 0, 1, -1)
    return t, t >= 0, t > 0


def _gdn_core_kernel(qkv_ref, g_ref, b_ref, o_ref, s_ref):
    d, s = pl.program_id(1), pl.program_id(2)

    @pl.when(s == 0)
    def _():
        s_ref[...] = jnp.zeros_like(s_ref)

    c, hd = CHUNK, GDN_HD
    rep = GDN_V_HEADS // GDN_QK_HEADS
    heads = range(GDN_V_HEADS)
    n_sub = qkv_ref.shape[0] // c
    subs = range(n_sub)
    offs = [pl.multiple_of(jnp.where(d == 0, i, n_sub - 1 - i) * c, c) for i in subs]
    chains = [(i, h) for i in subs for h in heads]
    t, incl, strict = _dir_masks(d, c)
    cum = _bf(jnp.where(incl, 1.0, 0.0))
    cum_t = _bf(jnp.where(t <= 0, 1.0, 0.0))
    g = [g_ref[0, pl.ds(offs[i], c), :] for i in subs]
    beta = [b_ref[0, pl.ds(offs[i], c), :] for i in subs]
    parts = [_split3(g[i]) for i in subs]
    gc = [sum(_dot(cum, p) for p in parts[i]) for i in subs]
    gr = [sum(_dot_tn(p, cum_t) for p in parts[i]) for i in subs]
    gtot = [jnp.sum(g[i], axis=0, keepdims=True) for i in subs]

    def rows(i, lo):
        return qkv_ref[pl.ds(offs[i], c), lo:lo + hd]

    q_l = [[rows(i, hq * hd) for hq in range(GDN_QK_HEADS)] for i in subs]
    k_l = [[rows(i, GDN_KEY + hq * hd) for hq in range(GDN_QK_HEADS)] for i in subs]
    gram = [[_dot_nt(_bf(jnp.concatenate([k, q], axis=0)), _bf(k)) for q, k in zip(q_l[i], k_l[i])]
            for i in subs]
    gcol = [gc[i][:, h:h + 1] for i, h in chains]
    bcol = [beta[i][:, h:h + 1] for i, h in chains]
    dec = [jnp.where(incl, jnp.exp(jnp.where(incl, gcol[n] - gr[i][h:h + 1, :], 0.0)), 0.0)
           for n, (i, h) in enumerate(chains)]
    eg = [jnp.exp(x) for x in gcol]
    m = [jnp.where(strict, -(gram[i][h // rep][:c] * dec[n] * bcol[n]), 0.0) for n, (i, h) in enumerate(chains)]
    rhs = [jnp.concatenate([k_l[i][h // rep] * (bcol[n] * eg[n]), rows(i, 2 * GDN_KEY + h * hd) * bcol[n]], axis=1)
           for n, (i, h) in enumerate(chains)]
    ids = range(len(chains))

    m_hi = [_bf(x) for x in m]
    zeros_cc = jnp.zeros((c, 2 * c), BF16)
    lane_cc = lax.broadcasted_iota(jnp.int32, (c, 2 * c), 1)
    row_cc = lax.broadcasted_iota(jnp.int32, (c, 2 * c), 0)
    eye_hi = _bf(jnp.where(lane_cc == row_cc + c, 1.0, 0.0))
    is_hi = lane_cc >= c
    cb = [jnp.concatenate([m_hi[n], jnp.zeros((c, c), BF16)], axis=1) + eye_hi for n in ids]
    for _ in range(6):
        r = [_dot(cb[n], jnp.concatenate([cb[n], zeros_cc], axis=0)) for n in ids]
        cb = [_bf(r[n]) + jnp.where(is_hi, cb[n], jnp.zeros_like(cb[n])) for n in ids]
    nb = [cb[n] - eye_hi for n in ids]
    zeros_rhs = jnp.zeros((c, 2 * hd), BF16)

    def apply_n(y):
        return [_dot(nb[n], jnp.concatenate([zeros_rhs, _bf(y[n])], axis=0)) for n in ids]

    nr = apply_n(rhs)
    x = [rhs[n] + nr[n] for n in ids]
    resid = [_dot(m_hi[n], _bf(x[n])) - nr[n] for n in ids]
    nr = apply_n(resid)
    x = [x[n] + (resid[n] + nr[n]) for n in ids]
    qk = [_bf(gram[i][h // rep][c:] * dec[n]) for n, (i, h) in enumerate(chains)]
    gt = [gtot[i][:, h:h + 1] for i, h in chains]
    kd = [_bf(k_l[i][h // rep] * jnp.exp(gt[n] - gcol[n])) for n, (i, h) in enumerate(chains)]
    wq = [_bf(jnp.concatenate([x[n][:, :hd], q_l[i][h // rep] * eg[n]], axis=0)) for n, (i, h) in enumerate(chains)]
    decay = [jnp.exp(x) for x in gt]

    st = [s_ref[h] for h in heads]
    for i in subs:
        base = i * GDN_V_HEADS
        r = [_dot(wq[base + h], _bf(st[h])) for h in heads]
        unb = [_bf(x[base + h][:, hd:] - r[h][:c]) for h in heads]
        for h in heads:
            o_ref[0, pl.ds(offs[i], c), h * hd:(h + 1) * hd] = _bf(r[h][c:] + _dot(qk[base + h], unb[h]))
        st = [st[h] * decay[base + h] + _dot_tn(kd[base + h], unb[h]) for h in heads]
    for h in heads:
        s_ref[h] = st[h]


def _gdn_core(rw, qkv, g_dir, b_dir):
    blk = GDN_BLOCK
    rb = lambda b, d, s: rw.chunk_block(b, d, s, blk)
    return pl.pallas_call(
        _gdn_core_kernel,
        grid=(rw.batch, 2, (rw.n_lat + rw.n_ctx) // blk),
        in_specs=[pl.BlockSpec((blk, GDN_CONV), lambda b, d, s: (rb(b, d, s), 0)),
                  pl.BlockSpec((1, blk, GDN_V_HEADS), lambda b, d, s: (d, rb(b, d, s), 0)),
                  pl.BlockSpec((1, blk, GDN_V_HEADS), lambda b, d, s: (d, rb(b, d, s), 0))],
        out_specs=pl.BlockSpec((1, blk, GDN_VAL), lambda b, d, s: (d, rb(b, d, s), 0)),
        out_shape=jax.ShapeDtypeStruct((2, rw.rows, GDN_VAL), BF16),
        scratch_shapes=[pltpu.VMEM((GDN_V_HEADS, GDN_HD, GDN_HD), F32)],
        compiler_params=_cparams(("parallel", "parallel", "arbitrary")),
    )(qkv, g_dir, b_dir)


def _rope_half(x, cos, sin_signed):
    return x * cos + pltpu.roll(x, x.shape[1] // 2, 1) * sin_signed


def _ret_core_kernel(q_ref, k_ref, v_ref, cos_ref, sin_ref, dl_ref, o_ref, s_ref, *, ctx_steps):
    d, s = pl.program_id(1), pl.program_id(2)

    @pl.when(s == 0)
    def _():
        s_ref[...] = jnp.zeros_like(s_ref)

    c = q_ref.shape[0]
    heads = range(RET_HEADS)
    is_lat = s >= ctx_steps
    cos = jnp.where(is_lat, cos_ref[...], 1.0)
    sin = jnp.where(is_lat, sin_ref[...], 0.0)
    t, incl, _ = _dir_masks(d, c)
    tf = t.astype(F32)
    pos = lax.broadcasted_iota(jnp.int32, (c, 1), 0)
    ridx = jnp.where(d == 0, pos + 1, c - pos).astype(F32)
    log_gamma = -_softplus(-dl_ref[0])

    def rope(x):
        hw = RET_DK // 2
        return jnp.concatenate([_rope_half(x[:, :hw], cos[:, :hw], sin[:, :hw]),
                                _rope_half(x[:, hw:], cos[:, hw:], sin[:, hw:])], axis=1)

    lg = [log_gamma[:, h:h + 1] for h in heads]
    q = [rope(q_ref[:, h * RET_DK:(h + 1) * RET_DK].astype(F32)) * (RET_DK ** -0.5) for h in heads]
    k = [rope(k_ref[:, h * RET_DK:(h + 1) * RET_DK].astype(F32)) for h in heads]
    vb = [v_ref[:, h * RET_DV:(h + 1) * RET_DV] for h in heads]
    gcum = [lg[h] * ridx for h in heads]
    dec = [jnp.where(incl, jnp.exp(jnp.where(incl, lg[h] * tf, 0.0)), 0.0) for h in heads]
    qk = [_bf(_dot_nt(_bf(q[h]), _bf(k[h])) * dec[h]) for h in heads]
    st = [s_ref[h] for h in heads]
    for h in heads:
        o_ref[0, :, h * RET_DV:(h + 1) * RET_DV] = _bf(
            _dot(_bf(q[h] * jnp.exp(gcum[h])), _bf(st[h])) + _dot(qk[h], vb[h]))
    gtot = [lg[h] * float(c) for h in heads]
    kd = [_bf(k[h] * jnp.exp(gtot[h] - gcum[h])) for h in heads]
    for h in heads:
        s_ref[h] = st[h] * jnp.exp(gtot[h]) + _dot_tn(kd[h], vb[h])


def _ret_core(rw, proj, cos_tab, sin_tab, decay_logit):
    c = RET_CHUNK
    rb = lambda b, d, s: rw.chunk_block(b, d, s, c)
    dl = jnp.zeros((2, 1, 128), F32).at[:, 0, :RET_HEADS].set(decay_logit)
    return pl.pallas_call(
        functools.partial(_ret_core_kernel, ctx_steps=rw.n_ctx // c),
        grid=(rw.batch, 2, (rw.n_lat + rw.n_ctx) // c),
        in_specs=[pl.BlockSpec((c, RET_KEY), lambda b, d, s: (rb(b, d, s), 0)),
                  pl.BlockSpec((c, RET_KEY), lambda b, d, s: (rb(b, d, s), 1)),
                  pl.BlockSpec((c, RET_VAL), lambda b, d, s: (rb(b, d, s), 1)),
                  pl.BlockSpec((c, RET_DK), lambda b, d, s: (rw.lat_chunk(d, s, c), 0)),
                  pl.BlockSpec((c, RET_DK), lambda b, d, s: (rw.lat_chunk(d, s, c), 0)),
                  pl.BlockSpec((1, 1, 128), lambda b, d, s: (d, 0, 0))],
        out_specs=pl.BlockSpec((1, c, RET_VAL), lambda b, d, s: (d, rb(b, d, s), 0)),
        out_shape=jax.ShapeDtypeStruct((2, rw.rows, RET_VAL), BF16),
        scratch_shapes=[pltpu.VMEM((RET_HEADS, RET_DK, RET_DV), F32)],
        compiler_params=_cparams(("parallel", "parallel", "arbitrary")),
    )(proj, proj, proj, cos_tab, sin_tab, dl)


def _lru_scan_kernel(x_ref, p_ref, n_ref, cw_ref, cb_ref, wg_ref, bg_ref, lam_ref, h_ref, carry_ref, *, rw, rev):
    @pl.when(pl.program_id(1) == 0)
    def _():
        carry_ref[...] = jnp.zeros_like(carry_ref)

    t = x_ref.shape[0]
    tile = rw.chunk_block(pl.program_id(0), 1 if rev else 0, pl.program_id(1), t)
    first, last = _seg_flags(rw, tile)
    xs = _conv_core(x_ref, p_ref, n_ref, cw_ref, first, last) + cb_ref[...]
    row8 = lax.broadcasted_iota(jnp.int32, (t // 8, 8, LRU_BW), 1)
    sp = _softplus(-lam_ref[...])
    for n in range(LRU_BLOCKS):
        sl = slice(n * LRU_BW, (n + 1) * LRU_BW)
        xn = xs[:, sl]
        gates = jax.nn.sigmoid(_dot(_bf(xn), wg_ref[n]) + bg_ref[n])
        log_a = -LRU_C * gates[:, :LRU_BW] * sp[:, sl]
        a = jnp.exp(log_a)
        b = jnp.sqrt(-jnp.tanh(log_a) * (a * a + 1.0)) * gates[:, LRU_BW:] * xn
        groups = t // 8
        a3 = a.reshape(groups, 8, LRU_BW)
        b3 = b.reshape(groups, 8, LRU_BW)
        for k in (1, 2, 4):
            if rev:
                keep = row8 < 8 - k
                shift = 8 - k
            else:
                keep = row8 >= k
                shift = k
            sa = jnp.where(keep, pltpu.roll(a3, shift, 1), 1.0)
            sb = jnp.where(keep, pltpu.roll(b3, shift, 1), 0.0)
            b3 = b3 + a3 * sb
            a3 = a3 * sa
        carry = carry_ref[:, sl]
        for gi in (reversed(range(groups)) if rev else range(groups)):
            hg = a3[gi] * carry + b3[gi]
            carry = hg[0:1] if rev else hg[7:8]
            h_ref[gi * 8:(gi + 1) * 8, sl] = hg
        carry_ref[:, sl] = carry


def _lru_scan(rw, proj, conv_w, conv_b, w_gate, b_gate, lam, rev):
    t = TM_CONV
    th = t // HALO
    nh = rw.rows // HALO
    d = 1 if rev else 0
    rb = lambda b, s: rw.chunk_block(b, d, s, t)
    return pl.pallas_call(
        functools.partial(_lru_scan_kernel, rw=rw, rev=rev),
        grid=(rw.batch, (rw.n_lat + rw.n_ctx) // t),
        in_specs=[pl.BlockSpec((t, LRU_W), lambda b, s: (rb(b, s), 0)),
                  pl.BlockSpec((HALO, LRU_W), lambda b, s: (jnp.maximum(rb(b, s) * th - 1, 0), 0)),
                  pl.BlockSpec((HALO, LRU_W), lambda b, s: (jnp.minimum((rb(b, s) + 1) * th, nh - 1), 0)),
                  pl.BlockSpec((4, LRU_W), lambda b, s: (0, 0)),
                  pl.BlockSpec((1, LRU_W), lambda b, s: (0, 0)),
                  pl.BlockSpec((LRU_BLOCKS, LRU_BW, 2 * LRU_BW), lambda b, s: (0, 0, 0)),
                  pl.BlockSpec((LRU_BLOCKS, 1, 2 * LRU_BW), lambda b, s: (0, 0, 0)),
                  pl.BlockSpec((1, LRU_W), lambda b, s: (0, 0))],
        out_specs=pl.BlockSpec((t, LRU_W), lambda b, s: (rb(b, s), 0)),
        out_shape=jax.ShapeDtypeStruct((rw.rows, LRU_W), F32),
        scratch_shapes=[pltpu.VMEM((1, LRU_W), F32)],
        compiler_params=_cparams(("parallel", "arbitrary")),
    )(proj, proj, proj, conv_w, conv_b.reshape(1, LRU_W), _bf(w_gate),
      b_gate.reshape(LRU_BLOCKS, 1, 2 * LRU_BW), lam.reshape(1, LRU_W))


def _rope16(x, tab):
    cos, s1, s2 = tab[:, :128], tab[:, 128:256], tab[:, 256:]
    cols = []
    for cidx in range(x.shape[1] // 128):
        xc = x[:, cidx * 128:(cidx + 1) * 128]
        cols.append(xc * cos + pltpu.roll(xc, 112, 1) * s1 + pltpu.roll(xc, 16, 1) * s2)
    return cols[0] if len(cols) == 1 else jnp.concatenate(cols, axis=1)


def _swa_kernel(sink_ref, q_ref, kp_ref, kc_ref, kn_ref, vp_ref, vc_ref, vn_ref, kx_ref, vx_ref,
                tp_ref, tc_ref, tn_ref, o_ref):
    qb = pl.program_id(1)
    nb = pl.num_programs(1)
    blk = SWA_BLOCK
    n_loc = 3 * blk
    n_keys = n_loc + kx_ref.shape[0]
    tab_c = tc_ref[...]
    q = _rope16(q_ref[...].astype(F32), tab_c) * (SWA_HD ** -0.5)
    k_all = jnp.concatenate([_rope16(kp_ref[...].astype(F32), tp_ref[...]), _rope16(kc_ref[...].astype(F32), tab_c),
                             _rope16(kn_ref[...].astype(F32), tn_ref[...]), kx_ref[...].astype(F32)], axis=0)
    v_all = jnp.concatenate([vp_ref[...], vc_ref[...], vn_ref[...], vx_ref[...]], axis=0).astype(F32)
    assert SWA_WINDOW >= blk - 1
    qi = lax.broadcasted_iota(jnp.int32, (blk, blk), 0)
    kj = lax.broadcasted_iota(jnp.int32, (blk, blk), 1)
    valid_prev = jnp.logical_and(kj - blk - qi >= -SWA_WINDOW, qb > 0)
    valid_next = jnp.logical_and(kj + blk - qi <= SWA_WINDOW, qb < nb - 1)
    lane = lax.broadcasted_iota(jnp.int32, (n_keys, 128), 1)
    low = lane < SWA_HD
    out_low = lax.broadcasted_iota(jnp.int32, (blk, 128), 1) < SWA_HD

    def expo(scores, sink):
        sc = jnp.concatenate([jnp.where(valid_prev, scores[:, :blk], -jnp.inf), scores[:, blk:2 * blk],
                              jnp.where(valid_next, scores[:, 2 * blk:n_loc], -jnp.inf), scores[:, n_loc:]], axis=1)
        mx = jnp.maximum(jnp.max(sc, axis=-1, keepdims=True), sink)
        return _bf(jnp.exp(sc - mx)), jnp.exp(sink - mx)

    group = SWA_QH // SWA_KVH
    for g in range(SWA_KVH):
        col = slice((g // 2) * 128, (g // 2 + 1) * 128)
        native_low = g % 2 == 0
        sel = low if native_low else jnp.logical_not(low)
        k_nat = jnp.where(sel, k_all[:, col], 0.0)
        v_nat = jnp.where(sel, v_all[:, col], 0.0)
        k_oth = pltpu.roll(k_nat, SWA_HD, 1)
        v_oth = pltpu.roll(v_nat, SWA_HD, 1)
        k_lo, k_hi = (k_nat, k_oth) if native_low else (k_oth, k_nat)
        v_lo, v_hi = (v_nat, v_oth) if native_low else (v_oth, v_nat)
        v_lo = jnp.where(lane == SWA_HD, 1.0, v_lo)
        v_hi = jnp.where(lane == 0, 1.0, v_hi)
        k_lo, k_hi, v_lo, v_hi = _bf(k_lo), _bf(k_hi), _bf(v_lo), _bf(v_hi)
        for cidx in range(group // 2):
            qc = (g * group) // 2 + cidx
            qv = _bf(q[:, qc * 128:(qc + 1) * 128])
            e_lo, sink_lo = expo(_dot_nt(qv, k_lo), sink_ref[2 * qc])
            e_hi, sink_hi = expo(_dot_nt(qv, k_hi), sink_ref[2 * qc + 1])
            a_lo = _dot(e_lo, v_lo)
            a_hi = _dot(e_hi, v_hi)
            inv_lo = 1.0 / (a_lo[:, SWA_HD:SWA_HD + 1] + sink_lo)
            inv_hi = 1.0 / (a_hi[:, 0:1] + sink_hi)
            o_ref[:, qc * 128:(qc + 1) * 128] = jnp.where(out_low, a_lo * inv_lo, a_hi * inv_hi)


def _swa_core(rw, proj, sink, tab):
    blk = SWA_BLOCK
    nb = rw.n_lat // blk
    kcol = SWA_QD // SWA_KVD
    vcol = kcol + 1
    ctx_blk = lambda b: rw.lat_rows // rw.n_ctx + b
    prev = lambda q: jnp.maximum(q - 1, 0)
    nxt = lambda q: jnp.minimum(q + 1, nb - 1)
    kv = lambda col, f: pl.BlockSpec((blk, SWA_KVD), lambda b, q: (b * nb + f(q), col))
    same = lambda q: q
    return pl.pallas_call(
        _swa_kernel,
        grid=(rw.batch, nb),
        in_specs=[pl.BlockSpec(memory_space=pltpu.SMEM),
                  pl.BlockSpec((blk, SWA_QD), lambda b, q: (b * nb + q, 0)),
                  kv(kcol, prev), kv(kcol, same), kv(kcol, nxt),
                  kv(vcol, prev), kv(vcol, same), kv(vcol, nxt),
                  pl.BlockSpec((rw.n_ctx, SWA_KVD), lambda b, q: (ctx_blk(b), kcol)),
                  pl.BlockSpec((rw.n_ctx, SWA_KVD), lambda b, q: (ctx_blk(b), vcol)),
                  pl.BlockSpec((blk, 384), lambda b, q: (prev(q), 0)),
                  pl.BlockSpec((blk, 384), lambda b, q: (q, 0)),
                  pl.BlockSpec((blk, 384), lambda b, q: (nxt(q), 0))],
        out_specs=pl.BlockSpec((blk, SWA_QD), lambda b, q: (b * nb + q, 0)),
        out_shape=jax.ShapeDtypeStruct((rw.lat_rows, SWA_QD), F32),
        compiler_params=_cparams(("parallel", "parallel")),
    )(sink, proj, proj, proj, proj, proj, proj, proj, proj, proj, tab, tab, tab)


def _axial_angles(n_lat, dh):
    t = jnp.arange(n_lat)
    row = (t // GRID_W).astype(F32)
    col = (t % GRID_W).astype(F32)
    nf = dh // 4
    inv = ROPE_THETA ** (-jnp.arange(nf, dtype=F32) / nf)
    return row[:, None] * inv, col[:, None] * inv


def _ret_tables(n_lat):
    ar, ac = _axial_angles(n_lat, RET_DK)
    cos = jnp.concatenate([jnp.cos(ar), jnp.cos(ar), jnp.cos(ac), jnp.cos(ac)], axis=1)
    sin = jnp.concatenate([-jnp.sin(ar), jnp.sin(ar), -jnp.sin(ac), jnp.sin(ac)], axis=1)
    return cos, sin


def _swa_table(n_lat):
    ar, ac = _axial_angles(n_lat, SWA_HD)
    z = jnp.zeros_like(ar)
    cos = jnp.concatenate([jnp.cos(ar), jnp.cos(ar), jnp.cos(ac), jnp.cos(ac)], axis=1)
    s1 = jnp.concatenate([-jnp.sin(ar), z, -jnp.sin(ac), z], axis=1)
    s2 = jnp.concatenate([z, jnp.sin(ar), z, jnp.sin(ac)], axis=1)
    return jnp.concatenate([jnp.tile(cos, (1, 2)), jnp.tile(s1, (1, 2)), jnp.tile(s2, (1, 2))], axis=1)


def _gdn_layer(rw, layer, x, mod3, gain, w_in, conv_w, a_log, dt_bias, norm_g, w_out):
    n_ab = w_in.shape[1] - GDN_MAIN
    w_ab = _bf(jnp.pad(w_in[:, GDN_MAIN:], ((0, 0), (0, 128 - n_ab))))
    proj, ab = _inproj(rw, x, gain, mod3, layer, _bf(w_in), w_ab, n_cols=GDN_MAIN)
    hv = GDN_V_HEADS
    alog_row = jnp.zeros((1, 128), F32).at[0, 0:hv].set(a_log[0]).at[0, 2 * hv:3 * hv].set(a_log[1])
    dtb_row = jnp.zeros((1, 128), F32).at[0, 0:hv].set(dt_bias[0]).at[0, 2 * hv:3 * hv].set(dt_bias[1])
    cb = 2048
    row_spec = pl.BlockSpec((TM_CONV, 128), lambda i, j: (i, 0))
    vec_spec = pl.BlockSpec((1, 128), lambda i, j: (0, 0))
    qkv, g_all, b_all = pl.pallas_call(
        functools.partial(_gdn_conv_kernel, rw=rw),
        grid=(rw.rows // TM_CONV, GDN_CONV // cb),
        in_specs=_conv_specs(rw, cb) + [row_spec, vec_spec, vec_spec],
        out_specs=[pl.BlockSpec((TM_CONV, cb), lambda i, j: (i, j)), row_spec, row_spec],
        out_shape=[jax.ShapeDtypeStruct((rw.rows, GDN_CONV), F32),
                   jax.ShapeDtypeStruct((rw.rows, 128), F32),
                   jax.ShapeDtypeStruct((rw.rows, 128), F32)],
        compiler_params=_cparams(("parallel", "arbitrary")),
    )(proj, proj, proj, conv_w, ab, alog_row, dtb_row)
    g_dir = jnp.stack([g_all[:, 0:hv], g_all[:, 2 * hv:3 * hv]])
    b_dir = jnp.stack([b_all[:, hv:2 * hv], b_all[:, 3 * hv:4 * hv]])
    o = _gdn_core(rw, qkv, g_dir, b_dir)
    return _two_dir_out(_gdn_out_kernel, rw, layer, x, mod3, o, proj, GDN_CONV // GDN_VAL, norm_g, _bf(w_out))


def _ret_layer(rw, layer, x, mod3, gain, w_in, decay_logit, norm_g, w_out):
    proj = _inproj(rw, x, gain, mod3, layer, _bf(w_in))
    cos_tab, sin_tab = _ret_tables(rw.n_lat)
    o = _ret_core(rw, proj, cos_tab, sin_tab, decay_logit)
    return _two_dir_out(_ret_out_kernel, rw, layer, x, mod3, o, proj, (2 * RET_KEY + RET_VAL) // RET_VAL,
                        norm_g, _bf(w_out))


def _lru_layer(rw, layer, x, mod3, gain, w_in, conv_w, conv_b, w_gate, b_gate, lam, w_out):
    proj = _inproj(rw, x, gain, mod3, layer, _bf(w_in))
    h0 = _lru_scan(rw, proj, conv_w, conv_b, w_gate[0], b_gate[0], lam[0], rev=False)
    h1 = _lru_scan(rw, proj, conv_w, conv_b, w_gate[1], b_gate[1], lam[1], rev=True)
    tm = TM_OUT
    specs = [pl.BlockSpec((tm, LRU_W), lambda i: (i, 0)),
             pl.BlockSpec((tm, LRU_W), lambda i: (i, 0)),
             pl.BlockSpec((tm, LRU_W), lambda i: (i, 1))]
    return _outproj_call(_lru_out_kernel, rw, rw.rows, layer, x, mod3, specs, [h0, h1, proj], _bf(w_out), [])


def _swa_layer(rw, layer, x, mod3, gain, w_in, sink, w_out):
    proj = _inproj(rw, x, gain, mod3, layer, _bf(w_in))
    o = _swa_core(rw, proj, sink, _swa_table(rw.n_lat))
    specs = [pl.BlockSpec((TM_OUT, SWA_QD), lambda i: (i, 0))]
    return _outproj_call(_swa_out_kernel, rw, rw.lat_rows, layer, x, mod3, specs, [o], _bf(w_out), [])


def kernel(x, c, ctx, c_ctx, norm_mix_g, norm_ffn_g, w_mod, b_mod, w_ff1, w_ff2, norm_out_g, gdn_w_in, gdn_conv_w, gdn_a_log, gdn_dt_bias, gdn_norm_g, gdn_w_out, ret_w_in, ret_decay_logit, ret_norm_g, ret_w_out, lru_w_in, lru_conv_w, lru_conv_b, lru_w_gate, lru_b_gate, lru_lambda, lru_w_out, swa_w_in, swa_sink, swa_w_out):
    batch, n_lat, _ = x.shape
    n_ctx = ctx.shape[1]
    assert batch + 1 <= MOD_ROWS and n_lat % TM_IN == 0 and (batch * n_ctx) % TM_IN == 0
    assert w_mod.shape[0] == DEPTH and gdn_w_in.shape[0] == 1 and swa_w_in.shape[0] == 1
    rw = _Rows(batch, n_lat, n_ctx)
    xf = jnp.concatenate([x.reshape(rw.lat_rows, D), ctx.reshape(batch * n_ctx, D)], axis=0)
    cc = jnp.zeros((MOD_ROWS, D), F32).at[:batch].set(c).at[batch].set(c_ctx)
    mod3 = _adaln(cc, w_mod, b_mod)

    def mlp(layer, xin, n_rows, final):
        return _mlp(rw, xin, n_rows, norm_ffn_g[layer], mod3, layer, w_ff1, w_ff2,
                    norm_out_g, final)

    xf = _gdn_layer(rw, 0, xf, mod3, norm_mix_g[0], gdn_w_in[0], gdn_conv_w[0], gdn_a_log[0], gdn_dt_bias[0],
                    gdn_norm_g[0], gdn_w_out[0])
    xf = mlp(0, xf, rw.rows, False)
    xf = _ret_layer(rw, 1, xf, mod3, norm_mix_g[1], ret_w_in[0], ret_decay_logit[0], ret_norm_g[0], ret_w_out[0])
    xf = mlp(1, xf, rw.rows, False)
    xf = _lru_layer(rw, 2, xf, mod3, norm_mix_g[2], lru_w_in[0], lru_conv_w[0], lru_conv_b[0], lru_w_gate[0],
                    lru_b_gate[0], lru_lambda[0], lru_w_out[0])
    xf = mlp(2, xf, rw.rows, False)
    xl = _swa_layer(rw, 3, xf, mod3, norm_mix_g[3], swa_w_in[0], swa_sink[0], swa_w_out[0])
    out = mlp(3, xl, rw.lat_rows, True)
    return out.reshape(batch, n_lat, D)
```

```python
import functools
import math

import jax
import jax.numpy as jnp
from jax import lax
from jax.experimental import pallas as pl
from jax.experimental.pallas import tpu as pltpu

F32 = jnp.float32
BF16 = jnp.bfloat16

D = 1024
D_FF = 4 * D
EPS = 1e-6
DEPTH = 4
ROPE_THETA = 10000.0
GRID_W = 64
CHUNK = 64
MOD_ROWS = 16

GDN_QK_HEADS = 8
GDN_V_HEADS = 16
GDN_HD = 128
GDN_KEY = GDN_QK_HEADS * GDN_HD
GDN_VAL = GDN_V_HEADS * GDN_HD
GDN_CONV = 2 * GDN_KEY + GDN_VAL
GDN_MAIN = GDN_CONV + GDN_VAL

RET_HEADS = 4
RET_DK = 256
RET_DV = 512
RET_KEY = RET_HEADS * RET_DK
RET_VAL = RET_HEADS * RET_DV

LRU_W = 1280
LRU_BLOCKS = 10
LRU_BW = LRU_W // LRU_BLOCKS
LRU_C = 8.0

SWA_QH = 16
SWA_KVH = 4
SWA_HD = 64
SWA_BLOCK = 128
SWA_WINDOW = 128
SWA_QD = SWA_QH * SWA_HD
SWA_KVD = SWA_KVH * SWA_HD

TM_IN = 1024
TM_OUT = 512
TM_CONV = 256
HALO = 16
TN_IN_MAX = 1536
FF_CHUNK = 1024
RET_CHUNK = 256
GDN_BLOCK = 4 * CHUNK
GDN_GROUP = 32
VMEM_LIMIT = 48 * 1024 * 1024
MLP_VMEM_LIMIT = 56 * 1024 * 1024


def _dot(a, b):
    return jnp.dot(a, b, preferred_element_type=F32)


def _dot_nt(a, b):
    return lax.dot_general(a, b, (((1,), (1,)), ((), ())), preferred_element_type=F32)


def _dot_tn(a, b):
    return lax.dot_general(a, b, (((0,), (0,)), ((), ())), preferred_element_type=F32)


def _bf(x):
    return x.astype(BF16)


def _split3(x):
    hi = _bf(x)
    r = x - hi.astype(F32)
    mid = _bf(r)
    lo = _bf(r - mid.astype(F32))
    return hi, mid, lo


def _silu(x):
    return x * jax.nn.sigmoid(x)


def _softplus(x):
    return jnp.maximum(x, 0.0) + jnp.log1p(jnp.exp(-jnp.abs(x)))


def _gelu_tanh(x):
    cdf = 0.5 * (1.0 + jnp.tanh(math.sqrt(2.0 / math.pi) * (x + 0.044715 * (x * x * x))))
    return x * cdf


def _cparams(sem, vmem=None):
    return pltpu.CompilerParams(dimension_semantics=sem, vmem_limit_bytes=VMEM_LIMIT if vmem is None else vmem)


class _Rows:
    def __init__(self, batch, n_lat, n_ctx):
        self.batch, self.n_lat, self.n_ctx = batch, n_lat, n_ctx
        self.lat_rows = batch * n_lat
        self.rows = self.lat_rows + batch * n_ctx
        self.n_steps = (n_lat + n_ctx) // CHUNK
        self.ctx_steps = n_ctx // CHUNK
        self.lat_steps = n_lat // CHUNK

    def mod_row(self, i, tm):
        return jnp.where(i < self.lat_rows // tm, i // (self.n_lat // tm), self.batch)

    def chunk_block(self, b, d, s, blk):
        nc, nl = self.n_ctx // blk, self.n_lat // blk
        c_ctx = jnp.where(d == 0, s, nc - 1 - s)
        c_lat = jnp.where(d == 0, s - nc, nl - 1 - (s - nc))
        return jnp.where(s < nc, self.lat_rows // blk + b * nc + c_ctx, b * nl + c_lat)

    def lat_chunk(self, d, s, blk):
        nc, nl = self.n_ctx // blk, self.n_lat // blk
        return jnp.where(s < nc, 0, jnp.where(d == 0, s - nc, nl - 1 - (s - nc)))


def _adaln_kernel(c_ref, w_ref, b_ref, o_ref):
    s = _silu(c_ref[...])
    w = w_ref[0]
    s_hi = _bf(s)
    s_lo = _bf(s - s_hi.astype(F32))
    w_hi = _bf(w)
    w_lo = _bf(w - w_hi.astype(F32))
    y = _dot(s_hi, w_hi) + (_dot(s_lo, w_hi) + _dot(s_hi, w_lo))
    o_ref[0] = y + b_ref[0]


def _adaln(cc, w_mod, b_mod):
    depth, _, n = w_mod.shape
    tn = 1024
    out = pl.pallas_call(
        _adaln_kernel,
        grid=(depth, n // tn),
        in_specs=[pl.BlockSpec((MOD_ROWS, D), lambda l, j: (0, 0)),
                  pl.BlockSpec((1, D, tn), lambda l, j: (l, 0, j)),
                  pl.BlockSpec((1, 1, tn), lambda l, j: (l, 0, j))],
        out_specs=pl.BlockSpec((1, MOD_ROWS, tn), lambda l, j: (l, 0, j)),
        out_shape=jax.ShapeDtypeStruct((depth, MOD_ROWS, n), F32),
        compiler_params=_cparams(("parallel", "parallel")),
    )(cc, w_mod, b_mod.reshape(depth, 1, n))
    return out.reshape(depth * MOD_ROWS * 6, 1, D)


def _mod_spec(rw, layer, which, tm):
    base = layer * MOD_ROWS * 6
    return pl.BlockSpec((1, 1, D), lambda i, *_: (base + rw.mod_row(i, tm) * 6 + which, 0, 0))


def _norm_mod(x, gain, shift, scale):
    y = x * lax.rsqrt(jnp.mean(x * x, axis=-1, keepdims=True) + EPS) * gain
    return y * (1.0 + scale) + shift


def _inproj_kernel(x_ref, g_ref, sh_ref, sc_ref, w_ref, *rest, has_extra):
    if has_extra:
        w2_ref, o_ref, o2_ref, h_ref = rest
    else:
        o_ref, h_ref = rest

    @pl.when(pl.program_id(1) == 0)
    def _():
        h = _bf(_norm_mod(x_ref[...], g_ref[...], sh_ref[0], sc_ref[0]))
        h_ref[...] = h
        if has_extra:
            o2_ref[...] = _dot(h, w2_ref[...])

    o_ref[...] = _bf(_dot(h_ref[...], w_ref[...]))


def _inproj(rw, x, gain, mod3, layer, w, w_extra=None, n_cols=None):
    n = w.shape[1] if n_cols is None else n_cols
    tm = TM_IN
    tn = max(t for t in range(256, TN_IN_MAX + 1, 256) if n % t == 0)
    has_extra = w_extra is not None
    in_specs = [pl.BlockSpec((tm, D), lambda i, j: (i, 0)),
                pl.BlockSpec((1, D), lambda i, j: (0, 0)),
                _mod_spec(rw, layer, 0, tm),
                _mod_spec(rw, layer, 1, tm),
                pl.BlockSpec((D, tn), lambda i, j: (0, j))]
    out_specs = [pl.BlockSpec((tm, tn), lambda i, j: (i, j))]
    out_shape = [jax.ShapeDtypeStruct((rw.rows, n), BF16)]
    args = [x, gain.reshape(1, D), mod3, mod3, w]
    if has_extra:
        ne = w_extra.shape[1]
        in_specs.append(pl.BlockSpec((D, ne), lambda i, j: (0, 0)))
        out_specs.append(pl.BlockSpec((tm, ne), lambda i, j: (i, 0)))
        out_shape.append(jax.ShapeDtypeStruct((rw.rows, ne), F32))
        args.append(w_extra)
    res = pl.pallas_call(
        functools.partial(_inproj_kernel, has_extra=has_extra),
        grid=(rw.rows // tm, n // tn),
        in_specs=in_specs, out_specs=out_specs, out_shape=out_shape,
        scratch_shapes=[pltpu.VMEM((tm, D), BF16)],
        compiler_params=_cparams(("parallel", "arbitrary")),
    )(*args)
    return res if has_extra else res[0]


def _mlp_kernel(x_ref, g_ref, sh_ref, sc_ref, gt_ref, w1_ref, w2_ref, go_ref, o_ref, h_ref, acc_ref,
                *, final_norm):
    k = pl.program_id(1)

    @pl.when(k == 0)
    def _():
        h_ref[...] = _bf(_norm_mod(x_ref[...], g_ref[...], sh_ref[0], sc_ref[0]))
        acc_ref[...] = jnp.zeros_like(acc_ref)

    a = jnp.square(jnp.maximum(_dot(h_ref[...], _bf(w1_ref[...])), 0.0))
    acc_ref[...] += _dot(_bf(a), _bf(w2_ref[...]))

    @pl.when(k == pl.num_programs(1) - 1)
    def _():
        y = x_ref[...] + gt_ref[0] * acc_ref[...]
        if final_norm:
            y = y * lax.rsqrt(jnp.mean(y * y, axis=-1, keepdims=True) + EPS) * go_ref[...]
        o_ref[...] = y


def _mlp(rw, x, n_rows, gain, mod3, layer, w1, w2, out_gain, final_norm):
    tm, ck = TM_IN, FF_CHUNK
    return pl.pallas_call(
        functools.partial(_mlp_kernel, final_norm=final_norm),
        grid=(n_rows // tm, D_FF // ck),
        in_specs=[pl.BlockSpec((tm, D), lambda i, k: (i, 0)),
                  pl.BlockSpec((1, D), lambda i, k: (0, 0)),
                  _mod_spec(rw, layer, 3, tm),
                  _mod_spec(rw, layer, 4, tm),
                  _mod_spec(rw, layer, 5, tm),
                  pl.BlockSpec((None, D, ck), lambda i, k: (layer, 0, k)),
                  pl.BlockSpec((None, ck, D), lambda i, k: (layer, k, 0)),
                  pl.BlockSpec((1, D), lambda i, k: (0, 0))],
        out_specs=pl.BlockSpec((tm, D), lambda i, k: (i, 0)),
        out_shape=jax.ShapeDtypeStruct((n_rows, D), F32),
        scratch_shapes=[pltpu.VMEM((tm, D), BF16), pltpu.VMEM((tm, D), F32)],
        compiler_params=_cparams(("parallel", "arbitrary"), MLP_VMEM_LIMIT),
    )(x, gain.reshape(1, D), mod3, mod3, mod3, w1, w2, out_gain.reshape(1, D))


OUT_SUB = 256


def _row_blocks(n_rows):
    return [slice(r, r + OUT_SUB) for r in range(0, n_rows, OUT_SUB)]


def _gdn_out_kernel(x_ref, gt_ref, o0_ref, o1_ref, z_ref, ng_ref, w_ref, out_ref, a_ref):
    ng = ng_ref[...]
    for rows in _row_blocks(x_ref.shape[0]):
        for h in range(GDN_V_HEADS):
            sl = slice(h * GDN_HD, (h + 1) * GDN_HD)
            o = o0_ref[0, rows, sl].astype(F32) + o1_ref[0, rows, sl].astype(F32)
            y = o * lax.rsqrt(jnp.mean(o * o, axis=-1, keepdims=True) + EPS) * ng
            a_ref[rows, sl] = _bf(y * _silu(z_ref[rows, sl].astype(F32)))
        out_ref[rows] = x_ref[rows] + gt_ref[0] * _dot(a_ref[rows], w_ref[...])


def _ret_out_kernel(x_ref, gt_ref, o0_ref, o1_ref, z_ref, ng_ref, w_ref, out_ref, a_ref):
    for rows in _row_blocks(x_ref.shape[0]):
        for h in range(RET_HEADS):
            sl = slice(h * RET_DV, (h + 1) * RET_DV)
            o = o0_ref[0, rows, sl].astype(F32) + o1_ref[0, rows, sl].astype(F32)
            mu = jnp.mean(o, axis=-1, keepdims=True)
            oc = o - mu
            var = jnp.mean(oc * oc, axis=-1, keepdims=True)
            y = oc * lax.rsqrt(var + EPS) * ng_ref[:, sl]
            a_ref[rows, sl] = _bf(y * _silu(z_ref[rows, sl].astype(F32)))
        out_ref[rows] = x_ref[rows] + gt_ref[0] * _dot(a_ref[rows], w_ref[...])


def _lru_out_kernel(x_ref, gt_ref, h0_ref, h1_ref, gb_ref, w_ref, out_ref):
    for rows in _row_blocks(x_ref.shape[0]):
        a = (h0_ref[rows] + h1_ref[rows]) * _gelu_tanh(gb_ref[rows].astype(F32))
        out_ref[rows] = x_ref[rows] + gt_ref[0] * _dot(_bf(a), w_ref[...])


def _swa_out_kernel(x_ref, gt_ref, o_ref, w_ref, out_ref):
    out_ref[...] = x_ref[...] + gt_ref[0] * _dot(_bf(o_ref[...]), w_ref[...])


def _outproj_call(kern, rw, n_rows, layer, x, mod3, extra_specs, extra_args, w, scratch):
    tm = TM_OUT
    k = w.shape[0]
    return pl.pallas_call(
        kern,
        grid=(n_rows // tm,),
        in_specs=[pl.BlockSpec((tm, D), lambda i: (i, 0)), _mod_spec(rw, layer, 2, tm)] + extra_specs
                 + [pl.BlockSpec((k, D), lambda i: (0, 0))],
        out_specs=pl.BlockSpec((tm, D), lambda i: (i, 0)),
        out_shape=jax.ShapeDtypeStruct((n_rows, D), F32),
        scratch_shapes=scratch,
        compiler_params=_cparams(("parallel",)),
    )(x, mod3, *extra_args, w)


def _two_dir_out(kern, rw, layer, x, mod3, o, proj, z_block, norm_g, w):
    tm, k = TM_OUT, w.shape[0]
    ng = norm_g.reshape(1, -1)
    specs = [pl.BlockSpec((1, tm, k), lambda i: (0, i, 0)),
             pl.BlockSpec((1, tm, k), lambda i: (1, i, 0)),
             pl.BlockSpec((tm, k), lambda i: (i, z_block)),
             pl.BlockSpec(ng.shape, lambda i: (0, 0))]
    return _outproj_call(kern, rw, rw.rows, layer, x, mod3, specs, [o, o, proj, ng], w,
                         [pltpu.VMEM((tm, k), BF16)])


def _shift_matrices(t):
    delta = lax.broadcasted_iota(jnp.int32, (t, t), 0) - lax.broadcasted_iota(jnp.int32, (t, t), 1)
    return {k: _bf(jnp.where(jnp.logical_or(delta == k, delta == k - t), 1.0, 0.0)) for k in (2, 1, t - 1)}


def _conv_core(x_ref, p_ref, n_ref, w_ref, first, last, shifts=None):
    t = x_ref.shape[0]
    xb = x_ref[...]
    x = xb.astype(F32)
    prev = p_ref[...].astype(F32)[HALO - 8:] * jnp.where(first, 0.0, 1.0)
    nxt = n_ref[...].astype(F32)[:8] * jnp.where(last, 0.0, 1.0)
    if shifts is not None:
        assert xb.dtype == BF16
        r2, r1, rp = _dot(shifts[2], xb), _dot(shifts[1], xb), _dot(shifts[t - 1], xb)
    else:
        r2, r1, rp = pltpu.roll(x, 2, 0), pltpu.roll(x, 1, 0), pltpu.roll(x, t - 1, 0)
    row8 = lax.broadcasted_iota(jnp.int32, prev.shape, 0)
    f2 = jnp.where(row8 < 2, pltpu.roll(prev, 2, 0), r2[:8])
    f1 = jnp.where(row8 < 1, pltpu.roll(prev, 1, 0), r1[:8])
    l1 = jnp.where(row8 == 7, pltpu.roll(nxt, 7, 0), rp[t - 8:])
    xm2 = jnp.concatenate([f2, r2[8:]], axis=0)
    xm1 = jnp.concatenate([f1, r1[8:]], axis=0)
    xp1 = jnp.concatenate([rp[:t - 8], l1], axis=0)
    w = w_ref[...]
    return w[0:1] * xm2 + w[1:2] * xm1 + w[2:3] * x + w[3:4] * xp1


def _seg_flags(rw, i):
    per = rw.n_lat // TM_CONV
    is_ctx = i >= rw.lat_rows // TM_CONV
    assert rw.n_ctx == TM_CONV
    first = jnp.logical_or(is_ctx, i % per == 0)
    last = jnp.logical_or(is_ctx, i % per == per - 1)
    return first, last


def _gdn_conv_kernel(x_ref, p_ref, n_ref, w_ref, ab_ref, alog_ref, dtb_ref, o_ref, g_ref, b_ref, *, rw):
    i, j = pl.program_id(0), pl.program_id(1)
    first, last = _seg_flags(rw, i)
    s = _silu(_conv_core(x_ref, p_ref, n_ref, w_ref, first, last, _shift_matrices(x_ref.shape[0])))
    width = s.shape[1]
    qk_blocks = 2 * GDN_KEY // width

    @pl.when(j < qk_blocks)
    def _():
        for h in range(width // GDN_HD):
            sl = slice(h * GDN_HD, (h + 1) * GDN_HD)
            sh = s[:, sl]
            is_q = j * width + h * GDN_HD < GDN_KEY
            scale = jnp.where(is_q, GDN_HD ** -0.5, 1.0)
            o_ref[:, sl] = sh * (lax.rsqrt(jnp.sum(sh * sh, axis=-1, keepdims=True) + EPS) * scale)

    @pl.when(j >= qk_blocks)
    def _():
        o_ref[...] = s

    @pl.when(j == 0)
    def _():
        ab = ab_ref[...]
        g_ref[...] = -jnp.exp(alog_ref[...]) * _softplus(ab + dtb_ref[...])
        b_ref[...] = jax.nn.sigmoid(ab)


def _conv_specs(rw, cb):
    th = TM_CONV // HALO
    nh = rw.rows // HALO
    return [pl.BlockSpec((TM_CONV, cb), lambda i, j: (i, j)),
            pl.BlockSpec((HALO, cb), lambda i, j: (jnp.maximum(i * th - 1, 0), j)),
            pl.BlockSpec((HALO, cb), lambda i, j: (jnp.minimum((i + 1) * th, nh - 1), j)),
            pl.BlockSpec((4, cb), lambda i, j: (0, j))]


def _dir_masks(d, n):
    ii = lax.broadcasted_iota(jnp.int32, (n, n), 0)
    jj = lax.broadcasted_iota(jnp.int32, (n, n), 1)
    t = (ii - jj) * jnp.where(d == 0, 1, -1)
    return t, t >= 0, t > 0


def _gdn_core_kernel(qkv_ref, g_ref, b_ref, o_ref, s_ref):
    d, s = pl.program_id(1), pl.program_id(2)

    @pl.when(s == 0)
    def _():
        s_ref[...] = jnp.zeros_like(s_ref)

    c, hd = CHUNK, GDN_HD
    rep = GDN_V_HEADS // GDN_QK_HEADS
    heads = range(GDN_V_HEADS)
    n_sub = qkv_ref.shape[0] // c
    subs = range(n_sub)
    offs = [pl.multiple_of(jnp.where(d == 0, i, n_sub - 1 - i) * c, c) for i in subs]
    t, incl, strict = _dir_masks(d, c)
    cum = _bf(jnp.where(incl, 1.0, 0.0))
    cum_t = _bf(jnp.where(t <= 0, 1.0, 0.0))
    g = [g_ref[0, pl.ds(offs[i], c), :] for i in subs]
    beta = [b_ref[0, pl.ds(offs[i], c), :] for i in subs]
    parts = [_split3(g[i]) for i in subs]
    gc = [sum(_dot(cum, p) for p in parts[i]) for i in subs]
    gr = [sum(_dot_tn(p, cum_t) for p in parts[i]) for i in subs]
    gtot = [jnp.sum(g[i], axis=0, keepdims=True) for i in subs]

    def rows(i, lo):
        return qkv_ref[pl.ds(offs[i], c), lo:lo + hd]

    q_l = [[rows(i, hq * hd) for hq in range(GDN_QK_HEADS)] for i in subs]
    k_l = [[rows(i, GDN_KEY + hq * hd) for hq in range(GDN_QK_HEADS)] for i in subs]
    gram = [[_dot_nt(_bf(jnp.concatenate([k, q], axis=0)), _bf(k)) for q, k in zip(q_l[i], k_l[i])]
            for i in subs]
    zeros_cc = jnp.zeros((c, 2 * c), BF16)
    zeros_rhs = jnp.zeros((c, 2 * hd), BF16)
    lane_cc = lax.broadcasted_iota(jnp.int32, (c, 2 * c), 1)
    row_cc = lax.broadcasted_iota(jnp.int32, (c, 2 * c), 0)
    eye_hi = _bf(jnp.where(lane_cc == row_cc + c, 1.0, 0.0))
    is_hi = lane_cc >= c

    def prepare(chains):
        ids = range(len(chains))
        gcol = [gc[i][:, h:h + 1] for i, h in chains]
        bcol = [beta[i][:, h:h + 1] for i, h in chains]
        dec = [jnp.where(incl, jnp.exp(jnp.where(incl, gcol[n] - gr[i][h:h + 1, :], 0.0)), 0.0)
               for n, (i, h) in enumerate(chains)]
        eg = [jnp.exp(x) for x in gcol]
        m = [jnp.where(strict, -(gram[i][h // rep][:c] * dec[n] * bcol[n]), 0.0) for n, (i, h) in enumerate(chains)]
        rhs = [jnp.concatenate([k_l[i][h // rep] * (bcol[n] * eg[n]), rows(i, 2 * GDN_KEY + h * hd) * bcol[n]],
                               axis=1) for n, (i, h) in enumerate(chains)]

        m_hi = [_bf(x) for x in m]
        cb = [jnp.concatenate([m_hi[n], jnp.zeros((c, c), BF16)], axis=1) + eye_hi for n in ids]
        for _ in range(6):
            r = [_dot(cb[n], jnp.concatenate([cb[n], zeros_cc], axis=0)) for n in ids]
            cb = [_bf(r[n]) + jnp.where(is_hi, cb[n], jnp.zeros_like(cb[n])) for n in ids]
        nb = [cb[n] - eye_hi for n in ids]

        def apply_n(y):
            return [_dot(nb[n], jnp.concatenate([zeros_rhs, _bf(y[n])], axis=0)) for n in ids]

        nr = apply_n(rhs)
        x = [rhs[n] + nr[n] for n in ids]
        resid = [_dot(m_hi[n], _bf(x[n])) - nr[n] for n in ids]
        nr = apply_n(resid)
        x = [x[n] + (resid[n] + nr[n]) for n in ids]
        qk = [_bf(gram[i][h // rep][c:] * dec[n]) for n, (i, h) in enumerate(chains)]
        gt = [gtot[i][:, h:h + 1] for i, h in chains]
        kd = [_bf(k_l[i][h // rep] * jnp.exp(gt[n] - gcol[n])) for n, (i, h) in enumerate(chains)]
        wq = [_bf(jnp.concatenate([x[n][:, :hd], q_l[i][h // rep] * eg[n]], axis=0))
              for n, (i, h) in enumerate(chains)]
        decay = [jnp.exp(x) for x in gt]
        return x, qk, kd, wq, decay

    per_group = GDN_GROUP // GDN_V_HEADS
    st = [s_ref[h] for h in heads]
    for first in range(0, n_sub, per_group):
        chunk_ids = range(first, first + per_group)
        x, qk, kd, wq, decay = prepare([(i, h) for i in chunk_ids for h in heads])
        for j, i in enumerate(chunk_ids):
            base = j * GDN_V_HEADS
            r = [_dot(wq[base + h], _bf(st[h])) for h in heads]
            unb = [_bf(x[base + h][:, hd:] - r[h][:c]) for h in heads]
            for h in heads:
                o_ref[0, pl.ds(offs[i], c), h * hd:(h + 1) * hd] = _bf(r[h][c:] + _dot(qk[base + h], unb[h]))
            st = [st[h] * decay[base + h] + _dot_tn(kd[base + h], unb[h]) for h in heads]
    for h in heads:
        s_ref[h] = st[h]


def _gdn_core(rw, qkv, g_dir, b_dir):
    blk = GDN_BLOCK
    rb = lambda b, d, s: rw.chunk_block(b, d, s, blk)
    return pl.pallas_call(
        _gdn_core_kernel,
        grid=(rw.batch, 2, (rw.n_lat + rw.n_ctx) // blk),
        in_specs=[pl.BlockSpec((blk, GDN_CONV), lambda b, d, s: (rb(b, d, s), 0)),
                  pl.BlockSpec((1, blk, GDN_V_HEADS), lambda b, d, s: (d, rb(b, d, s), 0)),
                  pl.BlockSpec((1, blk, GDN_V_HEADS), lambda b, d, s: (d, rb(b, d, s), 0))],
        out_specs=pl.BlockSpec((1, blk, GDN_VAL), lambda b, d, s: (d, rb(b, d, s), 0)),
        out_shape=jax.ShapeDtypeStruct((2, rw.rows, GDN_VAL), BF16),
        scratch_shapes=[pltpu.VMEM((GDN_V_HEADS, GDN_HD, GDN_HD), F32)],
        compiler_params=_cparams(("parallel", "parallel", "arbitrary")),
    )(qkv, g_dir, b_dir)


def _rope_half(x, cos, sin_signed):
    return x * cos + pltpu.roll(x, x.shape[1] // 2, 1) * sin_signed


def _ret_core_kernel(q_ref, k_ref, v_ref, cos_ref, sin_ref, dl_ref, o_ref, s_ref, *, ctx_steps):
    d, s = pl.program_id(1), pl.program_id(2)

    @pl.when(s == 0)
    def _():
        s_ref[...] = jnp.zeros_like(s_ref)

    c = q_ref.shape[0]
    heads = range(RET_HEADS)
    is_lat = s >= ctx_steps
    cos = jnp.where(is_lat, cos_ref[...], 1.0)
    sin = jnp.where(is_lat, sin_ref[...], 0.0)
    t, incl, _ = _dir_masks(d, c)
    tf = t.astype(F32)
    pos = lax.broadcasted_iota(jnp.int32, (c, 1), 0)
    ridx = jnp.where(d == 0, pos + 1, c - pos).astype(F32)
    log_gamma = -_softplus(-dl_ref[0])

    def rope(x):
        hw = RET_DK // 2
        return jnp.concatenate([_rope_half(x[:, :hw], cos[:, :hw], sin[:, :hw]),
                                _rope_half(x[:, hw:], cos[:, hw:], sin[:, hw:])], axis=1)

    lg = [log_gamma[:, h:h + 1] for h in heads]
    q = [rope(q_ref[:, h * RET_DK:(h + 1) * RET_DK].astype(F32)) * (RET_DK ** -0.5) for h in heads]
    k = [rope(k_ref[:, h * RET_DK:(h + 1) * RET_DK].astype(F32)) for h in heads]
    vb = [v_ref[:, h * RET_DV:(h + 1) * RET_DV] for h in heads]
    gcum = [lg[h] * ridx for h in heads]
    dec = [jnp.where(incl, jnp.exp(jnp.where(incl, lg[h] * tf, 0.0)), 0.0) for h in heads]
    qk = [_bf(_dot_nt(_bf(q[h]), _bf(k[h])) * dec[h]) for h in heads]
    st = [s_ref[h] for h in heads]
    for h in heads:
        o_ref[0, :, h * RET_DV:(h + 1) * RET_DV] = _bf(
            _dot(_bf(q[h] * jnp.exp(gcum[h])), _bf(st[h])) + _dot(qk[h], vb[h]))
    gtot = [lg[h] * float(c) for h in heads]
    kd = [_bf(k[h] * jnp.exp(gtot[h] - gcum[h])) for h in heads]
    for h in heads:
        s_ref[h] = st[h] * jnp.exp(gtot[h]) + _dot_tn(kd[h], vb[h])


def _ret_core(rw, proj, cos_tab, sin_tab, decay_logit):
    c = RET_CHUNK
    rb = lambda b, d, s: rw.chunk_block(b, d, s, c)
    dl = jnp.zeros((2, 1, 128), F32).at[:, 0, :RET_HEADS].set(decay_logit)
    return pl.pallas_call(
        functools.partial(_ret_core_kernel, ctx_steps=rw.n_ctx // c),
        grid=(rw.batch, 2, (rw.n_lat + rw.n_ctx) // c),
        in_specs=[pl.BlockSpec((c, RET_KEY), lambda b, d, s: (rb(b, d, s), 0)),
                  pl.BlockSpec((c, RET_KEY), lambda b, d, s: (rb(b, d, s), 1)),
                  pl.BlockSpec((c, RET_VAL), lambda b, d, s: (rb(b, d, s), 1)),
                  pl.BlockSpec((c, RET_DK), lambda b, d, s: (rw.lat_chunk(d, s, c), 0)),
                  pl.BlockSpec((c, RET_DK), lambda b, d, s: (rw.lat_chunk(d, s, c), 0)),
                  pl.BlockSpec((1, 1, 128), lambda b, d, s: (d, 0, 0))],
        out_specs=pl.BlockSpec((1, c, RET_VAL), lambda b, d, s: (d, rb(b, d, s), 0)),
        out_shape=jax.ShapeDtypeStruct((2, rw.rows, RET_VAL), BF16),
        scratch_shapes=[pltpu.VMEM((RET_HEADS, RET_DK, RET_DV), F32)],
        compiler_params=_cparams(("parallel", "parallel", "arbitrary")),
    )(proj, proj, proj, cos_tab, sin_tab, dl)


def _lru_scan_kernel(x_ref, p_ref, n_ref, cw_ref, cb_ref, wg_ref, bg_ref, lam_ref, h_ref, carry_ref, *, rw, rev):
    @pl.when(pl.program_id(1) == 0)
    def _():
        carry_ref[...] = jnp.zeros_like(carry_ref)

    t = x_ref.shape[0]
    tile = rw.chunk_block(pl.program_id(0), 1 if rev else 0, pl.program_id(1), t)
    first, last = _seg_flags(rw, tile)
    xs = _conv_core(x_ref, p_ref, n_ref, cw_ref, first, last) + cb_ref[...]
    row8 = lax.broadcasted_iota(jnp.int32, (t // 8, 8, LRU_BW), 1)
    sp = _softplus(-lam_ref[...])
    for n in range(LRU_BLOCKS):
        sl = slice(n * LRU_BW, (n + 1) * LRU_BW)
        xn = xs[:, sl]
        gates = jax.nn.sigmoid(_dot(_bf(xn), wg_ref[n]) + bg_ref[n])
        log_a = -LRU_C * gates[:, :LRU_BW] * sp[:, sl]
        a = jnp.exp(log_a)
        b = jnp.sqrt(-jnp.tanh(log_a) * (a * a + 1.0)) * gates[:, LRU_BW:] * xn
        groups = t // 8
        a3 = a.reshape(groups, 8, LRU_BW)
        b3 = b.reshape(groups, 8, LRU_BW)
        for k in (1, 2, 4):
            if rev:
                keep = row8 < 8 - k
                shift = 8 - k
            else:
                keep = row8 >= k
                shift = k
            sa = jnp.where(keep, pltpu.roll(a3, shift, 1), 1.0)
            sb = jnp.where(keep, pltpu.roll(b3, shift, 1), 0.0)
            b3 = b3 + a3 * sb
            a3 = a3 * sa
        carry = carry_ref[:, sl]
        for gi in (reversed(range(groups)) if rev else range(groups)):
            hg = a3[gi] * carry + b3[gi]
            carry = hg[0:1] if rev else hg[7:8]
            h_ref[gi * 8:(gi + 1) * 8, sl] = hg
        carry_ref[:, sl] = carry


def _lru_scan(rw, proj, conv_w, conv_b, w_gate, b_gate, lam, rev):
    t = TM_CONV
    th = t // HALO
    nh = rw.rows // HALO
    d = 1 if rev else 0
    rb = lambda b, s: rw.chunk_block(b, d, s, t)
    return pl.pallas_call(
        functools.partial(_lru_scan_kernel, rw=rw, rev=rev),
        grid=(rw.batch, (rw.n_lat + rw.n_ctx) // t),
        in_specs=[pl.BlockSpec((t, LRU_W), lambda b, s: (rb(b, s), 0)),
                  pl.BlockSpec((HALO, LRU_W), lambda b, s: (jnp.maximum(rb(b, s) * th - 1, 0), 0)),
                  pl.BlockSpec((HALO, LRU_W), lambda b, s: (jnp.minimum((rb(b, s) + 1) * th, nh - 1), 0)),
                  pl.BlockSpec((4, LRU_W), lambda b, s: (0, 0)),
                  pl.BlockSpec((1, LRU_W), lambda b, s: (0, 0)),
                  pl.BlockSpec((LRU_BLOCKS, LRU_BW, 2 * LRU_BW), lambda b, s: (0, 0, 0)),
                  pl.BlockSpec((LRU_BLOCKS, 1, 2 * LRU_BW), lambda b, s: (0, 0, 0)),
                  pl.BlockSpec((1, LRU_W), lambda b, s: (0, 0))],
        out_specs=pl.BlockSpec((t, LRU_W), lambda b, s: (rb(b, s), 0)),
        out_shape=jax.ShapeDtypeStruct((rw.rows, LRU_W), F32),
        scratch_shapes=[pltpu.VMEM((1, LRU_W), F32)],
        compiler_params=_cparams(("parallel", "arbitrary")),
    )(proj, proj, proj, conv_w, conv_b.reshape(1, LRU_W), _bf(w_gate),
      b_gate.reshape(LRU_BLOCKS, 1, 2 * LRU_BW), lam.reshape(1, LRU_W))


def _rope16(x, tab):
    cos, s1, s2 = tab[:, :128], tab[:, 128:256], tab[:, 256:]
    cols = []
    for cidx in range(x.shape[1] // 128):
        xc = x[:, cidx * 128:(cidx + 1) * 128]
        cols.append(xc * cos + pltpu.roll(xc, 112, 1) * s1 + pltpu.roll(xc, 16, 1) * s2)
    return cols[0] if len(cols) == 1 else jnp.concatenate(cols, axis=1)


def _swa_kernel(sink_ref, q_ref, kp_ref, kc_ref, kn_ref, vp_ref, vc_ref, vn_ref, kx_ref, vx_ref,
                tp_ref, tc_ref, tn_ref, o_ref):
    qb = pl.program_id(1)
    nb = pl.num_programs(1)
    blk = SWA_BLOCK
    n_loc = 3 * blk
    n_keys = n_loc + kx_ref.shape[0]
    tab_c = tc_ref[...]
    q = _rope16(q_ref[...].astype(F32), tab_c) * (SWA_HD ** -0.5)
    k_all = jnp.concatenate([_rope16(kp_ref[...].astype(F32), tp_ref[...]), _rope16(kc_ref[...].astype(F32), tab_c),
                             _rope16(kn_ref[...].astype(F32), tn_ref[...]), kx_ref[...].astype(F32)], axis=0)
    v_all = jnp.concatenate([vp_ref[...], vc_ref[...], vn_ref[...], vx_ref[...]], axis=0).astype(F32)
    assert SWA_WINDOW >= blk - 1
    qi = lax.broadcasted_iota(jnp.int32, (blk, blk), 0)
    kj = lax.broadcasted_iota(jnp.int32, (blk, blk), 1)
    valid_prev = jnp.logical_and(kj - blk - qi >= -SWA_WINDOW, qb > 0)
    valid_next = jnp.logical_and(kj + blk - qi <= SWA_WINDOW, qb < nb - 1)
    lane = lax.broadcasted_iota(jnp.int32, (n_keys, 128), 1)
    low = lane < SWA_HD
    out_low = lax.broadcasted_iota(jnp.int32, (blk, 128), 1) < SWA_HD

    def expo(scores, sink):
        sc = jnp.concatenate([jnp.where(valid_prev, scores[:, :blk], -jnp.inf), scores[:, blk:2 * blk],
                              jnp.where(valid_next, scores[:, 2 * blk:n_loc], -jnp.inf), scores[:, n_loc:]], axis=1)
        mx = jnp.maximum(jnp.max(sc, axis=-1, keepdims=True), sink)
        return _bf(jnp.exp(sc - mx)), jnp.exp(sink - mx)

    group = SWA_QH // SWA_KVH
    for g in range(SWA_KVH):
        col = slice((g // 2) * 128, (g // 2 + 1) * 128)
        native_low = g % 2 == 0
        sel = low if native_low else jnp.logical_not(low)
        k_nat = jnp.where(sel, k_all[:, col], 0.0)
        v_nat = jnp.where(sel, v_all[:, col], 0.0)
        k_oth = pltpu.roll(k_nat, SWA_HD, 1)
        v_oth = pltpu.roll(v_nat, SWA_HD, 1)
        k_lo, k_hi = (k_nat, k_oth) if native_low else (k_oth, k_nat)
        v_lo, v_hi = (v_nat, v_oth) if native_low else (v_oth, v_nat)
        v_lo = jnp.where(lane == SWA_HD, 1.0, v_lo)
        v_hi = jnp.where(lane == 0, 1.0, v_hi)
        k_lo, k_hi, v_lo, v_hi = _bf(k_lo), _bf(k_hi), _bf(v_lo), _bf(v_hi)
        for cidx in range(group // 2):
            qc = (g * group) // 2 + cidx
            qv = _bf(q[:, qc * 128:(qc + 1) * 128])
            e_lo, sink_lo = expo(_dot_nt(qv, k_lo), sink_ref[2 * qc])
            e_hi, sink_hi = expo(_dot_nt(qv, k_hi), sink_ref[2 * qc + 1])
            a_lo = _dot(e_lo, v_lo)
            a_hi = _dot(e_hi, v_hi)
            inv_lo = 1.0 / (a_lo[:, SWA_HD:SWA_HD + 1] + sink_lo)
            inv_hi = 1.0 / (a_hi[:, 0:1] + sink_hi)
            o_ref[:, qc * 128:(qc + 1) * 128] = jnp.where(out_low, a_lo * inv_lo, a_hi * inv_hi)


def _swa_core(rw, proj, sink, tab):
    blk = SWA_BLOCK
    nb = rw.n_lat // blk
    kcol = SWA_QD // SWA_KVD
    vcol = kcol + 1
    ctx_blk = lambda b: rw.lat_rows // rw.n_ctx + b
    prev = lambda q: jnp.maximum(q - 1, 0)
    nxt = lambda q: jnp.minimum(q + 1, nb - 1)
    kv = lambda col, f: pl.BlockSpec((blk, SWA_KVD), lambda b, q: (b * nb + f(q), col))
    same = lambda q: q
    return pl.pallas_call(
        _swa_kernel,
        grid=(rw.batch, nb),
        in_specs=[pl.BlockSpec(memory_space=pltpu.SMEM),
                  pl.BlockSpec((blk, SWA_QD), lambda b, q: (b * nb + q, 0)),
                  kv(kcol, prev), kv(kcol, same), kv(kcol, nxt),
                  kv(vcol, prev), kv(vcol, same), kv(vcol, nxt),
                  pl.BlockSpec((rw.n_ctx, SWA_KVD), lambda b, q: (ctx_blk(b), kcol)),
                  pl.BlockSpec((rw.n_ctx, SWA_KVD), lambda b, q: (ctx_blk(b), vcol)),
                  pl.BlockSpec((blk, 384), lambda b, q: (prev(q), 0)),
                  pl.BlockSpec((blk, 384), lambda b, q: (q, 0)),
                  pl.BlockSpec((blk, 384), lambda b, q: (nxt(q), 0))],
        out_specs=pl.BlockSpec((blk, SWA_QD), lambda b, q: (b * nb + q, 0)),
        out_shape=jax.ShapeDtypeStruct((rw.lat_rows, SWA_QD), F32),
        compiler_params=_cparams(("parallel", "parallel")),
    )(sink, proj, proj, proj, proj, proj, proj, proj, proj, proj, tab, tab, tab)


def _axial_angles(n_lat, dh):
    t = jnp.arange(n_lat)
    row = (t // GRID_W).astype(F32)
    col = (t % GRID_W).astype(F32)
    nf = dh // 4
    inv = ROPE_THETA ** (-jnp.arange(nf, dtype=F32) / nf)
    return row[:, None] * inv, col[:, None] * inv


def _ret_tables(n_lat):
    ar, ac = _axial_angles(n_lat, RET_DK)
    cos = jnp.concatenate([jnp.cos(ar), jnp.cos(ar), jnp.cos(ac), jnp.cos(ac)], axis=1)
    sin = jnp.concatenate([-jnp.sin(ar), jnp.sin(ar), -jnp.sin(ac), jnp.sin(ac)], axis=1)
    return cos, sin


def _swa_table(n_lat):
    ar, ac = _axial_angles(n_lat, SWA_HD)
    z = jnp.zeros_like(ar)
    cos = jnp.concatenate([jnp.cos(ar), jnp.cos(ar), jnp.cos(ac), jnp.cos(ac)], axis=1)
    s1 = jnp.concatenate([-jnp.sin(ar), z, -jnp.sin(ac), z], axis=1)
    s2 = jnp.concatenate([z, jnp.sin(ar), z, jnp.sin(ac)], axis=1)
    return jnp.concatenate([jnp.tile(cos, (1, 2)), jnp.tile(s1, (1, 2)), jnp.tile(s2, (1, 2))], axis=1)


def _gdn_layer(rw, layer, x, mod3, gain, w_in, conv_w, a_log, dt_bias, norm_g, w_out):
    n_ab = w_in.shape[1] - GDN_MAIN
    w_ab = _bf(jnp.pad(w_in[:, GDN_MAIN:], ((0, 0), (0, 128 - n_ab))))
    proj, ab = _inproj(rw, x, gain, mod3, layer, _bf(w_in), w_ab, n_cols=GDN_MAIN)
    hv = GDN_V_HEADS
    alog_row = jnp.zeros((1, 128), F32).at[0, 0:hv].set(a_log[0]).at[0, 2 * hv:3 * hv].set(a_log[1])
    dtb_row = jnp.zeros((1, 128), F32).at[0, 0:hv].set(dt_bias[0]).at[0, 2 * hv:3 * hv].set(dt_bias[1])
    cb = 2048
    row_spec = pl.BlockSpec((TM_CONV, 128), lambda i, j: (i, 0))
    vec_spec = pl.BlockSpec((1, 128), lambda i, j: (0, 0))
    qkv, g_all, b_all = pl.pallas_call(
        functools.partial(_gdn_conv_kernel, rw=rw),
        grid=(rw.rows // TM_CONV, GDN_CONV // cb),
        in_specs=_conv_specs(rw, cb) + [row_spec, vec_spec, vec_spec],
        out_specs=[pl.BlockSpec((TM_CONV, cb), lambda i, j: (i, j)), row_spec, row_spec],
        out_shape=[jax.ShapeDtypeStruct((rw.rows, GDN_CONV), F32),
                   jax.ShapeDtypeStruct((rw.rows, 128), F32),
                   jax.ShapeDtypeStruct((rw.rows, 128), F32)],
        compiler_params=_cparams(("parallel", "arbitrary")),
    )(proj, proj, proj, conv_w, ab, alog_row, dtb_row)
    g_dir = jnp.stack([g_all[:, 0:hv], g_all[:, 2 * hv:3 * hv]])
    b_dir = jnp.stack([b_all[:, hv:2 * hv], b_all[:, 3 * hv:4 * hv]])
    o = _gdn_core(rw, qkv, g_dir, b_dir)
    return _two_dir_out(_gdn_out_kernel, rw, layer, x, mod3, o, proj, GDN_CONV // GDN_VAL, norm_g, _bf(w_out))


def _ret_layer(rw, layer, x, mod3, gain, w_in, decay_logit, norm_g, w_out):
    proj = _inproj(rw, x, gain, mod3, layer, _bf(w_in))
    cos_tab, sin_tab = _ret_tables(rw.n_lat)
    o = _ret_core(rw, proj, cos_tab, sin_tab, decay_logit)
    return _two_dir_out(_ret_out_kernel, rw, layer, x, mod3, o, proj, (2 * RET_KEY + RET_VAL) // RET_VAL,
                        norm_g, _bf(w_out))


def _lru_layer(rw, layer, x, mod3, gain, w_in, conv_w, conv_b, w_gate, b_gate, lam, w_out):
    proj = _inproj(rw, x, gain, mod3, layer, _bf(w_in))
    h0 = _lru_scan(rw, proj, conv_w, conv_b, w_gate[0], b_gate[0], lam[0], rev=False)
    h1 = _lru_scan(rw, proj, conv_w, conv_b, w_gate[1], b_gate[1], lam[1], rev=True)
    tm = TM_OUT
    specs = [pl.BlockSpec((tm, LRU_W), lambda i: (i, 0)),
             pl.BlockSpec((tm, LRU_W), lambda i: (i, 0)),
             pl.BlockSpec((tm, LRU_W), lambda i: (i, 1))]
    return _outproj_call(_lru_out_kernel, rw, rw.rows, layer, x, mod3, specs, [h0, h1, proj], _bf(w_out), [])


def _swa_layer(rw, layer, x, mod3, gain, w_in, sink, w_out):
    proj = _inproj(rw, x, gain, mod3, layer, _bf(w_in))
    o = _swa_core(rw, proj, sink, _swa_table(rw.n_lat))
    specs = [pl.BlockSpec((TM_OUT, SWA_QD), lambda i: (i, 0))]
    return _outproj_call(_swa_out_kernel, rw, rw.lat_rows, layer, x, mod3, specs, [o], _bf(w_out), [])


def kernel(x, c, ctx, c_ctx, norm_mix_g, norm_ffn_g, w_mod, b_mod, w_ff1, w_ff2, norm_out_g, gdn_w_in, gdn_conv_w, gdn_a_log, gdn_dt_bias, gdn_norm_g, gdn_w_out, ret_w_in, ret_decay_logit, ret_norm_g, ret_w_out, lru_w_in, lru_conv_w, lru_conv_b, lru_w_gate, lru_b_gate, lru_lambda, lru_w_out, swa_w_in, swa_sink, swa_w_out):
    batch, n_lat, _ = x.shape
    n_ctx = ctx.shape[1]
    assert batch + 1 <= MOD_ROWS and n_lat % TM_IN == 0 and (batch * n_ctx) % TM_IN == 0
    assert w_mod.shape[0] == DEPTH and gdn_w_in.shape[0] == 1 and swa_w_in.shape[0] == 1
    rw = _Rows(batch, n_lat, n_ctx)
    xf = jnp.concatenate([x.reshape(rw.lat_rows, D), ctx.reshape(batch * n_ctx, D)], axis=0)
    cc = jnp.zeros((MOD_ROWS, D), F32).at[:batch].set(c).at[batch].set(c_ctx)
    mod3 = _adaln(cc, w_mod, b_mod)

    def mlp(layer, xin, n_rows, final):
        return _mlp(rw, xin, n_rows, norm_ffn_g[layer], mod3, layer, w_ff1, w_ff2,
                    norm_out_g, final)

    xf = _gdn_layer(rw, 0, xf, mod3, norm_mix_g[0], gdn_w_in[0], gdn_conv_w[0], gdn_a_log[0], gdn_dt_bias[0],
                    gdn_norm_g[0], gdn_w_out[0])
    xf = mlp(0, xf, rw.rows, False)
    xf = _ret_layer(rw, 1, xf, mod3, norm_mix_g[1], ret_w_in[0], ret_decay_logit[0], ret_norm_g[0], ret_w_out[0])
    xf = mlp(1, xf, rw.rows, False)
    xf = _lru_layer(rw, 2, xf, mod3, norm_mix_g[2], lru_w_in[0], lru_conv_w[0], lru_conv_b[0], lru_w_gate[0],
                    lru_b_gate[0], lru_lambda[0], lru_w_out[0])
    xf = mlp(2, xf, rw.rows, False)
    xl = _swa_layer(rw, 3, xf, mod3, norm_mix_g[3], swa_w_in[0], swa_sink[0], swa_w_out[0])
    out = mlp(3, xl, rw.lat_rows, True)
    return out.reshape(batch, n_lat, D)
```

```python
import functools
import math

import jax
import jax.numpy as jnp
from jax import lax
from jax.experimental import pallas as pl
from jax.experimental.pallas import tpu as pltpu

F32 = jnp.float32
BF16 = jnp.bfloat16

D = 1024
D_FF = 4 * D
EPS = 1e-6
DEPTH = 4
ROPE_THETA = 10000.0
GRID_W = 64
CHUNK = 64
MOD_ROWS = 16

GDN_QK_HEADS = 8
GDN_V_HEADS = 16
GDN_HD = 128
GDN_KEY = GDN_QK_HEADS * GDN_HD
GDN_VAL = GDN_V_HEADS * GDN_HD
GDN_CONV = 2 * GDN_KEY + GDN_VAL
GDN_MAIN = GDN_CONV + GDN_VAL

RET_HEADS = 4
RET_DK = 256
RET_DV = 512
RET_KEY = RET_HEADS * RET_DK
RET_VAL = RET_HEADS * RET_DV

LRU_W = 1280
LRU_BLOCKS = 10
LRU_BW = LRU_W // LRU_BLOCKS
LRU_C = 8.0

SWA_QH = 16
SWA_KVH = 4
SWA_HD = 64
SWA_BLOCK = 128
SWA_WINDOW = 128
SWA_QD = SWA_QH * SWA_HD
SWA_KVD = SWA_KVH * SWA_HD

TM_IN = 1024
TM_OUT = 512
TM_CONV = 256
HALO = 16
TN_IN_MAX = 1536
FF_CHUNK = 1024
RET_CHUNK = 256
GDN_BLOCK = 4 * CHUNK
GDN_GROUP = 32
VMEM_LIMIT = 48 * 1024 * 1024
MLP_VMEM_LIMIT = 56 * 1024 * 1024


def _dot(a, b):
    return jnp.dot(a, b, preferred_element_type=F32)


def _dot_nt(a, b):
    return lax.dot_general(a, b, (((1,), (1,)), ((), ())), preferred_element_type=F32)


def _dot_tn(a, b):
    return lax.dot_general(a, b, (((0,), (0,)), ((), ())), preferred_element_type=F32)


def _bf(x):
    return x.astype(BF16)


def _split3(x):
    hi = _bf(x)
    r = x - hi.astype(F32)
    mid = _bf(r)
    lo = _bf(r - mid.astype(F32))
    return hi, mid, lo


def _silu(x):
    return x * jax.nn.sigmoid(x)


def _softplus(x):
    return jnp.maximum(x, 0.0) + jnp.log1p(jnp.exp(-jnp.abs(x)))


def _gelu_tanh(x):
    cdf = 0.5 * (1.0 + jnp.tanh(math.sqrt(2.0 / math.pi) * (x + 0.044715 * (x * x * x))))
    return x * cdf


def _cparams(sem, vmem=None):
    return pltpu.CompilerParams(dimension_semantics=sem, vmem_limit_bytes=VMEM_LIMIT if vmem is None else vmem)


class _Rows:
    def __init__(self, batch, n_lat, n_ctx):
        self.batch, self.n_lat, self.n_ctx = batch, n_lat, n_ctx
        self.lat_rows = batch * n_lat
        self.rows = self.lat_rows + batch * n_ctx
        self.n_steps = (n_lat + n_ctx) // CHUNK
        self.ctx_steps = n_ctx // CHUNK
        self.lat_steps = n_lat // CHUNK

    def mod_row(self, i, tm):
        return jnp.where(i < self.lat_rows // tm, i // (self.n_lat // tm), self.batch)

    def chunk_block(self, b, d, s, blk):
        nc, nl = self.n_ctx // blk, self.n_lat // blk
        c_ctx = jnp.where(d == 0, s, nc - 1 - s)
        c_lat = jnp.where(d == 0, s - nc, nl - 1 - (s - nc))
        return jnp.where(s < nc, self.lat_rows // blk + b * nc + c_ctx, b * nl + c_lat)

    def lat_chunk(self, d, s, blk):
        nc, nl = self.n_ctx // blk, self.n_lat // blk
        return jnp.where(s < nc, 0, jnp.where(d == 0, s - nc, nl - 1 - (s - nc)))


def _adaln_kernel(c_ref, w_ref, b_ref, o_ref):
    s = _silu(c_ref[...])
    w = w_ref[0]
    s_hi = _bf(s)
    s_lo = _bf(s - s_hi.astype(F32))
    w_hi = _bf(w)
    w_lo = _bf(w - w_hi.astype(F32))
    y = _dot(s_hi, w_hi) + (_dot(s_lo, w_hi) + _dot(s_hi, w_lo))
    o_ref[0] = y + b_ref[0]


def _adaln(cc, w_mod, b_mod):
    depth, _, n = w_mod.shape
    tn = 1024
    out = pl.pallas_call(
        _adaln_kernel,
        grid=(depth, n // tn),
        in_specs=[pl.BlockSpec((MOD_ROWS, D), lambda l, j: (0, 0)),
                  pl.BlockSpec((1, D, tn), lambda l, j: (l, 0, j)),
                  pl.BlockSpec((1, 1, tn), lambda l, j: (l, 0, j))],
        out_specs=pl.BlockSpec((1, MOD_ROWS, tn), lambda l, j: (l, 0, j)),
        out_shape=jax.ShapeDtypeStruct((depth, MOD_ROWS, n), F32),
        compiler_params=_cparams(("parallel", "parallel")),
    )(cc, w_mod, b_mod.reshape(depth, 1, n))
    return out.reshape(depth * MOD_ROWS * 6, 1, D)


def _mod_spec(rw, layer, which, tm):
    base = layer * MOD_ROWS * 6
    return pl.BlockSpec((1, 1, D), lambda i, *_: (base + rw.mod_row(i, tm) * 6 + which, 0, 0))


def _norm_mod(x, gain, shift, scale):
    y = x * lax.rsqrt(jnp.mean(x * x, axis=-1, keepdims=True) + EPS) * gain
    return y * (1.0 + scale) + shift


def _inproj_kernel(x_ref, g_ref, sh_ref, sc_ref, w_ref, *rest, has_extra):
    if has_extra:
        w2_ref, o_ref, o2_ref, h_ref = rest
    else:
        o_ref, h_ref = rest

    @pl.when(pl.program_id(1) == 0)
    def _():
        h = _bf(_norm_mod(x_ref[...], g_ref[...], sh_ref[0], sc_ref[0]))
        h_ref[...] = h
        if has_extra:
            o2_ref[...] = _dot(h, w2_ref[...])

    o_ref[...] = _bf(_dot(h_ref[...], w_ref[...]))


def _inproj(rw, x, gain, mod3, layer, w, w_extra=None, n_cols=None):
    n = w.shape[1] if n_cols is None else n_cols
    tm = TM_IN
    tn = max(t for t in range(256, TN_IN_MAX + 1, 256) if n % t == 0)
    has_extra = w_extra is not None
    in_specs = [pl.BlockSpec((tm, D), lambda i, j: (i, 0)),
                pl.BlockSpec((1, D), lambda i, j: (0, 0)),
                _mod_spec(rw, layer, 0, tm),
                _mod_spec(rw, layer, 1, tm),
                pl.BlockSpec((D, tn), lambda i, j: (0, j))]
    out_specs = [pl.BlockSpec((tm, tn), lambda i, j: (i, j))]
    out_shape = [jax.ShapeDtypeStruct((rw.rows, n), BF16)]
    args = [x, gain.reshape(1, D), mod3, mod3, w]
    if has_extra:
        ne = w_extra.shape[1]
        in_specs.append(pl.BlockSpec((D, ne), lambda i, j: (0, 0)))
        out_specs.append(pl.BlockSpec((tm, ne), lambda i, j: (i, 0)))
        out_shape.append(jax.ShapeDtypeStruct((rw.rows, ne), F32))
        args.append(w_extra)
    res = pl.pallas_call(
        functools.partial(_inproj_kernel, has_extra=has_extra),
        grid=(rw.rows // tm, n // tn),
        in_specs=in_specs, out_specs=out_specs, out_shape=out_shape,
        scratch_shapes=[pltpu.VMEM((tm, D), BF16)],
        compiler_params=_cparams(("parallel", "arbitrary")),
    )(*args)
    return res if has_extra else res[0]


def _mlp_kernel(x_ref, g_ref, sh_ref, sc_ref, gt_ref, w1_ref, w2_ref, go_ref, o_ref, h_ref, acc_ref,
                *, final_norm):
    k = pl.program_id(1)

    @pl.when(k == 0)
    def _():
        h_ref[...] = _bf(_norm_mod(x_ref[...], g_ref[...], sh_ref[0], sc_ref[0]))
        acc_ref[...] = jnp.zeros_like(acc_ref)

    a = jnp.square(jnp.maximum(_dot(h_ref[...], _bf(w1_ref[...])), 0.0))
    acc_ref[...] += _dot(_bf(a), _bf(w2_ref[...]))

    @pl.when(k == pl.num_programs(1) - 1)
    def _():
        y = x_ref[...] + gt_ref[0] * acc_ref[...]
        if final_norm:
            y = y * lax.rsqrt(jnp.mean(y * y, axis=-1, keepdims=True) + EPS) * go_ref[...]
        o_ref[...] = y


def _mlp(rw, x, n_rows, gain, mod3, layer, w1, w2, out_gain, final_norm):
    tm, ck = TM_IN, FF_CHUNK
    return pl.pallas_call(
        functools.partial(_mlp_kernel, final_norm=final_norm),
        grid=(n_rows // tm, D_FF // ck),
        in_specs=[pl.BlockSpec((tm, D), lambda i, k: (i, 0)),
                  pl.BlockSpec((1, D), lambda i, k: (0, 0)),
                  _mod_spec(rw, layer, 3, tm),
                  _mod_spec(rw, layer, 4, tm),
                  _mod_spec(rw, layer, 5, tm),
                  pl.BlockSpec((None, D, ck), lambda i, k: (layer, 0, k)),
                  pl.BlockSpec((None, ck, D), lambda i, k: (layer, k, 0)),
                  pl.BlockSpec((1, D), lambda i, k: (0, 0))],
        out_specs=pl.BlockSpec((tm, D), lambda i, k: (i, 0)),
        out_shape=jax.ShapeDtypeStruct((n_rows, D), F32),
        scratch_shapes=[pltpu.VMEM((tm, D), BF16), pltpu.VMEM((tm, D), F32)],
        compiler_params=_cparams(("parallel", "arbitrary"), MLP_VMEM_LIMIT),
    )(x, gain.reshape(1, D), mod3, mod3, mod3, w1, w2, out_gain.reshape(1, D))


OUT_SUB = 256


def _row_blocks(n_rows):
    return [slice(r, r + OUT_SUB) for r in range(0, n_rows, OUT_SUB)]


def _gdn_out_kernel(x_ref, gt_ref, o0_ref, o1_ref, z_ref, ng_ref, w_ref, out_ref, a_ref):
    ng = ng_ref[...]
    for rows in _row_blocks(x_ref.shape[0]):
        for h in range(GDN_V_HEADS):
            sl = slice(h * GDN_HD, (h + 1) * GDN_HD)
            o = o0_ref[0, rows, sl].astype(F32) + o1_ref[0, rows, sl].astype(F32)
            y = o * lax.rsqrt(jnp.mean(o * o, axis=-1, keepdims=True) + EPS) * ng
            a_ref[rows, sl] = _bf(y * _silu(z_ref[rows, sl].astype(F32)))
        out_ref[rows] = x_ref[rows] + gt_ref[0] * _dot(a_ref[rows], w_ref[...])


def _ret_out_kernel(x_ref, gt_ref, o0_ref, o1_ref, z_ref, ng_ref, w_ref, out_ref, a_ref):
    for rows in _row_blocks(x_ref.shape[0]):
        for h in range(RET_HEADS):
            sl = slice(h * RET_DV, (h + 1) * RET_DV)
            o = o0_ref[0, rows, sl].astype(F32) + o1_ref[0, rows, sl].astype(F32)
            mu = jnp.mean(o, axis=-1, keepdims=True)
            oc = o - mu
            var = jnp.mean(oc * oc, axis=-1, keepdims=True)
            y = oc * lax.rsqrt(var + EPS) * ng_ref[:, sl]
            a_ref[rows, sl] = _bf(y * _silu(z_ref[rows, sl].astype(F32)))
        out_ref[rows] = x_ref[rows] + gt_ref[0] * _dot(a_ref[rows], w_ref[...])


def _lru_out_kernel(x_ref, gt_ref, h0_ref, h1_ref, gb_ref, w_ref, out_ref):
    for rows in _row_blocks(x_ref.shape[0]):
        a = (h0_ref[rows] + h1_ref[rows]) * _gelu_tanh(gb_ref[rows].astype(F32))
        out_ref[rows] = x_ref[rows] + gt_ref[0] * _dot(_bf(a), w_ref[...])


def _swa_out_kernel(x_ref, gt_ref, o_ref, w_ref, out_ref):
    out_ref[...] = x_ref[...] + gt_ref[0] * _dot(_bf(o_ref[...]), w_ref[...])


def _outproj_call(kern, rw, n_rows, layer, x, mod3, extra_specs, extra_args, w, scratch):
    tm = TM_OUT
    k = w.shape[0]
    return pl.pallas_call(
        kern,
        grid=(n_rows // tm,),
        in_specs=[pl.BlockSpec((tm, D), lambda i: (i, 0)), _mod_spec(rw, layer, 2, tm)] + extra_specs
                 + [pl.BlockSpec((k, D), lambda i: (0, 0))],
        out_specs=pl.BlockSpec((tm, D), lambda i: (i, 0)),
        out_shape=jax.ShapeDtypeStruct((n_rows, D), F32),
        scratch_shapes=scratch,
        compiler_params=_cparams(("parallel",)),
    )(x, mod3, *extra_args, w)


def _two_dir_out(kern, rw, layer, x, mod3, o, proj, z_block, norm_g, w):
    tm, k = TM_OUT, w.shape[0]
    ng = norm_g.reshape(1, -1)
    specs = [pl.BlockSpec((1, tm, k), lambda i: (0, i, 0)),
             pl.BlockSpec((1, tm, k), lambda i: (1, i, 0)),
             pl.BlockSpec((tm, k), lambda i: (i, z_block)),
             pl.BlockSpec(ng.shape, lambda i: (0, 0))]
    return _outproj_call(kern, rw, rw.rows, layer, x, mod3, specs, [o, o, proj, ng], w,
                         [pltpu.VMEM((tm, k), BF16)])


def _shift_matrices(t):
    delta = lax.broadcasted_iota(jnp.int32, (t, t), 0) - lax.broadcasted_iota(jnp.int32, (t, t), 1)
    return {k: _bf(jnp.where(jnp.logical_or(delta == k, delta == k - t), 1.0, 0.0)) for k in (2, 1, t - 1)}


def _conv_core(x_ref, p_ref, n_ref, w_ref, first, last, shifts=None):
    t = x_ref.shape[0]
    xb = x_ref[...]
    x = xb.astype(F32)
    prev = p_ref[...].astype(F32)[HALO - 8:] * jnp.where(first, 0.0, 1.0)
    nxt = n_ref[...].astype(F32)[:8] * jnp.where(last, 0.0, 1.0)
    if shifts is not None:
        assert xb.dtype == BF16
        r2, r1, rp = _dot(shifts[2], xb), _dot(shifts[1], xb), _dot(shifts[t - 1], xb)
    else:
        r2, r1, rp = pltpu.roll(x, 2, 0), pltpu.roll(x, 1, 0), pltpu.roll(x, t - 1, 0)
    row8 = lax.broadcasted_iota(jnp.int32, prev.shape, 0)
    f2 = jnp.where(row8 < 2, pltpu.roll(prev, 2, 0), r2[:8])
    f1 = jnp.where(row8 < 1, pltpu.roll(prev, 1, 0), r1[:8])
    l1 = jnp.where(row8 == 7, pltpu.roll(nxt, 7, 0), rp[t - 8:])
    xm2 = jnp.concatenate([f2, r2[8:]], axis=0)
    xm1 = jnp.concatenate([f1, r1[8:]], axis=0)
    xp1 = jnp.concatenate([rp[:t - 8], l1], axis=0)
    w = w_ref[...]
    return w[0:1] * xm2 + w[1:2] * xm1 + w[2:3] * x + w[3:4] * xp1


def _seg_flags(rw, i):
    per = rw.n_lat // TM_CONV
    is_ctx = i >= rw.lat_rows // TM_CONV
    assert rw.n_ctx == TM_CONV
    first = jnp.logical_or(is_ctx, i % per == 0)
    last = jnp.logical_or(is_ctx, i % per == per - 1)
    return first, last


def _gdn_conv_kernel(x_ref, p_ref, n_ref, w_ref, ab_ref, alog_ref, dtb_ref, o_ref, g_ref, b_ref, *, rw):
    i, j = pl.program_id(0), pl.program_id(1)
    first, last = _seg_flags(rw, i)
    s = _silu(_conv_core(x_ref, p_ref, n_ref, w_ref, first, last, _shift_matrices(x_ref.shape[0])))
    width = s.shape[1]
    qk_blocks = 2 * GDN_KEY // width

    @pl.when(j < qk_blocks)
    def _():
        for h in range(width // GDN_HD):
            sl = slice(h * GDN_HD, (h + 1) * GDN_HD)
            sh = s[:, sl]
            is_q = j * width + h * GDN_HD < GDN_KEY
            scale = jnp.where(is_q, GDN_HD ** -0.5, 1.0)
            o_ref[:, sl] = sh * (lax.rsqrt(jnp.sum(sh * sh, axis=-1, keepdims=True) + EPS) * scale)

    @pl.when(j >= qk_blocks)
    def _():
        o_ref[...] = s

    @pl.when(j == 0)
    def _():
        ab = ab_ref[...]
        g_ref[...] = -jnp.exp(alog_ref[...]) * _softplus(ab + dtb_ref[...])
        b_ref[...] = jax.nn.sigmoid(ab)


def _conv_specs(rw, cb):
    th = TM_CONV // HALO
    nh = rw.rows // HALO
    return [pl.BlockSpec((TM_CONV, cb), lambda i, j: (i, j)),
            pl.BlockSpec((HALO, cb), lambda i, j: (jnp.maximum(i * th - 1, 0), j)),
            pl.BlockSpec((HALO, cb), lambda i, j: (jnp.minimum((i + 1) * th, nh - 1), j)),
            pl.BlockSpec((4, cb), lambda i, j: (0, j))]


def _dir_masks(d, n):
    ii = lax.broadcasted_iota(jnp.int32, (n, n), 0)
    jj = lax.broadcasted_iota(jnp.int32, (n, n), 1)
    t = (ii - jj) * jnp.where(d == 0, 1, -1)
    return t, t >= 0, t > 0


def _gdn_core_kernel(qkv_ref, g_ref, b_ref, o_ref, s_ref):
    d, s = pl.program_id(1), pl.program_id(2)

    @pl.when(s == 0)
    def _():
        s_ref[...] = jnp.zeros_like(s_ref)

    c, hd = CHUNK, GDN_HD
    rep = GDN_V_HEADS // GDN_QK_HEADS
    heads = range(GDN_V_HEADS)
    n_sub = qkv_ref.shape[0] // c
    subs = range(n_sub)
    offs = [pl.multiple_of(jnp.where(d == 0, i, n_sub - 1 - i) * c, c) for i in subs]
    t, incl, strict = _dir_masks(d, c)
    cum = _bf(jnp.where(incl, 1.0, 0.0))
    cum_t = _bf(jnp.where(t <= 0, 1.0, 0.0))
    g = [g_ref[0, pl.ds(offs[i], c), :] for i in subs]
    beta = [b_ref[0, pl.ds(offs[i], c), :] for i in subs]
    parts = [_split3(g[i]) for i in subs]
    gc = [sum(_dot(cum, p) for p in parts[i]) for i in subs]
    gr = [sum(_dot_tn(p, cum_t) for p in parts[i]) for i in subs]
    gtot = [jnp.sum(g[i], axis=0, keepdims=True) for i in subs]

    def rows(i, lo):
        return qkv_ref[pl.ds(offs[i], c), lo:lo + hd]

    q_l = [[rows(i, hq * hd) for hq in range(GDN_QK_HEADS)] for i in subs]
    k_l = [[rows(i, GDN_KEY + hq * hd) for hq in range(GDN_QK_HEADS)] for i in subs]
    gram = [[_dot_nt(_bf(jnp.concatenate([k, q], axis=0)), _bf(k)) for q, k in zip(q_l[i], k_l[i])]
            for i in subs]
    zeros_cc = jnp.zeros((c, 2 * c), BF16)
    zeros_rhs = jnp.zeros((c, 2 * hd), BF16)
    lane_cc = lax.broadcasted_iota(jnp.int32, (c, 2 * c), 1)
    row_cc = lax.broadcasted_iota(jnp.int32, (c, 2 * c), 0)
    eye_hi = _bf(jnp.where(lane_cc == row_cc + c, 1.0, 0.0))
    is_hi = lane_cc >= c

    def prepare(chains):
        ids = range(len(chains))
        gcol = [gc[i][:, h:h + 1] for i, h in chains]
        bcol = [beta[i][:, h:h + 1] for i, h in chains]
        dec = [jnp.where(incl, jnp.exp(jnp.where(incl, gcol[n] - gr[i][h:h + 1, :], 0.0)), 0.0)
               for n, (i, h) in enumerate(chains)]
        eg = [jnp.exp(x) for x in gcol]
        m = [jnp.where(strict, -(gram[i][h // rep][:c] * dec[n] * bcol[n]), 0.0) for n, (i, h) in enumerate(chains)]
        rhs = [jnp.concatenate([k_l[i][h // rep] * (bcol[n] * eg[n]), rows(i, 2 * GDN_KEY + h * hd) * bcol[n]],
                               axis=1) for n, (i, h) in enumerate(chains)]

        m_hi = [_bf(x) for x in m]
        cb = [jnp.concatenate([m_hi[n], jnp.zeros((c, c), BF16)], axis=1) + eye_hi for n in ids]
        for _ in range(6):
            r = [_dot(cb[n], jnp.concatenate([cb[n], zeros_cc], axis=0)) for n in ids]
            cb = [_bf(r[n]) + jnp.where(is_hi, cb[n], jnp.zeros_like(cb[n])) for n in ids]
        nb = [cb[n] - eye_hi for n in ids]

        def apply_n(y):
            return [_dot(nb[n], jnp.concatenate([zeros_rhs, _bf(y[n])], axis=0)) for n in ids]

        nr = apply_n(rhs)
        x = [rhs[n] + nr[n] for n in ids]
        resid = [_dot(m_hi[n], _bf(x[n])) - nr[n] for n in ids]
        nr = apply_n(resid)
        x = [x[n] + (resid[n] + nr[n]) for n in ids]
        qk = [_bf(gram[i][h // rep][c:] * dec[n]) for n, (i, h) in enumerate(chains)]
        gt = [gtot[i][:, h:h + 1] for i, h in chains]
        kd = [_bf(k_l[i][h // rep] * jnp.exp(gt[n] - gcol[n])) for n, (i, h) in enumerate(chains)]
        wq = [_bf(jnp.concatenate([x[n][:, :hd], q_l[i][h // rep] * eg[n]], axis=0))
              for n, (i, h) in enumerate(chains)]
        decay = [jnp.exp(x) for x in gt]
        return x, qk, kd, wq, decay

    per_group = GDN_GROUP // GDN_V_HEADS
    st = [s_ref[h] for h in heads]
    for first in range(0, n_sub, per_group):
        chunk_ids = range(first, first + per_group)
        x, qk, kd, wq, decay = prepare([(i, h) for i in chunk_ids for h in heads])
        for j, i in enumerate(chunk_ids):
            base = j * GDN_V_HEADS
            r = [_dot(wq[base + h], _bf(st[h])) for h in heads]
            unb = [_bf(x[base + h][:, hd:] - r[h][:c]) for h in heads]
            for h in heads:
                o_ref[0, pl.ds(offs[i], c), h * hd:(h + 1) * hd] = _bf(r[h][c:] + _dot(qk[base + h], unb[h]))
            st = [st[h] * decay[base + h] + _dot_tn(kd[base + h], unb[h]) for h in heads]
    for h in heads:
        s_ref[h] = st[h]


def _gdn_core(rw, qkv, g_dir, b_dir):
    blk = GDN_BLOCK
    rb = lambda b, d, s: rw.chunk_block(b, d, s, blk)
    return pl.pallas_call(
        _gdn_core_kernel,
        grid=(rw.batch, 2, (rw.n_lat + rw.n_ctx) // blk),
        in_specs=[pl.BlockSpec((blk, GDN_CONV), lambda b, d, s: (rb(b, d, s), 0)),
                  pl.BlockSpec((1, blk, GDN_V_HEADS), lambda b, d, s: (d, rb(b, d, s), 0)),
                  pl.BlockSpec((1, blk, GDN_V_HEADS), lambda b, d, s: (d, rb(b, d, s), 0))],
        out_specs=pl.BlockSpec((1, blk, GDN_VAL), lambda b, d, s: (d, rb(b, d, s), 0)),
        out_shape=jax.ShapeDtypeStruct((2, rw.rows, GDN_VAL), BF16),
        scratch_shapes=[pltpu.VMEM((GDN_V_HEADS, GDN_HD, GDN_HD), F32)],
        compiler_params=_cparams(("parallel", "parallel", "arbitrary")),
    )(qkv, g_dir, b_dir)


def _rope_half(x, cos, sin_signed):
    return x * cos + pltpu.roll(x, x.shape[1] // 2, 1) * sin_signed


def _ret_core_kernel(q_ref, k_ref, v_ref, cos_ref, sin_ref, dl_ref, o_ref, s_ref, dec_ref, *, ctx_steps):
    d, s = pl.program_id(1), pl.program_id(2)
    c = q_ref.shape[0]
    heads = range(RET_HEADS)
    is_lat = s >= ctx_steps
    cos = jnp.where(is_lat, cos_ref[...], 1.0)
    sin = jnp.where(is_lat, sin_ref[...], 0.0)
    t, incl, _ = _dir_masks(d, c)
    tf = t.astype(F32)
    pos = lax.broadcasted_iota(jnp.int32, (c, 1), 0)
    ridx = jnp.where(d == 0, pos + 1, c - pos).astype(F32)
    log_gamma = -_softplus(-dl_ref[0])

    def rope(x):
        hw = RET_DK // 2
        return jnp.concatenate([_rope_half(x[:, :hw], cos[:, :hw], sin[:, :hw]),
                                _rope_half(x[:, hw:], cos[:, hw:], sin[:, hw:])], axis=1)

    lg = [log_gamma[:, h:h + 1] for h in heads]

    @pl.when(s == 0)
    def _():
        s_ref[...] = jnp.zeros_like(s_ref)
        for h in heads:
            dec_ref[h] = jnp.where(incl, jnp.exp(jnp.where(incl, lg[h] * tf, 0.0)), 0.0)

    q = [rope(q_ref[:, h * RET_DK:(h + 1) * RET_DK].astype(F32)) * (RET_DK ** -0.5) for h in heads]
    k = [rope(k_ref[:, h * RET_DK:(h + 1) * RET_DK].astype(F32)) for h in heads]
    vb = [v_ref[:, h * RET_DV:(h + 1) * RET_DV] for h in heads]
    gcum = [lg[h] * ridx for h in heads]
    qk = [_bf(_dot_nt(_bf(q[h]), _bf(k[h])) * dec_ref[h]) for h in heads]
    st = [s_ref[h] for h in heads]
    for h in heads:
        o_ref[0, :, h * RET_DV:(h + 1) * RET_DV] = _bf(
            _dot(_bf(q[h] * jnp.exp(gcum[h])), _bf(st[h])) + _dot(qk[h], vb[h]))
    gtot = [lg[h] * float(c) for h in heads]
    kd = [_bf(k[h] * jnp.exp(gtot[h] - gcum[h])) for h in heads]
    for h in heads:
        s_ref[h] = st[h] * jnp.exp(gtot[h]) + _dot_tn(kd[h], vb[h])


def _ret_core(rw, proj, cos_tab, sin_tab, decay_logit):
    c = RET_CHUNK
    rb = lambda b, d, s: rw.chunk_block(b, d, s, c)
    dl = jnp.zeros((2, 1, 128), F32).at[:, 0, :RET_HEADS].set(decay_logit)
    return pl.pallas_call(
        functools.partial(_ret_core_kernel, ctx_steps=rw.n_ctx // c),
        grid=(rw.batch, 2, (rw.n_lat + rw.n_ctx) // c),
        in_specs=[pl.BlockSpec((c, RET_KEY), lambda b, d, s: (rb(b, d, s), 0)),
                  pl.BlockSpec((c, RET_KEY), lambda b, d, s: (rb(b, d, s), 1)),
                  pl.BlockSpec((c, RET_VAL), lambda b, d, s: (rb(b, d, s), 1)),
                  pl.BlockSpec((c, RET_DK), lambda b, d, s: (rw.lat_chunk(d, s, c), 0)),
                  pl.BlockSpec((c, RET_DK), lambda b, d, s: (rw.lat_chunk(d, s, c), 0)),
                  pl.BlockSpec((1, 1, 128), lambda b, d, s: (d, 0, 0))],
        out_specs=pl.BlockSpec((1, c, RET_VAL), lambda b, d, s: (d, rb(b, d, s), 0)),
        out_shape=jax.ShapeDtypeStruct((2, rw.rows, RET_VAL), BF16),
        scratch_shapes=[pltpu.VMEM((RET_HEADS, RET_DK, RET_DV), F32), pltpu.VMEM((RET_HEADS, c, c), F32)],
        compiler_params=_cparams(("parallel", "parallel", "arbitrary")),
    )(proj, proj, proj, cos_tab, sin_tab, dl)


def _lru_scan_kernel(x_ref, p_ref, n_ref, cw_ref, cb_ref, wg_ref, bg_ref, lam_ref, h_ref, carry_ref, *, rw, rev):
    @pl.when(pl.program_id(1) == 0)
    def _():
        carry_ref[...] = jnp.zeros_like(carry_ref)

    t = x_ref.shape[0]
    tile = rw.chunk_block(pl.program_id(0), 1 if rev else 0, pl.program_id(1), t)
    first, last = _seg_flags(rw, tile)
    xs = _conv_core(x_ref, p_ref, n_ref, cw_ref, first, last) + cb_ref[...]
    row8 = lax.broadcasted_iota(jnp.int32, (t // 8, 8, LRU_BW), 1)
    sp = _softplus(-lam_ref[...])
    for n in range(LRU_BLOCKS):
        sl = slice(n * LRU_BW, (n + 1) * LRU_BW)
        xn = xs[:, sl]
        gates = jax.nn.sigmoid(_dot(_bf(xn), wg_ref[n]) + bg_ref[n])
        log_a = -LRU_C * gates[:, :LRU_BW] * sp[:, sl]
        a = jnp.exp(log_a)
        b = jnp.sqrt(-jnp.tanh(log_a) * (a * a + 1.0)) * gates[:, LRU_BW:] * xn
        groups = t // 8
        a3 = a.reshape(groups, 8, LRU_BW)
        b3 = b.reshape(groups, 8, LRU_BW)
        for k in (1, 2, 4):
            if rev:
                keep = row8 < 8 - k
                shift = 8 - k
            else:
                keep = row8 >= k
                shift = k
            sa = jnp.where(keep, pltpu.roll(a3, shift, 1), 1.0)
            sb = jnp.where(keep, pltpu.roll(b3, shift, 1), 0.0)
            b3 = b3 + a3 * sb
            a3 = a3 * sa
        carry = carry_ref[:, sl]
        for gi in (reversed(range(groups)) if rev else range(groups)):
            hg = a3[gi] * carry + b3[gi]
            carry = hg[0:1] if rev else hg[7:8]
            h_ref[gi * 8:(gi + 1) * 8, sl] = hg
        carry_ref[:, sl] = carry


def _lru_scan(rw, proj, conv_w, conv_b, w_gate, b_gate, lam, rev):
    t = TM_CONV
    th = t // HALO
    nh = rw.rows // HALO
    d = 1 if rev else 0
    rb = lambda b, s: rw.chunk_block(b, d, s, t)
    return pl.pallas_call(
        functools.partial(_lru_scan_kernel, rw=rw, rev=rev),
        grid=(rw.batch, (rw.n_lat + rw.n_ctx) // t),
        in_specs=[pl.BlockSpec((t, LRU_W), lambda b, s: (rb(b, s), 0)),
                  pl.BlockSpec((HALO, LRU_W), lambda b, s: (jnp.maximum(rb(b, s) * th - 1, 0), 0)),
                  pl.BlockSpec((HALO, LRU_W), lambda b, s: (jnp.minimum((rb(b, s) + 1) * th, nh - 1), 0)),
                  pl.BlockSpec((4, LRU_W), lambda b, s: (0, 0)),
                  pl.BlockSpec((1, LRU_W), lambda b, s: (0, 0)),
                  pl.BlockSpec((LRU_BLOCKS, LRU_BW, 2 * LRU_BW), lambda b, s: (0, 0, 0)),
                  pl.BlockSpec((LRU_BLOCKS, 1, 2 * LRU_BW), lambda b, s: (0, 0, 0)),
                  pl.BlockSpec((1, LRU_W), lambda b, s: (0, 0))],
        out_specs=pl.BlockSpec((t, LRU_W), lambda b, s: (rb(b, s), 0)),
        out_shape=jax.ShapeDtypeStruct((rw.rows, LRU_W), F32),
        scratch_shapes=[pltpu.VMEM((1, LRU_W), F32)],
        compiler_params=_cparams(("parallel", "arbitrary")),
    )(proj, proj, proj, conv_w, conv_b.reshape(1, LRU_W), _bf(w_gate),
      b_gate.reshape(LRU_BLOCKS, 1, 2 * LRU_BW), lam.reshape(1, LRU_W))


def _rope16(x, tab):
    cos, s1, s2 = tab[:, :128], tab[:, 128:256], tab[:, 256:]
    cols = []
    for cidx in range(x.shape[1] // 128):
        xc = x[:, cidx * 128:(cidx + 1) * 128]
        cols.append(xc * cos + pltpu.roll(xc, 112, 1) * s1 + pltpu.roll(xc, 16, 1) * s2)
    return cols[0] if len(cols) == 1 else jnp.concatenate(cols, axis=1)


def _swa_kernel(sink_ref, q_ref, kp_ref, kc_ref, kn_ref, vp_ref, vc_ref, vn_ref, kx_ref, vx_ref,
                tp_ref, tc_ref, tn_ref, o_ref, ctx_ref):
    qb = pl.program_id(1)
    nb = pl.num_programs(1)
    blk = SWA_BLOCK
    n_loc = 3 * blk
    tab_c = tc_ref[...]
    q = _rope16(q_ref[...].astype(F32), tab_c) * (SWA_HD ** -0.5)
    k_loc = jnp.concatenate([_rope16(kp_ref[...].astype(F32), tp_ref[...]), _rope16(kc_ref[...].astype(F32), tab_c),
                             _rope16(kn_ref[...].astype(F32), tn_ref[...])], axis=0)
    v_loc = jnp.concatenate([vp_ref[...], vc_ref[...], vn_ref[...]], axis=0).astype(F32)
    assert SWA_WINDOW >= blk - 1
    qi = lax.broadcasted_iota(jnp.int32, (blk, blk), 0)
    kj = lax.broadcasted_iota(jnp.int32, (blk, blk), 1)
    valid_prev = jnp.logical_and(kj - blk - qi >= -SWA_WINDOW, qb > 0)
    valid_next = jnp.logical_and(kj + blk - qi <= SWA_WINDOW, qb < nb - 1)
    out_low = lax.broadcasted_iota(jnp.int32, (blk, 128), 1) < SWA_HD

    def operands(k_rows, v_rows, g):
        col = slice((g // 2) * 128, (g // 2 + 1) * 128)
        lane = lax.broadcasted_iota(jnp.int32, (k_rows.shape[0], 128), 1)
        native_low = g % 2 == 0
        sel = (lane < SWA_HD) if native_low else (lane >= SWA_HD)
        k_nat = jnp.where(sel, k_rows[:, col], 0.0)
        v_nat = jnp.where(sel, v_rows[:, col], 0.0)
        k_oth = pltpu.roll(k_nat, SWA_HD, 1)
        v_oth = pltpu.roll(v_nat, SWA_HD, 1)
        k_lo, k_hi = (k_nat, k_oth) if native_low else (k_oth, k_nat)
        v_lo, v_hi = (v_nat, v_oth) if native_low else (v_oth, v_nat)
        v_lo = jnp.where(lane == SWA_HD, 1.0, v_lo)
        v_hi = jnp.where(lane == 0, 1.0, v_hi)
        return _bf(k_lo), _bf(k_hi), _bf(v_lo), _bf(v_hi)

    @pl.when(qb == 0)
    def _():
        kx = kx_ref[...].astype(F32)
        vx = vx_ref[...].astype(F32)
        for g in range(SWA_KVH):
            for n, op in enumerate(operands(kx, vx, g)):
                ctx_ref[g, n] = op

    def expo(scores, sink):
        sc = jnp.concatenate([jnp.where(valid_prev, scores[:, :blk], -jnp.inf), scores[:, blk:2 * blk],
                              jnp.where(valid_next, scores[:, 2 * blk:n_loc], -jnp.inf), scores[:, n_loc:]], axis=1)
        mx = jnp.maximum(jnp.max(sc, axis=-1, keepdims=True), sink)
        return _bf(jnp.exp(sc - mx)), jnp.exp(sink - mx)

    group = SWA_QH // SWA_KVH
    for g in range(SWA_KVH):
        k_lo, k_hi, v_lo, v_hi = (jnp.concatenate([loc, ctx_ref[g, n]], axis=0)
                                  for n, loc in enumerate(operands(k_loc, v_loc, g)))
        for cidx in range(group // 2):
            qc = (g * group) // 2 + cidx
            qv = _bf(q[:, qc * 128:(qc + 1) * 128])
            e_lo, sink_lo = expo(_dot_nt(qv, k_lo), sink_ref[2 * qc])
            e_hi, sink_hi = expo(_dot_nt(qv, k_hi), sink_ref[2 * qc + 1])
            a_lo = _dot(e_lo, v_lo)
            a_hi = _dot(e_hi, v_hi)
            inv_lo = 1.0 / (a_lo[:, SWA_HD:SWA_HD + 1] + sink_lo)
            inv_hi = 1.0 / (a_hi[:, 0:1] + sink_hi)
            o_ref[:, qc * 128:(qc + 1) * 128] = jnp.where(out_low, a_lo * inv_lo, a_hi * inv_hi)


def _swa_core(rw, proj, sink, tab):
    blk = SWA_BLOCK
    nb = rw.n_lat // blk
    kcol = SWA_QD // SWA_KVD
    vcol = kcol + 1
    ctx_blk = lambda b: rw.lat_rows // rw.n_ctx + b
    prev = lambda q: jnp.maximum(q - 1, 0)
    nxt = lambda q: jnp.minimum(q + 1, nb - 1)
    kv = lambda col, f: pl.BlockSpec((blk, SWA_KVD), lambda b, q: (b * nb + f(q), col))
    same = lambda q: q
    return pl.pallas_call(
        _swa_kernel,
        grid=(rw.batch, nb),
        in_specs=[pl.BlockSpec(memory_space=pltpu.SMEM),
                  pl.BlockSpec((blk, SWA_QD), lambda b, q: (b * nb + q, 0)),
                  kv(kcol, prev), kv(kcol, same), kv(kcol, nxt),
                  kv(vcol, prev), kv(vcol, same), kv(vcol, nxt),
                  pl.BlockSpec((rw.n_ctx, SWA_KVD), lambda b, q: (ctx_blk(b), kcol)),
                  pl.BlockSpec((rw.n_ctx, SWA_KVD), lambda b, q: (ctx_blk(b), vcol)),
                  pl.BlockSpec((blk, 384), lambda b, q: (prev(q), 0)),
                  pl.BlockSpec((blk, 384), lambda b, q: (q, 0)),
                  pl.BlockSpec((blk, 384), lambda b, q: (nxt(q), 0))],
        out_specs=pl.BlockSpec((blk, SWA_QD), lambda b, q: (b * nb + q, 0)),
        out_shape=jax.ShapeDtypeStruct((rw.lat_rows, SWA_QD), F32),
        scratch_shapes=[pltpu.VMEM((SWA_KVH, 4, rw.n_ctx, 128), BF16)],
        compiler_params=_cparams(("parallel", "arbitrary")),
    )(sink, proj, proj, proj, proj, proj, proj, proj, proj, proj, tab, tab, tab)


def _axial_angles(n_lat, dh):
    t = jnp.arange(n_lat)
    row = (t // GRID_W).astype(F32)
    col = (t % GRID_W).astype(F32)
    nf = dh // 4
    inv = ROPE_THETA ** (-jnp.arange(nf, dtype=F32) / nf)
    return row[:, None] * inv, col[:, None] * inv


def _ret_tables(n_lat):
    ar, ac = _axial_angles(n_lat, RET_DK)
    cos = jnp.concatenate([jnp.cos(ar), jnp.cos(ar), jnp.cos(ac), jnp.cos(ac)], axis=1)
    sin = jnp.concatenate([-jnp.sin(ar), jnp.sin(ar), -jnp.sin(ac), jnp.sin(ac)], axis=1)
    return cos, sin


def _swa_table(n_lat):
    ar, ac = _axial_angles(n_lat, SWA_HD)
    z = jnp.zeros_like(ar)
    cos = jnp.concatenate([jnp.cos(ar), jnp.cos(ar), jnp.cos(ac), jnp.cos(ac)], axis=1)
    s1 = jnp.concatenate([-jnp.sin(ar), z, -jnp.sin(ac), z], axis=1)
    s2 = jnp.concatenate([z, jnp.sin(ar), z, jnp.sin(ac)], axis=1)
    return jnp.concatenate([jnp.tile(cos, (1, 2)), jnp.tile(s1, (1, 2)), jnp.tile(s2, (1, 2))], axis=1)


def _gdn_layer(rw, layer, x, mod3, gain, w_in, conv_w, a_log, dt_bias, norm_g, w_out):
    n_ab = w_in.shape[1] - GDN_MAIN
    w_ab = _bf(jnp.pad(w_in[:, GDN_MAIN:], ((0, 0), (0, 128 - n_ab))))
    proj, ab = _inproj(rw, x, gain, mod3, layer, _bf(w_in), w_ab, n_cols=GDN_MAIN)
    hv = GDN_V_HEADS
    alog_row = jnp.zeros((1, 128), F32).at[0, 0:hv].set(a_log[0]).at[0, 2 * hv:3 * hv].set(a_log[1])
    dtb_row = jnp.zeros((1, 128), F32).at[0, 0:hv].set(dt_bias[0]).at[0, 2 * hv:3 * hv].set(dt_bias[1])
    cb = 2048
    row_spec = pl.BlockSpec((TM_CONV, 128), lambda i, j: (i, 0))
    vec_spec = pl.BlockSpec((1, 128), lambda i, j: (0, 0))
    qkv, g_all, b_all = pl.pallas_call(
        functools.partial(_gdn_conv_kernel, rw=rw),
        grid=(rw.rows // TM_CONV, GDN_CONV // cb),
        in_specs=_conv_specs(rw, cb) + [row_spec, vec_spec, vec_spec],
        out_specs=[pl.BlockSpec((TM_CONV, cb), lambda i, j: (i, j)), row_spec, row_spec],
        out_shape=[jax.ShapeDtypeStruct((rw.rows, GDN_CONV), F32),
                   jax.ShapeDtypeStruct((rw.rows, 128), F32),
                   jax.ShapeDtypeStruct((rw.rows, 128), F32)],
        compiler_params=_cparams(("parallel", "arbitrary")),
    )(proj, proj, proj, conv_w, ab, alog_row, dtb_row)
    g_dir = jnp.stack([g_all[:, 0:hv], g_all[:, 2 * hv:3 * hv]])
    b_dir = jnp.stack([b_all[:, hv:2 * hv], b_all[:, 3 * hv:4 * hv]])
    o = _gdn_core(rw, qkv, g_dir, b_dir)
    return _two_dir_out(_gdn_out_kernel, rw, layer, x, mod3, o, proj, GDN_CONV // GDN_VAL, norm_g, _bf(w_out))


def _ret_layer(rw, layer, x, mod3, gain, w_in, decay_logit, norm_g, w_out):
    proj = _inproj(rw, x, gain, mod3, layer, _bf(w_in))
    cos_tab, sin_tab = _ret_tables(rw.n_lat)
    o = _ret_core(rw, proj, cos_tab, sin_tab, decay_logit)
    return _two_dir_out(_ret_out_kernel, rw, layer, x, mod3, o, proj, (2 * RET_KEY + RET_VAL) // RET_VAL,
                        norm_g, _bf(w_out))


def _lru_layer(rw, layer, x, mod3, gain, w_in, conv_w, conv_b, w_gate, b_gate, lam, w_out):
    proj = _inproj(rw, x, gain, mod3, layer, _bf(w_in))
    h0 = _lru_scan(rw, proj, conv_w, conv_b, w_gate[0], b_gate[0], lam[0], rev=False)
    h1 = _lru_scan(rw, proj, conv_w, conv_b, w_gate[1], b_gate[1], lam[1], rev=True)
    tm = TM_OUT
    specs = [pl.BlockSpec((tm, LRU_W), lambda i: (i, 0)),
             pl.BlockSpec((tm, LRU_W), lambda i: (i, 0)),
             pl.BlockSpec((tm, LRU_W), lambda i: (i, 1))]
    return _outproj_call(_lru_out_kernel, rw, rw.rows, layer, x, mod3, specs, [h0, h1, proj], _bf(w_out), [])


def _swa_layer(rw, layer, x, mod3, gain, w_in, sink, w_out):
    proj = _inproj(rw, x, gain, mod3, layer, _bf(w_in))
    o = _swa_core(rw, proj, sink, _swa_table(rw.n_lat))
    specs = [pl.BlockSpec((TM_OUT, SWA_QD), lambda i: (i, 0))]
    return _outproj_call(_swa_out_kernel, rw, rw.lat_rows, layer, x, mod3, specs, [o], _bf(w_out), [])


def kernel(x, c, ctx, c_ctx, norm_mix_g, norm_ffn_g, w_mod, b_mod, w_ff1, w_ff2, norm_out_g, gdn_w_in, gdn_conv_w, gdn_a_log, gdn_dt_bias, gdn_norm_g, gdn_w_out, ret_w_in, ret_decay_logit, ret_norm_g, ret_w_out, lru_w_in, lru_conv_w, lru_conv_b, lru_w_gate, lru_b_gate, lru_lambda, lru_w_out, swa_w_in, swa_sink, swa_w_out):
    batch, n_lat, _ = x.shape
    n_ctx = ctx.shape[1]
    assert batch + 1 <= MOD_ROWS and n_lat % TM_IN == 0 and (batch * n_ctx) % TM_IN == 0
    assert w_mod.shape[0] == DEPTH and gdn_w_in.shape[0] == 1 and swa_w_in.shape[0] == 1
    rw = _Rows(batch, n_lat, n_ctx)
    xf = jnp.concatenate([x.reshape(rw.lat_rows, D), ctx.reshape(batch * n_ctx, D)], axis=0)
    cc = jnp.zeros((MOD_ROWS, D), F32).at[:batch].set(c).at[batch].set(c_ctx)
    mod3 = _adaln(cc, w_mod, b_mod)

    def mlp(layer, xin, n_rows, final):
        return _mlp(rw, xin, n_rows, norm_ffn_g[layer], mod3, layer, w_ff1, w_ff2,
                    norm_out_g, final)

    xf = _gdn_layer(rw, 0, xf, mod3, norm_mix_g[0], gdn_w_in[0], gdn_conv_w[0], gdn_a_log[0], gdn_dt_bias[0],
                    gdn_norm_g[0], gdn_w_out[0])
    xf = mlp(0, xf, rw.rows, False)
    xf = _ret_layer(rw, 1, xf, mod3, norm_mix_g[1], ret_w_in[0], ret_decay_logit[0], ret_norm_g[0], ret_w_out[0])
    xf = mlp(1, xf, rw.rows, False)
    xf = _lru_layer(rw, 2, xf, mod3, norm_mix_g[2], lru_w_in[0], lru_conv_w[0], lru_conv_b[0], lru_w_gate[0],
                    lru_b_gate[0], lru_lambda[0], lru_w_out[0])
    xf = mlp(2, xf, rw.rows, False)
    xl = _swa_layer(rw, 3, xf, mod3, norm_mix_g[3], swa_w_in[0], swa_sink[0], swa_w_out[0])
    out = mlp(3, xl, rw.lat_rows, True)
    return out.reshape(batch, n_lat, D)
```

```python
import functools
import math

import jax
import jax.numpy as jnp
from jax import lax
from jax.experimental import pallas as pl
from jax.experimental.pallas import tpu as pltpu

F32 = jnp.float32
BF16 = jnp.bfloat16

D = 1024
D_FF = 4 * D
EPS = 1e-6
LOG2E = math.log2(math.e)
DEPTH = 4
ROPE_THETA = 10000.0
GRID_W = 64
CHUNK = 64
MOD_ROWS = 16

GDN_QK_HEADS = 8
GDN_V_HEADS = 16
GDN_HD = 128
GDN_KEY = GDN_QK_HEADS * GDN_HD
GDN_VAL = GDN_V_HEADS * GDN_HD
GDN_CONV = 2 * GDN_KEY + GDN_VAL
GDN_MAIN = GDN_CONV + GDN_VAL

RET_HEADS = 4
RET_DK = 256
RET_DV = 512
RET_KEY = RET_HEADS * RET_DK
RET_VAL = RET_HEADS * RET_DV

LRU_W = 1280
LRU_BLOCKS = 10
LRU_BW = LRU_W // LRU_BLOCKS
LRU_C = 8.0

SWA_QH = 16
SWA_KVH = 4
SWA_HD = 64
SWA_BLOCK = 128
SWA_WINDOW = 128
SWA_QD = SWA_QH * SWA_HD
SWA_KVD = SWA_KVH * SWA_HD

TM_IN = 1024
TM_OUT = 512
TM_CONV = 256
HALO = 16
TN_IN_MAX = 1536
FF_CHUNK = 1024
RET_CHUNK = 256
GDN_BLOCK = 4 * CHUNK
GDN_GROUP = 32
VMEM_LIMIT = 48 * 1024 * 1024
MLP_VMEM_LIMIT = 56 * 1024 * 1024


def _dot(a, b):
    return jnp.dot(a, b, preferred_element_type=F32)


def _dot_nt(a, b):
    return lax.dot_general(a, b, (((1,), (1,)), ((), ())), preferred_element_type=F32)


def _dot_tn(a, b):
    return lax.dot_general(a, b, (((0,), (0,)), ((), ())), preferred_element_type=F32)


def _bf(x):
    return x.astype(BF16)


def _split3(x):
    hi = _bf(x)
    r = x - hi.astype(F32)
    mid = _bf(r)
    lo = _bf(r - mid.astype(F32))
    return hi, mid, lo


def _silu(x):
    return x * jax.nn.sigmoid(x)


def _softplus(x):
    return jnp.maximum(x, 0.0) + jnp.log1p(jnp.exp(-jnp.abs(x)))


def _gelu_tanh(x):
    cdf = 0.5 * (1.0 + jnp.tanh(math.sqrt(2.0 / math.pi) * (x + 0.044715 * (x * x * x))))
    return x * cdf


def _cparams(sem, vmem=None):
    return pltpu.CompilerParams(dimension_semantics=sem, vmem_limit_bytes=VMEM_LIMIT if vmem is None else vmem)


class _Rows:
    def __init__(self, batch, n_lat, n_ctx):
        self.batch, self.n_lat, self.n_ctx = batch, n_lat, n_ctx
        self.lat_rows = batch * n_lat
        self.rows = self.lat_rows + batch * n_ctx
        self.n_steps = (n_lat + n_ctx) // CHUNK
        self.ctx_steps = n_ctx // CHUNK
        self.lat_steps = n_lat // CHUNK

    def mod_row(self, i, tm):
        return jnp.where(i < self.lat_rows // tm, i // (self.n_lat // tm), self.batch)

    def chunk_block(self, b, d, s, blk):
        nc, nl = self.n_ctx // blk, self.n_lat // blk
        c_ctx = jnp.where(d == 0, s, nc - 1 - s)
        c_lat = jnp.where(d == 0, s - nc, nl - 1 - (s - nc))
        return jnp.where(s < nc, self.lat_rows // blk + b * nc + c_ctx, b * nl + c_lat)

    def lat_chunk(self, d, s, blk):
        nc, nl = self.n_ctx // blk, self.n_lat // blk
        return jnp.where(s < nc, 0, jnp.where(d == 0, s - nc, nl - 1 - (s - nc)))


def _adaln_kernel(c_ref, w_ref, b_ref, o_ref):
    s = _silu(c_ref[...])
    w = w_ref[0]
    s_hi = _bf(s)
    s_lo = _bf(s - s_hi.astype(F32))
    w_hi = _bf(w)
    w_lo = _bf(w - w_hi.astype(F32))
    y = _dot(s_hi, w_hi) + (_dot(s_lo, w_hi) + _dot(s_hi, w_lo))
    o_ref[0] = y + b_ref[0]


def _adaln(cc, w_mod, b_mod):
    depth, _, n = w_mod.shape
    tn = 1024
    out = pl.pallas_call(
        _adaln_kernel,
        grid=(depth, n // tn),
        in_specs=[pl.BlockSpec((MOD_ROWS, D), lambda l, j: (0, 0)),
                  pl.BlockSpec((1, D, tn), lambda l, j: (l, 0, j)),
                  pl.BlockSpec((1, 1, tn), lambda l, j: (l, 0, j))],
        out_specs=pl.BlockSpec((1, MOD_ROWS, tn), lambda l, j: (l, 0, j)),
        out_shape=jax.ShapeDtypeStruct((depth, MOD_ROWS, n), F32),
        compiler_params=_cparams(("parallel", "parallel")),
    )(cc, w_mod, b_mod.reshape(depth, 1, n))
    return out.reshape(depth * MOD_ROWS * 6, 1, D)


def _mod_spec(rw, layer, which, tm):
    base = layer * MOD_ROWS * 6
    return pl.BlockSpec((1, 1, D), lambda i, *_: (base + rw.mod_row(i, tm) * 6 + which, 0, 0))


def _norm_mod(x, gain, shift, scale):
    y = x * lax.rsqrt(jnp.mean(x * x, axis=-1, keepdims=True) + EPS) * gain
    return y * (1.0 + scale) + shift


def _inproj_kernel(x_ref, g_ref, sh_ref, sc_ref, w_ref, *rest, has_extra):
    if has_extra:
        w2_ref, o_ref, o2_ref, h_ref = rest
    else:
        o_ref, h_ref = rest

    @pl.when(pl.program_id(1) == 0)
    def _():
        h = _bf(_norm_mod(x_ref[...], g_ref[...], sh_ref[0], sc_ref[0]))
        h_ref[...] = h
        if has_extra:
            o2_ref[...] = _dot(h, w2_ref[...])

    o_ref[...] = _bf(_dot(h_ref[...], w_ref[...]))


def _inproj(rw, x, gain, mod3, layer, w, w_extra=None, n_cols=None):
    n = w.shape[1] if n_cols is None else n_cols
    tm = TM_IN
    tn = max(t for t in range(256, TN_IN_MAX + 1, 256) if n % t == 0)
    has_extra = w_extra is not None
    in_specs = [pl.BlockSpec((tm, D), lambda i, j: (i, 0)),
                pl.BlockSpec((1, D), lambda i, j: (0, 0)),
                _mod_spec(rw, layer, 0, tm),
                _mod_spec(rw, layer, 1, tm),
                pl.BlockSpec((D, tn), lambda i, j: (0, j))]
    out_specs = [pl.BlockSpec((tm, tn), lambda i, j: (i, j))]
    out_shape = [jax.ShapeDtypeStruct((rw.rows, n), BF16)]
    args = [x, gain.reshape(1, D), mod3, mod3, w]
    if has_extra:
        ne = w_extra.shape[1]
        in_specs.append(pl.BlockSpec((D, ne), lambda i, j: (0, 0)))
        out_specs.append(pl.BlockSpec((tm, ne), lambda i, j: (i, 0)))
        out_shape.append(jax.ShapeDtypeStruct((rw.rows, ne), F32))
        args.append(w_extra)
    res = pl.pallas_call(
        functools.partial(_inproj_kernel, has_extra=has_extra),
        grid=(rw.rows // tm, n // tn),
        in_specs=in_specs, out_specs=out_specs, out_shape=out_shape,
        scratch_shapes=[pltpu.VMEM((tm, D), BF16)],
        compiler_params=_cparams(("parallel", "arbitrary")),
    )(*args)
    return res if has_extra else res[0]


def _mlp_kernel(x_ref, g_ref, sh_ref, sc_ref, gt_ref, w1_ref, w2_ref, go_ref, o_ref, h_ref, acc_ref,
                *, final_norm):
    k = pl.program_id(1)

    @pl.when(k == 0)
    def _():
        h_ref[...] = _bf(_norm_mod(x_ref[...], g_ref[...], sh_ref[0], sc_ref[0]))
        acc_ref[...] = jnp.zeros_like(acc_ref)

    a = jnp.square(jnp.maximum(_dot(h_ref[...], _bf(w1_ref[...])), 0.0))
    acc_ref[...] += _dot(_bf(a), _bf(w2_ref[...]))

    @pl.when(k == pl.num_programs(1) - 1)
    def _():
        y = x_ref[...] + gt_ref[0] * acc_ref[...]
        if final_norm:
            y = y * lax.rsqrt(jnp.mean(y * y, axis=-1, keepdims=True) + EPS) * go_ref[...]
        o_ref[...] = y


def _mlp(rw, x, n_rows, gain, mod3, layer, w1, w2, out_gain, final_norm):
    tm, ck = TM_IN, FF_CHUNK
    return pl.pallas_call(
        functools.partial(_mlp_kernel, final_norm=final_norm),
        grid=(n_rows // tm, D_FF // ck),
        in_specs=[pl.BlockSpec((tm, D), lambda i, k: (i, 0)),
                  pl.BlockSpec((1, D), lambda i, k: (0, 0)),
                  _mod_spec(rw, layer, 3, tm),
                  _mod_spec(rw, layer, 4, tm),
                  _mod_spec(rw, layer, 5, tm),
                  pl.BlockSpec((None, D, ck), lambda i, k: (layer, 0, k)),
                  pl.BlockSpec((None, ck, D), lambda i, k: (layer, k, 0)),
                  pl.BlockSpec((1, D), lambda i, k: (0, 0))],
        out_specs=pl.BlockSpec((tm, D), lambda i, k: (i, 0)),
        out_shape=jax.ShapeDtypeStruct((n_rows, D), F32),
        scratch_shapes=[pltpu.VMEM((tm, D), BF16), pltpu.VMEM((tm, D), F32)],
        compiler_params=_cparams(("parallel", "arbitrary"), MLP_VMEM_LIMIT),
    )(x, gain.reshape(1, D), mod3, mod3, mod3, w1, w2, out_gain.reshape(1, D))


OUT_SUB = 256


def _row_blocks(n_rows):
    return [slice(r, r + OUT_SUB) for r in range(0, n_rows, OUT_SUB)]


def _gdn_out_kernel(x_ref, gt_ref, o0_ref, o1_ref, z_ref, ng_ref, w_ref, out_ref, a_ref):
    ng = ng_ref[...]
    for rows in _row_blocks(x_ref.shape[0]):
        for h in range(GDN_V_HEADS):
            sl = slice(h * GDN_HD, (h + 1) * GDN_HD)
            o = o0_ref[0, rows, sl].astype(F32) + o1_ref[0, rows, sl].astype(F32)
            y = o * lax.rsqrt(jnp.mean(o * o, axis=-1, keepdims=True) + EPS) * ng
            a_ref[rows, sl] = _bf(y * _silu(z_ref[rows, sl].astype(F32)))
        out_ref[rows] = x_ref[rows] + gt_ref[0] * _dot(a_ref[rows], w_ref[...])


def _ret_out_kernel(x_ref, gt_ref, o0_ref, o1_ref, z_ref, ng_ref, w_ref, out_ref, a_ref):
    for rows in _row_blocks(x_ref.shape[0]):
        for h in range(RET_HEADS):
            sl = slice(h * RET_DV, (h + 1) * RET_DV)
            o = o0_ref[0, rows, sl].astype(F32) + o1_ref[0, rows, sl].astype(F32)
            mu = jnp.mean(o, axis=-1, keepdims=True)
            oc = o - mu
            var = jnp.mean(oc * oc, axis=-1, keepdims=True)
            y = oc * lax.rsqrt(var + EPS) * ng_ref[:, sl]
            a_ref[rows, sl] = _bf(y * _silu(z_ref[rows, sl].astype(F32)))
        out_ref[rows] = x_ref[rows] + gt_ref[0] * _dot(a_ref[rows], w_ref[...])


def _lru_out_kernel(x_ref, gt_ref, h0_ref, h1_ref, gb_ref, w_ref, out_ref):
    for rows in _row_blocks(x_ref.shape[0]):
        a = (h0_ref[rows] + h1_ref[rows]) * _gelu_tanh(gb_ref[rows].astype(F32))
        out_ref[rows] = x_ref[rows] + gt_ref[0] * _dot(_bf(a), w_ref[...])


def _swa_out_kernel(x_ref, gt_ref, o_ref, w_ref, out_ref):
    out_ref[...] = x_ref[...] + gt_ref[0] * _dot(_bf(o_ref[...]), w_ref[...])


def _outproj_call(kern, rw, n_rows, layer, x, mod3, extra_specs, extra_args, w, scratch):
    tm = TM_OUT
    k = w.shape[0]
    return pl.pallas_call(
        kern,
        grid=(n_rows // tm,),
        in_specs=[pl.BlockSpec((tm, D), lambda i: (i, 0)), _mod_spec(rw, layer, 2, tm)] + extra_specs
                 + [pl.BlockSpec((k, D), lambda i: (0, 0))],
        out_specs=pl.BlockSpec((tm, D), lambda i: (i, 0)),
        out_shape=jax.ShapeDtypeStruct((n_rows, D), F32),
        scratch_shapes=scratch,
        compiler_params=_cparams(("parallel",)),
    )(x, mod3, *extra_args, w)


def _two_dir_out(kern, rw, layer, x, mod3, o, proj, z_block, norm_g, w):
    tm, k = TM_OUT, w.shape[0]
    ng = norm_g.reshape(1, -1)
    specs = [pl.BlockSpec((1, tm, k), lambda i: (0, i, 0)),
             pl.BlockSpec((1, tm, k), lambda i: (1, i, 0)),
             pl.BlockSpec((tm, k), lambda i: (i, z_block)),
             pl.BlockSpec(ng.shape, lambda i: (0, 0))]
    return _outproj_call(kern, rw, rw.rows, layer, x, mod3, specs, [o, o, proj, ng], w,
                         [pltpu.VMEM((tm, k), BF16)])


def _shift_matrices(t):
    delta = lax.broadcasted_iota(jnp.int32, (t, t), 0) - lax.broadcasted_iota(jnp.int32, (t, t), 1)
    return {k: _bf(jnp.where(jnp.logical_or(delta == k, delta == k - t), 1.0, 0.0)) for k in (2, 1, t - 1)}


def _conv_core(x_ref, p_ref, n_ref, w_ref, first, last, shifts=None):
    t = x_ref.shape[0]
    xb = x_ref[...]
    x = xb.astype(F32)
    prev = p_ref[...].astype(F32)[HALO - 8:] * jnp.where(first, 0.0, 1.0)
    nxt = n_ref[...].astype(F32)[:8] * jnp.where(last, 0.0, 1.0)
    if shifts is not None:
        assert xb.dtype == BF16
        r2, r1, rp = _dot(shifts[2], xb), _dot(shifts[1], xb), _dot(shifts[t - 1], xb)
    else:
        r2, r1, rp = pltpu.roll(x, 2, 0), pltpu.roll(x, 1, 0), pltpu.roll(x, t - 1, 0)
    row8 = lax.broadcasted_iota(jnp.int32, prev.shape, 0)
    f2 = jnp.where(row8 < 2, pltpu.roll(prev, 2, 0), r2[:8])
    f1 = jnp.where(row8 < 1, pltpu.roll(prev, 1, 0), r1[:8])
    l1 = jnp.where(row8 == 7, pltpu.roll(nxt, 7, 0), rp[t - 8:])
    xm2 = jnp.concatenate([f2, r2[8:]], axis=0)
    xm1 = jnp.concatenate([f1, r1[8:]], axis=0)
    xp1 = jnp.concatenate([rp[:t - 8], l1], axis=0)
    w = w_ref[...]
    return w[0:1] * xm2 + w[1:2] * xm1 + w[2:3] * x + w[3:4] * xp1


def _seg_flags(rw, i):
    per = rw.n_lat // TM_CONV
    is_ctx = i >= rw.lat_rows // TM_CONV
    assert rw.n_ctx == TM_CONV
    first = jnp.logical_or(is_ctx, i % per == 0)
    last = jnp.logical_or(is_ctx, i % per == per - 1)
    return first, last


def _gdn_conv_kernel(x_ref, p_ref, n_ref, w_ref, ab_ref, alog_ref, dtb_ref, o_ref, g_ref, b_ref, *, rw):
    i, j = pl.program_id(0), pl.program_id(1)
    first, last = _seg_flags(rw, i)
    s = _silu(_conv_core(x_ref, p_ref, n_ref, w_ref, first, last, _shift_matrices(x_ref.shape[0])))
    width = s.shape[1]
    qk_blocks = 2 * GDN_KEY // width

    @pl.when(j < qk_blocks)
    def _():
        for h in range(width // GDN_HD):
            sl = slice(h * GDN_HD, (h + 1) * GDN_HD)
            sh = s[:, sl]
            is_q = j * width + h * GDN_HD < GDN_KEY
            scale = jnp.where(is_q, GDN_HD ** -0.5, 1.0)
            o_ref[:, sl] = sh * (lax.rsqrt(jnp.sum(sh * sh, axis=-1, keepdims=True) + EPS) * scale)

    @pl.when(j >= qk_blocks)
    def _():
        o_ref[...] = s

    @pl.when(j == 0)
    def _():
        ab = ab_ref[...]
        g_ref[...] = -jnp.exp(alog_ref[...]) * _softplus(ab + dtb_ref[...])
        b_ref[...] = jax.nn.sigmoid(ab)


def _conv_specs(rw, cb):
    th = TM_CONV // HALO
    nh = rw.rows // HALO
    return [pl.BlockSpec((TM_CONV, cb), lambda i, j: (i, j)),
            pl.BlockSpec((HALO, cb), lambda i, j: (jnp.maximum(i * th - 1, 0), j)),
            pl.BlockSpec((HALO, cb), lambda i, j: (jnp.minimum((i + 1) * th, nh - 1), j)),
            pl.BlockSpec((4, cb), lambda i, j: (0, j))]


def _dir_masks(d, n):
    ii = lax.broadcasted_iota(jnp.int32, (n, n), 0)
    jj = lax.broadcasted_iota(jnp.int32, (n, n), 1)
    t = (ii - jj) * jnp.where(d == 0, 1, -1)
    return t, t >= 0, t > 0


def _gdn_core_kernel(qkv_ref, g_ref, b_ref, o_ref, s_ref):
    d, s = pl.program_id(1), pl.program_id(2)

    @pl.when(s == 0)
    def _():
        s_ref[...] = jnp.zeros_like(s_ref)

    c, hd = CHUNK, GDN_HD
    rep = GDN_V_HEADS // GDN_QK_HEADS
    heads = range(GDN_V_HEADS)
    n_sub = qkv_ref.shape[0] // c
    subs = range(n_sub)
    offs = [pl.multiple_of(jnp.where(d == 0, i, n_sub - 1 - i) * c, c) for i in subs]
    t, incl, strict = _dir_masks(d, c)
    cum = _bf(jnp.where(incl, 1.0, 0.0))
    cum_t = _bf(jnp.where(t <= 0, 1.0, 0.0))
    g = [g_ref[0, pl.ds(offs[i], c), :] for i in subs]
    beta = [b_ref[0, pl.ds(offs[i], c), :] for i in subs]
    parts = [_split3(g[i]) for i in subs]
    gc = [sum(_dot(cum, p) for p in parts[i]) for i in subs]
    gr = [sum(_dot_tn(p, cum_t) for p in parts[i]) for i in subs]
    gtot = [jnp.sum(g[i], axis=0, keepdims=True) for i in subs]

    def rows(i, lo):
        return qkv_ref[pl.ds(offs[i], c), lo:lo + hd]

    q_l = [[rows(i, hq * hd) for hq in range(GDN_QK_HEADS)] for i in subs]
    k_l = [[rows(i, GDN_KEY + hq * hd) for hq in range(GDN_QK_HEADS)] for i in subs]
    gram = [[_dot_nt(_bf(jnp.concatenate([k, q], axis=0)), _bf(k)) for q, k in zip(q_l[i], k_l[i])]
            for i in subs]
    zeros_cc = jnp.zeros((c, 2 * c), BF16)
    zeros_rhs = jnp.zeros((c, 2 * hd), BF16)
    lane_cc = lax.broadcasted_iota(jnp.int32, (c, 2 * c), 1)
    row_cc = lax.broadcasted_iota(jnp.int32, (c, 2 * c), 0)
    eye_hi = _bf(jnp.where(lane_cc == row_cc + c, 1.0, 0.0))
    is_hi = lane_cc >= c

    def prepare(chains):
        ids = range(len(chains))
        gcol = [gc[i][:, h:h + 1] for i, h in chains]
        bcol = [beta[i][:, h:h + 1] for i, h in chains]
        dec = [jnp.where(incl, jnp.exp(jnp.where(incl, gcol[n] - gr[i][h:h + 1, :], 0.0)), 0.0)
               for n, (i, h) in enumerate(chains)]
        eg = [jnp.exp(x) for x in gcol]
        m = [jnp.where(strict, -(gram[i][h // rep][:c] * dec[n] * bcol[n]), 0.0) for n, (i, h) in enumerate(chains)]
        rhs = [jnp.concatenate([k_l[i][h // rep] * (bcol[n] * eg[n]), rows(i, 2 * GDN_KEY + h * hd) * bcol[n]],
                               axis=1) for n, (i, h) in enumerate(chains)]

        m_hi = [_bf(x) for x in m]
        cb = [jnp.concatenate([m_hi[n], jnp.zeros((c, c), BF16)], axis=1) + eye_hi for n in ids]
        for _ in range(6):
            r = [_dot(cb[n], jnp.concatenate([cb[n], zeros_cc], axis=0)) for n in ids]
            cb = [_bf(r[n]) + jnp.where(is_hi, cb[n], jnp.zeros_like(cb[n])) for n in ids]
        nb = [cb[n] - eye_hi for n in ids]

        def apply_n(y):
            return [_dot(nb[n], jnp.concatenate([zeros_rhs, _bf(y[n])], axis=0)) for n in ids]

        nr = apply_n(rhs)
        x = [rhs[n] + nr[n] for n in ids]
        resid = [_dot(m_hi[n], _bf(x[n])) - nr[n] for n in ids]
        nr = apply_n(resid)
        x = [x[n] + (resid[n] + nr[n]) for n in ids]
        qk = [_bf(gram[i][h // rep][c:] * dec[n]) for n, (i, h) in enumerate(chains)]
        gt = [gtot[i][:, h:h + 1] for i, h in chains]
        kd = [_bf(k_l[i][h // rep] * jnp.exp(gt[n] - gcol[n])) for n, (i, h) in enumerate(chains)]
        wq = [_bf(jnp.concatenate([x[n][:, :hd], q_l[i][h // rep] * eg[n]], axis=0))
              for n, (i, h) in enumerate(chains)]
        decay = [jnp.exp(x) for x in gt]
        return x, qk, kd, wq, decay

    per_group = GDN_GROUP // GDN_V_HEADS
    st = [s_ref[h] for h in heads]
    for first in range(0, n_sub, per_group):
        chunk_ids = range(first, first + per_group)
        x, qk, kd, wq, decay = prepare([(i, h) for i in chunk_ids for h in heads])
        for j, i in enumerate(chunk_ids):
            base = j * GDN_V_HEADS
            r = [_dot(wq[base + h], _bf(st[h])) for h in heads]
            unb = [_bf(x[base + h][:, hd:] - r[h][:c]) for h in heads]
            for h in heads:
                o_ref[0, pl.ds(offs[i], c), h * hd:(h + 1) * hd] = _bf(r[h][c:] + _dot(qk[base + h], unb[h]))
            st = [st[h] * decay[base + h] + _dot_tn(kd[base + h], unb[h]) for h in heads]
    for h in heads:
        s_ref[h] = st[h]


def _gdn_core(rw, qkv, g_dir, b_dir):
    blk = GDN_BLOCK
    rb = lambda b, d, s: rw.chunk_block(b, d, s, blk)
    return pl.pallas_call(
        _gdn_core_kernel,
        grid=(rw.batch, 2, (rw.n_lat + rw.n_ctx) // blk),
        in_specs=[pl.BlockSpec((blk, GDN_CONV), lambda b, d, s: (rb(b, d, s), 0)),
                  pl.BlockSpec((1, blk, GDN_V_HEADS), lambda b, d, s: (d, rb(b, d, s), 0)),
                  pl.BlockSpec((1, blk, GDN_V_HEADS), lambda b, d, s: (d, rb(b, d, s), 0))],
        out_specs=pl.BlockSpec((1, blk, GDN_VAL), lambda b, d, s: (d, rb(b, d, s), 0)),
        out_shape=jax.ShapeDtypeStruct((2, rw.rows, GDN_VAL), BF16),
        scratch_shapes=[pltpu.VMEM((GDN_V_HEADS, GDN_HD, GDN_HD), F32)],
        compiler_params=_cparams(("parallel", "parallel", "arbitrary")),
    )(qkv, g_dir, b_dir)


def _rope_half(x, cos, sin_signed):
    return x * cos + pltpu.roll(x, x.shape[1] // 2, 1) * sin_signed


def _ret_core_kernel(q_ref, k_ref, v_ref, cos_ref, sin_ref, dl_ref, o_ref, s_ref, dec_ref, *, ctx_steps):
    d, s = pl.program_id(1), pl.program_id(2)
    c = q_ref.shape[0]
    heads = range(RET_HEADS)
    is_lat = s >= ctx_steps
    cos = jnp.where(is_lat, cos_ref[...], 1.0)
    sin = jnp.where(is_lat, sin_ref[...], 0.0)
    t, incl, _ = _dir_masks(d, c)
    tf = t.astype(F32)
    pos = lax.broadcasted_iota(jnp.int32, (c, 1), 0)
    ridx = jnp.where(d == 0, pos + 1, c - pos).astype(F32)
    log_gamma = -_softplus(-dl_ref[0])

    def rope(x):
        hw = RET_DK // 2
        return jnp.concatenate([_rope_half(x[:, :hw], cos[:, :hw], sin[:, :hw]),
                                _rope_half(x[:, hw:], cos[:, hw:], sin[:, hw:])], axis=1)

    lg = [log_gamma[:, h:h + 1] for h in heads]

    @pl.when(s == 0)
    def _():
        s_ref[...] = jnp.zeros_like(s_ref)
        for h in heads:
            dec_ref[h] = jnp.where(incl, jnp.exp(jnp.where(incl, lg[h] * tf, 0.0)), 0.0)

    q = [rope(q_ref[:, h * RET_DK:(h + 1) * RET_DK].astype(F32)) * (RET_DK ** -0.5) for h in heads]
    k = [rope(k_ref[:, h * RET_DK:(h + 1) * RET_DK].astype(F32)) for h in heads]
    vb = [v_ref[:, h * RET_DV:(h + 1) * RET_DV] for h in heads]
    gcum = [lg[h] * ridx for h in heads]
    qk = [_bf(_dot_nt(_bf(q[h]), _bf(k[h])) * dec_ref[h]) for h in heads]
    st = [s_ref[h] for h in heads]
    for h in heads:
        o_ref[0, :, h * RET_DV:(h + 1) * RET_DV] = _bf(
            _dot(_bf(q[h] * jnp.exp(gcum[h])), _bf(st[h])) + _dot(qk[h], vb[h]))
    gtot = [lg[h] * float(c) for h in heads]
    kd = [_bf(k[h] * jnp.exp(gtot[h] - gcum[h])) for h in heads]
    for h in heads:
        s_ref[h] = st[h] * jnp.exp(gtot[h]) + _dot_tn(kd[h], vb[h])


def _ret_core(rw, proj, cos_tab, sin_tab, decay_logit):
    c = RET_CHUNK
    rb = lambda b, d, s: rw.chunk_block(b, d, s, c)
    dl = jnp.zeros((2, 1, 128), F32).at[:, 0, :RET_HEADS].set(decay_logit)
    return pl.pallas_call(
        functools.partial(_ret_core_kernel, ctx_steps=rw.n_ctx // c),
        grid=(rw.batch, 2, (rw.n_lat + rw.n_ctx) // c),
        in_specs=[pl.BlockSpec((c, RET_KEY), lambda b, d, s: (rb(b, d, s), 0)),
                  pl.BlockSpec((c, RET_KEY), lambda b, d, s: (rb(b, d, s), 1)),
                  pl.BlockSpec((c, RET_VAL), lambda b, d, s: (rb(b, d, s), 1)),
                  pl.BlockSpec((c, RET_DK), lambda b, d, s: (rw.lat_chunk(d, s, c), 0)),
                  pl.BlockSpec((c, RET_DK), lambda b, d, s: (rw.lat_chunk(d, s, c), 0)),
                  pl.BlockSpec((1, 1, 128), lambda b, d, s: (d, 0, 0))],
        out_specs=pl.BlockSpec((1, c, RET_VAL), lambda b, d, s: (d, rb(b, d, s), 0)),
        out_shape=jax.ShapeDtypeStruct((2, rw.rows, RET_VAL), BF16),
        scratch_shapes=[pltpu.VMEM((RET_HEADS, RET_DK, RET_DV), F32), pltpu.VMEM((RET_HEADS, c, c), F32)],
        compiler_params=_cparams(("parallel", "parallel", "arbitrary")),
    )(proj, proj, proj, cos_tab, sin_tab, dl)


def _lru_scan_kernel(x_ref, p_ref, n_ref, cw_ref, cb_ref, wg_ref, bg_ref, lam_ref, h_ref, carry_ref, *, rw, rev):
    @pl.when(pl.program_id(1) == 0)
    def _():
        carry_ref[...] = jnp.zeros_like(carry_ref)

    t = x_ref.shape[0]
    tile = rw.chunk_block(pl.program_id(0), 1 if rev else 0, pl.program_id(1), t)
    first, last = _seg_flags(rw, tile)
    xs = _conv_core(x_ref, p_ref, n_ref, cw_ref, first, last) + cb_ref[...]
    row8 = lax.broadcasted_iota(jnp.int32, (t // 8, 8, LRU_BW), 1)
    sp = _softplus(-lam_ref[...])
    for n in range(LRU_BLOCKS):
        sl = slice(n * LRU_BW, (n + 1) * LRU_BW)
        xn = xs[:, sl]
        gates = jax.nn.sigmoid(_dot(_bf(xn), wg_ref[n]) + bg_ref[n])
        log_a = -LRU_C * gates[:, :LRU_BW] * sp[:, sl]
        a = jnp.exp(log_a)
        y = -jnp.tanh(log_a) * (a * a + 1.0)
        b = jnp.where(y > 0.0, y * lax.rsqrt(y), 0.0) * gates[:, LRU_BW:] * xn
        groups = t // 8
        a3 = a.reshape(groups, 8, LRU_BW)
        b3 = b.reshape(groups, 8, LRU_BW)
        for k in (1, 2, 4):
            if rev:
                keep = row8 < 8 - k
                shift = 8 - k
            else:
                keep = row8 >= k
                shift = k
            sa = jnp.where(keep, pltpu.roll(a3, shift, 1), 1.0)
            sb = jnp.where(keep, pltpu.roll(b3, shift, 1), 0.0)
            b3 = b3 + a3 * sb
            a3 = a3 * sa
        carry = carry_ref[:, sl]
        for gi in (reversed(range(groups)) if rev else range(groups)):
            hg = a3[gi] * carry + b3[gi]
            carry = hg[0:1] if rev else hg[7:8]
            h_ref[gi * 8:(gi + 1) * 8, sl] = hg
        carry_ref[:, sl] = carry


def _lru_scan(rw, proj, conv_w, conv_b, w_gate, b_gate, lam, rev):
    t = TM_CONV
    th = t // HALO
    nh = rw.rows // HALO
    d = 1 if rev else 0
    rb = lambda b, s: rw.chunk_block(b, d, s, t)
    return pl.pallas_call(
        functools.partial(_lru_scan_kernel, rw=rw, rev=rev),
        grid=(rw.batch, (rw.n_lat + rw.n_ctx) // t),
        in_specs=[pl.BlockSpec((t, LRU_W), lambda b, s: (rb(b, s), 0)),
                  pl.BlockSpec((HALO, LRU_W), lambda b, s: (jnp.maximum(rb(b, s) * th - 1, 0), 0)),
                  pl.BlockSpec((HALO, LRU_W), lambda b, s: (jnp.minimum((rb(b, s) + 1) * th, nh - 1), 0)),
                  pl.BlockSpec((4, LRU_W), lambda b, s: (0, 0)),
                  pl.BlockSpec((1, LRU_W), lambda b, s: (0, 0)),
                  pl.BlockSpec((LRU_BLOCKS, LRU_BW, 2 * LRU_BW), lambda b, s: (0, 0, 0)),
                  pl.BlockSpec((LRU_BLOCKS, 1, 2 * LRU_BW), lambda b, s: (0, 0, 0)),
                  pl.BlockSpec((1, LRU_W), lambda b, s: (0, 0))],
        out_specs=pl.BlockSpec((t, LRU_W), lambda b, s: (rb(b, s), 0)),
        out_shape=jax.ShapeDtypeStruct((rw.rows, LRU_W), F32),
        scratch_shapes=[pltpu.VMEM((1, LRU_W), F32)],
        compiler_params=_cparams(("parallel", "arbitrary")),
    )(proj, proj, proj, conv_w, conv_b.reshape(1, LRU_W), _bf(w_gate),
      b_gate.reshape(LRU_BLOCKS, 1, 2 * LRU_BW), lam.reshape(1, LRU_W))


def _rope16(x, tab):
    cos, s1, s2 = tab[:, :128], tab[:, 128:256], tab[:, 256:]
    cols = []
    for cidx in range(x.shape[1] // 128):
        xc = x[:, cidx * 128:(cidx + 1) * 128]
        cols.append(xc * cos + pltpu.roll(xc, 112, 1) * s1 + pltpu.roll(xc, 16, 1) * s2)
    return cols[0] if len(cols) == 1 else jnp.concatenate(cols, axis=1)


def _swa_kernel(sink_ref, q_ref, kp_ref, kc_ref, kn_ref, vp_ref, vc_ref, vn_ref, kx_ref, vx_ref,
                tp_ref, tc_ref, tn_ref, o_ref):
    qb = pl.program_id(1)
    nb = pl.num_programs(1)
    blk = SWA_BLOCK
    n_loc = 3 * blk
    n_keys = n_loc + kx_ref.shape[0]
    tab_c = tc_ref[...]
    q = _rope16(q_ref[...].astype(F32), tab_c) * (SWA_HD ** -0.5 * LOG2E)
    k_all = jnp.concatenate([_rope16(kp_ref[...].astype(F32), tp_ref[...]), _rope16(kc_ref[...].astype(F32), tab_c),
                             _rope16(kn_ref[...].astype(F32), tn_ref[...]), kx_ref[...].astype(F32)], axis=0)
    v_all = jnp.concatenate([vp_ref[...], vc_ref[...], vn_ref[...], vx_ref[...]], axis=0).astype(F32)
    assert SWA_WINDOW >= blk - 1
    qi = lax.broadcasted_iota(jnp.int32, (blk, blk), 0)
    kj = lax.broadcasted_iota(jnp.int32, (blk, blk), 1)
    valid_prev = jnp.logical_and(kj - blk - qi >= -SWA_WINDOW, qb > 0)
    valid_next = jnp.logical_and(kj + blk - qi <= SWA_WINDOW, qb < nb - 1)
    lane = lax.broadcasted_iota(jnp.int32, (n_keys, 128), 1)
    low = lane < SWA_HD
    out_low = lax.broadcasted_iota(jnp.int32, (blk, 128), 1) < SWA_HD

    def expo(scores, sink):
        sc = jnp.concatenate([jnp.where(valid_prev, scores[:, :blk], -jnp.inf), scores[:, blk:2 * blk],
                              jnp.where(valid_next, scores[:, 2 * blk:n_loc], -jnp.inf), scores[:, n_loc:]], axis=1)
        mx = jnp.maximum(jnp.max(sc, axis=-1, keepdims=True), sink)
        return _bf(jnp.exp2(sc - mx)), jnp.exp2(sink - mx)

    group = SWA_QH // SWA_KVH
    for g in range(SWA_KVH):
        col = slice((g // 2) * 128, (g // 2 + 1) * 128)
        native_low = g % 2 == 0
        sel = low if native_low else jnp.logical_not(low)
        k_nat = jnp.where(sel, k_all[:, col], 0.0)
        v_nat = jnp.where(sel, v_all[:, col], 0.0)
        k_oth = pltpu.roll(k_nat, SWA_HD, 1)
        v_oth = pltpu.roll(v_nat, SWA_HD, 1)
        k_lo, k_hi = (k_nat, k_oth) if native_low else (k_oth, k_nat)
        v_lo, v_hi = (v_nat, v_oth) if native_low else (v_oth, v_nat)
        v_lo = jnp.where(lane == SWA_HD, 1.0, v_lo)
        v_hi = jnp.where(lane == 0, 1.0, v_hi)
        k_lo, k_hi, v_lo, v_hi = _bf(k_lo), _bf(k_hi), _bf(v_lo), _bf(v_hi)
        for cidx in range(group // 2):
            qc = (g * group) // 2 + cidx
            qv = _bf(q[:, qc * 128:(qc + 1) * 128])
            e_lo, sink_lo = expo(_dot_nt(qv, k_lo), sink_ref[2 * qc] * LOG2E)
            e_hi, sink_hi = expo(_dot_nt(qv, k_hi), sink_ref[2 * qc + 1] * LOG2E)
            a_lo = _dot(e_lo, v_lo)
            a_hi = _dot(e_hi, v_hi)
            inv_lo = 1.0 / (a_lo[:, SWA_HD:SWA_HD + 1] + sink_lo)
            inv_hi = 1.0 / (a_hi[:, 0:1] + sink_hi)
            o_ref[:, qc * 128:(qc + 1) * 128] = jnp.where(out_low, a_lo * inv_lo, a_hi * inv_hi)


def _swa_core(rw, proj, sink, tab):
    blk = SWA_BLOCK
    nb = rw.n_lat // blk
    kcol = SWA_QD // SWA_KVD
    vcol = kcol + 1
    ctx_blk = lambda b: rw.lat_rows // rw.n_ctx + b
    prev = lambda q: jnp.maximum(q - 1, 0)
    nxt = lambda q: jnp.minimum(q + 1, nb - 1)
    kv = lambda col, f: pl.BlockSpec((blk, SWA_KVD), lambda b, q: (b * nb + f(q), col))
    same = lambda q: q
    return pl.pallas_call(
        _swa_kernel,
        grid=(rw.batch, nb),
        in_specs=[pl.BlockSpec(memory_space=pltpu.SMEM),
                  pl.BlockSpec((blk, SWA_QD), lambda b, q: (b * nb + q, 0)),
                  kv(kcol, prev), kv(kcol, same), kv(kcol, nxt),
                  kv(vcol, prev), kv(vcol, same), kv(vcol, nxt),
                  pl.BlockSpec((rw.n_ctx, SWA_KVD), lambda b, q: (ctx_blk(b), kcol)),
                  pl.BlockSpec((rw.n_ctx, SWA_KVD), lambda b, q: (ctx_blk(b), vcol)),
                  pl.BlockSpec((blk, 384), lambda b, q: (prev(q), 0)),
                  pl.BlockSpec((blk, 384), lambda b, q: (q, 0)),
                  pl.BlockSpec((blk, 384), lambda b, q: (nxt(q), 0))],
        out_specs=pl.BlockSpec((blk, SWA_QD), lambda b, q: (b * nb + q, 0)),
        out_shape=jax.ShapeDtypeStruct((rw.lat_rows, SWA_QD), F32),
        compiler_params=_cparams(("parallel", "parallel")),
    )(sink, proj, proj, proj, proj, proj, proj, proj, proj, proj, tab, tab, tab)


def _axial_angles(n_lat, dh):
    t = jnp.arange(n_lat)
    row = (t // GRID_W).astype(F32)
    col = (t % GRID_W).astype(F32)
    nf = dh // 4
    inv = ROPE_THETA ** (-jnp.arange(nf, dtype=F32) / nf)
    return row[:, None] * inv, col[:, None] * inv


def _ret_tables(n_lat):
    ar, ac = _axial_angles(n_lat, RET_DK)
    cos = jnp.concatenate([jnp.cos(ar), jnp.cos(ar), jnp.cos(ac), jnp.cos(ac)], axis=1)
    sin = jnp.concatenate([-jnp.sin(ar), jnp.sin(ar), -jnp.sin(ac), jnp.sin(ac)], axis=1)
    return cos, sin


def _swa_table(n_lat):
    ar, ac = _axial_angles(n_lat, SWA_HD)
    z = jnp.zeros_like(ar)
    cos = jnp.concatenate([jnp.cos(ar), jnp.cos(ar), jnp.cos(ac), jnp.cos(ac)], axis=1)
    s1 = jnp.concatenate([-jnp.sin(ar), z, -jnp.sin(ac), z], axis=1)
    s2 = jnp.concatenate([z, jnp.sin(ar), z, jnp.sin(ac)], axis=1)
    return jnp.concatenate([jnp.tile(cos, (1, 2)), jnp.tile(s1, (1, 2)), jnp.tile(s2, (1, 2))], axis=1)


def _gdn_layer(rw, layer, x, mod3, gain, w_in, conv_w, a_log, dt_bias, norm_g, w_out):
    n_ab = w_in.shape[1] - GDN_MAIN
    w_ab = _bf(jnp.pad(w_in[:, GDN_MAIN:], ((0, 0), (0, 128 - n_ab))))
    proj, ab = _inproj(rw, x, gain, mod3, layer, _bf(w_in), w_ab, n_cols=GDN_MAIN)
    hv = GDN_V_HEADS
    alog_row = jnp.zeros((1, 128), F32).at[0, 0:hv].set(a_log[0]).at[0, 2 * hv:3 * hv].set(a_log[1])
    dtb_row = jnp.zeros((1, 128), F32).at[0, 0:hv].set(dt_bias[0]).at[0, 2 * hv:3 * hv].set(dt_bias[1])
    cb = 2048
    row_spec = pl.BlockSpec((TM_CONV, 128), lambda i, j: (i, 0))
    vec_spec = pl.BlockSpec((1, 128), lambda i, j: (0, 0))
    qkv, g_all, b_all = pl.pallas_call(
        functools.partial(_gdn_conv_kernel, rw=rw),
        grid=(rw.rows // TM_CONV, GDN_CONV // cb),
        in_specs=_conv_specs(rw, cb) + [row_spec, vec_spec, vec_spec],
        out_specs=[pl.BlockSpec((TM_CONV, cb), lambda i, j: (i, j)), row_spec, row_spec],
        out_shape=[jax.ShapeDtypeStruct((rw.rows, GDN_CONV), F32),
                   jax.ShapeDtypeStruct((rw.rows, 128), F32),
                   jax.ShapeDtypeStruct((rw.rows, 128), F32)],
        compiler_params=_cparams(("parallel", "arbitrary")),
    )(proj, proj, proj, conv_w, ab, alog_row, dtb_row)
    g_dir = jnp.stack([g_all[:, 0:hv], g_all[:, 2 * hv:3 * hv]])
    b_dir = jnp.stack([b_all[:, hv:2 * hv], b_all[:, 3 * hv:4 * hv]])
    o = _gdn_core(rw, qkv, g_dir, b_dir)
    return _two_dir_out(_gdn_out_kernel, rw, layer, x, mod3, o, proj, GDN_CONV // GDN_VAL, norm_g, _bf(w_out))


def _ret_layer(rw, layer, x, mod3, gain, w_in, decay_logit, norm_g, w_out):
    proj = _inproj(rw, x, gain, mod3, layer, _bf(w_in))
    cos_tab, sin_tab = _ret_tables(rw.n_lat)
    o = _ret_core(rw, proj, cos_tab, sin_tab, decay_logit)
    return _two_dir_out(_ret_out_kernel, rw, layer, x, mod3, o, proj, (2 * RET_KEY + RET_VAL) // RET_VAL,
                        norm_g, _bf(w_out))


def _lru_layer(rw, layer, x, mod3, gain, w_in, conv_w, conv_b, w_gate, b_gate, lam, w_out):
    proj = _inproj(rw, x, gain, mod3, layer, _bf(w_in))
    h0 = _lru_scan(rw, proj, conv_w, conv_b, w_gate[0], b_gate[0], lam[0], rev=False)
    h1 = _lru_scan(rw, proj, conv_w, conv_b, w_gate[1], b_gate[1], lam[1], rev=True)
    tm = TM_OUT
    specs = [pl.BlockSpec((tm, LRU_W), lambda i: (i, 0)),
             pl.BlockSpec((tm, LRU_W), lambda i: (i, 0)),
             pl.BlockSpec((tm, LRU_W), lambda i: (i, 1))]
    return _outproj_call(_lru_out_kernel, rw, rw.rows, layer, x, mod3, specs, [h0, h1, proj], _bf(w_out), [])


def _swa_layer(rw, layer, x, mod3, gain, w_in, sink, w_out):
    proj = _inproj(rw, x, gain, mod3, layer, _bf(w_in))
    o = _swa_core(rw, proj, sink, _swa_table(rw.n_lat))
    specs = [pl.BlockSpec((TM_OUT, SWA_QD), lambda i: (i, 0))]
    return _outproj_call(_swa_out_kernel, rw, rw.lat_rows, layer, x, mod3, specs, [o], _bf(w_out), [])


def kernel(x, c, ctx, c_ctx, norm_mix_g, norm_ffn_g, w_mod, b_mod, w_ff1, w_ff2, norm_out_g, gdn_w_in, gdn_conv_w, gdn_a_log, gdn_dt_bias, gdn_norm_g, gdn_w_out, ret_w_in, ret_decay_logit, ret_norm_g, ret_w_out, lru_w_in, lru_conv_w, lru_conv_b, lru_w_gate, lru_b_gate, lru_lambda, lru_w_out, swa_w_in, swa_sink, swa_w_out):
    batch, n_lat, _ = x.shape
    n_ctx = ctx.shape[1]
    assert batch + 1 <= MOD_ROWS and n_lat % TM_IN == 0 and (batch * n_ctx) % TM_IN == 0
    assert w_mod.shape[0] == DEPTH and gdn_w_in.shape[0] == 1 and swa_w_in.shape[0] == 1
    rw = _Rows(batch, n_lat, n_ctx)
    xf = jnp.concatenate([x.reshape(rw.lat_rows, D), ctx.reshape(batch * n_ctx, D)], axis=0)
    cc = jnp.zeros((MOD_ROWS, D), F32).at[:batch].set(c).at[batch].set(c_ctx)
    mod3 = _adaln(cc, w_mod, b_mod)

    def mlp(layer, xin, n_rows, final):
        return _mlp(rw, xin, n_rows, norm_ffn_g[layer], mod3, layer, w_ff1, w_ff2,
                    norm_out_g, final)

    xf = _gdn_layer(rw, 0, xf, mod3, norm_mix_g[0], gdn_w_in[0], gdn_conv_w[0], gdn_a_log[0], gdn_dt_bias[0],
                    gdn_norm_g[0], gdn_w_out[0])
    xf = mlp(0, xf, rw.rows, False)
    xf = _ret_layer(rw, 1, xf, mod3, norm_mix_g[1], ret_w_in[0], ret_decay_logit[0], ret_norm_g[0], ret_w_out[0])
    xf = mlp(1, xf, rw.rows, False)
    xf = _lru_layer(rw, 2, xf, mod3, norm_mix_g[2], lru_w_in[0], lru_conv_w[0], lru_conv_b[0], lru_w_gate[0],
                    lru_b_gate[0], lru_lambda[0], lru_w_out[0])
    xf = mlp(2, xf, rw.rows, False)
    xl = _swa_layer(rw, 3, xf, mod3, norm_mix_g[3], swa_w_in[0], swa_sink[0], swa_w_out[0])
    out = mlp(3, xl, rw.lat_rows, True)
    return out.reshape(batch, n_lat, D)
```

```python
import functools
import math

import jax
import jax.numpy as jnp
from jax import lax
from jax.experimental import pallas as pl
from jax.experimental.pallas import tpu as pltpu

F32 = jnp.float32
BF16 = jnp.bfloat16

D = 1024
D_FF = 4 * D
EPS = 1e-6
LOG2E = math.log2(math.e)
DEPTH = 4
ROPE_THETA = 10000.0
GRID_W = 64
CHUNK = 64
MOD_ROWS = 16

GDN_QK_HEADS = 8
GDN_V_HEADS = 16
GDN_HD = 128
GDN_KEY = GDN_QK_HEADS * GDN_HD
GDN_VAL = GDN_V_HEADS * GDN_HD
GDN_CONV = 2 * GDN_KEY + GDN_VAL
GDN_MAIN = GDN_CONV + GDN_VAL

RET_HEADS = 4
RET_DK = 256
RET_DV = 512
RET_KEY = RET_HEADS * RET_DK
RET_VAL = RET_HEADS * RET_DV

LRU_W = 1280
LRU_BLOCKS = 10
LRU_BW = LRU_W // LRU_BLOCKS
LRU_C = 8.0

SWA_QH = 16
SWA_KVH = 4
SWA_HD = 64
SWA_BLOCK = 128
SWA_WINDOW = 128
SWA_QD = SWA_QH * SWA_HD
SWA_KVD = SWA_KVH * SWA_HD

TM_IN = 1024
TM_OUT = 512
TM_CONV = 256
HALO = 16
TN_IN_MAX = 1536
FF_CHUNK = 1024
RET_CHUNK = 256
GDN_BLOCK = 4 * CHUNK
GDN_GROUP = 32
VMEM_LIMIT = 48 * 1024 * 1024
MLP_VMEM_LIMIT = 56 * 1024 * 1024


def _dot(a, b):
    return jnp.dot(a, b, preferred_element_type=F32)


def _dot_nt(a, b):
    return lax.dot_general(a, b, (((1,), (1,)), ((), ())), preferred_element_type=F32)


def _dot_tn(a, b):
    return lax.dot_general(a, b, (((0,), (0,)), ((), ())), preferred_element_type=F32)


def _bf(x):
    return x.astype(BF16)


def _split3(x):
    hi = _bf(x)
    r = x - hi.astype(F32)
    mid = _bf(r)
    lo = _bf(r - mid.astype(F32))
    return hi, mid, lo


def _silu(x):
    return x * jax.nn.sigmoid(x)


def _softplus(x):
    return jnp.maximum(x, 0.0) + jnp.log1p(jnp.exp(-jnp.abs(x)))


def _gelu_tanh(x):
    cdf = 0.5 * (1.0 + jnp.tanh(math.sqrt(2.0 / math.pi) * (x + 0.044715 * (x * x * x))))
    return x * cdf


def _cparams(sem, vmem=None):
    return pltpu.CompilerParams(dimension_semantics=sem, vmem_limit_bytes=VMEM_LIMIT if vmem is None else vmem)


class _Rows:
    def __init__(self, batch, n_lat, n_ctx):
        self.batch, self.n_lat, self.n_ctx = batch, n_lat, n_ctx
        self.lat_rows = batch * n_lat
        self.rows = self.lat_rows + batch * n_ctx

    def mod_row(self, i, tm):
        return jnp.where(i < self.lat_rows // tm, i // (self.n_lat // tm), self.batch)

    def chunk_block(self, b, d, s, blk):
        nc, nl = self.n_ctx // blk, self.n_lat // blk
        c_ctx = jnp.where(d == 0, s, nc - 1 - s)
        c_lat = jnp.where(d == 0, s - nc, nl - 1 - (s - nc))
        return jnp.where(s < nc, self.lat_rows // blk + b * nc + c_ctx, b * nl + c_lat)

    def lat_chunk(self, d, s, blk):
        nc, nl = self.n_ctx // blk, self.n_lat // blk
        return jnp.where(s < nc, 0, jnp.where(d == 0, s - nc, nl - 1 - (s - nc)))


def _adaln_kernel(c_ref, w_ref, b_ref, o_ref):
    s = _silu(c_ref[...])
    w = w_ref[0]
    s_hi = _bf(s)
    s_lo = _bf(s - s_hi.astype(F32))
    w_hi = _bf(w)
    w_lo = _bf(w - w_hi.astype(F32))
    y = _dot(s_hi, w_hi) + (_dot(s_lo, w_hi) + _dot(s_hi, w_lo))
    o_ref[0] = y + b_ref[0]


def _adaln(cc, w_mod, b_mod):
    depth, _, n = w_mod.shape
    tn = 1024
    out = pl.pallas_call(
        _adaln_kernel,
        grid=(depth, n // tn),
        in_specs=[pl.BlockSpec((MOD_ROWS, D), lambda l, j: (0, 0)),
                  pl.BlockSpec((1, D, tn), lambda l, j: (l, 0, j)),
                  pl.BlockSpec((1, 1, tn), lambda l, j: (l, 0, j))],
        out_specs=pl.BlockSpec((1, MOD_ROWS, tn), lambda l, j: (l, 0, j)),
        out_shape=jax.ShapeDtypeStruct((depth, MOD_ROWS, n), F32),
        compiler_params=_cparams(("parallel", "parallel")),
    )(cc, w_mod, b_mod.reshape(depth, 1, n))
    return out.reshape(depth * MOD_ROWS * 6, 1, D)


def _mod_spec(rw, layer, which, tm):
    base = layer * MOD_ROWS * 6
    return pl.BlockSpec((1, 1, D), lambda i, *_: (base + rw.mod_row(i, tm) * 6 + which, 0, 0))


def _norm_mod(x, gain, shift, scale):
    y = x * lax.rsqrt(jnp.mean(x * x, axis=-1, keepdims=True) + EPS) * gain
    return y * (1.0 + scale) + shift


def _pick_rows(i, n_lat_tiles, x_ref, xc_ref, rows=slice(None)):
    return jnp.where(i < n_lat_tiles, x_ref[rows], xc_ref[rows])


def _stream_specs(rw, tm, split):
    if not split:
        return [pl.BlockSpec((tm, D), lambda i, *_: (i, 0))]
    nl = rw.lat_rows // tm
    return [pl.BlockSpec((tm, D), lambda i, *_: (jnp.minimum(i, nl - 1), 0)),
            pl.BlockSpec((tm, D), lambda i, *_: (jnp.maximum(i - nl, 0), 0))]


def _inproj_kernel(x_ref, *rest, has_extra, n_lat_tiles):
    if n_lat_tiles is not None:
        xc_ref, rest = rest[0], rest[1:]
    g_ref, sh_ref, sc_ref, w_ref = rest[:4]
    if has_extra:
        w2_ref, o_ref, o2_ref, h_ref = rest[4:]
    else:
        o_ref, h_ref = rest[4:]

    @pl.when(pl.program_id(1) == 0)
    def _():
        x = x_ref[...] if n_lat_tiles is None else _pick_rows(pl.program_id(0), n_lat_tiles, x_ref, xc_ref)
        h = _bf(_norm_mod(x, g_ref[...], sh_ref[0], sc_ref[0]))
        h_ref[...] = h
        if has_extra:
            o2_ref[...] = _dot(h, w2_ref[...])

    o_ref[...] = _bf(_dot(h_ref[...], w_ref[...]))


def _inproj(rw, x, gain, mod3, layer, w, w_extra=None, n_cols=None):
    n = w.shape[1] if n_cols is None else n_cols
    tm = TM_IN
    tn = max(t for t in range(256, TN_IN_MAX + 1, 256) if n % t == 0)
    has_extra = w_extra is not None
    split = isinstance(x, tuple)
    xs = list(x) if split else [x]
    in_specs = _stream_specs(rw, tm, split) + [
                pl.BlockSpec((1, D), lambda i, j: (0, 0)),
                _mod_spec(rw, layer, 0, tm),
                _mod_spec(rw, layer, 1, tm),
                pl.BlockSpec((D, tn), lambda i, j: (0, j))]
    out_specs = [pl.BlockSpec((tm, tn), lambda i, j: (i, j))]
    out_shape = [jax.ShapeDtypeStruct((rw.rows, n), BF16)]
    args = xs + [gain.reshape(1, D), mod3, mod3, w]
    if has_extra:
        ne = w_extra.shape[1]
        in_specs.append(pl.BlockSpec((D, ne), lambda i, j: (0, 0)))
        out_specs.append(pl.BlockSpec((tm, ne), lambda i, j: (i, 0)))
        out_shape.append(jax.ShapeDtypeStruct((rw.rows, ne), F32))
        args.append(w_extra)
    res = pl.pallas_call(
        functools.partial(_inproj_kernel, has_extra=has_extra, n_lat_tiles=rw.lat_rows // tm if split else None),
        grid=(rw.rows // tm, n // tn),
        in_specs=in_specs, out_specs=out_specs, out_shape=out_shape,
        scratch_shapes=[pltpu.VMEM((tm, D), BF16)],
        compiler_params=_cparams(("parallel", "arbitrary")),
    )(*args)
    return res if has_extra else res[0]


def _mlp_kernel(x_ref, g_ref, sh_ref, sc_ref, gt_ref, w1_ref, w2_ref, go_ref, o_ref, h_ref, acc_ref,
                *, final_norm):
    k = pl.program_id(1)

    @pl.when(k == 0)
    def _():
        h_ref[...] = _bf(_norm_mod(x_ref[...], g_ref[...], sh_ref[0], sc_ref[0]))
        acc_ref[...] = jnp.zeros_like(acc_ref)

    a = jnp.square(jnp.maximum(_dot(h_ref[...], _bf(w1_ref[...])), 0.0))
    acc_ref[...] += _dot(_bf(a), _bf(w2_ref[...]))

    @pl.when(k == pl.num_programs(1) - 1)
    def _():
        y = x_ref[...] + gt_ref[0] * acc_ref[...]
        if final_norm:
            y = y * lax.rsqrt(jnp.mean(y * y, axis=-1, keepdims=True) + EPS) * go_ref[...]
        o_ref[...] = y


def _mlp(rw, x, n_rows, gain, mod3, layer, w1, w2, out_gain, final_norm):
    tm, ck = TM_IN, FF_CHUNK
    return pl.pallas_call(
        functools.partial(_mlp_kernel, final_norm=final_norm),
        grid=(n_rows // tm, D_FF // ck),
        in_specs=[pl.BlockSpec((tm, D), lambda i, k: (i, 0)),
                  pl.BlockSpec((1, D), lambda i, k: (0, 0)),
                  _mod_spec(rw, layer, 3, tm),
                  _mod_spec(rw, layer, 4, tm),
                  _mod_spec(rw, layer, 5, tm),
                  pl.BlockSpec((None, D, ck), lambda i, k: (layer, 0, k)),
                  pl.BlockSpec((None, ck, D), lambda i, k: (layer, k, 0)),
                  pl.BlockSpec((1, D), lambda i, k: (0, 0))],
        out_specs=pl.BlockSpec((tm, D), lambda i, k: (i, 0)),
        out_shape=jax.ShapeDtypeStruct((n_rows, D), F32),
        scratch_shapes=[pltpu.VMEM((tm, D), BF16), pltpu.VMEM((tm, D), F32)],
        compiler_params=_cparams(("parallel", "arbitrary"), MLP_VMEM_LIMIT),
    )(x, gain.reshape(1, D), mod3, mod3, mod3, w1, w2, out_gain.reshape(1, D))


OUT_SUB = 256


def _row_blocks(n_rows):
    return [slice(r, r + OUT_SUB) for r in range(0, n_rows, OUT_SUB)]


def _gdn_out_kernel(x_ref, *rest, n_lat_tiles):
    if n_lat_tiles is not None:
        xc_ref, rest = rest[0], rest[1:]
    gt_ref, o0_ref, o1_ref, z_ref, ng_ref, w_ref, out_ref, a_ref = rest
    ng = ng_ref[...]
    for rows in _row_blocks(x_ref.shape[0]):
        for h in range(GDN_V_HEADS):
            sl = slice(h * GDN_HD, (h + 1) * GDN_HD)
            o = o0_ref[0, rows, sl].astype(F32) + o1_ref[0, rows, sl].astype(F32)
            y = o * lax.rsqrt(jnp.mean(o * o, axis=-1, keepdims=True) + EPS) * ng
            a_ref[rows, sl] = _bf(y * _silu(z_ref[rows, sl].astype(F32)))
        x = x_ref[rows] if n_lat_tiles is None else _pick_rows(pl.program_id(0), n_lat_tiles, x_ref, xc_ref, rows)
        out_ref[rows] = x + gt_ref[0] * _dot(a_ref[rows], w_ref[...])


def _ret_out_kernel(x_ref, gt_ref, o0_ref, o1_ref, z_ref, ng_ref, w_ref, out_ref, a_ref):
    for rows in _row_blocks(x_ref.shape[0]):
        for h in range(RET_HEADS):
            sl = slice(h * RET_DV, (h + 1) * RET_DV)
            o = o0_ref[0, rows, sl].astype(F32) + o1_ref[0, rows, sl].astype(F32)
            mu = jnp.mean(o, axis=-1, keepdims=True)
            oc = o - mu
            var = jnp.mean(oc * oc, axis=-1, keepdims=True)
            y = oc * lax.rsqrt(var + EPS) * ng_ref[:, sl]
            a_ref[rows, sl] = _bf(y * _silu(z_ref[rows, sl].astype(F32)))
        out_ref[rows] = x_ref[rows] + gt_ref[0] * _dot(a_ref[rows], w_ref[...])


def _lru_out_kernel(x_ref, gt_ref, h0_ref, h1_ref, gb_ref, w_ref, out_ref):
    for rows in _row_blocks(x_ref.shape[0]):
        a = (h0_ref[rows] + h1_ref[rows]) * _gelu_tanh(gb_ref[rows].astype(F32))
        out_ref[rows] = x_ref[rows] + gt_ref[0] * _dot(_bf(a), w_ref[...])


def _swa_out_kernel(x_ref, gt_ref, o_ref, w_ref, out_ref):
    out_ref[...] = x_ref[...] + gt_ref[0] * _dot(_bf(o_ref[...]), w_ref[...])


def _outproj_call(kern, rw, n_rows, layer, x, mod3, extra_specs, extra_args, w, scratch):
    tm = TM_OUT
    k = w.shape[0]
    split = isinstance(x, tuple)
    xs = list(x) if split else [x]
    return pl.pallas_call(
        kern,
        grid=(n_rows // tm,),
        in_specs=_stream_specs(rw, tm, split) + [_mod_spec(rw, layer, 2, tm)] + extra_specs
                 + [pl.BlockSpec((k, D), lambda i: (0, 0))],
        out_specs=pl.BlockSpec((tm, D), lambda i: (i, 0)),
        out_shape=jax.ShapeDtypeStruct((n_rows, D), F32),
        scratch_shapes=scratch,
        compiler_params=_cparams(("parallel",)),
    )(*xs, mod3, *extra_args, w)


def _two_dir_out(kern, rw, layer, x, mod3, o, proj, z_block, norm_g, w):
    tm, k = TM_OUT, w.shape[0]
    ng = norm_g.reshape(1, -1)
    specs = [pl.BlockSpec((1, tm, k), lambda i: (0, i, 0)),
             pl.BlockSpec((1, tm, k), lambda i: (1, i, 0)),
             pl.BlockSpec((tm, k), lambda i: (i, z_block)),
             pl.BlockSpec(ng.shape, lambda i: (0, 0))]
    return _outproj_call(kern, rw, rw.rows, layer, x, mod3, specs, [o, o, proj, ng], w,
                         [pltpu.VMEM((tm, k), BF16)])


def _shift_matrices(t):
    delta = lax.broadcasted_iota(jnp.int32, (t, t), 0) - lax.broadcasted_iota(jnp.int32, (t, t), 1)
    return {k: _bf(jnp.where(jnp.logical_or(delta == k, delta == k - t), 1.0, 0.0)) for k in (2, 1, t - 1)}


def _conv_core(x_ref, p_ref, n_ref, w_ref, first, last, shifts=None):
    t = x_ref.shape[0]
    xb = x_ref[...]
    x = xb.astype(F32)
    prev = p_ref[...].astype(F32)[HALO - 8:] * jnp.where(first, 0.0, 1.0)
    nxt = n_ref[...].astype(F32)[:8] * jnp.where(last, 0.0, 1.0)
    if shifts is not None:
        assert xb.dtype == BF16
        r2, r1, rp = _dot(shifts[2], xb), _dot(shifts[1], xb), _dot(shifts[t - 1], xb)
    else:
        r2, r1, rp = pltpu.roll(x, 2, 0), pltpu.roll(x, 1, 0), pltpu.roll(x, t - 1, 0)
    row8 = lax.broadcasted_iota(jnp.int32, prev.shape, 0)
    f2 = jnp.where(row8 < 2, pltpu.roll(prev, 2, 0), r2[:8])
    f1 = jnp.where(row8 < 1, pltpu.roll(prev, 1, 0), r1[:8])
    l1 = jnp.where(row8 == 7, pltpu.roll(nxt, 7, 0), rp[t - 8:])
    xm2 = jnp.concatenate([f2, r2[8:]], axis=0)
    xm1 = jnp.concatenate([f1, r1[8:]], axis=0)
    xp1 = jnp.concatenate([rp[:t - 8], l1], axis=0)
    w = w_ref[...]
    return w[0:1] * xm2 + w[1:2] * xm1 + w[2:3] * x + w[3:4] * xp1


def _seg_flags(rw, i):
    per = rw.n_lat // TM_CONV
    is_ctx = i >= rw.lat_rows // TM_CONV
    assert rw.n_ctx == TM_CONV
    first = jnp.logical_or(is_ctx, i % per == 0)
    last = jnp.logical_or(is_ctx, i % per == per - 1)
    return first, last


def _gdn_conv_kernel(x_ref, p_ref, n_ref, w_ref, ab_ref, alog_ref, dtb_ref, o_ref, g_ref, b_ref, *, rw):
    i, j = pl.program_id(0), pl.program_id(1)
    first, last = _seg_flags(rw, i)
    s = _silu(_conv_core(x_ref, p_ref, n_ref, w_ref, first, last, _shift_matrices(x_ref.shape[0])))
    width = s.shape[1]
    qk_blocks = 2 * GDN_KEY // width

    @pl.when(j < qk_blocks)
    def _():
        for h in range(width // GDN_HD):
            sl = slice(h * GDN_HD, (h + 1) * GDN_HD)
            sh = s[:, sl]
            is_q = j * width + h * GDN_HD < GDN_KEY
            scale = jnp.where(is_q, GDN_HD ** -0.5, 1.0)
            o_ref[:, sl] = sh * (lax.rsqrt(jnp.sum(sh * sh, axis=-1, keepdims=True) + EPS) * scale)

    @pl.when(j >= qk_blocks)
    def _():
        o_ref[...] = s

    @pl.when(j == 0)
    def _():
        ab = ab_ref[...]
        g_ref[...] = -jnp.exp(alog_ref[...]) * _softplus(ab + dtb_ref[...])
        b_ref[...] = jax.nn.sigmoid(ab)


def _conv_specs(rw, cb):
    th = TM_CONV // HALO
    nh = rw.rows // HALO
    return [pl.BlockSpec((TM_CONV, cb), lambda i, j: (i, j)),
            pl.BlockSpec((HALO, cb), lambda i, j: (jnp.maximum(i * th - 1, 0), j)),
            pl.BlockSpec((HALO, cb), lambda i, j: (jnp.minimum((i + 1) * th, nh - 1), j)),
            pl.BlockSpec((4, cb), lambda i, j: (0, j))]


def _dir_masks(d, n):
    ii = lax.broadcasted_iota(jnp.int32, (n, n), 0)
    jj = lax.broadcasted_iota(jnp.int32, (n, n), 1)
    t = (ii - jj) * jnp.where(d == 0, 1, -1)
    return t, t >= 0, t > 0


def _gdn_core_kernel(qkv_ref, g_ref, b_ref, o_ref, s_ref):
    d, s = pl.program_id(1), pl.program_id(2)

    @pl.when(s == 0)
    def _():
        s_ref[...] = jnp.zeros_like(s_ref)

    c, hd = CHUNK, GDN_HD
    rep = GDN_V_HEADS // GDN_QK_HEADS
    heads = range(GDN_V_HEADS)
    n_sub = qkv_ref.shape[0] // c
    subs = range(n_sub)
    offs = [pl.multiple_of(jnp.where(d == 0, i, n_sub - 1 - i) * c, c) for i in subs]
    t, incl, strict = _dir_masks(d, c)
    cum = _bf(jnp.where(incl, 1.0, 0.0))
    cum_t = _bf(jnp.where(t <= 0, 1.0, 0.0))
    g = [g_ref[0, pl.ds(offs[i], c), :] for i in subs]
    beta = [b_ref[0, pl.ds(offs[i], c), :] for i in subs]
    parts = [_split3(g[i]) for i in subs]
    gc = [sum(_dot(cum, p) for p in parts[i]) for i in subs]
    gr = [sum(_dot_tn(p, cum_t) for p in parts[i]) for i in subs]
    gtot = [jnp.sum(g[i], axis=0, keepdims=True) for i in subs]

    def rows(i, lo):
        return qkv_ref[pl.ds(offs[i], c), lo:lo + hd]

    q_l = [[rows(i, hq * hd) for hq in range(GDN_QK_HEADS)] for i in subs]
    k_l = [[rows(i, GDN_KEY + hq * hd) for hq in range(GDN_QK_HEADS)] for i in subs]
    gram = [[_dot_nt(_bf(jnp.concatenate([k, q], axis=0)), _bf(k)) for q, k in zip(q_l[i], k_l[i])]
            for i in subs]
    zeros_cc = jnp.zeros((c, 2 * c), BF16)
    zeros_rhs = jnp.zeros((c, 2 * hd), BF16)
    lane_cc = lax.broadcasted_iota(jnp.int32, (c, 2 * c), 1)
    row_cc = lax.broadcasted_iota(jnp.int32, (c, 2 * c), 0)
    eye_hi = _bf(jnp.where(lane_cc == row_cc + c, 1.0, 0.0))
    is_hi = lane_cc >= c

    def prepare(chains):
        ids = range(len(chains))
        gcol = [gc[i][:, h:h + 1] for i, h in chains]
        bcol = [beta[i][:, h:h + 1] for i, h in chains]
        dec = [jnp.where(incl, jnp.exp(jnp.where(incl, gcol[n] - gr[i][h:h + 1, :], 0.0)), 0.0)
               for n, (i, h) in enumerate(chains)]
        eg = [jnp.exp(x) for x in gcol]
        m = [jnp.where(strict, -(gram[i][h // rep][:c] * dec[n] * bcol[n]), 0.0) for n, (i, h) in enumerate(chains)]
        rhs = [jnp.concatenate([k_l[i][h // rep] * (bcol[n] * eg[n]), rows(i, 2 * GDN_KEY + h * hd) * bcol[n]],
                               axis=1) for n, (i, h) in enumerate(chains)]

        m_hi = [_bf(x) for x in m]
        cb = [jnp.concatenate([m_hi[n], jnp.zeros((c, c), BF16)], axis=1) + eye_hi for n in ids]
        for _ in range(6):
            r = [_dot(cb[n], jnp.concatenate([cb[n], zeros_cc], axis=0)) for n in ids]
            cb = [_bf(r[n]) + jnp.where(is_hi, cb[n], jnp.zeros_like(cb[n])) for n in ids]
        nb = [cb[n] - eye_hi for n in ids]

        def apply_n(y):
            return [_dot(nb[n], jnp.concatenate([zeros_rhs, _bf(y[n])], axis=0)) for n in ids]

        nr = apply_n(rhs)
        x = [rhs[n] + nr[n] for n in ids]
        resid = [_dot(m_hi[n], _bf(x[n])) - nr[n] for n in ids]
        nr = apply_n(resid)
        x = [x[n] + (resid[n] + nr[n]) for n in ids]
        qk = [_bf(gram[i][h // rep][c:] * dec[n]) for n, (i, h) in enumerate(chains)]
        gt = [gtot[i][:, h:h + 1] for i, h in chains]
        kd = [_bf(k_l[i][h // rep] * jnp.exp(gt[n] - gcol[n])) for n, (i, h) in enumerate(chains)]
        wq = [_bf(jnp.concatenate([x[n][:, :hd], q_l[i][h // rep] * eg[n]], axis=0))
              for n, (i, h) in enumerate(chains)]
        decay = [jnp.exp(x) for x in gt]
        return x, qk, kd, wq, decay

    per_group = GDN_GROUP // GDN_V_HEADS
    st = [s_ref[h] for h in heads]
    for first in range(0, n_sub, per_group):
        chunk_ids = range(first, first + per_group)
        x, qk, kd, wq, decay = prepare([(i, h) for i in chunk_ids for h in heads])
        for j, i in enumerate(chunk_ids):
            base = j * GDN_V_HEADS
            r = [_dot(wq[base + h], _bf(st[h])) for h in heads]
            unb = [_bf(x[base + h][:, hd:] - r[h][:c]) for h in heads]
            for h in heads:
                o_ref[0, pl.ds(offs[i], c), h * hd:(h + 1) * hd] = _bf(r[h][c:] + _dot(qk[base + h], unb[h]))
            st = [st[h] * decay[base + h] + _dot_tn(kd[base + h], unb[h]) for h in heads]
    for h in heads:
        s_ref[h] = st[h]


def _gdn_core(rw, qkv, g_dir, b_dir):
    blk = GDN_BLOCK
    rb = lambda b, d, s: rw.chunk_block(b, d, s, blk)
    return pl.pallas_call(
        _gdn_core_kernel,
        grid=(rw.batch, 2, (rw.n_lat + rw.n_ctx) // blk),
        in_specs=[pl.BlockSpec((blk, GDN_CONV), lambda b, d, s: (rb(b, d, s), 0)),
                  pl.BlockSpec((1, blk, GDN_V_HEADS), lambda b, d, s: (d, rb(b, d, s), 0)),
                  pl.BlockSpec((1, blk, GDN_V_HEADS), lambda b, d, s: (d, rb(b, d, s), 0))],
        out_specs=pl.BlockSpec((1, blk, GDN_VAL), lambda b, d, s: (d, rb(b, d, s), 0)),
        out_shape=jax.ShapeDtypeStruct((2, rw.rows, GDN_VAL), BF16),
        scratch_shapes=[pltpu.VMEM((GDN_V_HEADS, GDN_HD, GDN_HD), F32)],
        compiler_params=_cparams(("parallel", "parallel", "arbitrary")),
    )(qkv, g_dir, b_dir)


def _rope_half(x, cos, sin_signed):
    return x * cos + pltpu.roll(x, x.shape[1] // 2, 1) * sin_signed


def _ret_core_kernel(q_ref, k_ref, v_ref, cos_ref, sin_ref, dl_ref, o_ref, s_ref, dec_ref, *, ctx_steps):
    d, s = pl.program_id(1), pl.program_id(2)
    c = q_ref.shape[0]
    heads = range(RET_HEADS)
    is_lat = s >= ctx_steps
    cos = jnp.where(is_lat, cos_ref[...], 1.0)
    sin = jnp.where(is_lat, sin_ref[...], 0.0)
    t, incl, _ = _dir_masks(d, c)
    tf = t.astype(F32)
    pos = lax.broadcasted_iota(jnp.int32, (c, 1), 0)
    ridx = jnp.where(d == 0, pos + 1, c - pos).astype(F32)
    log_gamma = -_softplus(-dl_ref[0])

    def rope(x):
        hw = RET_DK // 2
        return jnp.concatenate([_rope_half(x[:, :hw], cos[:, :hw], sin[:, :hw]),
                                _rope_half(x[:, hw:], cos[:, hw:], sin[:, hw:])], axis=1)

    lg = [log_gamma[:, h:h + 1] for h in heads]

    @pl.when(s == 0)
    def _():
        s_ref[...] = jnp.zeros_like(s_ref)
        for h in heads:
            dec_ref[h] = jnp.where(incl, jnp.exp(jnp.where(incl, lg[h] * tf, 0.0)), 0.0)

    q = [rope(q_ref[:, h * RET_DK:(h + 1) * RET_DK].astype(F32)) * (RET_DK ** -0.5) for h in heads]
    k = [rope(k_ref[:, h * RET_DK:(h + 1) * RET_DK].astype(F32)) for h in heads]
    vb = [v_ref[:, h * RET_DV:(h + 1) * RET_DV] for h in heads]
    gcum = [lg[h] * ridx for h in heads]
    qk = [_bf(_dot_nt(_bf(q[h]), _bf(k[h])) * dec_ref[h]) for h in heads]
    st = [s_ref[h] for h in heads]
    for h in heads:
        o_ref[0, :, h * RET_DV:(h + 1) * RET_DV] = _bf(
            _dot(_bf(q[h] * jnp.exp(gcum[h])), _bf(st[h])) + _dot(qk[h], vb[h]))
    gtot = [lg[h] * float(c) for h in heads]
    kd = [_bf(k[h] * jnp.exp(gtot[h] - gcum[h])) for h in heads]
    for h in heads:
        s_ref[h] = st[h] * jnp.exp(gtot[h]) + _dot_tn(kd[h], vb[h])


def _ret_core(rw, proj, cos_tab, sin_tab, decay_logit):
    c = RET_CHUNK
    rb = lambda b, d, s: rw.chunk_block(b, d, s, c)
    dl = jnp.zeros((2, 1, 128), F32).at[:, 0, :RET_HEADS].set(decay_logit)
    return pl.pallas_call(
        functools.partial(_ret_core_kernel, ctx_steps=rw.n_ctx // c),
        grid=(rw.batch, 2, (rw.n_lat + rw.n_ctx) // c),
        in_specs=[pl.BlockSpec((c, RET_KEY), lambda b, d, s: (rb(b, d, s), 0)),
                  pl.BlockSpec((c, RET_KEY), lambda b, d, s: (rb(b, d, s), 1)),
                  pl.BlockSpec((c, RET_VAL), lambda b, d, s: (rb(b, d, s), 1)),
                  pl.BlockSpec((c, RET_DK), lambda b, d, s: (rw.lat_chunk(d, s, c), 0)),
                  pl.BlockSpec((c, RET_DK), lambda b, d, s: (rw.lat_chunk(d, s, c), 0)),
                  pl.BlockSpec((1, 1, 128), lambda b, d, s: (d, 0, 0))],
        out_specs=pl.BlockSpec((1, c, RET_VAL), lambda b, d, s: (d, rb(b, d, s), 0)),
        out_shape=jax.ShapeDtypeStruct((2, rw.rows, RET_VAL), BF16),
        scratch_shapes=[pltpu.VMEM((RET_HEADS, RET_DK, RET_DV), F32), pltpu.VMEM((RET_HEADS, c, c), F32)],
        compiler_params=_cparams(("parallel", "parallel", "arbitrary")),
    )(proj, proj, proj, cos_tab, sin_tab, dl)


def _lru_scan_kernel(x_ref, p_ref, n_ref, cw_ref, cb_ref, wg_ref, bg_ref, lam_ref, h_ref, carry_ref, *, rw, rev):
    @pl.when(pl.program_id(1) == 0)
    def _():
        carry_ref[...] = jnp.zeros_like(carry_ref)

    t = x_ref.shape[0]
    tile = rw.chunk_block(pl.program_id(0), 1 if rev else 0, pl.program_id(1), t)
    first, last = _seg_flags(rw, tile)
    xs = _conv_core(x_ref, p_ref, n_ref, cw_ref, first, last) + cb_ref[...]
    row8 = lax.broadcasted_iota(jnp.int32, (t // 8, 8, LRU_BW), 1)
    sp = _softplus(-lam_ref[...])
    for n in range(LRU_BLOCKS):
        sl = slice(n * LRU_BW, (n + 1) * LRU_BW)
        xn = xs[:, sl]
        gates = jax.nn.sigmoid(_dot(_bf(xn), wg_ref[n]) + bg_ref[n])
        log_a = -LRU_C * gates[:, :LRU_BW] * sp[:, sl]
        a = jnp.exp(log_a)
        y = -jnp.tanh(log_a) * (a * a + 1.0)
        b = jnp.where(y > 0.0, y * lax.rsqrt(y), 0.0) * gates[:, LRU_BW:] * xn
        groups = t // 8
        a3 = a.reshape(groups, 8, LRU_BW)
        b3 = b.reshape(groups, 8, LRU_BW)
        for k in (1, 2, 4):
            if rev:
                keep = row8 < 8 - k
                shift = 8 - k
            else:
                keep = row8 >= k
                shift = k
            sa = jnp.where(keep, pltpu.roll(a3, shift, 1), 1.0)
            sb = jnp.where(keep, pltpu.roll(b3, shift, 1), 0.0)
            b3 = b3 + a3 * sb
            a3 = a3 * sa
        carry = carry_ref[:, sl]
        for gi in (reversed(range(groups)) if rev else range(groups)):
            hg = a3[gi] * carry + b3[gi]
            carry = hg[0:1] if rev else hg[7:8]
            h_ref[gi * 8:(gi + 1) * 8, sl] = hg
        carry_ref[:, sl] = carry


def _lru_scan(rw, proj, conv_w, conv_b, w_gate, b_gate, lam, rev):
    t = TM_CONV
    th = t // HALO
    nh = rw.rows // HALO
    d = 1 if rev else 0
    rb = lambda b, s: rw.chunk_block(b, d, s, t)
    return pl.pallas_call(
        functools.partial(_lru_scan_kernel, rw=rw, rev=rev),
        grid=(rw.batch, (rw.n_lat + rw.n_ctx) // t),
        in_specs=[pl.BlockSpec((t, LRU_W), lambda b, s: (rb(b, s), 0)),
                  pl.BlockSpec((HALO, LRU_W), lambda b, s: (jnp.maximum(rb(b, s) * th - 1, 0), 0)),
                  pl.BlockSpec((HALO, LRU_W), lambda b, s: (jnp.minimum((rb(b, s) + 1) * th, nh - 1), 0)),
                  pl.BlockSpec((4, LRU_W), lambda b, s: (0, 0)),
                  pl.BlockSpec((1, LRU_W), lambda b, s: (0, 0)),
                  pl.BlockSpec((LRU_BLOCKS, LRU_BW, 2 * LRU_BW), lambda b, s: (0, 0, 0)),
                  pl.BlockSpec((LRU_BLOCKS, 1, 2 * LRU_BW), lambda b, s: (0, 0, 0)),
                  pl.BlockSpec((1, LRU_W), lambda b, s: (0, 0))],
        out_specs=pl.BlockSpec((t, LRU_W), lambda b, s: (rb(b, s), 0)),
        out_shape=jax.ShapeDtypeStruct((rw.rows, LRU_W), F32),
        scratch_shapes=[pltpu.VMEM((1, LRU_W), F32)],
        compiler_params=_cparams(("parallel", "arbitrary")),
    )(proj, proj, proj, conv_w, conv_b.reshape(1, LRU_W), _bf(w_gate),
      b_gate.reshape(LRU_BLOCKS, 1, 2 * LRU_BW), lam.reshape(1, LRU_W))


def _rope16(x, tab):
    cos, s1, s2 = tab[:, :128], tab[:, 128:256], tab[:, 256:]
    cols = []
    for cidx in range(x.shape[1] // 128):
        xc = x[:, cidx * 128:(cidx + 1) * 128]
        cols.append(xc * cos + pltpu.roll(xc, 112, 1) * s1 + pltpu.roll(xc, 16, 1) * s2)
    return cols[0] if len(cols) == 1 else jnp.concatenate(cols, axis=1)


def _swa_kernel(sink_ref, q_ref, kp_ref, kc_ref, kn_ref, vp_ref, vc_ref, vn_ref, kx_ref, vx_ref,
                tp_ref, tc_ref, tn_ref, o_ref):
    qb = pl.program_id(1)
    nb = pl.num_programs(1)
    blk = SWA_BLOCK
    n_loc = 3 * blk
    n_keys = n_loc + kx_ref.shape[0]
    tab_c = tc_ref[...]
    q = _rope16(q_ref[...].astype(F32), tab_c) * (SWA_HD ** -0.5 * LOG2E)
    k_all = jnp.concatenate([_rope16(kp_ref[...].astype(F32), tp_ref[...]), _rope16(kc_ref[...].astype(F32), tab_c),
                             _rope16(kn_ref[...].astype(F32), tn_ref[...]), kx_ref[...].astype(F32)], axis=0)
    v_all = jnp.concatenate([vp_ref[...], vc_ref[...], vn_ref[...], vx_ref[...]], axis=0).astype(F32)
    assert SWA_WINDOW >= blk - 1
    qi = lax.broadcasted_iota(jnp.int32, (blk, blk), 0)
    kj = lax.broadcasted_iota(jnp.int32, (blk, blk), 1)
    valid_prev = jnp.logical_and(kj - blk - qi >= -SWA_WINDOW, qb > 0)
    valid_next = jnp.logical_and(kj + blk - qi <= SWA_WINDOW, qb < nb - 1)
    lane = lax.broadcasted_iota(jnp.int32, (n_keys, 128), 1)
    low = lane < SWA_HD
    out_low = lax.broadcasted_iota(jnp.int32, (blk, 128), 1) < SWA_HD

    def expo(scores, sink):
        sc = jnp.concatenate([jnp.where(valid_prev, scores[:, :blk], -jnp.inf), scores[:, blk:2 * blk],
                              jnp.where(valid_next, scores[:, 2 * blk:n_loc], -jnp.inf), scores[:, n_loc:]], axis=1)
        mx = jnp.maximum(jnp.max(sc, axis=-1, keepdims=True), sink)
        return _bf(jnp.exp2(sc - mx)), jnp.exp2(sink - mx)

    group = SWA_QH // SWA_KVH
    for g in range(SWA_KVH):
        col = slice((g // 2) * 128, (g // 2 + 1) * 128)
        native_low = g % 2 == 0
        sel = low if native_low else jnp.logical_not(low)
        k_nat = jnp.where(sel, k_all[:, col], 0.0)
        v_nat = jnp.where(sel, v_all[:, col], 0.0)
        k_oth = pltpu.roll(k_nat, SWA_HD, 1)
        v_oth = pltpu.roll(v_nat, SWA_HD, 1)
        k_lo, k_hi = (k_nat, k_oth) if native_low else (k_oth, k_nat)
        v_lo, v_hi = (v_nat, v_oth) if native_low else (v_oth, v_nat)
        v_lo = jnp.where(lane == SWA_HD, 1.0, v_lo)
        v_hi = jnp.where(lane == 0, 1.0, v_hi)
        k_lo, k_hi, v_lo, v_hi = _bf(k_lo), _bf(k_hi), _bf(v_lo), _bf(v_hi)
        for cidx in range(group // 2):
            qc = (g * group) // 2 + cidx
            qv = _bf(q[:, qc * 128:(qc + 1) * 128])
            e_lo, sink_lo = expo(_dot_nt(qv, k_lo), sink_ref[2 * qc] * LOG2E)
            e_hi, sink_hi = expo(_dot_nt(qv, k_hi), sink_ref[2 * qc + 1] * LOG2E)
            a_lo = _dot(e_lo, v_lo)
            a_hi = _dot(e_hi, v_hi)
            inv_lo = 1.0 / (a_lo[:, SWA_HD:SWA_HD + 1] + sink_lo)
            inv_hi = 1.0 / (a_hi[:, 0:1] + sink_hi)
            o_ref[:, qc * 128:(qc + 1) * 128] = jnp.where(out_low, a_lo * inv_lo, a_hi * inv_hi)


def _swa_core(rw, proj, sink, tab):
    blk = SWA_BLOCK
    nb = rw.n_lat // blk
    kcol = SWA_QD // SWA_KVD
    vcol = kcol + 1
    ctx_blk = lambda b: rw.lat_rows // rw.n_ctx + b
    prev = lambda q: jnp.maximum(q - 1, 0)
    nxt = lambda q: jnp.minimum(q + 1, nb - 1)
    kv = lambda col, f: pl.BlockSpec((blk, SWA_KVD), lambda b, q: (b * nb + f(q), col))
    same = lambda q: q
    return pl.pallas_call(
        _swa_kernel,
        grid=(rw.batch, nb),
        in_specs=[pl.BlockSpec(memory_space=pltpu.SMEM),
                  pl.BlockSpec((blk, SWA_QD), lambda b, q: (b * nb + q, 0)),
                  kv(kcol, prev), kv(kcol, same), kv(kcol, nxt),
                  kv(vcol, prev), kv(vcol, same), kv(vcol, nxt),
                  pl.BlockSpec((rw.n_ctx, SWA_KVD), lambda b, q: (ctx_blk(b), kcol)),
                  pl.BlockSpec((rw.n_ctx, SWA_KVD), lambda b, q: (ctx_blk(b), vcol)),
                  pl.BlockSpec((blk, 384), lambda b, q: (prev(q), 0)),
                  pl.BlockSpec((blk, 384), lambda b, q: (q, 0)),
                  pl.BlockSpec((blk, 384), lambda b, q: (nxt(q), 0))],
        out_specs=pl.BlockSpec((blk, SWA_QD), lambda b, q: (b * nb + q, 0)),
        out_shape=jax.ShapeDtypeStruct((rw.lat_rows, SWA_QD), F32),
        compiler_params=_cparams(("parallel", "parallel")),
    )(sink, proj, proj, proj, proj, proj, proj, proj, proj, proj, tab, tab, tab)


def _axial_angles(n_lat, dh):
    t = jnp.arange(n_lat)
    row = (t // GRID_W).astype(F32)
    col = (t % GRID_W).astype(F32)
    nf = dh // 4
    inv = ROPE_THETA ** (-jnp.arange(nf, dtype=F32) / nf)
    return row[:, None] * inv, col[:, None] * inv


def _ret_tables(n_lat):
    ar, ac = _axial_angles(n_lat, RET_DK)
    cos = jnp.concatenate([jnp.cos(ar), jnp.cos(ar), jnp.cos(ac), jnp.cos(ac)], axis=1)
    sin = jnp.concatenate([-jnp.sin(ar), jnp.sin(ar), -jnp.sin(ac), jnp.sin(ac)], axis=1)
    return cos, sin


def _swa_table(n_lat):
    ar, ac = _axial_angles(n_lat, SWA_HD)
    z = jnp.zeros_like(ar)
    cos = jnp.concatenate([jnp.cos(ar), jnp.cos(ar), jnp.cos(ac), jnp.cos(ac)], axis=1)
    s1 = jnp.concatenate([-jnp.sin(ar), z, -jnp.sin(ac), z], axis=1)
    s2 = jnp.concatenate([z, jnp.sin(ar), z, jnp.sin(ac)], axis=1)
    return jnp.concatenate([jnp.tile(cos, (1, 2)), jnp.tile(s1, (1, 2)), jnp.tile(s2, (1, 2))], axis=1)


def _gdn_layer(rw, layer, x, mod3, gain, w_in, conv_w, a_log, dt_bias, norm_g, w_out):
    n_ab = w_in.shape[1] - GDN_MAIN
    w_ab = _bf(jnp.pad(w_in[:, GDN_MAIN:], ((0, 0), (0, 128 - n_ab))))
    proj, ab = _inproj(rw, x, gain, mod3, layer, _bf(w_in), w_ab, n_cols=GDN_MAIN)
    hv = GDN_V_HEADS
    alog_row = jnp.zeros((1, 128), F32).at[0, 0:hv].set(a_log[0]).at[0, 2 * hv:3 * hv].set(a_log[1])
    dtb_row = jnp.zeros((1, 128), F32).at[0, 0:hv].set(dt_bias[0]).at[0, 2 * hv:3 * hv].set(dt_bias[1])
    cb = 2048
    row_spec = pl.BlockSpec((TM_CONV, 128), lambda i, j: (i, 0))
    vec_spec = pl.BlockSpec((1, 128), lambda i, j: (0, 0))
    qkv, g_all, b_all = pl.pallas_call(
        functools.partial(_gdn_conv_kernel, rw=rw),
        grid=(rw.rows // TM_CONV, GDN_CONV // cb),
        in_specs=_conv_specs(rw, cb) + [row_spec, vec_spec, vec_spec],
        out_specs=[pl.BlockSpec((TM_CONV, cb), lambda i, j: (i, j)), row_spec, row_spec],
        out_shape=[jax.ShapeDtypeStruct((rw.rows, GDN_CONV), F32),
                   jax.ShapeDtypeStruct((rw.rows, 128), F32),
                   jax.ShapeDtypeStruct((rw.rows, 128), F32)],
        compiler_params=_cparams(("parallel", "arbitrary")),
    )(proj, proj, proj, conv_w, ab, alog_row, dtb_row)
    g_dir = jnp.stack([g_all[:, 0:hv], g_all[:, 2 * hv:3 * hv]])
    b_dir = jnp.stack([b_all[:, hv:2 * hv], b_all[:, 3 * hv:4 * hv]])
    o = _gdn_core(rw, qkv, g_dir, b_dir)
    kern = functools.partial(_gdn_out_kernel, n_lat_tiles=rw.lat_rows // TM_OUT if isinstance(x, tuple) else None)
    return _two_dir_out(kern, rw, layer, x, mod3, o, proj, GDN_CONV // GDN_VAL, norm_g, _bf(w_out))


def _ret_layer(rw, layer, x, mod3, gain, w_in, decay_logit, norm_g, w_out):
    proj = _inproj(rw, x, gain, mod3, layer, _bf(w_in))
    cos_tab, sin_tab = _ret_tables(rw.n_lat)
    o = _ret_core(rw, proj, cos_tab, sin_tab, decay_logit)
    return _two_dir_out(_ret_out_kernel, rw, layer, x, mod3, o, proj, (2 * RET_KEY + RET_VAL) // RET_VAL,
                        norm_g, _bf(w_out))


def _lru_layer(rw, layer, x, mod3, gain, w_in, conv_w, conv_b, w_gate, b_gate, lam, w_out):
    proj = _inproj(rw, x, gain, mod3, layer, _bf(w_in))
    h0 = _lru_scan(rw, proj, conv_w, conv_b, w_gate[0], b_gate[0], lam[0], rev=False)
    h1 = _lru_scan(rw, proj, conv_w, conv_b, w_gate[1], b_gate[1], lam[1], rev=True)
    tm = TM_OUT
    specs = [pl.BlockSpec((tm, LRU_W), lambda i: (i, 0)),
             pl.BlockSpec((tm, LRU_W), lambda i: (i, 0)),
             pl.BlockSpec((tm, LRU_W), lambda i: (i, 1))]
    return _outproj_call(_lru_out_kernel, rw, rw.rows, layer, x, mod3, specs, [h0, h1, proj], _bf(w_out), [])


def _swa_layer(rw, layer, x, mod3, gain, w_in, sink, w_out):
    proj = _inproj(rw, x, gain, mod3, layer, _bf(w_in))
    o = _swa_core(rw, proj, sink, _swa_table(rw.n_lat))
    specs = [pl.BlockSpec((TM_OUT, SWA_QD), lambda i: (i, 0))]
    return _outproj_call(_swa_out_kernel, rw, rw.lat_rows, layer, x, mod3, specs, [o], _bf(w_out), [])


def kernel(x, c, ctx, c_ctx, norm_mix_g, norm_ffn_g, w_mod, b_mod, w_ff1, w_ff2, norm_out_g, gdn_w_in, gdn_conv_w, gdn_a_log, gdn_dt_bias, gdn_norm_g, gdn_w_out, ret_w_in, ret_decay_logit, ret_norm_g, ret_w_out, lru_w_in, lru_conv_w, lru_conv_b, lru_w_gate, lru_b_gate, lru_lambda, lru_w_out, swa_w_in, swa_sink, swa_w_out):
    batch, n_lat, _ = x.shape
    n_ctx = ctx.shape[1]
    assert batch + 1 <= MOD_ROWS and n_lat % TM_IN == 0 and (batch * n_ctx) % TM_IN == 0
    assert w_mod.shape[0] == DEPTH and gdn_w_in.shape[0] == 1 and swa_w_in.shape[0] == 1
    rw = _Rows(batch, n_lat, n_ctx)
    xf = (x.reshape(rw.lat_rows, D), ctx.reshape(batch * n_ctx, D))
    cc = jnp.zeros((MOD_ROWS, D), F32).at[:batch].set(c).at[batch].set(c_ctx)
    mod3 = _adaln(cc, w_mod, b_mod)

    def mlp(layer, xin, n_rows, final):
        return _mlp(rw, xin, n_rows, norm_ffn_g[layer], mod3, layer, w_ff1, w_ff2,
                    norm_out_g, final)

    xf = _gdn_layer(rw, 0, xf, mod3, norm_mix_g[0], gdn_w_in[0], gdn_conv_w[0], gdn_a_log[0], gdn_dt_bias[0],
                    gdn_norm_g[0], gdn_w_out[0])
    xf = mlp(0, xf, rw.rows, False)
    xf = _ret_layer(rw, 1, xf, mod3, norm_mix_g[1], ret_w_in[0], ret_decay_logit[0], ret_norm_g[0], ret_w_out[0])
    xf = mlp(1, xf, rw.rows, False)
    xf = _lru_layer(rw, 2, xf, mod3, norm_mix_g[2], lru_w_in[0], lru_conv_w[0], lru_conv_b[0], lru_w_gate[0],
                    lru_b_gate[0], lru_lambda[0], lru_w_out[0])
    xf = mlp(2, xf, rw.rows, False)
    xl = _swa_layer(rw, 3, xf, mod3, norm_mix_g[3], swa_w_in[0], swa_sink[0], swa_w_out[0])
    out = mlp(3, xl, rw.lat_rows, True)
    return out.reshape(batch, n_lat, D)
```

```python
import functools
import math

import jax
import jax.numpy as jnp
from jax import lax
from jax.experimental import pallas as pl
from jax.experimental.pallas import tpu as pltpu

F32 = jnp.float32
BF16 = jnp.bfloat16

D = 1024
D_FF = 4 * D
EPS = 1e-6
LOG2E = math.log2(math.e)
DEPTH = 4
ROPE_THETA = 10000.0
GRID_W = 64
CHUNK = 64
MOD_ROWS = 16

GDN_QK_HEADS = 8
GDN_V_HEADS = 16
GDN_HD = 128
GDN_KEY = GDN_QK_HEADS * GDN_HD
GDN_VAL = GDN_V_HEADS * GDN_HD
GDN_CONV = 2 * GDN_KEY + GDN_VAL
GDN_MAIN = GDN_CONV + GDN_VAL

RET_HEADS = 4
RET_DK = 256
RET_DV = 512
RET_KEY = RET_HEADS * RET_DK
RET_VAL = RET_HEADS * RET_DV

LRU_W = 1280
LRU_BLOCKS = 10
LRU_BW = LRU_W // LRU_BLOCKS
LRU_C = 8.0

SWA_QH = 16
SWA_KVH = 4
SWA_HD = 64
SWA_BLOCK = 128
SWA_WINDOW = 128
SWA_QD = SWA_QH * SWA_HD
SWA_KVD = SWA_KVH * SWA_HD

TM_IN = 1024
TM_OUT = 512
TM_CONV = 256
HALO = 16
TN_IN_MAX = 1536
FF_CHUNK = 1024
RET_CHUNK = 256
GDN_BLOCK = 4 * CHUNK
GDN_GROUP = 32
VMEM_LIMIT = 48 * 1024 * 1024
MLP_VMEM_LIMIT = 56 * 1024 * 1024


def _dot(a, b):
    return jnp.dot(a, b, preferred_element_type=F32)


def _dot_nt(a, b):
    return lax.dot_general(a, b, (((1,), (1,)), ((), ())), preferred_element_type=F32)


def _dot_tn(a, b):
    return lax.dot_general(a, b, (((0,), (0,)), ((), ())), preferred_element_type=F32)


def _bf(x):
    return x.astype(BF16)


def _split3(x):
    hi = _bf(x)
    r = x - hi.astype(F32)
    mid = _bf(r)
    lo = _bf(r - mid.astype(F32))
    return hi, mid, lo


def _silu(x):
    return x * jax.nn.sigmoid(x)


def _softplus(x):
    return jnp.maximum(x, 0.0) + jnp.log1p(jnp.exp(-jnp.abs(x)))


def _gelu_tanh(x):
    cdf = 0.5 * (1.0 + jnp.tanh(math.sqrt(2.0 / math.pi) * (x + 0.044715 * (x * x * x))))
    return x * cdf


def _cparams(sem, vmem=None):
    return pltpu.CompilerParams(dimension_semantics=sem, vmem_limit_bytes=VMEM_LIMIT if vmem is None else vmem)


class _Rows:
    def __init__(self, batch, n_lat, n_ctx):
        self.batch, self.n_lat, self.n_ctx = batch, n_lat, n_ctx
        self.lat_rows = batch * n_lat
        self.rows = self.lat_rows + batch * n_ctx

    def mod_row(self, i, tm):
        return jnp.where(i < self.lat_rows // tm, i // (self.n_lat // tm), self.batch)

    def chunk_block(self, b, d, s, blk):
        nc, nl = self.n_ctx // blk, self.n_lat // blk
        c_ctx = jnp.where(d == 0, s, nc - 1 - s)
        c_lat = jnp.where(d == 0, s - nc, nl - 1 - (s - nc))
        return jnp.where(s < nc, self.lat_rows // blk + b * nc + c_ctx, b * nl + c_lat)

    def lat_chunk(self, d, s, blk):
        nc, nl = self.n_ctx // blk, self.n_lat // blk
        return jnp.where(s < nc, 0, jnp.where(d == 0, s - nc, nl - 1 - (s - nc)))


def _adaln_kernel(c_ref, w_ref, b_ref, o_ref):
    s = _silu(c_ref[...])
    w = w_ref[0]
    s_hi = _bf(s)
    s_lo = _bf(s - s_hi.astype(F32))
    w_hi = _bf(w)
    w_lo = _bf(w - w_hi.astype(F32))
    y = _dot(s_hi, w_hi) + (_dot(s_lo, w_hi) + _dot(s_hi, w_lo))
    o_ref[0] = y + b_ref[0]


def _adaln(cc, w_mod, b_mod):
    depth, _, n = w_mod.shape
    tn = 1024
    out = pl.pallas_call(
        _adaln_kernel,
        grid=(depth, n // tn),
        in_specs=[pl.BlockSpec((MOD_ROWS, D), lambda l, j: (0, 0)),
                  pl.BlockSpec((1, D, tn), lambda l, j: (l, 0, j)),
                  pl.BlockSpec((1, 1, tn), lambda l, j: (l, 0, j))],
        out_specs=pl.BlockSpec((1, MOD_ROWS, tn), lambda l, j: (l, 0, j)),
        out_shape=jax.ShapeDtypeStruct((depth, MOD_ROWS, n), F32),
        compiler_params=_cparams(("parallel", "parallel")),
    )(cc, w_mod, b_mod.reshape(depth, 1, n))
    return out.reshape(depth * MOD_ROWS * 6, 1, D)


def _mod_spec(rw, layer, which, tm):
    base = layer * MOD_ROWS * 6
    return pl.BlockSpec((1, 1, D), lambda i, *_: (base + rw.mod_row(i, tm) * 6 + which, 0, 0))


def _norm_mod(x, gain, shift, scale):
    y = x * lax.rsqrt(jnp.mean(x * x, axis=-1, keepdims=True) + EPS) * gain
    return y * (1.0 + scale) + shift


def _pick_rows(i, n_lat_tiles, x_ref, xc_ref, rows=slice(None)):
    return jnp.where(i < n_lat_tiles, x_ref[rows], xc_ref[rows])


def _stream_specs(rw, tm, split):
    if not split:
        return [pl.BlockSpec((tm, D), lambda i, *_: (i, 0))]
    nl = rw.lat_rows // tm
    return [pl.BlockSpec((tm, D), lambda i, *_: (jnp.minimum(i, nl - 1), 0)),
            pl.BlockSpec((tm, D), lambda i, *_: (jnp.maximum(i - nl, 0), 0))]


def _inproj_kernel(x_ref, *rest, has_extra, n_lat_tiles):
    if n_lat_tiles is not None:
        xc_ref, rest = rest[0], rest[1:]
    g_ref, sh_ref, sc_ref, w_ref = rest[:4]
    if has_extra:
        w2_ref, o_ref, o2_ref, h_ref = rest[4:]
    else:
        o_ref, h_ref = rest[4:]

    @pl.when(pl.program_id(1) == 0)
    def _():
        x = x_ref[...] if n_lat_tiles is None else _pick_rows(pl.program_id(0), n_lat_tiles, x_ref, xc_ref)
        h = _bf(_norm_mod(x, g_ref[...], sh_ref[0], sc_ref[0]))
        h_ref[...] = h
        if has_extra:
            o2_ref[...] = _dot(h, w2_ref[...])

    o_ref[...] = _bf(_dot(h_ref[...], w_ref[...]))


def _inproj(rw, x, gain, mod3, layer, w, w_extra=None, n_cols=None):
    n = w.shape[1] if n_cols is None else n_cols
    tm = TM_IN
    tn = max(t for t in range(256, TN_IN_MAX + 1, 256) if n % t == 0)
    has_extra = w_extra is not None
    split = isinstance(x, tuple)
    xs = list(x) if split else [x]
    in_specs = _stream_specs(rw, tm, split) + [
                pl.BlockSpec((1, D), lambda i, j: (0, 0)),
                _mod_spec(rw, layer, 0, tm),
                _mod_spec(rw, layer, 1, tm),
                pl.BlockSpec((D, tn), lambda i, j: (0, j))]
    out_specs = [pl.BlockSpec((tm, tn), lambda i, j: (i, j))]
    out_shape = [jax.ShapeDtypeStruct((rw.rows, n), BF16)]
    args = xs + [gain.reshape(1, D), mod3, mod3, w]
    if has_extra:
        ne = w_extra.shape[1]
        in_specs.append(pl.BlockSpec((D, ne), lambda i, j: (0, 0)))
        out_specs.append(pl.BlockSpec((tm, ne), lambda i, j: (i, 0)))
        out_shape.append(jax.ShapeDtypeStruct((rw.rows, ne), F32))
        args.append(w_extra)
    res = pl.pallas_call(
        functools.partial(_inproj_kernel, has_extra=has_extra, n_lat_tiles=rw.lat_rows // tm if split else None),
        grid=(rw.rows // tm, n // tn),
        in_specs=in_specs, out_specs=out_specs, out_shape=out_shape,
        scratch_shapes=[pltpu.VMEM((tm, D), BF16)],
        compiler_params=_cparams(("parallel", "arbitrary")),
    )(*args)
    return res if has_extra else res[0]


def _mlp_kernel(x_ref, g_ref, sh_ref, sc_ref, gt_ref, w1_ref, w2_ref, go_ref, o_ref, h_ref, acc_ref,
                *, final_norm):
    k = pl.program_id(1)

    @pl.when(k == 0)
    def _():
        h_ref[...] = _bf(_norm_mod(x_ref[...], g_ref[...], sh_ref[0], sc_ref[0]))
        acc_ref[...] = jnp.zeros_like(acc_ref)

    a = jnp.square(jnp.maximum(_dot(h_ref[...], _bf(w1_ref[...])), 0.0))
    acc_ref[...] += _dot(_bf(a), _bf(w2_ref[...]))

    @pl.when(k == pl.num_programs(1) - 1)
    def _():
        y = x_ref[...] + gt_ref[0] * acc_ref[...]
        if final_norm:
            y = y * lax.rsqrt(jnp.mean(y * y, axis=-1, keepdims=True) + EPS) * go_ref[...]
        o_ref[...] = y


def _mlp(rw, x, n_rows, gain, mod3, layer, w1, w2, out_gain, final_norm):
    tm, ck = TM_IN, FF_CHUNK
    return pl.pallas_call(
        functools.partial(_mlp_kernel, final_norm=final_norm),
        grid=(n_rows // tm, D_FF // ck),
        in_specs=[pl.BlockSpec((tm, D), lambda i, k: (i, 0)),
                  pl.BlockSpec((1, D), lambda i, k: (0, 0)),
                  _mod_spec(rw, layer, 3, tm),
                  _mod_spec(rw, layer, 4, tm),
                  _mod_spec(rw, layer, 5, tm),
                  pl.BlockSpec((None, D, ck), lambda i, k: (layer, 0, k)),
                  pl.BlockSpec((None, ck, D), lambda i, k: (layer, k, 0)),
                  pl.BlockSpec((1, D), lambda i, k: (0, 0))],
        out_specs=pl.BlockSpec((tm, D), lambda i, k: (i, 0)),
        out_shape=jax.ShapeDtypeStruct((n_rows, D), F32),
        scratch_shapes=[pltpu.VMEM((tm, D), BF16), pltpu.VMEM((tm, D), F32)],
        compiler_params=_cparams(("parallel", "arbitrary"), MLP_VMEM_LIMIT),
    )(x, gain.reshape(1, D), mod3, mod3, mod3, w1, w2, out_gain.reshape(1, D))


OUT_SUB = 256


def _row_blocks(n_rows):
    return [slice(r, r + OUT_SUB) for r in range(0, n_rows, OUT_SUB)]


def _gdn_out_kernel(x_ref, *rest, n_lat_tiles):
    if n_lat_tiles is not None:
        xc_ref, rest = rest[0], rest[1:]
    gt_ref, o0_ref, o1_ref, z_ref, ng_ref, w_ref, out_ref, a_ref = rest
    ng = ng_ref[...]
    for rows in _row_blocks(x_ref.shape[0]):
        for h in range(GDN_V_HEADS):
            sl = slice(h * GDN_HD, (h + 1) * GDN_HD)
            o = o0_ref[0, rows, sl].astype(F32) + o1_ref[0, rows, sl].astype(F32)
            y = o * lax.rsqrt(jnp.mean(o * o, axis=-1, keepdims=True) + EPS) * ng
            a_ref[rows, sl] = _bf(y * _silu(z_ref[rows, sl].astype(F32)))
        x = x_ref[rows] if n_lat_tiles is None else _pick_rows(pl.program_id(0), n_lat_tiles, x_ref, xc_ref, rows)
        out_ref[rows] = x + gt_ref[0] * _dot(a_ref[rows], w_ref[...])


def _ret_out_kernel(x_ref, gt_ref, o0_ref, o1_ref, z_ref, ng_ref, w_ref, out_ref, a_ref):
    for rows in _row_blocks(x_ref.shape[0]):
        for h in range(RET_HEADS):
            sl = slice(h * RET_DV, (h + 1) * RET_DV)
            o = o0_ref[0, rows, sl].astype(F32) + o1_ref[0, rows, sl].astype(F32)
            mu = jnp.mean(o, axis=-1, keepdims=True)
            oc = o - mu
            var = jnp.mean(oc * oc, axis=-1, keepdims=True)
            y = oc * lax.rsqrt(var + EPS) * ng_ref[:, sl]
            a_ref[rows, sl] = _bf(y * _silu(z_ref[rows, sl].astype(F32)))
        out_ref[rows] = x_ref[rows] + gt_ref[0] * _dot(a_ref[rows], w_ref[...])


def _lru_out_kernel(x_ref, gt_ref, h0_ref, h1_ref, gb_ref, w_ref, out_ref):
    for rows in _row_blocks(x_ref.shape[0]):
        a = (h0_ref[rows].astype(F32) + h1_ref[rows].astype(F32)) * _gelu_tanh(gb_ref[rows].astype(F32))
        out_ref[rows] = x_ref[rows] + gt_ref[0] * _dot(_bf(a), w_ref[...])


def _swa_out_kernel(x_ref, gt_ref, o_ref, w_ref, out_ref):
    out_ref[...] = x_ref[...] + gt_ref[0] * _dot(_bf(o_ref[...]), w_ref[...])


def _outproj_call(kern, rw, n_rows, layer, x, mod3, extra_specs, extra_args, w, scratch):
    tm = TM_OUT
    k = w.shape[0]
    split = isinstance(x, tuple)
    xs = list(x) if split else [x]
    return pl.pallas_call(
        kern,
        grid=(n_rows // tm,),
        in_specs=_stream_specs(rw, tm, split) + [_mod_spec(rw, layer, 2, tm)] + extra_specs
                 + [pl.BlockSpec((k, D), lambda i: (0, 0))],
        out_specs=pl.BlockSpec((tm, D), lambda i: (i, 0)),
        out_shape=jax.ShapeDtypeStruct((n_rows, D), F32),
        scratch_shapes=scratch,
        compiler_params=_cparams(("parallel",)),
    )(*xs, mod3, *extra_args, w)


def _two_dir_out(kern, rw, layer, x, mod3, o, proj, z_block, norm_g, w):
    tm, k = TM_OUT, w.shape[0]
    ng = norm_g.reshape(1, -1)
    specs = [pl.BlockSpec((1, tm, k), lambda i: (0, i, 0)),
             pl.BlockSpec((1, tm, k), lambda i: (1, i, 0)),
             pl.BlockSpec((tm, k), lambda i: (i, z_block)),
             pl.BlockSpec(ng.shape, lambda i: (0, 0))]
    return _outproj_call(kern, rw, rw.rows, layer, x, mod3, specs, [o, o, proj, ng], w,
                         [pltpu.VMEM((tm, k), BF16)])


def _shift_matrices(t):
    delta = lax.broadcasted_iota(jnp.int32, (t, t), 0) - lax.broadcasted_iota(jnp.int32, (t, t), 1)
    return {k: _bf(jnp.where(jnp.logical_or(delta == k, delta == k - t), 1.0, 0.0)) for k in (2, 1, t - 1)}


def _conv_core(x_ref, p_ref, n_ref, w_ref, first, last, shifts=None):
    t = x_ref.shape[0]
    xb = x_ref[...]
    x = xb.astype(F32)
    prev = p_ref[...].astype(F32)[HALO - 8:] * jnp.where(first, 0.0, 1.0)
    nxt = n_ref[...].astype(F32)[:8] * jnp.where(last, 0.0, 1.0)
    if shifts is not None:
        assert xb.dtype == BF16
        r2, r1, rp = _dot(shifts[2], xb), _dot(shifts[1], xb), _dot(shifts[t - 1], xb)
    else:
        r2, r1, rp = pltpu.roll(x, 2, 0), pltpu.roll(x, 1, 0), pltpu.roll(x, t - 1, 0)
    row8 = lax.broadcasted_iota(jnp.int32, prev.shape, 0)
    f2 = jnp.where(row8 < 2, pltpu.roll(prev, 2, 0), r2[:8])
    f1 = jnp.where(row8 < 1, pltpu.roll(prev, 1, 0), r1[:8])
    l1 = jnp.where(row8 == 7, pltpu.roll(nxt, 7, 0), rp[t - 8:])
    xm2 = jnp.concatenate([f2, r2[8:]], axis=0)
    xm1 = jnp.concatenate([f1, r1[8:]], axis=0)
    xp1 = jnp.concatenate([rp[:t - 8], l1], axis=0)
    w = w_ref[...]
    return w[0:1] * xm2 + w[1:2] * xm1 + w[2:3] * x + w[3:4] * xp1


def _seg_flags(rw, i):
    per = rw.n_lat // TM_CONV
    is_ctx = i >= rw.lat_rows // TM_CONV
    assert rw.n_ctx == TM_CONV
    first = jnp.logical_or(is_ctx, i % per == 0)
    last = jnp.logical_or(is_ctx, i % per == per - 1)
    return first, last


def _gdn_conv_kernel(x_ref, p_ref, n_ref, w_ref, ab_ref, alog_ref, dtb_ref, o_ref, g_ref, b_ref, *, rw):
    i, j = pl.program_id(0), pl.program_id(1)
    first, last = _seg_flags(rw, i)
    s = _silu(_conv_core(x_ref, p_ref, n_ref, w_ref, first, last, _shift_matrices(x_ref.shape[0])))
    width = s.shape[1]
    qk_blocks = 2 * GDN_KEY // width

    @pl.when(j < qk_blocks)
    def _():
        for h in range(width // GDN_HD):
            sl = slice(h * GDN_HD, (h + 1) * GDN_HD)
            sh = s[:, sl]
            is_q = j * width + h * GDN_HD < GDN_KEY
            scale = jnp.where(is_q, GDN_HD ** -0.5, 1.0)
            o_ref[:, sl] = sh * (lax.rsqrt(jnp.sum(sh * sh, axis=-1, keepdims=True) + EPS) * scale)

    @pl.when(j >= qk_blocks)
    def _():
        o_ref[...] = s

    @pl.when(j == 0)
    def _():
        ab = ab_ref[...]
        g_ref[...] = -jnp.exp(alog_ref[...]) * _softplus(ab + dtb_ref[...])
        b_ref[...] = jax.nn.sigmoid(ab)


def _conv_specs(rw, cb):
    th = TM_CONV // HALO
    nh = rw.rows // HALO
    return [pl.BlockSpec((TM_CONV, cb), lambda i, j: (i, j)),
            pl.BlockSpec((HALO, cb), lambda i, j: (jnp.maximum(i * th - 1, 0), j)),
            pl.BlockSpec((HALO, cb), lambda i, j: (jnp.minimum((i + 1) * th, nh - 1), j)),
            pl.BlockSpec((4, cb), lambda i, j: (0, j))]


def _dir_masks(d, n):
    ii = lax.broadcasted_iota(jnp.int32, (n, n), 0)
    jj = lax.broadcasted_iota(jnp.int32, (n, n), 1)
    t = (ii - jj) * jnp.where(d == 0, 1, -1)
    return t, t >= 0, t > 0


def _gdn_core_kernel(qkv_ref, g_ref, b_ref, o_ref, s_ref):
    d, s = pl.program_id(1), pl.program_id(2)

    @pl.when(s == 0)
    def _():
        s_ref[...] = jnp.zeros_like(s_ref)

    c, hd = CHUNK, GDN_HD
    rep = GDN_V_HEADS // GDN_QK_HEADS
    heads = range(GDN_V_HEADS)
    n_sub = qkv_ref.shape[0] // c
    subs = range(n_sub)
    offs = [pl.multiple_of(jnp.where(d == 0, i, n_sub - 1 - i) * c, c) for i in subs]
    t, incl, strict = _dir_masks(d, c)
    cum = _bf(jnp.where(incl, 1.0, 0.0))
    cum_t = _bf(jnp.where(t <= 0, 1.0, 0.0))
    g = [g_ref[0, pl.ds(offs[i], c), :] for i in subs]
    beta = [b_ref[0, pl.ds(offs[i], c), :] for i in subs]
    parts = [_split3(g[i]) for i in subs]
    gc = [sum(_dot(cum, p) for p in parts[i]) for i in subs]
    gr = [sum(_dot_tn(p, cum_t) for p in parts[i]) for i in subs]
    gtot = [jnp.sum(g[i], axis=0, keepdims=True) for i in subs]

    def rows(i, lo):
        return qkv_ref[pl.ds(offs[i], c), lo:lo + hd]

    q_l = [[rows(i, hq * hd) for hq in range(GDN_QK_HEADS)] for i in subs]
    k_l = [[rows(i, GDN_KEY + hq * hd) for hq in range(GDN_QK_HEADS)] for i in subs]
    gram = [[_dot_nt(_bf(jnp.concatenate([k, q], axis=0)), _bf(k)) for q, k in zip(q_l[i], k_l[i])]
            for i in subs]
    zeros_cc = jnp.zeros((c, 2 * c), BF16)
    zeros_rhs = jnp.zeros((c, 2 * hd), BF16)
    lane_cc = lax.broadcasted_iota(jnp.int32, (c, 2 * c), 1)
    row_cc = lax.broadcasted_iota(jnp.int32, (c, 2 * c), 0)
    eye_hi = _bf(jnp.where(lane_cc == row_cc + c, 1.0, 0.0))
    is_hi = lane_cc >= c

    def prepare(chains):
        ids = range(len(chains))
        gcol = [gc[i][:, h:h + 1] for i, h in chains]
        bcol = [beta[i][:, h:h + 1] for i, h in chains]
        dec = [jnp.where(incl, jnp.exp(jnp.where(incl, gcol[n] - gr[i][h:h + 1, :], 0.0)), 0.0)
               for n, (i, h) in enumerate(chains)]
        eg = [jnp.exp(x) for x in gcol]
        m = [jnp.where(strict, -(gram[i][h // rep][:c] * dec[n] * bcol[n]), 0.0) for n, (i, h) in enumerate(chains)]
        rhs = [jnp.concatenate([k_l[i][h // rep] * (bcol[n] * eg[n]), rows(i, 2 * GDN_KEY + h * hd) * bcol[n]],
                               axis=1) for n, (i, h) in enumerate(chains)]

        m_hi = [_bf(x) for x in m]
        cb = [jnp.concatenate([m_hi[n], jnp.zeros((c, c), BF16)], axis=1) + eye_hi for n in ids]
        for _ in range(6):
            r = [_dot(cb[n], jnp.concatenate([cb[n], zeros_cc], axis=0)) for n in ids]
            cb = [_bf(r[n]) + jnp.where(is_hi, cb[n], jnp.zeros_like(cb[n])) for n in ids]
        nb = [cb[n] - eye_hi for n in ids]

        def apply_n(y):
            return [_dot(nb[n], jnp.concatenate([zeros_rhs, _bf(y[n])], axis=0)) for n in ids]

        nr = apply_n(rhs)
        x = [rhs[n] + nr[n] for n in ids]
        resid = [_dot(m_hi[n], _bf(x[n])) - nr[n] for n in ids]
        nr = apply_n(resid)
        x = [x[n] + (resid[n] + nr[n]) for n in ids]
        qk = [_bf(gram[i][h // rep][c:] * dec[n]) for n, (i, h) in enumerate(chains)]
        gt = [gtot[i][:, h:h + 1] for i, h in chains]
        kd = [_bf(k_l[i][h // rep] * jnp.exp(gt[n] - gcol[n])) for n, (i, h) in enumerate(chains)]
        wq = [_bf(jnp.concatenate([x[n][:, :hd], q_l[i][h // rep] * eg[n]], axis=0))
              for n, (i, h) in enumerate(chains)]
        decay = [jnp.exp(x) for x in gt]
        return x, qk, kd, wq, decay

    per_group = GDN_GROUP // GDN_V_HEADS
    st = [s_ref[h] for h in heads]
    for first in range(0, n_sub, per_group):
        chunk_ids = range(first, first + per_group)
        x, qk, kd, wq, decay = prepare([(i, h) for i in chunk_ids for h in heads])
        for j, i in enumerate(chunk_ids):
            base = j * GDN_V_HEADS
            r = [_dot(wq[base + h], _bf(st[h])) for h in heads]
            unb = [_bf(x[base + h][:, hd:] - r[h][:c]) for h in heads]
            for h in heads:
                o_ref[0, pl.ds(offs[i], c), h * hd:(h + 1) * hd] = _bf(r[h][c:] + _dot(qk[base + h], unb[h]))
            st = [st[h] * decay[base + h] + _dot_tn(kd[base + h], unb[h]) for h in heads]
    for h in heads:
        s_ref[h] = st[h]


def _gdn_core(rw, qkv, g_dir, b_dir):
    blk = GDN_BLOCK
    rb = lambda b, d, s: rw.chunk_block(b, d, s, blk)
    return pl.pallas_call(
        _gdn_core_kernel,
        grid=(rw.batch, 2, (rw.n_lat + rw.n_ctx) // blk),
        in_specs=[pl.BlockSpec((blk, GDN_CONV), lambda b, d, s: (rb(b, d, s), 0)),
                  pl.BlockSpec((1, blk, GDN_V_HEADS), lambda b, d, s: (d, rb(b, d, s), 0)),
                  pl.BlockSpec((1, blk, GDN_V_HEADS), lambda b, d, s: (d, rb(b, d, s), 0))],
        out_specs=pl.BlockSpec((1, blk, GDN_VAL), lambda b, d, s: (d, rb(b, d, s), 0)),
        out_shape=jax.ShapeDtypeStruct((2, rw.rows, GDN_VAL), BF16),
        scratch_shapes=[pltpu.VMEM((GDN_V_HEADS, GDN_HD, GDN_HD), F32)],
        compiler_params=_cparams(("parallel", "parallel", "arbitrary")),
    )(qkv, g_dir, b_dir)


def _rope_half(x, cos, sin_signed):
    return x * cos + pltpu.roll(x, x.shape[1] // 2, 1) * sin_signed


def _ret_core_kernel(qk_ref, v_ref, cos_ref, sin_ref, dl_ref, o_ref, s_ref, dec_ref, *, ctx_steps):
    d, s = pl.program_id(1), pl.program_id(2)
    c = qk_ref.shape[0]
    heads = range(RET_HEADS)
    is_lat = s >= ctx_steps
    cos = jnp.where(is_lat, cos_ref[...], 1.0)
    sin = jnp.where(is_lat, sin_ref[...], 0.0)
    t, incl, _ = _dir_masks(d, c)
    tf = t.astype(F32)
    pos = lax.broadcasted_iota(jnp.int32, (c, 1), 0)
    ridx = jnp.where(d == 0, pos + 1, c - pos).astype(F32)
    log_gamma = -_softplus(-dl_ref[0])

    def rope(x):
        hw = RET_DK // 2
        return jnp.concatenate([_rope_half(x[:, :hw], cos[:, :hw], sin[:, :hw]),
                                _rope_half(x[:, hw:], cos[:, hw:], sin[:, hw:])], axis=1)

    lg = [log_gamma[:, h:h + 1] for h in heads]

    @pl.when(s == 0)
    def _():
        s_ref[...] = jnp.zeros_like(s_ref)
        for h in heads:
            dec_ref[h] = jnp.where(incl, jnp.exp(jnp.where(incl, lg[h] * tf, 0.0)), 0.0)

    q = [rope(qk_ref[:, h * RET_DK:(h + 1) * RET_DK].astype(F32)) * (RET_DK ** -0.5) for h in heads]
    k = [rope(qk_ref[:, RET_KEY + h * RET_DK:RET_KEY + (h + 1) * RET_DK].astype(F32)) for h in heads]
    vb = [v_ref[:, h * RET_DV:(h + 1) * RET_DV] for h in heads]
    gcum = [lg[h] * ridx for h in heads]
    qk = [_bf(_dot_nt(_bf(q[h]), _bf(k[h])) * dec_ref[h]) for h in heads]
    st = [s_ref[h] for h in heads]
    for h in heads:
        o_ref[0, :, h * RET_DV:(h + 1) * RET_DV] = _bf(
            _dot(_bf(q[h] * jnp.exp(gcum[h])), _bf(st[h])) + _dot(qk[h], vb[h]))
    gtot = [lg[h] * float(c) for h in heads]
    kd = [_bf(k[h] * jnp.exp(gtot[h] - gcum[h])) for h in heads]
    for h in heads:
        s_ref[h] = st[h] * jnp.exp(gtot[h]) + _dot_tn(kd[h], vb[h])


def _ret_core(rw, proj, cos_tab, sin_tab, decay_logit):
    c = RET_CHUNK
    rb = lambda b, d, s: rw.chunk_block(b, d, s, c)
    dl = jnp.zeros((2, 1, 128), F32).at[:, 0, :RET_HEADS].set(decay_logit)
    return pl.pallas_call(
        functools.partial(_ret_core_kernel, ctx_steps=rw.n_ctx // c),
        grid=(rw.batch, 2, (rw.n_lat + rw.n_ctx) // c),
        in_specs=[pl.BlockSpec((c, 2 * RET_KEY), lambda b, d, s: (rb(b, d, s), 0)),
                  pl.BlockSpec((c, RET_VAL), lambda b, d, s: (rb(b, d, s), 1)),
                  pl.BlockSpec((c, RET_DK), lambda b, d, s: (rw.lat_chunk(d, s, c), 0)),
                  pl.BlockSpec((c, RET_DK), lambda b, d, s: (rw.lat_chunk(d, s, c), 0)),
                  pl.BlockSpec((1, 1, 128), lambda b, d, s: (d, 0, 0))],
        out_specs=pl.BlockSpec((1, c, RET_VAL), lambda b, d, s: (d, rb(b, d, s), 0)),
        out_shape=jax.ShapeDtypeStruct((2, rw.rows, RET_VAL), BF16),
        scratch_shapes=[pltpu.VMEM((RET_HEADS, RET_DK, RET_DV), F32), pltpu.VMEM((RET_HEADS, c, c), F32)],
        compiler_params=_cparams(("parallel", "parallel", "arbitrary")),
    )(proj, proj, cos_tab, sin_tab, dl)


def _lru_scan_kernel(x_ref, p_ref, n_ref, cw_ref, cb_ref, wg_ref, bg_ref, lam_ref, h_ref, carry_ref, *, rw, rev):
    @pl.when(pl.program_id(1) == 0)
    def _():
        carry_ref[...] = jnp.zeros_like(carry_ref)

    t = x_ref.shape[0]
    tile = rw.chunk_block(pl.program_id(0), 1 if rev else 0, pl.program_id(1), t)
    first, last = _seg_flags(rw, tile)
    xs = _conv_core(x_ref, p_ref, n_ref, cw_ref, first, last) + cb_ref[...]
    row8 = lax.broadcasted_iota(jnp.int32, (t // 8, 8, LRU_BW), 1)
    sp = _softplus(-lam_ref[...])
    for n in range(LRU_BLOCKS):
        sl = slice(n * LRU_BW, (n + 1) * LRU_BW)
        xn = xs[:, sl]
        gates = jax.nn.sigmoid(_dot(_bf(xn), wg_ref[n]) + bg_ref[n])
        log_a = -LRU_C * gates[:, :LRU_BW] * sp[:, sl]
        a = jnp.exp(log_a)
        y = -jnp.tanh(log_a) * (a * a + 1.0)
        b = jnp.where(y > 0.0, y * lax.rsqrt(y), 0.0) * gates[:, LRU_BW:] * xn
        groups = t // 8
        a3 = a.reshape(groups, 8, LRU_BW)
        b3 = b.reshape(groups, 8, LRU_BW)
        for k in (1, 2, 4):
            if rev:
                keep = row8 < 8 - k
                shift = 8 - k
            else:
                keep = row8 >= k
                shift = k
            sa = jnp.where(keep, pltpu.roll(a3, shift, 1), 1.0)
            sb = jnp.where(keep, pltpu.roll(b3, shift, 1), 0.0)
            b3 = b3 + a3 * sb
            a3 = a3 * sa
        carry = carry_ref[:, sl]
        done = {}
        for gi in (reversed(range(groups)) if rev else range(groups)):
            hg = a3[gi] * carry + b3[gi]
            carry = hg[0:1] if rev else hg[7:8]
            done[gi] = hg
            lo = gi - gi % 2
            if lo in done and lo + 1 in done:
                h_ref[lo * 8:(lo + 2) * 8, sl] = _bf(jnp.concatenate([done.pop(lo), done.pop(lo + 1)], axis=0))
        carry_ref[:, sl] = carry


def _lru_scan(rw, proj, conv_w, conv_b, w_gate, b_gate, lam, rev):
    t = TM_CONV
    th = t // HALO
    nh = rw.rows // HALO
    d = 1 if rev else 0
    rb = lambda b, s: rw.chunk_block(b, d, s, t)
    return pl.pallas_call(
        functools.partial(_lru_scan_kernel, rw=rw, rev=rev),
        grid=(rw.batch, (rw.n_lat + rw.n_ctx) // t),
        in_specs=[pl.BlockSpec((t, LRU_W), lambda b, s: (rb(b, s), 0)),
                  pl.BlockSpec((HALO, LRU_W), lambda b, s: (jnp.maximum(rb(b, s) * th - 1, 0), 0)),
                  pl.BlockSpec((HALO, LRU_W), lambda b, s: (jnp.minimum((rb(b, s) + 1) * th, nh - 1), 0)),
                  pl.BlockSpec((4, LRU_W), lambda b, s: (0, 0)),
                  pl.BlockSpec((1, LRU_W), lambda b, s: (0, 0)),
                  pl.BlockSpec((LRU_BLOCKS, LRU_BW, 2 * LRU_BW), lambda b, s: (0, 0, 0)),
                  pl.BlockSpec((LRU_BLOCKS, 1, 2 * LRU_BW), lambda b, s: (0, 0, 0)),
                  pl.BlockSpec((1, LRU_W), lambda b, s: (0, 0))],
        out_specs=pl.BlockSpec((t, LRU_W), lambda b, s: (rb(b, s), 0)),
        out_shape=jax.ShapeDtypeStruct((rw.rows, LRU_W), BF16),
        scratch_shapes=[pltpu.VMEM((1, LRU_W), F32)],
        compiler_params=_cparams(("parallel", "arbitrary")),
    )(proj, proj, proj, conv_w, conv_b.reshape(1, LRU_W), _bf(w_gate),
      b_gate.reshape(LRU_BLOCKS, 1, 2 * LRU_BW), lam.reshape(1, LRU_W))


def _rope16(x, tab):
    cos, s1, s2 = tab[:, :128], tab[:, 128:256], tab[:, 256:]
    cols = []
    for cidx in range(x.shape[1] // 128):
        xc = x[:, cidx * 128:(cidx + 1) * 128]
        cols.append(xc * cos + pltpu.roll(xc, 112, 1) * s1 + pltpu.roll(xc, 16, 1) * s2)
    return cols[0] if len(cols) == 1 else jnp.concatenate(cols, axis=1)


def _swa_kernel(sink_ref, q_ref, kvp_ref, kvc_ref, kvn_ref, kvx_ref, tp_ref, tc_ref, tn_ref, o_ref):
    qb = pl.program_id(1)
    nb = pl.num_programs(1)
    blk = SWA_BLOCK
    n_loc = 3 * blk
    n_keys = n_loc + kvx_ref.shape[0]
    tab_c = tc_ref[...]
    keys = lambda ref: ref[:, :SWA_KVD].astype(F32)
    q = _rope16(q_ref[...].astype(F32), tab_c) * (SWA_HD ** -0.5 * LOG2E)
    k_all = jnp.concatenate([_rope16(keys(kvp_ref), tp_ref[...]), _rope16(keys(kvc_ref), tab_c),
                             _rope16(keys(kvn_ref), tn_ref[...]), keys(kvx_ref)], axis=0)
    v_all = jnp.concatenate([ref[:, SWA_KVD:] for ref in (kvp_ref, kvc_ref, kvn_ref, kvx_ref)], axis=0).astype(F32)
    assert SWA_WINDOW >= blk - 1
    qi = lax.broadcasted_iota(jnp.int32, (blk, blk), 0)
    kj = lax.broadcasted_iota(jnp.int32, (blk, blk), 1)
    valid_prev = jnp.logical_and(kj - blk - qi >= -SWA_WINDOW, qb > 0)
    valid_next = jnp.logical_and(kj + blk - qi <= SWA_WINDOW, qb < nb - 1)
    lane = lax.broadcasted_iota(jnp.int32, (n_keys, 128), 1)
    low = lane < SWA_HD
    out_low = lax.broadcasted_iota(jnp.int32, (blk, 128), 1) < SWA_HD

    def expo(scores, sink):
        sc = jnp.concatenate([jnp.where(valid_prev, scores[:, :blk], -jnp.inf), scores[:, blk:2 * blk],
                              jnp.where(valid_next, scores[:, 2 * blk:n_loc], -jnp.inf), scores[:, n_loc:]], axis=1)
        mx = jnp.maximum(jnp.max(sc, axis=-1, keepdims=True), sink)
        return _bf(jnp.exp2(sc - mx)), jnp.exp2(sink - mx)

    group = SWA_QH // SWA_KVH
    for g in range(SWA_KVH):
        col = slice((g // 2) * 128, (g // 2 + 1) * 128)
        native_low = g % 2 == 0
        sel = low if native_low else jnp.logical_not(low)
        k_nat = jnp.where(sel, k_all[:, col], 0.0)
        v_nat = jnp.where(sel, v_all[:, col], 0.0)
        k_oth = pltpu.roll(k_nat, SWA_HD, 1)
        v_oth = pltpu.roll(v_nat, SWA_HD, 1)
        k_lo, k_hi = (k_nat, k_oth) if native_low else (k_oth, k_nat)
        v_lo, v_hi = (v_nat, v_oth) if native_low else (v_oth, v_nat)
        v_lo = jnp.where(lane == SWA_HD, 1.0, v_lo)
        v_hi = jnp.where(lane == 0, 1.0, v_hi)
        k_lo, k_hi, v_lo, v_hi = _bf(k_lo), _bf(k_hi), _bf(v_lo), _bf(v_hi)
        for cidx in range(group // 2):
            qc = (g * group) // 2 + cidx
            qv = _bf(q[:, qc * 128:(qc + 1) * 128])
            e_lo, sink_lo = expo(_dot_nt(qv, k_lo), sink_ref[2 * qc] * LOG2E)
            e_hi, sink_hi = expo(_dot_nt(qv, k_hi), sink_ref[2 * qc + 1] * LOG2E)
            a_lo = _dot(e_lo, v_lo)
            a_hi = _dot(e_hi, v_hi)
            inv_lo = 1.0 / (a_lo[:, SWA_HD:SWA_HD + 1] + sink_lo)
            inv_hi = 1.0 / (a_hi[:, 0:1] + sink_hi)
            o_ref[:, qc * 128:(qc + 1) * 128] = jnp.where(out_low, a_lo * inv_lo, a_hi * inv_hi)


def _swa_core(rw, proj, sink, tab):
    blk = SWA_BLOCK
    nb = rw.n_lat // blk
    kvcol = SWA_QD // (2 * SWA_KVD)
    ctx_blk = lambda b: rw.lat_rows // rw.n_ctx + b
    prev = lambda q: jnp.maximum(q - 1, 0)
    nxt = lambda q: jnp.minimum(q + 1, nb - 1)
    kv = lambda f: pl.BlockSpec((blk, 2 * SWA_KVD), lambda b, q: (b * nb + f(q), kvcol))
    same = lambda q: q
    return pl.pallas_call(
        _swa_kernel,
        grid=(rw.batch, nb),
        in_specs=[pl.BlockSpec(memory_space=pltpu.SMEM),
                  pl.BlockSpec((blk, SWA_QD), lambda b, q: (b * nb + q, 0)),
                  kv(prev), kv(same), kv(nxt),
                  pl.BlockSpec((rw.n_ctx, 2 * SWA_KVD), lambda b, q: (ctx_blk(b), kvcol)),
                  pl.BlockSpec((blk, 384), lambda b, q: (prev(q), 0)),
                  pl.BlockSpec((blk, 384), lambda b, q: (q, 0)),
                  pl.BlockSpec((blk, 384), lambda b, q: (nxt(q), 0))],
        out_specs=pl.BlockSpec((blk, SWA_QD), lambda b, q: (b * nb + q, 0)),
        out_shape=jax.ShapeDtypeStruct((rw.lat_rows, SWA_QD), F32),
        compiler_params=_cparams(("parallel", "parallel")),
    )(sink, proj, proj, proj, proj, proj, tab, tab, tab)


def _axial_angles(n_lat, dh):
    t = jnp.arange(n_lat)
    row = (t // GRID_W).astype(F32)
    col = (t % GRID_W).astype(F32)
    nf = dh // 4
    inv = ROPE_THETA ** (-jnp.arange(nf, dtype=F32) / nf)
    return row[:, None] * inv, col[:, None] * inv


def _ret_tables(n_lat):
    ar, ac = _axial_angles(n_lat, RET_DK)
    cos = jnp.concatenate([jnp.cos(ar), jnp.cos(ar), jnp.cos(ac), jnp.cos(ac)], axis=1)
    sin = jnp.concatenate([-jnp.sin(ar), jnp.sin(ar), -jnp.sin(ac), jnp.sin(ac)], axis=1)
    return cos, sin


def _swa_table(n_lat):
    ar, ac = _axial_angles(n_lat, SWA_HD)
    z = jnp.zeros_like(ar)
    cos = jnp.concatenate([jnp.cos(ar), jnp.cos(ar), jnp.cos(ac), jnp.cos(ac)], axis=1)
    s1 = jnp.concatenate([-jnp.sin(ar), z, -jnp.sin(ac), z], axis=1)
    s2 = jnp.concatenate([z, jnp.sin(ar), z, jnp.sin(ac)], axis=1)
    return jnp.concatenate([jnp.tile(cos, (1, 2)), jnp.tile(s1, (1, 2)), jnp.tile(s2, (1, 2))], axis=1)


def _gdn_layer(rw, layer, x, mod3, gain, w_in, conv_w, a_log, dt_bias, norm_g, w_out):
    n_ab = w_in.shape[1] - GDN_MAIN
    w_ab = _bf(jnp.pad(w_in[:, GDN_MAIN:], ((0, 0), (0, 128 - n_ab))))
    proj, ab = _inproj(rw, x, gain, mod3, layer, _bf(w_in), w_ab, n_cols=GDN_MAIN)
    hv = GDN_V_HEADS
    alog_row = jnp.zeros((1, 128), F32).at[0, 0:hv].set(a_log[0]).at[0, 2 * hv:3 * hv].set(a_log[1])
    dtb_row = jnp.zeros((1, 128), F32).at[0, 0:hv].set(dt_bias[0]).at[0, 2 * hv:3 * hv].set(dt_bias[1])
    cb = 2048
    row_spec = pl.BlockSpec((TM_CONV, 128), lambda i, j: (i, 0))
    vec_spec = pl.BlockSpec((1, 128), lambda i, j: (0, 0))
    qkv, g_all, b_all = pl.pallas_call(
        functools.partial(_gdn_conv_kernel, rw=rw),
        grid=(rw.rows // TM_CONV, GDN_CONV // cb),
        in_specs=_conv_specs(rw, cb) + [row_spec, vec_spec, vec_spec],
        out_specs=[pl.BlockSpec((TM_CONV, cb), lambda i, j: (i, j)), row_spec, row_spec],
        out_shape=[jax.ShapeDtypeStruct((rw.rows, GDN_CONV), F32),
                   jax.ShapeDtypeStruct((rw.rows, 128), F32),
                   jax.ShapeDtypeStruct((rw.rows, 128), F32)],
        compiler_params=_cparams(("parallel", "arbitrary")),
    )(proj, proj, proj, conv_w, ab, alog_row, dtb_row)
    g_dir = jnp.stack([g_all[:, 0:hv], g_all[:, 2 * hv:3 * hv]])
    b_dir = jnp.stack([b_all[:, hv:2 * hv], b_all[:, 3 * hv:4 * hv]])
    o = _gdn_core(rw, qkv, g_dir, b_dir)
    kern = functools.partial(_gdn_out_kernel, n_lat_tiles=rw.lat_rows // TM_OUT if isinstance(x, tuple) else None)
    return _two_dir_out(kern, rw, layer, x, mod3, o, proj, GDN_CONV // GDN_VAL, norm_g, _bf(w_out))


def _ret_layer(rw, layer, x, mod3, gain, w_in, decay_logit, norm_g, w_out):
    proj = _inproj(rw, x, gain, mod3, layer, _bf(w_in))
    cos_tab, sin_tab = _ret_tables(rw.n_lat)
    o = _ret_core(rw, proj, cos_tab, sin_tab, decay_logit)
    return _two_dir_out(_ret_out_kernel, rw, layer, x, mod3, o, proj, (2 * RET_KEY + RET_VAL) // RET_VAL,
                        norm_g, _bf(w_out))


def _lru_layer(rw, layer, x, mod3, gain, w_in, conv_w, conv_b, w_gate, b_gate, lam, w_out):
    proj = _inproj(rw, x, gain, mod3, layer, _bf(w_in))
    h0 = _lru_scan(rw, proj, conv_w, conv_b, w_gate[0], b_gate[0], lam[0], rev=False)
    h1 = _lru_scan(rw, proj, conv_w, conv_b, w_gate[1], b_gate[1], lam[1], rev=True)
    tm = TM_OUT
    specs = [pl.BlockSpec((tm, LRU_W), lambda i: (i, 0)),
             pl.BlockSpec((tm, LRU_W), lambda i: (i, 0)),
             pl.BlockSpec((tm, LRU_W), lambda i: (i, 1))]
    return _outproj_call(_lru_out_kernel, rw, rw.rows, layer, x, mod3, specs, [h0, h1, proj], _bf(w_out), [])


def _swa_layer(rw, layer, x, mod3, gain, w_in, sink, w_out):
    proj = _inproj(rw, x, gain, mod3, layer, _bf(w_in))
    o = _swa_core(rw, proj, sink, _swa_table(rw.n_lat))
    specs = [pl.BlockSpec((TM_OUT, SWA_QD), lambda i: (i, 0))]
    return _outproj_call(_swa_out_kernel, rw, rw.lat_rows, layer, x, mod3, specs, [o], _bf(w_out), [])


def kernel(x, c, ctx, c_ctx, norm_mix_g, norm_ffn_g, w_mod, b_mod, w_ff1, w_ff2, norm_out_g, gdn_w_in, gdn_conv_w, gdn_a_log, gdn_dt_bias, gdn_norm_g, gdn_w_out, ret_w_in, ret_decay_logit, ret_norm_g, ret_w_out, lru_w_in, lru_conv_w, lru_conv_b, lru_w_gate, lru_b_gate, lru_lambda, lru_w_out, swa_w_in, swa_sink, swa_w_out):
    batch, n_lat, _ = x.shape
    n_ctx = ctx.shape[1]
    assert batch + 1 <= MOD_ROWS and n_lat % TM_IN == 0 and (batch * n_ctx) % TM_IN == 0
    assert w_mod.shape[0] == DEPTH and gdn_w_in.shape[0] == 1 and swa_w_in.shape[0] == 1
    rw = _Rows(batch, n_lat, n_ctx)
    xf = (x.reshape(rw.lat_rows, D), ctx.reshape(batch * n_ctx, D))
    cc = jnp.zeros((MOD_ROWS, D), F32).at[:batch].set(c).at[batch].set(c_ctx)
    mod3 = _adaln(cc, w_mod, b_mod)

    def mlp(layer, xin, n_rows, final):
        return _mlp(rw, xin, n_rows, norm_ffn_g[layer], mod3, layer, w_ff1, w_ff2,
                    norm_out_g, final)

    xf = _gdn_layer(rw, 0, xf, mod3, norm_mix_g[0], gdn_w_in[0], gdn_conv_w[0], gdn_a_log[0], gdn_dt_bias[0],
                    gdn_norm_g[0], gdn_w_out[0])
    xf = mlp(0, xf, rw.rows, False)
    xf = _ret_layer(rw, 1, xf, mod3, norm_mix_g[1], ret_w_in[0], ret_decay_logit[0], ret_norm_g[0], ret_w_out[0])
    xf = mlp(1, xf, rw.rows, False)
    xf = _lru_layer(rw, 2, xf, mod3, norm_mix_g[2], lru_w_in[0], lru_conv_w[0], lru_conv_b[0], lru_w_gate[0],
                    lru_b_gate[0], lru_lambda[0], lru_w_out[0])
    xf = mlp(2, xf, rw.rows, False)
    xl = _swa_layer(rw, 3, xf, mod3, norm_mix_g[3], swa_w_in[0], swa_sink[0], swa_w_out[0])
    out = mlp(3, xl, rw.lat_rows, True)
    return out.reshape(batch, n_lat, D)
```

```python
import functools
import math

import jax
import jax.numpy as jnp
from jax import lax
from jax.experimental import pallas as pl
from jax.experimental.pallas import tpu as pltpu

F32 = jnp.float32
BF16 = jnp.bfloat16

D = 1024
D_FF = 4 * D
EPS = 1e-6
LOG2E = math.log2(math.e)
DEPTH = 4
ROPE_THETA = 10000.0
GRID_W = 64
CHUNK = 64
MOD_ROWS = 16

GDN_QK_HEADS = 8
GDN_V_HEADS = 16
GDN_HD = 128
GDN_KEY = GDN_QK_HEADS * GDN_HD
GDN_VAL = GDN_V_HEADS * GDN_HD
GDN_CONV = 2 * GDN_KEY + GDN_VAL
GDN_MAIN = GDN_CONV + GDN_VAL

RET_HEADS = 4
RET_DK = 256
RET_DV = 512
RET_KEY = RET_HEADS * RET_DK
RET_VAL = RET_HEADS * RET_DV

LRU_W = 1280
LRU_BLOCKS = 10
LRU_BW = LRU_W // LRU_BLOCKS
LRU_C = 8.0

SWA_QH = 16
SWA_KVH = 4
SWA_HD = 64
SWA_BLOCK = 128
SWA_WINDOW = 128
SWA_QD = SWA_QH * SWA_HD
SWA_KVD = SWA_KVH * SWA_HD

TM_IN = 1024
TM_OUT = 512
TM_CONV = 256
HALO = 16
TN_IN_MAX = 1536
FF_CHUNK = 1024
RET_CHUNK = 256
GDN_BLOCK = 4 * CHUNK
GDN_GROUP = 32
VMEM_LIMIT = 48 * 1024 * 1024
MLP_VMEM_LIMIT = 56 * 1024 * 1024


def _dot(a, b):
    return jnp.dot(a, b, preferred_element_type=F32)


def _dot_nt(a, b):
    return lax.dot_general(a, b, (((1,), (1,)), ((), ())), preferred_element_type=F32)


def _dot_tn(a, b):
    return lax.dot_general(a, b, (((0,), (0,)), ((), ())), preferred_element_type=F32)


def _bf(x):
    return x.astype(BF16)


def _split3(x):
    hi = _bf(x)
    r = x - hi.astype(F32)
    mid = _bf(r)
    lo = _bf(r - mid.astype(F32))
    return hi, mid, lo


def _silu(x):
    return x * jax.nn.sigmoid(x)


def _softplus(x):
    return jnp.maximum(x, 0.0) + jnp.log1p(jnp.exp(-jnp.abs(x)))


def _gelu_tanh(x):
    cdf = 0.5 * (1.0 + jnp.tanh(math.sqrt(2.0 / math.pi) * (x + 0.044715 * (x * x * x))))
    return x * cdf


def _cparams(sem, vmem=None):
    return pltpu.CompilerParams(dimension_semantics=sem, vmem_limit_bytes=VMEM_LIMIT if vmem is None else vmem)


class _Rows:
    def __init__(self, batch, n_lat, n_ctx):
        self.batch, self.n_lat, self.n_ctx = batch, n_lat, n_ctx
        self.lat_rows = batch * n_lat
        self.rows = self.lat_rows + batch * n_ctx

    def mod_row(self, i, tm):
        return jnp.where(i < self.lat_rows // tm, i // (self.n_lat // tm), self.batch)

    def chunk_block(self, b, d, s, blk):
        nc, nl = self.n_ctx // blk, self.n_lat // blk
        c_ctx = jnp.where(d == 0, s, nc - 1 - s)
        c_lat = jnp.where(d == 0, s - nc, nl - 1 - (s - nc))
        return jnp.where(s < nc, self.lat_rows // blk + b * nc + c_ctx, b * nl + c_lat)

    def lat_chunk(self, d, s, blk):
        nc, nl = self.n_ctx // blk, self.n_lat // blk
        return jnp.where(s < nc, 0, jnp.where(d == 0, s - nc, nl - 1 - (s - nc)))


def _adaln_kernel(c_ref, w_ref, b_ref, o_ref):
    s = _silu(c_ref[...])
    w = w_ref[0]
    s_hi = _bf(s)
    s_lo = _bf(s - s_hi.astype(F32))
    w_hi = _bf(w)
    w_lo = _bf(w - w_hi.astype(F32))
    y = _dot(s_hi, w_hi) + (_dot(s_lo, w_hi) + _dot(s_hi, w_lo))
    o_ref[0] = y + b_ref[0]


def _adaln(cc, w_mod, b_mod):
    depth, _, n = w_mod.shape
    tn = 1024
    out = pl.pallas_call(
        _adaln_kernel,
        grid=(depth, n // tn),
        in_specs=[pl.BlockSpec((MOD_ROWS, D), lambda l, j: (0, 0)),
                  pl.BlockSpec((1, D, tn), lambda l, j: (l, 0, j)),
                  pl.BlockSpec((1, 1, tn), lambda l, j: (l, 0, j))],
        out_specs=pl.BlockSpec((1, MOD_ROWS, tn), lambda l, j: (l, 0, j)),
        out_shape=jax.ShapeDtypeStruct((depth, MOD_ROWS, n), F32),
        compiler_params=_cparams(("parallel", "parallel")),
    )(cc, w_mod, b_mod.reshape(depth, 1, n))
    return out.reshape(depth * MOD_ROWS * 6, 1, D)


def _mod_spec(rw, layer, which, tm):
    base = layer * MOD_ROWS * 6
    return pl.BlockSpec((1, 1, D), lambda i, *_: (base + rw.mod_row(i, tm) * 6 + which, 0, 0))


def _norm_mod(x, gain, shift, scale):
    y = x * lax.rsqrt(jnp.mean(x * x, axis=-1, keepdims=True) + EPS) * gain
    return y * (1.0 + scale) + shift


def _pick_rows(i, n_lat_tiles, x_ref, xc_ref, rows=slice(None)):
    return jnp.where(i < n_lat_tiles, x_ref[rows], xc_ref[rows])


def _stream_specs(rw, tm, split):
    if not split:
        return [pl.BlockSpec((tm, D), lambda i, *_: (i, 0))]
    nl = rw.lat_rows // tm
    return [pl.BlockSpec((tm, D), lambda i, *_: (jnp.minimum(i, nl - 1), 0)),
            pl.BlockSpec((tm, D), lambda i, *_: (jnp.maximum(i - nl, 0), 0))]


def _inproj_kernel(x_ref, *rest, has_extra, n_lat_tiles):
    if n_lat_tiles is not None:
        xc_ref, rest = rest[0], rest[1:]
    g_ref, sh_ref, sc_ref, w_ref = rest[:4]
    if has_extra:
        w2_ref, o_ref, o2_ref, h_ref = rest[4:]
    else:
        o_ref, h_ref = rest[4:]

    @pl.when(pl.program_id(1) == 0)
    def _():
        x = x_ref[...] if n_lat_tiles is None else _pick_rows(pl.program_id(0), n_lat_tiles, x_ref, xc_ref)
        h = _bf(_norm_mod(x, g_ref[...], sh_ref[0], sc_ref[0]))
        h_ref[...] = h
        if has_extra:
            o2_ref[...] = _dot(h, w2_ref[...])

    o_ref[...] = _bf(_dot(h_ref[...], w_ref[...]))


def _inproj(rw, x, gain, mod3, layer, w, w_extra=None, n_cols=None):
    n = w.shape[1] if n_cols is None else n_cols
    tm = TM_IN
    tn = max(t for t in range(256, TN_IN_MAX + 1, 256) if n % t == 0)
    has_extra = w_extra is not None
    split = isinstance(x, tuple)
    xs = list(x) if split else [x]
    in_specs = _stream_specs(rw, tm, split) + [
                pl.BlockSpec((1, D), lambda i, j: (0, 0)),
                _mod_spec(rw, layer, 0, tm),
                _mod_spec(rw, layer, 1, tm),
                pl.BlockSpec((D, tn), lambda i, j: (0, j))]
    out_specs = [pl.BlockSpec((tm, tn), lambda i, j: (i, j))]
    out_shape = [jax.ShapeDtypeStruct((rw.rows, n), BF16)]
    args = xs + [gain.reshape(1, D), mod3, mod3, w]
    if has_extra:
        ne = w_extra.shape[1]
        in_specs.append(pl.BlockSpec((D, ne), lambda i, j: (0, 0)))
        out_specs.append(pl.BlockSpec((tm, ne), lambda i, j: (i, 0)))
        out_shape.append(jax.ShapeDtypeStruct((rw.rows, ne), F32))
        args.append(w_extra)
    res = pl.pallas_call(
        functools.partial(_inproj_kernel, has_extra=has_extra, n_lat_tiles=rw.lat_rows // tm if split else None),
        grid=(rw.rows // tm, n // tn),
        in_specs=in_specs, out_specs=out_specs, out_shape=out_shape,
        scratch_shapes=[pltpu.VMEM((tm, D), BF16)],
        compiler_params=_cparams(("parallel", "arbitrary")),
    )(*args)
    return res if has_extra else res[0]


def _mlp_kernel(x_ref, g_ref, sh_ref, sc_ref, gt_ref, w1_ref, w2_ref, go_ref, o_ref, h_ref, acc_ref,
                *, final_norm):
    k = pl.program_id(1)

    @pl.when(k == 0)
    def _():
        h_ref[...] = _bf(_norm_mod(x_ref[...], g_ref[...], sh_ref[0], sc_ref[0]))
        acc_ref[...] = jnp.zeros_like(acc_ref)

    a = jnp.square(jnp.maximum(_dot(h_ref[...], _bf(w1_ref[...])), 0.0))
    acc_ref[...] += _dot(_bf(a), _bf(w2_ref[...]))

    @pl.when(k == pl.num_programs(1) - 1)
    def _():
        y = x_ref[...] + gt_ref[0] * acc_ref[...]
        if final_norm:
            y = y * lax.rsqrt(jnp.mean(y * y, axis=-1, keepdims=True) + EPS) * go_ref[...]
        o_ref[...] = y


def _mlp(rw, x, n_rows, gain, mod3, layer, w1, w2, out_gain, final_norm):
    tm, ck = TM_IN, FF_CHUNK
    return pl.pallas_call(
        functools.partial(_mlp_kernel, final_norm=final_norm),
        grid=(n_rows // tm, D_FF // ck),
        in_specs=[pl.BlockSpec((tm, D), lambda i, k: (i, 0)),
                  pl.BlockSpec((1, D), lambda i, k: (0, 0)),
                  _mod_spec(rw, layer, 3, tm),
                  _mod_spec(rw, layer, 4, tm),
                  _mod_spec(rw, layer, 5, tm),
                  pl.BlockSpec((None, D, ck), lambda i, k: (layer, 0, k)),
                  pl.BlockSpec((None, ck, D), lambda i, k: (layer, k, 0)),
                  pl.BlockSpec((1, D), lambda i, k: (0, 0))],
        out_specs=pl.BlockSpec((tm, D), lambda i, k: (i, 0)),
        out_shape=jax.ShapeDtypeStruct((n_rows, D), F32),
        scratch_shapes=[pltpu.VMEM((tm, D), BF16), pltpu.VMEM((tm, D), F32)],
        compiler_params=_cparams(("parallel", "arbitrary"), MLP_VMEM_LIMIT),
    )(x, gain.reshape(1, D), mod3, mod3, mod3, w1, w2, out_gain.reshape(1, D))


OUT_SUB = 256


def _row_blocks(n_rows):
    return [slice(r, r + OUT_SUB) for r in range(0, n_rows, OUT_SUB)]


def _gdn_out_kernel(x_ref, *rest, n_lat_tiles):
    if n_lat_tiles is not None:
        xc_ref, rest = rest[0], rest[1:]
    gt_ref, o0_ref, o1_ref, z_ref, ng_ref, w_ref, out_ref, a_ref = rest
    ng = ng_ref[...]
    for rows in _row_blocks(x_ref.shape[0]):
        for h in range(GDN_V_HEADS):
            sl = slice(h * GDN_HD, (h + 1) * GDN_HD)
            o = o0_ref[0, rows, sl].astype(F32) + o1_ref[0, rows, sl].astype(F32)
            y = o * lax.rsqrt(jnp.mean(o * o, axis=-1, keepdims=True) + EPS) * ng
            a_ref[rows, sl] = _bf(y * _silu(z_ref[rows, sl].astype(F32)))
        x = x_ref[rows] if n_lat_tiles is None else _pick_rows(pl.program_id(0), n_lat_tiles, x_ref, xc_ref, rows)
        out_ref[rows] = x + gt_ref[0] * _dot(a_ref[rows], w_ref[...])


def _ret_out_kernel(x_ref, gt_ref, o0_ref, o1_ref, z_ref, ng_ref, w_ref, out_ref, a_ref):
    for rows in _row_blocks(x_ref.shape[0]):
        for h in range(RET_HEADS):
            sl = slice(h * RET_DV, (h + 1) * RET_DV)
            o = o0_ref[0, rows, sl].astype(F32) + o1_ref[0, rows, sl].astype(F32)
            mu = jnp.mean(o, axis=-1, keepdims=True)
            oc = o - mu
            var = jnp.mean(oc * oc, axis=-1, keepdims=True)
            y = oc * lax.rsqrt(var + EPS) * ng_ref[:, sl]
            a_ref[rows, sl] = _bf(y * _silu(z_ref[rows, sl].astype(F32)))
        out_ref[rows] = x_ref[rows] + gt_ref[0] * _dot(a_ref[rows], w_ref[...])


def _lru_out_kernel(x_ref, gt_ref, h0_ref, h1_ref, gb_ref, w_ref, out_ref):
    for rows in _row_blocks(x_ref.shape[0]):
        a = (h0_ref[rows] + h1_ref[rows]) * _gelu_tanh(gb_ref[rows].astype(F32))
        out_ref[rows] = x_ref[rows] + gt_ref[0] * _dot(_bf(a), w_ref[...])


def _swa_out_kernel(x_ref, gt_ref, o_ref, w_ref, out_ref):
    out_ref[...] = x_ref[...] + gt_ref[0] * _dot(o_ref[...], w_ref[...])


def _outproj_call(kern, rw, n_rows, layer, x, mod3, extra_specs, extra_args, w, scratch):
    tm = TM_OUT
    k = w.shape[0]
    split = isinstance(x, tuple)
    xs = list(x) if split else [x]
    return pl.pallas_call(
        kern,
        grid=(n_rows // tm,),
        in_specs=_stream_specs(rw, tm, split) + [_mod_spec(rw, layer, 2, tm)] + extra_specs
                 + [pl.BlockSpec((k, D), lambda i: (0, 0))],
        out_specs=pl.BlockSpec((tm, D), lambda i: (i, 0)),
        out_shape=jax.ShapeDtypeStruct((n_rows, D), F32),
        scratch_shapes=scratch,
        compiler_params=_cparams(("parallel",)),
    )(*xs, mod3, *extra_args, w)


def _two_dir_out(kern, rw, layer, x, mod3, o, proj, z_block, norm_g, w):
    tm, k = TM_OUT, w.shape[0]
    ng = norm_g.reshape(1, -1)
    specs = [pl.BlockSpec((1, tm, k), lambda i: (0, i, 0)),
             pl.BlockSpec((1, tm, k), lambda i: (1, i, 0)),
             pl.BlockSpec((tm, k), lambda i: (i, z_block)),
             pl.BlockSpec(ng.shape, lambda i: (0, 0))]
    return _outproj_call(kern, rw, rw.rows, layer, x, mod3, specs, [o, o, proj, ng], w,
                         [pltpu.VMEM((tm, k), BF16)])


def _shift_matrices(t):
    delta = lax.broadcasted_iota(jnp.int32, (t, t), 0) - lax.broadcasted_iota(jnp.int32, (t, t), 1)
    return {k: _bf(jnp.where(jnp.logical_or(delta == k, delta == k - t), 1.0, 0.0)) for k in (2, 1, t - 1)}


def _conv_core(x_ref, p_ref, n_ref, w_ref, first, last, shifts=None):
    t = x_ref.shape[0]
    xb = x_ref[...]
    x = xb.astype(F32)
    prev = p_ref[...].astype(F32)[HALO - 8:] * jnp.where(first, 0.0, 1.0)
    nxt = n_ref[...].astype(F32)[:8] * jnp.where(last, 0.0, 1.0)
    if shifts is not None:
        assert xb.dtype == BF16
        r2, r1, rp = _dot(shifts[2], xb), _dot(shifts[1], xb), _dot(shifts[t - 1], xb)
    else:
        r2, r1, rp = pltpu.roll(x, 2, 0), pltpu.roll(x, 1, 0), pltpu.roll(x, t - 1, 0)
    row8 = lax.broadcasted_iota(jnp.int32, prev.shape, 0)
    f2 = jnp.where(row8 < 2, pltpu.roll(prev, 2, 0), r2[:8])
    f1 = jnp.where(row8 < 1, pltpu.roll(prev, 1, 0), r1[:8])
    l1 = jnp.where(row8 == 7, pltpu.roll(nxt, 7, 0), rp[t - 8:])
    xm2 = jnp.concatenate([f2, r2[8:]], axis=0)
    xm1 = jnp.concatenate([f1, r1[8:]], axis=0)
    xp1 = jnp.concatenate([rp[:t - 8], l1], axis=0)
    w = w_ref[...]
    return w[0:1] * xm2 + w[1:2] * xm1 + w[2:3] * x + w[3:4] * xp1


def _seg_flags(rw, i):
    per = rw.n_lat // TM_CONV
    is_ctx = i >= rw.lat_rows // TM_CONV
    assert rw.n_ctx == TM_CONV
    first = jnp.logical_or(is_ctx, i % per == 0)
    last = jnp.logical_or(is_ctx, i % per == per - 1)
    return first, last


def _gdn_conv_kernel(x_ref, p_ref, n_ref, w_ref, ab_ref, alog_ref, dtb_ref, o_ref, g_ref, b_ref, *, rw):
    i, j = pl.program_id(0), pl.program_id(1)
    first, last = _seg_flags(rw, i)
    s = _silu(_conv_core(x_ref, p_ref, n_ref, w_ref, first, last, _shift_matrices(x_ref.shape[0])))
    width = s.shape[1]
    qk_blocks = 2 * GDN_KEY // width

    @pl.when(j < qk_blocks)
    def _():
        for h in range(width // GDN_HD):
            sl = slice(h * GDN_HD, (h + 1) * GDN_HD)
            sh = s[:, sl]
            is_q = j * width + h * GDN_HD < GDN_KEY
            scale = jnp.where(is_q, GDN_HD ** -0.5, 1.0)
            o_ref[:, sl] = sh * (lax.rsqrt(jnp.sum(sh * sh, axis=-1, keepdims=True) + EPS) * scale)

    @pl.when(j >= qk_blocks)
    def _():
        o_ref[...] = s

    @pl.when(j == 0)
    def _():
        ab = ab_ref[...]
        g_ref[...] = -jnp.exp(alog_ref[...]) * _softplus(ab + dtb_ref[...])
        b_ref[...] = jax.nn.sigmoid(ab)


def _conv_specs(rw, cb):
    th = TM_CONV // HALO
    nh = rw.rows // HALO
    return [pl.BlockSpec((TM_CONV, cb), lambda i, j: (i, j)),
            pl.BlockSpec((HALO, cb), lambda i, j: (jnp.maximum(i * th - 1, 0), j)),
            pl.BlockSpec((HALO, cb), lambda i, j: (jnp.minimum((i + 1) * th, nh - 1), j)),
            pl.BlockSpec((4, cb), lambda i, j: (0, j))]


def _dir_masks(d, n):
    ii = lax.broadcasted_iota(jnp.int32, (n, n), 0)
    jj = lax.broadcasted_iota(jnp.int32, (n, n), 1)
    t = (ii - jj) * jnp.where(d == 0, 1, -1)
    return t, t >= 0, t > 0


def _gdn_core_kernel(qkv_ref, g_ref, b_ref, o_ref, s_ref):
    d, s = pl.program_id(1), pl.program_id(2)

    @pl.when(s == 0)
    def _():
        s_ref[...] = jnp.zeros_like(s_ref)

    c, hd = CHUNK, GDN_HD
    rep = GDN_V_HEADS // GDN_QK_HEADS
    heads = range(GDN_V_HEADS)
    n_sub = qkv_ref.shape[0] // c
    subs = range(n_sub)
    offs = [pl.multiple_of(jnp.where(d == 0, i, n_sub - 1 - i) * c, c) for i in subs]
    t, incl, strict = _dir_masks(d, c)
    cum = _bf(jnp.where(incl, 1.0, 0.0))
    cum_t = _bf(jnp.where(t <= 0, 1.0, 0.0))
    g = [g_ref[0, pl.ds(offs[i], c), :] for i in subs]
    beta = [b_ref[0, pl.ds(offs[i], c), :] for i in subs]
    parts = [_split3(g[i]) for i in subs]
    gc = [sum(_dot(cum, p) for p in parts[i]) for i in subs]
    gr = [sum(_dot_tn(p, cum_t) for p in parts[i]) for i in subs]
    gtot = [jnp.sum(g[i], axis=0, keepdims=True) for i in subs]

    def rows(i, lo):
        return qkv_ref[pl.ds(offs[i], c), lo:lo + hd]

    q_l = [[rows(i, hq * hd) for hq in range(GDN_QK_HEADS)] for i in subs]
    k_l = [[rows(i, GDN_KEY + hq * hd) for hq in range(GDN_QK_HEADS)] for i in subs]
    gram = [[_dot_nt(_bf(jnp.concatenate([k, q], axis=0)), _bf(k)) for q, k in zip(q_l[i], k_l[i])]
            for i in subs]
    zeros_cc = jnp.zeros((c, 2 * c), BF16)
    zeros_rhs = jnp.zeros((c, 2 * hd), BF16)
    lane_cc = lax.broadcasted_iota(jnp.int32, (c, 2 * c), 1)
    row_cc = lax.broadcasted_iota(jnp.int32, (c, 2 * c), 0)
    eye_hi = _bf(jnp.where(lane_cc == row_cc + c, 1.0, 0.0))
    is_hi = lane_cc >= c

    def prepare(chains):
        ids = range(len(chains))
        gcol = [gc[i][:, h:h + 1] for i, h in chains]
        bcol = [beta[i][:, h:h + 1] for i, h in chains]
        dec = [jnp.where(incl, jnp.exp(jnp.where(incl, gcol[n] - gr[i][h:h + 1, :], 0.0)), 0.0)
               for n, (i, h) in enumerate(chains)]
        eg = [jnp.exp(x) for x in gcol]
        m = [jnp.where(strict, -(gram[i][h // rep][:c] * dec[n] * bcol[n]), 0.0) for n, (i, h) in enumerate(chains)]
        rhs = [jnp.concatenate([k_l[i][h // rep] * (bcol[n] * eg[n]), rows(i, 2 * GDN_KEY + h * hd) * bcol[n]],
                               axis=1) for n, (i, h) in enumerate(chains)]

        m_hi = [_bf(x) for x in m]
        cb = [jnp.concatenate([m_hi[n], jnp.zeros((c, c), BF16)], axis=1) + eye_hi for n in ids]
        for _ in range(6):
            r = [_dot(cb[n], jnp.concatenate([cb[n], zeros_cc], axis=0)) for n in ids]
            cb = [_bf(r[n]) + jnp.where(is_hi, cb[n], jnp.zeros_like(cb[n])) for n in ids]
        nb = [cb[n] - eye_hi for n in ids]

        def apply_n(y):
            return [_dot(nb[n], jnp.concatenate([zeros_rhs, _bf(y[n])], axis=0)) for n in ids]

        nr = apply_n(rhs)
        x = [rhs[n] + nr[n] for n in ids]
        resid = [_dot(m_hi[n], _bf(x[n])) - nr[n] for n in ids]
        nr = apply_n(resid)
        x = [x[n] + (resid[n] + nr[n]) for n in ids]
        qk = [_bf(gram[i][h // rep][c:] * dec[n]) for n, (i, h) in enumerate(chains)]
        gt = [gtot[i][:, h:h + 1] for i, h in chains]
        kd = [_bf(k_l[i][h // rep] * jnp.exp(gt[n] - gcol[n])) for n, (i, h) in enumerate(chains)]
        wq = [_bf(jnp.concatenate([x[n][:, :hd], q_l[i][h // rep] * eg[n]], axis=0))
              for n, (i, h) in enumerate(chains)]
        decay = [jnp.exp(x) for x in gt]
        return x, qk, kd, wq, decay

    per_group = GDN_GROUP // GDN_V_HEADS
    st = [s_ref[h] for h in heads]
    for first in range(0, n_sub, per_group):
        chunk_ids = range(first, first + per_group)
        x, qk, kd, wq, decay = prepare([(i, h) for i in chunk_ids for h in heads])
        for j, i in enumerate(chunk_ids):
            base = j * GDN_V_HEADS
            r = [_dot(wq[base + h], _bf(st[h])) for h in heads]
            unb = [_bf(x[base + h][:, hd:] - r[h][:c]) for h in heads]
            for h in heads:
                o_ref[0, pl.ds(offs[i], c), h * hd:(h + 1) * hd] = _bf(r[h][c:] + _dot(qk[base + h], unb[h]))
            st = [st[h] * decay[base + h] + _dot_tn(kd[base + h], unb[h]) for h in heads]
    for h in heads:
        s_ref[h] = st[h]


def _gdn_core(rw, qkv, g_dir, b_dir):
    blk = GDN_BLOCK
    rb = lambda b, d, s: rw.chunk_block(b, d, s, blk)
    return pl.pallas_call(
        _gdn_core_kernel,
        grid=(rw.batch, 2, (rw.n_lat + rw.n_ctx) // blk),
        in_specs=[pl.BlockSpec((blk, GDN_CONV), lambda b, d, s: (rb(b, d, s), 0)),
                  pl.BlockSpec((1, blk, GDN_V_HEADS), lambda b, d, s: (d, rb(b, d, s), 0)),
                  pl.BlockSpec((1, blk, GDN_V_HEADS), lambda b, d, s: (d, rb(b, d, s), 0))],
        out_specs=pl.BlockSpec((1, blk, GDN_VAL), lambda b, d, s: (d, rb(b, d, s), 0)),
        out_shape=jax.ShapeDtypeStruct((2, rw.rows, GDN_VAL), BF16),
        scratch_shapes=[pltpu.VMEM((GDN_V_HEADS, GDN_HD, GDN_HD), F32)],
        compiler_params=_cparams(("parallel", "parallel", "arbitrary")),
    )(qkv, g_dir, b_dir)


def _rope_half(x, cos, sin_signed):
    return x * cos + pltpu.roll(x, x.shape[1] // 2, 1) * sin_signed


def _ret_core_kernel(qk_ref, v_ref, cos_ref, sin_ref, dl_ref, o_ref, s_ref, dec_ref, *, ctx_steps):
    d, s = pl.program_id(1), pl.program_id(2)
    c = qk_ref.shape[0]
    heads = range(RET_HEADS)
    is_lat = s >= ctx_steps
    cos = jnp.where(is_lat, cos_ref[...], 1.0)
    sin = jnp.where(is_lat, sin_ref[...], 0.0)
    t, incl, _ = _dir_masks(d, c)
    tf = t.astype(F32)
    pos = lax.broadcasted_iota(jnp.int32, (c, 1), 0)
    ridx = jnp.where(d == 0, pos + 1, c - pos).astype(F32)
    log_gamma = -_softplus(-dl_ref[0])

    def rope(x):
        hw = RET_DK // 2
        return jnp.concatenate([_rope_half(x[:, :hw], cos[:, :hw], sin[:, :hw]),
                                _rope_half(x[:, hw:], cos[:, hw:], sin[:, hw:])], axis=1)

    lg = [log_gamma[:, h:h + 1] for h in heads]

    @pl.when(s == 0)
    def _():
        s_ref[...] = jnp.zeros_like(s_ref)
        for h in heads:
            dec_ref[h] = jnp.where(incl, jnp.exp(jnp.where(incl, lg[h] * tf, 0.0)), 0.0)

    q = [rope(qk_ref[:, h * RET_DK:(h + 1) * RET_DK].astype(F32)) * (RET_DK ** -0.5) for h in heads]
    k = [rope(qk_ref[:, RET_KEY + h * RET_DK:RET_KEY + (h + 1) * RET_DK].astype(F32)) for h in heads]
    vb = [v_ref[:, h * RET_DV:(h + 1) * RET_DV] for h in heads]
    gcum = [lg[h] * ridx for h in heads]
    qk = [_bf(_dot_nt(_bf(q[h]), _bf(k[h])) * dec_ref[h]) for h in heads]
    st = [s_ref[h] for h in heads]
    for h in heads:
        o_ref[0, :, h * RET_DV:(h + 1) * RET_DV] = _bf(
            _dot(_bf(q[h] * jnp.exp(gcum[h])), _bf(st[h])) + _dot(qk[h], vb[h]))
    gtot = [lg[h] * float(c) for h in heads]
    kd = [_bf(k[h] * jnp.exp(gtot[h] - gcum[h])) for h in heads]
    for h in heads:
        s_ref[h] = st[h] * jnp.exp(gtot[h]) + _dot_tn(kd[h], vb[h])


def _ret_core(rw, proj, cos_tab, sin_tab, decay_logit):
    c = RET_CHUNK
    rb = lambda b, d, s: rw.chunk_block(b, d, s, c)
    dl = jnp.zeros((2, 1, 128), F32).at[:, 0, :RET_HEADS].set(decay_logit)
    return pl.pallas_call(
        functools.partial(_ret_core_kernel, ctx_steps=rw.n_ctx // c),
        grid=(rw.batch, 2, (rw.n_lat + rw.n_ctx) // c),
        in_specs=[pl.BlockSpec((c, 2 * RET_KEY), lambda b, d, s: (rb(b, d, s), 0)),
                  pl.BlockSpec((c, RET_VAL), lambda b, d, s: (rb(b, d, s), 1)),
                  pl.BlockSpec((c, RET_DK), lambda b, d, s: (rw.lat_chunk(d, s, c), 0)),
                  pl.BlockSpec((c, RET_DK), lambda b, d, s: (rw.lat_chunk(d, s, c), 0)),
                  pl.BlockSpec((1, 1, 128), lambda b, d, s: (d, 0, 0))],
        out_specs=pl.BlockSpec((1, c, RET_VAL), lambda b, d, s: (d, rb(b, d, s), 0)),
        out_shape=jax.ShapeDtypeStruct((2, rw.rows, RET_VAL), BF16),
        scratch_shapes=[pltpu.VMEM((RET_HEADS, RET_DK, RET_DV), F32), pltpu.VMEM((RET_HEADS, c, c), F32)],
        compiler_params=_cparams(("parallel", "parallel", "arbitrary")),
    )(proj, proj, cos_tab, sin_tab, dl)


def _lru_scan_kernel(x_ref, p_ref, n_ref, cw_ref, cb_ref, wg_ref, bg_ref, lam_ref, h_ref, carry_ref, *, rw, rev):
    @pl.when(pl.program_id(1) == 0)
    def _():
        carry_ref[...] = jnp.zeros_like(carry_ref)

    t = x_ref.shape[0]
    tile = rw.chunk_block(pl.program_id(0), 1 if rev else 0, pl.program_id(1), t)
    first, last = _seg_flags(rw, tile)
    xs = _conv_core(x_ref, p_ref, n_ref, cw_ref, first, last) + cb_ref[...]
    row8 = lax.broadcasted_iota(jnp.int32, (t // 8, 8, LRU_BW), 1)
    sp = _softplus(-lam_ref[...])
    for n in range(LRU_BLOCKS):
        sl = slice(n * LRU_BW, (n + 1) * LRU_BW)
        xn = xs[:, sl]
        gates = jax.nn.sigmoid(_dot(_bf(xn), wg_ref[n]) + bg_ref[n])
        log_a = -LRU_C * gates[:, :LRU_BW] * sp[:, sl]
        a = jnp.exp(log_a)
        y = -jnp.tanh(log_a) * (a * a + 1.0)
        b = jnp.where(y > 0.0, y * lax.rsqrt(y), 0.0) * gates[:, LRU_BW:] * xn
        groups = t // 8
        a3 = a.reshape(groups, 8, LRU_BW)
        b3 = b.reshape(groups, 8, LRU_BW)
        for k in (1, 2, 4):
            if rev:
                keep = row8 < 8 - k
                shift = 8 - k
            else:
                keep = row8 >= k
                shift = k
            sa = jnp.where(keep, pltpu.roll(a3, shift, 1), 1.0)
            sb = jnp.where(keep, pltpu.roll(b3, shift, 1), 0.0)
            b3 = b3 + a3 * sb
            a3 = a3 * sa
        carry = carry_ref[:, sl]
        for gi in (reversed(range(groups)) if rev else range(groups)):
            hg = a3[gi] * carry + b3[gi]
            carry = hg[0:1] if rev else hg[7:8]
            h_ref[gi * 8:(gi + 1) * 8, sl] = hg
        carry_ref[:, sl] = carry


def _lru_scan(rw, proj, conv_w, conv_b, w_gate, b_gate, lam, rev):
    t = TM_CONV
    th = t // HALO
    nh = rw.rows // HALO
    d = 1 if rev else 0
    rb = lambda b, s: rw.chunk_block(b, d, s, t)
    return pl.pallas_call(
        functools.partial(_lru_scan_kernel, rw=rw, rev=rev),
        grid=(rw.batch, (rw.n_lat + rw.n_ctx) // t),
        in_specs=[pl.BlockSpec((t, LRU_W), lambda b, s: (rb(b, s), 0)),
                  pl.BlockSpec((HALO, LRU_W), lambda b, s: (jnp.maximum(rb(b, s) * th - 1, 0), 0)),
                  pl.BlockSpec((HALO, LRU_W), lambda b, s: (jnp.minimum((rb(b, s) + 1) * th, nh - 1), 0)),
                  pl.BlockSpec((4, LRU_W), lambda b, s: (0, 0)),
                  pl.BlockSpec((1, LRU_W), lambda b, s: (0, 0)),
                  pl.BlockSpec((LRU_BLOCKS, LRU_BW, 2 * LRU_BW), lambda b, s: (0, 0, 0)),
                  pl.BlockSpec((LRU_BLOCKS, 1, 2 * LRU_BW), lambda b, s: (0, 0, 0)),
                  pl.BlockSpec((1, LRU_W), lambda b, s: (0, 0))],
        out_specs=pl.BlockSpec((t, LRU_W), lambda b, s: (rb(b, s), 0)),
        out_shape=jax.ShapeDtypeStruct((rw.rows, LRU_W), F32),
        scratch_shapes=[pltpu.VMEM((1, LRU_W), F32)],
        compiler_params=_cparams(("parallel", "arbitrary")),
    )(proj, proj, proj, conv_w, conv_b.reshape(1, LRU_W), _bf(w_gate),
      b_gate.reshape(LRU_BLOCKS, 1, 2 * LRU_BW), lam.reshape(1, LRU_W))


def _rope16(x, tab):
    cos, s1, s2 = tab[:, :128], tab[:, 128:256], tab[:, 256:]
    cols = []
    for cidx in range(x.shape[1] // 128):
        xc = x[:, cidx * 128:(cidx + 1) * 128]
        cols.append(xc * cos + pltpu.roll(xc, 112, 1) * s1 + pltpu.roll(xc, 16, 1) * s2)
    return cols[0] if len(cols) == 1 else jnp.concatenate(cols, axis=1)


def _swa_kernel(sink_ref, q_ref, kvp_ref, kvc_ref, kvn_ref, kvx_ref, tp_ref, tc_ref, tn_ref, o_ref):
    qb = pl.program_id(1)
    nb = pl.num_programs(1)
    blk = SWA_BLOCK
    n_loc = 3 * blk
    n_keys = n_loc + kvx_ref.shape[0]
    tab_c = tc_ref[...]
    keys = lambda ref: ref[:, :SWA_KVD].astype(F32)
    q = _rope16(q_ref[...].astype(F32), tab_c) * (SWA_HD ** -0.5 * LOG2E)
    k_all = jnp.concatenate([_rope16(keys(kvp_ref), tp_ref[...]), _rope16(keys(kvc_ref), tab_c),
                             _rope16(keys(kvn_ref), tn_ref[...]), keys(kvx_ref)], axis=0)
    v_all = jnp.concatenate([ref[:, SWA_KVD:] for ref in (kvp_ref, kvc_ref, kvn_ref, kvx_ref)], axis=0).astype(F32)
    assert SWA_WINDOW >= blk - 1
    qi = lax.broadcasted_iota(jnp.int32, (blk, blk), 0)
    kj = lax.broadcasted_iota(jnp.int32, (blk, blk), 1)
    valid_prev = jnp.logical_and(kj - blk - qi >= -SWA_WINDOW, qb > 0)
    valid_next = jnp.logical_and(kj + blk - qi <= SWA_WINDOW, qb < nb - 1)
    lane = lax.broadcasted_iota(jnp.int32, (n_keys, 128), 1)
    low = lane < SWA_HD
    out_low = lax.broadcasted_iota(jnp.int32, (blk, 128), 1) < SWA_HD

    def expo(scores, sink):
        sc = jnp.concatenate([jnp.where(valid_prev, scores[:, :blk], -jnp.inf), scores[:, blk:2 * blk],
                              jnp.where(valid_next, scores[:, 2 * blk:n_loc], -jnp.inf), scores[:, n_loc:]], axis=1)
        mx = jnp.maximum(jnp.max(sc, axis=-1, keepdims=True), sink)
        return _bf(jnp.exp2(sc - mx)), jnp.exp2(sink - mx)

    group = SWA_QH // SWA_KVH
    for g in range(SWA_KVH):
        col = slice((g // 2) * 128, (g // 2 + 1) * 128)
        native_low = g % 2 == 0
        sel = low if native_low else jnp.logical_not(low)
        k_nat = jnp.where(sel, k_all[:, col], 0.0)
        v_nat = jnp.where(sel, v_all[:, col], 0.0)
        k_oth = pltpu.roll(k_nat, SWA_HD, 1)
        v_oth = pltpu.roll(v_nat, SWA_HD, 1)
        k_lo, k_hi = (k_nat, k_oth) if native_low else (k_oth, k_nat)
        v_lo, v_hi = (v_nat, v_oth) if native_low else (v_oth, v_nat)
        v_lo = jnp.where(lane == SWA_HD, 1.0, v_lo)
        v_hi = jnp.where(lane == 0, 1.0, v_hi)
        k_lo, k_hi, v_lo, v_hi = _bf(k_lo), _bf(k_hi), _bf(v_lo), _bf(v_hi)
        for cidx in range(group // 2):
            qc = (g * group) // 2 + cidx
            qv = _bf(q[:, qc * 128:(qc + 1) * 128])
            e_lo, sink_lo = expo(_dot_nt(qv, k_lo), sink_ref[2 * qc] * LOG2E)
            e_hi, sink_hi = expo(_dot_nt(qv, k_hi), sink_ref[2 * qc + 1] * LOG2E)
            a_lo = _dot(e_lo, v_lo)
            a_hi = _dot(e_hi, v_hi)
            inv_lo = 1.0 / (a_lo[:, SWA_HD:SWA_HD + 1] + sink_lo)
            inv_hi = 1.0 / (a_hi[:, 0:1] + sink_hi)
            o_ref[:, qc * 128:(qc + 1) * 128] = _bf(jnp.where(out_low, a_lo * inv_lo, a_hi * inv_hi))


def _swa_core(rw, proj, sink, tab):
    blk = SWA_BLOCK
    nb = rw.n_lat // blk
    kvcol = SWA_QD // (2 * SWA_KVD)
    ctx_blk = lambda b: rw.lat_rows // rw.n_ctx + b
    prev = lambda q: jnp.maximum(q - 1, 0)
    nxt = lambda q: jnp.minimum(q + 1, nb - 1)
    kv = lambda f: pl.BlockSpec((blk, 2 * SWA_KVD), lambda b, q: (b * nb + f(q), kvcol))
    same = lambda q: q
    return pl.pallas_call(
        _swa_kernel,
        grid=(rw.batch, nb),
        in_specs=[pl.BlockSpec(memory_space=pltpu.SMEM),
                  pl.BlockSpec((blk, SWA_QD), lambda b, q: (b * nb + q, 0)),
                  kv(prev), kv(same), kv(nxt),
                  pl.BlockSpec((rw.n_ctx, 2 * SWA_KVD), lambda b, q: (ctx_blk(b), kvcol)),
                  pl.BlockSpec((blk, 384), lambda b, q: (prev(q), 0)),
                  pl.BlockSpec((blk, 384), lambda b, q: (q, 0)),
                  pl.BlockSpec((blk, 384), lambda b, q: (nxt(q), 0))],
        out_specs=pl.BlockSpec((blk, SWA_QD), lambda b, q: (b * nb + q, 0)),
        out_shape=jax.ShapeDtypeStruct((rw.lat_rows, SWA_QD), BF16),
        compiler_params=_cparams(("parallel", "parallel")),
    )(sink, proj, proj, proj, proj, proj, tab, tab, tab)


def _axial_angles(n_lat, dh):
    t = jnp.arange(n_lat)
    row = (t // GRID_W).astype(F32)
    col = (t % GRID_W).astype(F32)
    nf = dh // 4
    inv = ROPE_THETA ** (-jnp.arange(nf, dtype=F32) / nf)
    return row[:, None] * inv, col[:, None] * inv


def _ret_tables(n_lat):
    ar, ac = _axial_angles(n_lat, RET_DK)
    cos = jnp.concatenate([jnp.cos(ar), jnp.cos(ar), jnp.cos(ac), jnp.cos(ac)], axis=1)
    sin = jnp.concatenate([-jnp.sin(ar), jnp.sin(ar), -jnp.sin(ac), jnp.sin(ac)], axis=1)
    return cos, sin


def _swa_table(n_lat):
    ar, ac = _axial_angles(n_lat, SWA_HD)
    z = jnp.zeros_like(ar)
    cos = jnp.concatenate([jnp.cos(ar), jnp.cos(ar), jnp.cos(ac), jnp.cos(ac)], axis=1)
    s1 = jnp.concatenate([-jnp.sin(ar), z, -jnp.sin(ac), z], axis=1)
    s2 = jnp.concatenate([z, jnp.sin(ar), z, jnp.sin(ac)], axis=1)
    return jnp.concatenate([jnp.tile(cos, (1, 2)), jnp.tile(s1, (1, 2)), jnp.tile(s2, (1, 2))], axis=1)


def _gdn_layer(rw, layer, x, mod3, gain, w_in, conv_w, a_log, dt_bias, norm_g, w_out):
    n_ab = w_in.shape[1] - GDN_MAIN
    w_ab = _bf(jnp.pad(w_in[:, GDN_MAIN:], ((0, 0), (0, 128 - n_ab))))
    proj, ab = _inproj(rw, x, gain, mod3, layer, _bf(w_in), w_ab, n_cols=GDN_MAIN)
    hv = GDN_V_HEADS
    alog_row = jnp.zeros((1, 128), F32).at[0, 0:hv].set(a_log[0]).at[0, 2 * hv:3 * hv].set(a_log[1])
    dtb_row = jnp.zeros((1, 128), F32).at[0, 0:hv].set(dt_bias[0]).at[0, 2 * hv:3 * hv].set(dt_bias[1])
    cb = 2048
    row_spec = pl.BlockSpec((TM_CONV, 128), lambda i, j: (i, 0))
    vec_spec = pl.BlockSpec((1, 128), lambda i, j: (0, 0))
    qkv, g_all, b_all = pl.pallas_call(
        functools.partial(_gdn_conv_kernel, rw=rw),
        grid=(rw.rows // TM_CONV, GDN_CONV // cb),
        in_specs=_conv_specs(rw, cb) + [row_spec, vec_spec, vec_spec],
        out_specs=[pl.BlockSpec((TM_CONV, cb), lambda i, j: (i, j)), row_spec, row_spec],
        out_shape=[jax.ShapeDtypeStruct((rw.rows, GDN_CONV), F32),
                   jax.ShapeDtypeStruct((rw.rows, 128), F32),
                   jax.ShapeDtypeStruct((rw.rows, 128), F32)],
        compiler_params=_cparams(("parallel", "arbitrary")),
    )(proj, proj, proj, conv_w, ab, alog_row, dtb_row)
    g_dir = jnp.stack([g_all[:, 0:hv], g_all[:, 2 * hv:3 * hv]])
    b_dir = jnp.stack([b_all[:, hv:2 * hv], b_all[:, 3 * hv:4 * hv]])
    o = _gdn_core(rw, qkv, g_dir, b_dir)
    kern = functools.partial(_gdn_out_kernel, n_lat_tiles=rw.lat_rows // TM_OUT if isinstance(x, tuple) else None)
    return _two_dir_out(kern, rw, layer, x, mod3, o, proj, GDN_CONV // GDN_VAL, norm_g, _bf(w_out))


def _ret_layer(rw, layer, x, mod3, gain, w_in, decay_logit, norm_g, w_out):
    proj = _inproj(rw, x, gain, mod3, layer, _bf(w_in))
    cos_tab, sin_tab = _ret_tables(rw.n_lat)
    o = _ret_core(rw, proj, cos_tab, sin_tab, decay_logit)
    return _two_dir_out(_ret_out_kernel, rw, layer, x, mod3, o, proj, (2 * RET_KEY + RET_VAL) // RET_VAL,
                        norm_g, _bf(w_out))


def _lru_layer(rw, layer, x, mod3, gain, w_in, conv_w, conv_b, w_gate, b_gate, lam, w_out):
    proj = _inproj(rw, x, gain, mod3, layer, _bf(w_in))
    h0 = _lru_scan(rw, proj, conv_w, conv_b, w_gate[0], b_gate[0], lam[0], rev=False)
    h1 = _lru_scan(rw, proj, conv_w, conv_b, w_gate[1], b_gate[1], lam[1], rev=True)
    tm = TM_OUT
    specs = [pl.BlockSpec((tm, LRU_W), lambda i: (i, 0)),
             pl.BlockSpec((tm, LRU_W), lambda i: (i, 0)),
             pl.BlockSpec((tm, LRU_W), lambda i: (i, 1))]
    return _outproj_call(_lru_out_kernel, rw, rw.rows, layer, x, mod3, specs, [h0, h1, proj], _bf(w_out), [])


def _swa_layer(rw, layer, x, mod3, gain, w_in, sink, w_out):
    proj = _inproj(rw, x, gain, mod3, layer, _bf(w_in))
    o = _swa_core(rw, proj, sink, _swa_table(rw.n_lat))
    specs = [pl.BlockSpec((TM_OUT, SWA_QD), lambda i: (i, 0))]
    return _outproj_call(_swa_out_kernel, rw, rw.lat_rows, layer, x, mod3, specs, [o], _bf(w_out), [])


def kernel(x, c, ctx, c_ctx, norm_mix_g, norm_ffn_g, w_mod, b_mod, w_ff1, w_ff2, norm_out_g, gdn_w_in, gdn_conv_w, gdn_a_log, gdn_dt_bias, gdn_norm_g, gdn_w_out, ret_w_in, ret_decay_logit, ret_norm_g, ret_w_out, lru_w_in, lru_conv_w, lru_conv_b, lru_w_gate, lru_b_gate, lru_lambda, lru_w_out, swa_w_in, swa_sink, swa_w_out):
    batch, n_lat, _ = x.shape
    n_ctx = ctx.shape[1]
    assert batch + 1 <= MOD_ROWS and n_lat % TM_IN == 0 and (batch * n_ctx) % TM_IN == 0
    assert w_mod.shape[0] == DEPTH and gdn_w_in.shape[0] == 1 and swa_w_in.shape[0] == 1
    rw = _Rows(batch, n_lat, n_ctx)
    xf = (x.reshape(rw.lat_rows, D), ctx.reshape(batch * n_ctx, D))
    cc = jnp.zeros((MOD_ROWS, D), F32).at[:batch].set(c).at[batch].set(c_ctx)
    mod3 = _adaln(cc, w_mod, b_mod)

    def mlp(layer, xin, n_rows, final):
        return _mlp(rw, xin, n_rows, norm_ffn_g[layer], mod3, layer, w_ff1, w_ff2,
                    norm_out_g, final)

    xf = _gdn_layer(rw, 0, xf, mod3, norm_mix_g[0], gdn_w_in[0], gdn_conv_w[0], gdn_a_log[0], gdn_dt_bias[0],
                    gdn_norm_g[0], gdn_w_out[0])
    xf = mlp(0, xf, rw.rows, False)
    xf = _ret_layer(rw, 1, xf, mod3, norm_mix_g[1], ret_w_in[0], ret_decay_logit[0], ret_norm_g[0], ret_w_out[0])
    xf = mlp(1, xf, rw.rows, False)
    xf = _lru_layer(rw, 2, xf, mod3, norm_mix_g[2], lru_w_in[0], lru_conv_w[0], lru_conv_b[0], lru_w_gate[0],
                    lru_b_gate[0], lru_lambda[0], lru_w_out[0])
    xf = mlp(2, xf, rw.rows, False)
    xl = _swa_layer(rw, 3, xf, mod3, norm_mix_g[3], swa_w_in[0], swa_sink[0], swa_w_out[0])
    out = mlp(3, xl, rw.lat_rows, True)
    return out.reshape(batch, n_lat, D)
```

```python
import functools
import math

import jax
import jax.numpy as jnp
from jax import lax
from jax.experimental import pallas as pl
from jax.experimental.pallas import tpu as pltpu

F32 = jnp.float32
BF16 = jnp.bfloat16

D = 1024
D_FF = 4 * D
EPS = 1e-6
LOG2E = math.log2(math.e)
DEPTH = 4
ROPE_THETA = 10000.0
GRID_W = 64
CHUNK = 64
MOD_ROWS = 16

GDN_QK_HEADS = 8
GDN_V_HEADS = 16
GDN_HD = 128
GDN_KEY = GDN_QK_HEADS * GDN_HD
GDN_VAL = GDN_V_HEADS * GDN_HD
GDN_CONV = 2 * GDN_KEY + GDN_VAL
GDN_MAIN = GDN_CONV + GDN_VAL

RET_HEADS = 4
RET_DK = 256
RET_DV = 512
RET_KEY = RET_HEADS * RET_DK
RET_VAL = RET_HEADS * RET_DV

LRU_W = 1280
LRU_BLOCKS = 10
LRU_BW = LRU_W // LRU_BLOCKS
LRU_C = 8.0

SWA_QH = 16
SWA_KVH = 4
SWA_HD = 64
SWA_BLOCK = 128
SWA_WINDOW = 128
SWA_QD = SWA_QH * SWA_HD
SWA_KVD = SWA_KVH * SWA_HD

TM_IN = 1024
TM_OUT = 512
TM_CONV = 256
HALO = 16
TN_IN_MAX = 1536
FF_CHUNK = 1024
RET_CHUNK = 256
GDN_BLOCK = 4 * CHUNK
GDN_GROUP = 32
VMEM_LIMIT = 48 * 1024 * 1024
MLP_VMEM_LIMIT = 56 * 1024 * 1024


def _dot(a, b):
    return jnp.dot(a, b, preferred_element_type=F32)


def _dot_nt(a, b):
    return lax.dot_general(a, b, (((1,), (1,)), ((), ())), preferred_element_type=F32)


def _dot_tn(a, b):
    return lax.dot_general(a, b, (((0,), (0,)), ((), ())), preferred_element_type=F32)


def _bf(x):
    return x.astype(BF16)


def _split3(x):
    hi = _bf(x)
    r = x - hi.astype(F32)
    mid = _bf(r)
    lo = _bf(r - mid.astype(F32))
    return hi, mid, lo


def _silu(x):
    return x * jax.nn.sigmoid(x)


def _softplus(x):
    return jnp.maximum(x, 0.0) + jnp.log1p(jnp.exp(-jnp.abs(x)))


def _gelu_tanh(x):
    cdf = 0.5 * (1.0 + jnp.tanh(math.sqrt(2.0 / math.pi) * (x + 0.044715 * (x * x * x))))
    return x * cdf


def _cparams(sem, vmem=None):
    return pltpu.CompilerParams(dimension_semantics=sem, vmem_limit_bytes=VMEM_LIMIT if vmem is None else vmem)


class _Rows:
    def __init__(self, batch, n_lat, n_ctx):
        self.batch, self.n_lat, self.n_ctx = batch, n_lat, n_ctx
        self.lat_rows = batch * n_lat
        self.rows = self.lat_rows + batch * n_ctx

    def mod_row(self, i, tm):
        return jnp.where(i < self.lat_rows // tm, i // (self.n_lat // tm), self.batch)

    def chunk_block(self, b, d, s, blk):
        nc, nl = self.n_ctx // blk, self.n_lat // blk
        c_ctx = jnp.where(d == 0, s, nc - 1 - s)
        c_lat = jnp.where(d == 0, s - nc, nl - 1 - (s - nc))
        return jnp.where(s < nc, self.lat_rows // blk + b * nc + c_ctx, b * nl + c_lat)

    def lat_chunk(self, d, s, blk):
        nc, nl = self.n_ctx // blk, self.n_lat // blk
        return jnp.where(s < nc, 0, jnp.where(d == 0, s - nc, nl - 1 - (s - nc)))


def _adaln_kernel(c_ref, w_ref, b_ref, o_ref):
    s = _silu(c_ref[...])
    w = w_ref[0]
    s_hi = _bf(s)
    s_lo = _bf(s - s_hi.astype(F32))
    w_hi = _bf(w)
    w_lo = _bf(w - w_hi.astype(F32))
    y = _dot(s_hi, w_hi) + (_dot(s_lo, w_hi) + _dot(s_hi, w_lo))
    o_ref[0] = y + b_ref[0]


def _adaln(cc, w_mod, b_mod):
    depth, _, n = w_mod.shape
    tn = 1024
    out = pl.pallas_call(
        _adaln_kernel,
        grid=(depth, n // tn),
        in_specs=[pl.BlockSpec((MOD_ROWS, D), lambda l, j: (0, 0)),
                  pl.BlockSpec((1, D, tn), lambda l, j: (l, 0, j)),
                  pl.BlockSpec((1, 1, tn), lambda l, j: (l, 0, j))],
        out_specs=pl.BlockSpec((1, MOD_ROWS, tn), lambda l, j: (l, 0, j)),
        out_shape=jax.ShapeDtypeStruct((depth, MOD_ROWS, n), F32),
        compiler_params=_cparams(("parallel", "parallel")),
    )(cc, w_mod, b_mod.reshape(depth, 1, n))
    return out.reshape(depth * MOD_ROWS * 6, 1, D)


def _mod_spec(rw, layer, which, tm):
    base = layer * MOD_ROWS * 6
    return pl.BlockSpec((1, 1, D), lambda i, *_: (base + rw.mod_row(i, tm) * 6 + which, 0, 0))


def _norm_mod(x, gain, shift, scale):
    y = x * lax.rsqrt(jnp.mean(x * x, axis=-1, keepdims=True) + EPS) * gain
    return y * (1.0 + scale) + shift


def _pick_rows(i, n_lat_tiles, x_ref, xc_ref, rows=slice(None)):
    return jnp.where(i < n_lat_tiles, x_ref[rows], xc_ref[rows])


def _stream_specs(rw, tm, split):
    if not split:
        return [pl.BlockSpec((tm, D), lambda i, *_: (i, 0))]
    nl = rw.lat_rows // tm
    return [pl.BlockSpec((tm, D), lambda i, *_: (jnp.minimum(i, nl - 1), 0)),
            pl.BlockSpec((tm, D), lambda i, *_: (jnp.maximum(i - nl, 0), 0))]


def _inproj_kernel(x_ref, *rest, has_extra, n_lat_tiles):
    if n_lat_tiles is not None:
        xc_ref, rest = rest[0], rest[1:]
    g_ref, sh_ref, sc_ref, w_ref = rest[:4]
    if has_extra:
        w2_ref, o_ref, o2_ref, h_ref = rest[4:]
    else:
        o_ref, h_ref = rest[4:]

    @pl.when(pl.program_id(1) == 0)
    def _():
        x = x_ref[...] if n_lat_tiles is None else _pick_rows(pl.program_id(0), n_lat_tiles, x_ref, xc_ref)
        h = _bf(_norm_mod(x, g_ref[...], sh_ref[0], sc_ref[0]))
        h_ref[...] = h
        if has_extra:
            o2_ref[...] = _dot(h, w2_ref[...])

    o_ref[...] = _bf(_dot(h_ref[...], w_ref[...]))


def _inproj(rw, x, gain, mod3, layer, w, w_extra=None, n_cols=None):
    n = w.shape[1] if n_cols is None else n_cols
    tm = TM_IN
    tn = max(t for t in range(256, TN_IN_MAX + 1, 256) if n % t == 0)
    has_extra = w_extra is not None
    split = isinstance(x, tuple)
    xs = list(x) if split else [x]
    in_specs = _stream_specs(rw, tm, split) + [
                pl.BlockSpec((1, D), lambda i, j: (0, 0)),
                _mod_spec(rw, layer, 0, tm),
                _mod_spec(rw, layer, 1, tm),
                pl.BlockSpec((D, tn), lambda i, j: (0, j))]
    out_specs = [pl.BlockSpec((tm, tn), lambda i, j: (i, j))]
    out_shape = [jax.ShapeDtypeStruct((rw.rows, n), BF16)]
    args = xs + [gain.reshape(1, D), mod3, mod3, w]
    if has_extra:
        ne = w_extra.shape[1]
        in_specs.append(pl.BlockSpec((D, ne), lambda i, j: (0, 0)))
        out_specs.append(pl.BlockSpec((tm, ne), lambda i, j: (i, 0)))
        out_shape.append(jax.ShapeDtypeStruct((rw.rows, ne), F32))
        args.append(w_extra)
    res = pl.pallas_call(
        functools.partial(_inproj_kernel, has_extra=has_extra, n_lat_tiles=rw.lat_rows // tm if split else None),
        grid=(rw.rows // tm, n // tn),
        in_specs=in_specs, out_specs=out_specs, out_shape=out_shape,
        scratch_shapes=[pltpu.VMEM((tm, D), BF16)],
        compiler_params=_cparams(("parallel", "arbitrary")),
    )(*args)
    return res if has_extra else res[0]


def _mlp_kernel(x_ref, g_ref, sh_ref, sc_ref, gt_ref, w1_ref, w2_ref, go_ref, o_ref, h_ref, acc_ref,
                *, final_norm):
    k = pl.program_id(1)

    @pl.when(k == 0)
    def _():
        h_ref[...] = _bf(_norm_mod(x_ref[...], g_ref[...], sh_ref[0], sc_ref[0]))
        acc_ref[...] = jnp.zeros_like(acc_ref)

    a = jnp.square(jnp.maximum(_dot(h_ref[...], _bf(w1_ref[...])), 0.0))
    acc_ref[...] += _dot(_bf(a), _bf(w2_ref[...]))

    @pl.when(k == pl.num_programs(1) - 1)
    def _():
        y = x_ref[...] + gt_ref[0] * acc_ref[...]
        if final_norm:
            y = y * lax.rsqrt(jnp.mean(y * y, axis=-1, keepdims=True) + EPS) * go_ref[...]
        o_ref[...] = y


def _mlp(rw, x, n_rows, gain, mod3, layer, w1, w2, out_gain, final_norm):
    tm, ck = TM_IN, FF_CHUNK
    return pl.pallas_call(
        functools.partial(_mlp_kernel, final_norm=final_norm),
        grid=(n_rows // tm, D_FF // ck),
        in_specs=[pl.BlockSpec((tm, D), lambda i, k: (i, 0)),
                  pl.BlockSpec((1, D), lambda i, k: (0, 0)),
                  _mod_spec(rw, layer, 3, tm),
                  _mod_spec(rw, layer, 4, tm),
                  _mod_spec(rw, layer, 5, tm),
                  pl.BlockSpec((None, D, ck), lambda i, k: (layer, 0, k)),
                  pl.BlockSpec((None, ck, D), lambda i, k: (layer, k, 0)),
                  pl.BlockSpec((1, D), lambda i, k: (0, 0))],
        out_specs=pl.BlockSpec((tm, D), lambda i, k: (i, 0)),
        out_shape=jax.ShapeDtypeStruct((n_rows, D), F32),
        scratch_shapes=[pltpu.VMEM((tm, D), BF16), pltpu.VMEM((tm, D), F32)],
        compiler_params=_cparams(("parallel", "arbitrary"), MLP_VMEM_LIMIT),
    )(x, gain.reshape(1, D), mod3, mod3, mod3, w1, w2, out_gain.reshape(1, D))


OUT_SUB = 256


def _row_blocks(n_rows):
    return [slice(r, r + OUT_SUB) for r in range(0, n_rows, OUT_SUB)]


def _gdn_out_kernel(x_ref, *rest, n_lat_tiles):
    if n_lat_tiles is not None:
        xc_ref, rest = rest[0], rest[1:]
    gt_ref, o0_ref, o1_ref, z_ref, ng_ref, w_ref, out_ref, a_ref = rest
    ng = ng_ref[...]
    for rows in _row_blocks(x_ref.shape[0]):
        for h in range(GDN_V_HEADS):
            sl = slice(h * GDN_HD, (h + 1) * GDN_HD)
            o = o0_ref[0, rows, sl].astype(F32) + o1_ref[0, rows, sl].astype(F32)
            y = o * lax.rsqrt(jnp.mean(o * o, axis=-1, keepdims=True) + EPS) * ng
            a_ref[rows, sl] = _bf(y * _silu(z_ref[rows, sl].astype(F32)))
        x = x_ref[rows] if n_lat_tiles is None else _pick_rows(pl.program_id(0), n_lat_tiles, x_ref, xc_ref, rows)
        out_ref[rows] = x + gt_ref[0] * _dot(a_ref[rows], w_ref[...])


def _ret_out_kernel(x_ref, gt_ref, o0_ref, o1_ref, z_ref, ng_ref, w_ref, out_ref, a_ref):
    for rows in _row_blocks(x_ref.shape[0]):
        for h in range(RET_HEADS):
            sl = slice(h * RET_DV, (h + 1) * RET_DV)
            o = o0_ref[0, rows, sl].astype(F32) + o1_ref[0, rows, sl].astype(F32)
            mu = jnp.mean(o, axis=-1, keepdims=True)
            oc = o - mu
            var = jnp.mean(oc * oc, axis=-1, keepdims=True)
            y = oc * lax.rsqrt(var + EPS) * ng_ref[:, sl]
            a_ref[rows, sl] = _bf(y * _silu(z_ref[rows, sl].astype(F32)))
        out_ref[rows] = x_ref[rows] + gt_ref[0] * _dot(a_ref[rows], w_ref[...])


def _lru_out_kernel(x_ref, gt_ref, h0_ref, h1_ref, gb_ref, w_ref, out_ref):
    for rows in _row_blocks(x_ref.shape[0]):
        a = (h0_ref[rows] + h1_ref[rows]) * _gelu_tanh(gb_ref[rows].astype(F32))
        out_ref[rows] = x_ref[rows] + gt_ref[0] * _dot(_bf(a), w_ref[...])


def _swa_out_kernel(x_ref, gt_ref, o_ref, w_ref, out_ref):
    out_ref[...] = x_ref[...] + gt_ref[0] * _dot(o_ref[...], w_ref[...])


def _outproj_call(kern, rw, n_rows, layer, x, mod3, extra_specs, extra_args, w, scratch):
    tm = TM_OUT
    k = w.shape[0]
    split = isinstance(x, tuple)
    xs = list(x) if split else [x]
    return pl.pallas_call(
        kern,
        grid=(n_rows // tm,),
        in_specs=_stream_specs(rw, tm, split) + [_mod_spec(rw, layer, 2, tm)] + extra_specs
                 + [pl.BlockSpec((k, D), lambda i: (0, 0))],
        out_specs=pl.BlockSpec((tm, D), lambda i: (i, 0)),
        out_shape=jax.ShapeDtypeStruct((n_rows, D), F32),
        scratch_shapes=scratch,
        compiler_params=_cparams(("parallel",)),
    )(*xs, mod3, *extra_args, w)


def _two_dir_out(kern, rw, layer, x, mod3, o, proj, z_block, norm_g, w):
    tm, k = TM_OUT, w.shape[0]
    ng = norm_g.reshape(1, -1)
    specs = [pl.BlockSpec((1, tm, k), lambda i: (0, i, 0)),
             pl.BlockSpec((1, tm, k), lambda i: (1, i, 0)),
             pl.BlockSpec((tm, k), lambda i: (i, z_block)),
             pl.BlockSpec(ng.shape, lambda i: (0, 0))]
    return _outproj_call(kern, rw, rw.rows, layer, x, mod3, specs, [o, o, proj, ng], w,
                         [pltpu.VMEM((tm, k), BF16)])


def _shift_matrices(t):
    delta = lax.broadcasted_iota(jnp.int32, (t, t), 0) - lax.broadcasted_iota(jnp.int32, (t, t), 1)
    return {k: _bf(jnp.where(jnp.logical_or(delta == k, delta == k - t), 1.0, 0.0)) for k in (2, 1, t - 1)}


def _conv_core(x_ref, p_ref, n_ref, w_ref, first, last, shifts=None):
    t = x_ref.shape[0]
    xb = x_ref[...]
    x = xb.astype(F32)
    prev = p_ref[...].astype(F32)[HALO - 8:] * jnp.where(first, 0.0, 1.0)
    nxt = n_ref[...].astype(F32)[:8] * jnp.where(last, 0.0, 1.0)
    if shifts is not None:
        assert xb.dtype == BF16
        r2, r1, rp = _dot(shifts[2], xb), _dot(shifts[1], xb), _dot(shifts[t - 1], xb)
    else:
        r2, r1, rp = pltpu.roll(x, 2, 0), pltpu.roll(x, 1, 0), pltpu.roll(x, t - 1, 0)
    row8 = lax.broadcasted_iota(jnp.int32, prev.shape, 0)
    f2 = jnp.where(row8 < 2, pltpu.roll(prev, 2, 0), r2[:8])
    f1 = jnp.where(row8 < 1, pltpu.roll(prev, 1, 0), r1[:8])
    l1 = jnp.where(row8 == 7, pltpu.roll(nxt, 7, 0), rp[t - 8:])
    xm2 = jnp.concatenate([f2, r2[8:]], axis=0)
    xm1 = jnp.concatenate([f1, r1[8:]], axis=0)
    xp1 = jnp.concatenate([rp[:t - 8], l1], axis=0)
    w = w_ref[...]
    return w[0:1] * xm2 + w[1:2] * xm1 + w[2:3] * x + w[3:4] * xp1


def _seg_flags(rw, i):
    per = rw.n_lat // TM_CONV
    is_ctx = i >= rw.lat_rows // TM_CONV
    assert rw.n_ctx == TM_CONV
    first = jnp.logical_or(is_ctx, i % per == 0)
    last = jnp.logical_or(is_ctx, i % per == per - 1)
    return first, last


def _gdn_conv_kernel(x_ref, p_ref, n_ref, w_ref, ab_ref, alog_ref, dtb_ref, o_ref, g_ref, b_ref, *, rw):
    i, j = pl.program_id(0), pl.program_id(1)
    first, last = _seg_flags(rw, i)
    s = _silu(_conv_core(x_ref, p_ref, n_ref, w_ref, first, last, _shift_matrices(x_ref.shape[0])))
    width = s.shape[1]
    qk_blocks = 2 * GDN_KEY // width

    @pl.when(j < qk_blocks)
    def _():
        for h in range(width // GDN_HD):
            sl = slice(h * GDN_HD, (h + 1) * GDN_HD)
            sh = s[:, sl]
            is_q = j * width + h * GDN_HD < GDN_KEY
            scale = jnp.where(is_q, GDN_HD ** -0.5, 1.0)
            o_ref[:, sl] = sh * (lax.rsqrt(jnp.sum(sh * sh, axis=-1, keepdims=True) + EPS) * scale)

    @pl.when(j >= qk_blocks)
    def _():
        o_ref[...] = s

    @pl.when(j == 0)
    def _():
        ab = ab_ref[...]
        hv = GDN_V_HEADS
        g_all = -jnp.exp(alog_ref[...]) * _softplus(ab + dtb_ref[...])
        b_all = jax.nn.sigmoid(ab)
        for d in range(2):
            g_ref[d] = g_all[:, 2 * d * hv:(2 * d + 1) * hv]
            b_ref[d] = b_all[:, (2 * d + 1) * hv:(2 * d + 2) * hv]


def _conv_specs(rw, cb):
    th = TM_CONV // HALO
    nh = rw.rows // HALO
    return [pl.BlockSpec((TM_CONV, cb), lambda i, j: (i, j)),
            pl.BlockSpec((HALO, cb), lambda i, j: (jnp.maximum(i * th - 1, 0), j)),
            pl.BlockSpec((HALO, cb), lambda i, j: (jnp.minimum((i + 1) * th, nh - 1), j)),
            pl.BlockSpec((4, cb), lambda i, j: (0, j))]


def _dir_masks(d, n):
    ii = lax.broadcasted_iota(jnp.int32, (n, n), 0)
    jj = lax.broadcasted_iota(jnp.int32, (n, n), 1)
    t = (ii - jj) * jnp.where(d == 0, 1, -1)
    return t, t >= 0, t > 0


def _gdn_core_kernel(qkv_ref, g_ref, b_ref, o_ref, s_ref):
    d, s = pl.program_id(1), pl.program_id(2)

    @pl.when(s == 0)
    def _():
        s_ref[...] = jnp.zeros_like(s_ref)

    c, hd = CHUNK, GDN_HD
    rep = GDN_V_HEADS // GDN_QK_HEADS
    heads = range(GDN_V_HEADS)
    n_sub = qkv_ref.shape[0] // c
    subs = range(n_sub)
    offs = [pl.multiple_of(jnp.where(d == 0, i, n_sub - 1 - i) * c, c) for i in subs]
    t, incl, strict = _dir_masks(d, c)
    cum = _bf(jnp.where(incl, 1.0, 0.0))
    cum_t = _bf(jnp.where(t <= 0, 1.0, 0.0))
    g = [g_ref[0, pl.ds(offs[i], c), :] for i in subs]
    beta = [b_ref[0, pl.ds(offs[i], c), :] for i in subs]
    parts = [_split3(g[i]) for i in subs]
    gc = [sum(_dot(cum, p) for p in parts[i]) for i in subs]
    gr = [sum(_dot_tn(p, cum_t) for p in parts[i]) for i in subs]
    gtot = [jnp.sum(g[i], axis=0, keepdims=True) for i in subs]

    def rows(i, lo):
        return qkv_ref[pl.ds(offs[i], c), lo:lo + hd]

    q_l = [[rows(i, hq * hd) for hq in range(GDN_QK_HEADS)] for i in subs]
    k_l = [[rows(i, GDN_KEY + hq * hd) for hq in range(GDN_QK_HEADS)] for i in subs]
    gram = [[_dot_nt(_bf(jnp.concatenate([k, q], axis=0)), _bf(k)) for q, k in zip(q_l[i], k_l[i])]
            for i in subs]
    zeros_cc = jnp.zeros((c, 2 * c), BF16)
    zeros_rhs = jnp.zeros((c, 2 * hd), BF16)
    lane_cc = lax.broadcasted_iota(jnp.int32, (c, 2 * c), 1)
    row_cc = lax.broadcasted_iota(jnp.int32, (c, 2 * c), 0)
    eye_hi = _bf(jnp.where(lane_cc == row_cc + c, 1.0, 0.0))
    is_hi = lane_cc >= c

    def prepare(chains):
        ids = range(len(chains))
        gcol = [gc[i][:, h:h + 1] for i, h in chains]
        bcol = [beta[i][:, h:h + 1] for i, h in chains]
        dec = [jnp.where(incl, jnp.exp(jnp.where(incl, gcol[n] - gr[i][h:h + 1, :], 0.0)), 0.0)
               for n, (i, h) in enumerate(chains)]
        eg = [jnp.exp(x) for x in gcol]
        m = [jnp.where(strict, -(gram[i][h // rep][:c] * dec[n] * bcol[n]), 0.0) for n, (i, h) in enumerate(chains)]
        rhs = [jnp.concatenate([k_l[i][h // rep] * (bcol[n] * eg[n]), rows(i, 2 * GDN_KEY + h * hd) * bcol[n]],
                               axis=1) for n, (i, h) in enumerate(chains)]

        m_hi = [_bf(x) for x in m]
        cb = [jnp.concatenate([m_hi[n], jnp.zeros((c, c), BF16)], axis=1) + eye_hi for n in ids]
        for _ in range(6):
            r = [_dot(cb[n], jnp.concatenate([cb[n], zeros_cc], axis=0)) for n in ids]
            cb = [_bf(r[n]) + jnp.where(is_hi, cb[n], jnp.zeros_like(cb[n])) for n in ids]
        nb = [cb[n] - eye_hi for n in ids]

        def apply_n(y):
            return [_dot(nb[n], jnp.concatenate([zeros_rhs, _bf(y[n])], axis=0)) for n in ids]

        nr = apply_n(rhs)
        x = [rhs[n] + nr[n] for n in ids]
        resid = [_dot(m_hi[n], _bf(x[n])) - nr[n] for n in ids]
        nr = apply_n(resid)
        x = [x[n] + (resid[n] + nr[n]) for n in ids]
        qk = [_bf(gram[i][h // rep][c:] * dec[n]) for n, (i, h) in enumerate(chains)]
        gt = [gtot[i][:, h:h + 1] for i, h in chains]
        kd = [_bf(k_l[i][h // rep] * jnp.exp(gt[n] - gcol[n])) for n, (i, h) in enumerate(chains)]
        wq = [_bf(jnp.concatenate([x[n][:, :hd], q_l[i][h // rep] * eg[n]], axis=0))
              for n, (i, h) in enumerate(chains)]
        decay = [jnp.exp(x) for x in gt]
        return x, qk, kd, wq, decay

    per_group = GDN_GROUP // GDN_V_HEADS
    st = [s_ref[h] for h in heads]
    for first in range(0, n_sub, per_group):
        chunk_ids = range(first, first + per_group)
        x, qk, kd, wq, decay = prepare([(i, h) for i in chunk_ids for h in heads])
        for j, i in enumerate(chunk_ids):
            base = j * GDN_V_HEADS
            r = [_dot(wq[base + h], _bf(st[h])) for h in heads]
            unb = [_bf(x[base + h][:, hd:] - r[h][:c]) for h in heads]
            for h in heads:
                o_ref[0, pl.ds(offs[i], c), h * hd:(h + 1) * hd] = _bf(r[h][c:] + _dot(qk[base + h], unb[h]))
            st = [st[h] * decay[base + h] + _dot_tn(kd[base + h], unb[h]) for h in heads]
    for h in heads:
        s_ref[h] = st[h]


def _gdn_core(rw, qkv, g_dir, b_dir):
    blk = GDN_BLOCK
    rb = lambda b, d, s: rw.chunk_block(b, d, s, blk)
    return pl.pallas_call(
        _gdn_core_kernel,
        grid=(rw.batch, 2, (rw.n_lat + rw.n_ctx) // blk),
        in_specs=[pl.BlockSpec((blk, GDN_CONV), lambda b, d, s: (rb(b, d, s), 0)),
                  pl.BlockSpec((1, blk, GDN_V_HEADS), lambda b, d, s: (d, rb(b, d, s), 0)),
                  pl.BlockSpec((1, blk, GDN_V_HEADS), lambda b, d, s: (d, rb(b, d, s), 0))],
        out_specs=pl.BlockSpec((1, blk, GDN_VAL), lambda b, d, s: (d, rb(b, d, s), 0)),
        out_shape=jax.ShapeDtypeStruct((2, rw.rows, GDN_VAL), BF16),
        scratch_shapes=[pltpu.VMEM((GDN_V_HEADS, GDN_HD, GDN_HD), F32)],
        compiler_params=_cparams(("parallel", "parallel", "arbitrary")),
    )(qkv, g_dir, b_dir)


def _rope_half(x, cos, sin_signed):
    return x * cos + pltpu.roll(x, x.shape[1] // 2, 1) * sin_signed


def _ret_core_kernel(qk_ref, v_ref, cos_ref, sin_ref, dl_ref, o_ref, s_ref, dec_ref, *, ctx_steps):
    d, s = pl.program_id(1), pl.program_id(2)
    c = qk_ref.shape[0]
    heads = range(RET_HEADS)
    is_lat = s >= ctx_steps
    cos = jnp.where(is_lat, cos_ref[...], 1.0)
    sin = jnp.where(is_lat, sin_ref[...], 0.0)
    t, incl, _ = _dir_masks(d, c)
    tf = t.astype(F32)
    pos = lax.broadcasted_iota(jnp.int32, (c, 1), 0)
    ridx = jnp.where(d == 0, pos + 1, c - pos).astype(F32)
    log_gamma = -_softplus(-dl_ref[0])

    def rope(x):
        hw = RET_DK // 2
        return jnp.concatenate([_rope_half(x[:, :hw], cos[:, :hw], sin[:, :hw]),
                                _rope_half(x[:, hw:], cos[:, hw:], sin[:, hw:])], axis=1)

    lg = [log_gamma[:, h:h + 1] for h in heads]

    @pl.when(s == 0)
    def _():
        s_ref[...] = jnp.zeros_like(s_ref)
        for h in heads:
            dec_ref[h] = jnp.where(incl, jnp.exp(jnp.where(incl, lg[h] * tf, 0.0)), 0.0)

    q = [rope(qk_ref[:, h * RET_DK:(h + 1) * RET_DK].astype(F32)) * (RET_DK ** -0.5) for h in heads]
    k = [rope(qk_ref[:, RET_KEY + h * RET_DK:RET_KEY + (h + 1) * RET_DK].astype(F32)) for h in heads]
    vb = [v_ref[:, h * RET_DV:(h + 1) * RET_DV] for h in heads]
    gcum = [lg[h] * ridx for h in heads]
    qk = [_bf(_dot_nt(_bf(q[h]), _bf(k[h])) * dec_ref[h]) for h in heads]
    st = [s_ref[h] for h in heads]
    for h in heads:
        o_ref[0, :, h * RET_DV:(h + 1) * RET_DV] = _bf(
            _dot(_bf(q[h] * jnp.exp(gcum[h])), _bf(st[h])) + _dot(qk[h], vb[h]))
    gtot = [lg[h] * float(c) for h in heads]
    kd = [_bf(k[h] * jnp.exp(gtot[h] - gcum[h])) for h in heads]
    for h in heads:
        s_ref[h] = st[h] * jnp.exp(gtot[h]) + _dot_tn(kd[h], vb[h])


def _ret_core(rw, proj, cos_tab, sin_tab, decay_logit):
    c = RET_CHUNK
    rb = lambda b, d, s: rw.chunk_block(b, d, s, c)
    dl = jnp.zeros((2, 1, 128), F32).at[:, 0, :RET_HEADS].set(decay_logit)
    return pl.pallas_call(
        functools.partial(_ret_core_kernel, ctx_steps=rw.n_ctx // c),
        grid=(rw.batch, 2, (rw.n_lat + rw.n_ctx) // c),
        in_specs=[pl.BlockSpec((c, 2 * RET_KEY), lambda b, d, s: (rb(b, d, s), 0)),
                  pl.BlockSpec((c, RET_VAL), lambda b, d, s: (rb(b, d, s), 1)),
                  pl.BlockSpec((c, RET_DK), lambda b, d, s: (rw.lat_chunk(d, s, c), 0)),
                  pl.BlockSpec((c, RET_DK), lambda b, d, s: (rw.lat_chunk(d, s, c), 0)),
                  pl.BlockSpec((1, 1, 128), lambda b, d, s: (d, 0, 0))],
        out_specs=pl.BlockSpec((1, c, RET_VAL), lambda b, d, s: (d, rb(b, d, s), 0)),
        out_shape=jax.ShapeDtypeStruct((2, rw.rows, RET_VAL), BF16),
        scratch_shapes=[pltpu.VMEM((RET_HEADS, RET_DK, RET_DV), F32), pltpu.VMEM((RET_HEADS, c, c), F32)],
        compiler_params=_cparams(("parallel", "parallel", "arbitrary")),
    )(proj, proj, cos_tab, sin_tab, dl)


def _lru_scan_kernel(x_ref, p_ref, n_ref, cw_ref, cb_ref, wg_ref, bg_ref, lam_ref, h_ref, carry_ref, *, rw, rev):
    @pl.when(pl.program_id(1) == 0)
    def _():
        carry_ref[...] = jnp.zeros_like(carry_ref)

    t = x_ref.shape[0]
    tile = rw.chunk_block(pl.program_id(0), 1 if rev else 0, pl.program_id(1), t)
    first, last = _seg_flags(rw, tile)
    xs = _conv_core(x_ref, p_ref, n_ref, cw_ref, first, last) + cb_ref[...]
    row8 = lax.broadcasted_iota(jnp.int32, (t // 8, 8, LRU_BW), 1)
    sp = _softplus(-lam_ref[...])
    for n in range(LRU_BLOCKS):
        sl = slice(n * LRU_BW, (n + 1) * LRU_BW)
        xn = xs[:, sl]
        gates = jax.nn.sigmoid(_dot(_bf(xn), wg_ref[n]) + bg_ref[n])
        log_a = -LRU_C * gates[:, :LRU_BW] * sp[:, sl]
        a = jnp.exp(log_a)
        y = -jnp.tanh(log_a) * (a * a + 1.0)
        b = jnp.where(y > 0.0, y * lax.rsqrt(y), 0.0) * gates[:, LRU_BW:] * xn
        groups = t // 8
        a3 = a.reshape(groups, 8, LRU_BW)
        b3 = b.reshape(groups, 8, LRU_BW)
        for k in (1, 2, 4):
            if rev:
                keep = row8 < 8 - k
                shift = 8 - k
            else:
                keep = row8 >= k
                shift = k
            sa = jnp.where(keep, pltpu.roll(a3, shift, 1), 1.0)
            sb = jnp.where(keep, pltpu.roll(b3, shift, 1), 0.0)
            b3 = b3 + a3 * sb
            a3 = a3 * sa
        carry = carry_ref[:, sl]
        for gi in (reversed(range(groups)) if rev else range(groups)):
            hg = a3[gi] * carry + b3[gi]
            carry = hg[0:1] if rev else hg[7:8]
            h_ref[gi * 8:(gi + 1) * 8, sl] = hg
        carry_ref[:, sl] = carry


def _lru_scan(rw, proj, conv_w, conv_b, w_gate, b_gate, lam, rev):
    t = TM_CONV
    th = t // HALO
    nh = rw.rows // HALO
    d = 1 if rev else 0
    rb = lambda b, s: rw.chunk_block(b, d, s, t)
    return pl.pallas_call(
        functools.partial(_lru_scan_kernel, rw=rw, rev=rev),
        grid=(rw.batch, (rw.n_lat + rw.n_ctx) // t),
        in_specs=[pl.BlockSpec((t, LRU_W), lambda b, s: (rb(b, s), 0)),
                  pl.BlockSpec((HALO, LRU_W), lambda b, s: (jnp.maximum(rb(b, s) * th - 1, 0), 0)),
                  pl.BlockSpec((HALO, LRU_W), lambda b, s: (jnp.minimum((rb(b, s) + 1) * th, nh - 1), 0)),
                  pl.BlockSpec((4, LRU_W), lambda b, s: (0, 0)),
                  pl.BlockSpec((1, LRU_W), lambda b, s: (0, 0)),
                  pl.BlockSpec((LRU_BLOCKS, LRU_BW, 2 * LRU_BW), lambda b, s: (0, 0, 0)),
                  pl.BlockSpec((LRU_BLOCKS, 1, 2 * LRU_BW), lambda b, s: (0, 0, 0)),
                  pl.BlockSpec((1, LRU_W), lambda b, s: (0, 0))],
        out_specs=pl.BlockSpec((t, LRU_W), lambda b, s: (rb(b, s), 0)),
        out_shape=jax.ShapeDtypeStruct((rw.rows, LRU_W), F32),
        scratch_shapes=[pltpu.VMEM((1, LRU_W), F32)],
        compiler_params=_cparams(("parallel", "arbitrary")),
    )(proj, proj, proj, conv_w, conv_b.reshape(1, LRU_W), _bf(w_gate),
      b_gate.reshape(LRU_BLOCKS, 1, 2 * LRU_BW), lam.reshape(1, LRU_W))


def _rope16(x, tab):
    cos, s1, s2 = tab[:, :128], tab[:, 128:256], tab[:, 256:]
    cols = []
    for cidx in range(x.shape[1] // 128):
        xc = x[:, cidx * 128:(cidx + 1) * 128]
        cols.append(xc * cos + pltpu.roll(xc, 112, 1) * s1 + pltpu.roll(xc, 16, 1) * s2)
    return cols[0] if len(cols) == 1 else jnp.concatenate(cols, axis=1)


def _swa_kernel(sink_ref, q_ref, kvp_ref, kvc_ref, kvn_ref, kvx_ref, tp_ref, tc_ref, tn_ref, o_ref):
    qb = pl.program_id(1)
    nb = pl.num_programs(1)
    blk = SWA_BLOCK
    n_loc = 3 * blk
    n_keys = n_loc + kvx_ref.shape[0]
    tab_c = tc_ref[...]
    keys = lambda ref: ref[:, :SWA_KVD].astype(F32)
    q = _rope16(q_ref[...].astype(F32), tab_c) * (SWA_HD ** -0.5 * LOG2E)
    k_all = jnp.concatenate([_rope16(keys(kvp_ref), tp_ref[...]), _rope16(keys(kvc_ref), tab_c),
                             _rope16(keys(kvn_ref), tn_ref[...]), keys(kvx_ref)], axis=0)
    v_all = jnp.concatenate([ref[:, SWA_KVD:] for ref in (kvp_ref, kvc_ref, kvn_ref, kvx_ref)], axis=0).astype(F32)
    assert SWA_WINDOW >= blk - 1
    qi = lax.broadcasted_iota(jnp.int32, (blk, blk), 0)
    kj = lax.broadcasted_iota(jnp.int32, (blk, blk), 1)
    valid_prev = jnp.logical_and(kj - blk - qi >= -SWA_WINDOW, qb > 0)
    valid_next = jnp.logical_and(kj + blk - qi <= SWA_WINDOW, qb < nb - 1)
    lane = lax.broadcasted_iota(jnp.int32, (n_keys, 128), 1)
    low = lane < SWA_HD
    out_low = lax.broadcasted_iota(jnp.int32, (blk, 128), 1) < SWA_HD

    def expo(scores, sink):
        sc = jnp.concatenate([jnp.where(valid_prev, scores[:, :blk], -jnp.inf), scores[:, blk:2 * blk],
                              jnp.where(valid_next, scores[:, 2 * blk:n_loc], -jnp.inf), scores[:, n_loc:]], axis=1)
        mx = jnp.maximum(jnp.max(sc, axis=-1, keepdims=True), sink)
        return _bf(jnp.exp2(sc - mx)), jnp.exp2(sink - mx)

    group = SWA_QH // SWA_KVH
    for g in range(SWA_KVH):
        col = slice((g // 2) * 128, (g // 2 + 1) * 128)
        native_low = g % 2 == 0
        sel = low if native_low else jnp.logical_not(low)
        k_nat = jnp.where(sel, k_all[:, col], 0.0)
        v_nat = jnp.where(sel, v_all[:, col], 0.0)
        k_oth = pltpu.roll(k_nat, SWA_HD, 1)
        v_oth = pltpu.roll(v_nat, SWA_HD, 1)
        k_lo, k_hi = (k_nat, k_oth) if native_low else (k_oth, k_nat)
        v_lo, v_hi = (v_nat, v_oth) if native_low else (v_oth, v_nat)
        v_lo = jnp.where(lane == SWA_HD, 1.0, v_lo)
        v_hi = jnp.where(lane == 0, 1.0, v_hi)
        k_lo, k_hi, v_lo, v_hi = _bf(k_lo), _bf(k_hi), _bf(v_lo), _bf(v_hi)
        for cidx in range(group // 2):
            qc = (g * group) // 2 + cidx
            qv = _bf(q[:, qc * 128:(qc + 1) * 128])
            e_lo, sink_lo = expo(_dot_nt(qv, k_lo), sink_ref[2 * qc] * LOG2E)
            e_hi, sink_hi = expo(_dot_nt(qv, k_hi), sink_ref[2 * qc + 1] * LOG2E)
            a_lo = _dot(e_lo, v_lo)
            a_hi = _dot(e_hi, v_hi)
            inv_lo = 1.0 / (a_lo[:, SWA_HD:SWA_HD + 1] + sink_lo)
            inv_hi = 1.0 / (a_hi[:, 0:1] + sink_hi)
            o_ref[:, qc * 128:(qc + 1) * 128] = _bf(jnp.where(out_low, a_lo * inv_lo, a_hi * inv_hi))


def _swa_core(rw, proj, sink, tab):
    blk = SWA_BLOCK
    nb = rw.n_lat // blk
    kvcol = SWA_QD // (2 * SWA_KVD)
    ctx_blk = lambda b: rw.lat_rows // rw.n_ctx + b
    prev = lambda q: jnp.maximum(q - 1, 0)
    nxt = lambda q: jnp.minimum(q + 1, nb - 1)
    kv = lambda f: pl.BlockSpec((blk, 2 * SWA_KVD), lambda b, q: (b * nb + f(q), kvcol))
    same = lambda q: q
    return pl.pallas_call(
        _swa_kernel,
        grid=(rw.batch, nb),
        in_specs=[pl.BlockSpec(memory_space=pltpu.SMEM),
                  pl.BlockSpec((blk, SWA_QD), lambda b, q: (b * nb + q, 0)),
                  kv(prev), kv(same), kv(nxt),
                  pl.BlockSpec((rw.n_ctx, 2 * SWA_KVD), lambda b, q: (ctx_blk(b), kvcol)),
                  pl.BlockSpec((blk, 384), lambda b, q: (prev(q), 0)),
                  pl.BlockSpec((blk, 384), lambda b, q: (q, 0)),
                  pl.BlockSpec((blk, 384), lambda b, q: (nxt(q), 0))],
        out_specs=pl.BlockSpec((blk, SWA_QD), lambda b, q: (b * nb + q, 0)),
        out_shape=jax.ShapeDtypeStruct((rw.lat_rows, SWA_QD), BF16),
        compiler_params=_cparams(("parallel", "parallel")),
    )(sink, proj, proj, proj, proj, proj, tab, tab, tab)


def _axial_angles(n_lat, dh):
    t = jnp.arange(n_lat)
    row = (t // GRID_W).astype(F32)
    col = (t % GRID_W).astype(F32)
    nf = dh // 4
    inv = ROPE_THETA ** (-jnp.arange(nf, dtype=F32) / nf)
    return row[:, None] * inv, col[:, None] * inv


def _ret_tables(n_lat):
    ar, ac = _axial_angles(n_lat, RET_DK)
    cos = jnp.concatenate([jnp.cos(ar), jnp.cos(ar), jnp.cos(ac), jnp.cos(ac)], axis=1)
    sin = jnp.concatenate([-jnp.sin(ar), jnp.sin(ar), -jnp.sin(ac), jnp.sin(ac)], axis=1)
    return cos, sin


def _swa_table(n_lat):
    ar, ac = _axial_angles(n_lat, SWA_HD)
    z = jnp.zeros_like(ar)
    cos = jnp.concatenate([jnp.cos(ar), jnp.cos(ar), jnp.cos(ac), jnp.cos(ac)], axis=1)
    s1 = jnp.concatenate([-jnp.sin(ar), z, -jnp.sin(ac), z], axis=1)
    s2 = jnp.concatenate([z, jnp.sin(ar), z, jnp.sin(ac)], axis=1)
    return jnp.concatenate([jnp.tile(cos, (1, 2)), jnp.tile(s1, (1, 2)), jnp.tile(s2, (1, 2))], axis=1)


def _gdn_layer(rw, layer, x, mod3, gain, w_in, conv_w, a_log, dt_bias, norm_g, w_out):
    n_ab = w_in.shape[1] - GDN_MAIN
    w_ab = _bf(jnp.pad(w_in[:, GDN_MAIN:], ((0, 0), (0, 128 - n_ab))))
    proj, ab = _inproj(rw, x, gain, mod3, layer, _bf(w_in), w_ab, n_cols=GDN_MAIN)
    hv = GDN_V_HEADS
    alog_row = jnp.zeros((1, 128), F32).at[0, 0:hv].set(a_log[0]).at[0, 2 * hv:3 * hv].set(a_log[1])
    dtb_row = jnp.zeros((1, 128), F32).at[0, 0:hv].set(dt_bias[0]).at[0, 2 * hv:3 * hv].set(dt_bias[1])
    cb = 2048
    row_spec = pl.BlockSpec((TM_CONV, 128), lambda i, j: (i, 0))
    vec_spec = pl.BlockSpec((1, 128), lambda i, j: (0, 0))
    dir_spec = pl.BlockSpec((2, TM_CONV, hv), lambda i, j: (0, i, 0))
    qkv, g_dir, b_dir = pl.pallas_call(
        functools.partial(_gdn_conv_kernel, rw=rw),
        grid=(rw.rows // TM_CONV, GDN_CONV // cb),
        in_specs=_conv_specs(rw, cb) + [row_spec, vec_spec, vec_spec],
        out_specs=[pl.BlockSpec((TM_CONV, cb), lambda i, j: (i, j)), dir_spec, dir_spec],
        out_shape=[jax.ShapeDtypeStruct((rw.rows, GDN_CONV), F32),
                   jax.ShapeDtypeStruct((2, rw.rows, hv), F32),
                   jax.ShapeDtypeStruct((2, rw.rows, hv), F32)],
        compiler_params=_cparams(("parallel", "arbitrary")),
    )(proj, proj, proj, conv_w, ab, alog_row, dtb_row)
    o = _gdn_core(rw, qkv, g_dir, b_dir)
    kern = functools.partial(_gdn_out_kernel, n_lat_tiles=rw.lat_rows // TM_OUT if isinstance(x, tuple) else None)
    return _two_dir_out(kern, rw, layer, x, mod3, o, proj, GDN_CONV // GDN_VAL, norm_g, _bf(w_out))


def _ret_layer(rw, layer, x, mod3, gain, w_in, decay_logit, norm_g, w_out):
    proj = _inproj(rw, x, gain, mod3, layer, _bf(w_in))
    cos_tab, sin_tab = _ret_tables(rw.n_lat)
    o = _ret_core(rw, proj, cos_tab, sin_tab, decay_logit)
    return _two_dir_out(_ret_out_kernel, rw, layer, x, mod3, o, proj, (2 * RET_KEY + RET_VAL) // RET_VAL,
                        norm_g, _bf(w_out))


def _lru_layer(rw, layer, x, mod3, gain, w_in, conv_w, conv_b, w_gate, b_gate, lam, w_out):
    proj = _inproj(rw, x, gain, mod3, layer, _bf(w_in))
    h0 = _lru_scan(rw, proj, conv_w, conv_b, w_gate[0], b_gate[0], lam[0], rev=False)
    h1 = _lru_scan(rw, proj, conv_w, conv_b, w_gate[1], b_gate[1], lam[1], rev=True)
    tm = TM_OUT
    specs = [pl.BlockSpec((tm, LRU_W), lambda i: (i, 0)),
             pl.BlockSpec((tm, LRU_W), lambda i: (i, 0)),
             pl.BlockSpec((tm, LRU_W), lambda i: (i, 1))]
    return _outproj_call(_lru_out_kernel, rw, rw.rows, layer, x, mod3, specs, [h0, h1, proj], _bf(w_out), [])


def _swa_layer(rw, layer, x, mod3, gain, w_in, sink, w_out):
    proj = _inproj(rw, x, gain, mod3, layer, _bf(w_in))
    o = _swa_core(rw, proj, sink, _swa_table(rw.n_lat))
    specs = [pl.BlockSpec((TM_OUT, SWA_QD), lambda i: (i, 0))]
    return _outproj_call(_swa_out_kernel, rw, rw.lat_rows, layer, x, mod3, specs, [o], _bf(w_out), [])


def kernel(x, c, ctx, c_ctx, norm_mix_g, norm_ffn_g, w_mod, b_mod, w_ff1, w_ff2, norm_out_g, gdn_w_in, gdn_conv_w, gdn_a_log, gdn_dt_bias, gdn_norm_g, gdn_w_out, ret_w_in, ret_decay_logit, ret_norm_g, ret_w_out, lru_w_in, lru_conv_w, lru_conv_b, lru_w_gate, lru_b_gate, lru_lambda, lru_w_out, swa_w_in, swa_sink, swa_w_out):
    batch, n_lat, _ = x.shape
    n_ctx = ctx.shape[1]
    assert batch + 1 <= MOD_ROWS and n_lat % TM_IN == 0 and (batch * n_ctx) % TM_IN == 0
    assert w_mod.shape[0] == DEPTH and gdn_w_in.shape[0] == 1 and swa_w_in.shape[0] == 1
    rw = _Rows(batch, n_lat, n_ctx)
    xf = (x.reshape(rw.lat_rows, D), ctx.reshape(batch * n_ctx, D))
    cc = jnp.zeros((MOD_ROWS, D), F32).at[:batch].set(c).at[batch].set(c_ctx)
    mod3 = _adaln(cc, w_mod, b_mod)

    def mlp(layer, xin, n_rows, final):
        return _mlp(rw, xin, n_rows, norm_ffn_g[layer], mod3, layer, w_ff1, w_ff2,
                    norm_out_g, final)

    xf = _gdn_layer(rw, 0, xf, mod3, norm_mix_g[0], gdn_w_in[0], gdn_conv_w[0], gdn_a_log[0], gdn_dt_bias[0],
                    gdn_norm_g[0], gdn_w_out[0])
    xf = mlp(0, xf, rw.rows, False)
    xf = _ret_layer(rw, 1, xf, mod3, norm_mix_g[1], ret_w_in[0], ret_decay_logit[0], ret_norm_g[0], ret_w_out[0])
    xf = mlp(1, xf, rw.rows, False)
    xf = _lru_layer(rw, 2, xf, mod3, norm_mix_g[2], lru_w_in[0], lru_conv_w[0], lru_conv_b[0], lru_w_gate[0],
                    lru_b_gate[0], lru_lambda[0], lru_w_out[0])
    xf = mlp(2, xf, rw.rows, False)
    xl = _swa_layer(rw, 3, xf, mod3, norm_mix_g[3], swa_w_in[0], swa_sink[0], swa_w_out[0])
    out = mlp(3, xl, rw.lat_rows, True)
    return out.reshape(batch, n_lat, D)
```

```python
import functools
import math

import jax
import jax.numpy as jnp
from jax import lax
from jax.experimental import pallas as pl
from jax.experimental.pallas import tpu as pltpu

F32 = jnp.float32
BF16 = jnp.bfloat16

D = 1024
D_FF = 4 * D
EPS = 1e-6
LOG2E = math.log2(math.e)
DEPTH = 4
ROPE_THETA = 10000.0
GRID_W = 64
CHUNK = 64
MOD_ROWS = 16

GDN_QK_HEADS = 8
GDN_V_HEADS = 16
GDN_HD = 128
GDN_KEY = GDN_QK_HEADS * GDN_HD
GDN_VAL = GDN_V_HEADS * GDN_HD
GDN_CONV = 2 * GDN_KEY + GDN_VAL
GDN_MAIN = GDN_CONV + GDN_VAL

RET_HEADS = 4
RET_DK = 256
RET_DV = 512
RET_KEY = RET_HEADS * RET_DK
RET_VAL = RET_HEADS * RET_DV

LRU_W = 1280
LRU_BLOCKS = 10
LRU_BW = LRU_W // LRU_BLOCKS
LRU_C = 8.0

SWA_QH = 16
SWA_KVH = 4
SWA_HD = 64
SWA_BLOCK = 128
SWA_WINDOW = 128
SWA_QD = SWA_QH * SWA_HD
SWA_KVD = SWA_KVH * SWA_HD

TM_IN = 1024
TM_OUT = 512
TM_CONV = 256
HALO = 16
TN_IN_MAX = 2048
FF_CHUNK = 1024
RET_CHUNK = 256
GDN_BLOCK = 4 * CHUNK
GDN_GROUP = 32
VMEM_LIMIT = 48 * 1024 * 1024
MLP_VMEM_LIMIT = 56 * 1024 * 1024


def _dot(a, b):
    return jnp.dot(a, b, preferred_element_type=F32)


def _dot_nt(a, b):
    return lax.dot_general(a, b, (((1,), (1,)), ((), ())), preferred_element_type=F32)


def _dot_tn(a, b):
    return lax.dot_general(a, b, (((0,), (0,)), ((), ())), preferred_element_type=F32)


def _bf(x):
    return x.astype(BF16)


def _split3(x):
    hi = _bf(x)
    r = x - hi.astype(F32)
    mid = _bf(r)
    lo = _bf(r - mid.astype(F32))
    return hi, mid, lo


def _silu(x):
    return x * jax.nn.sigmoid(x)


def _softplus(x):
    return jnp.maximum(x, 0.0) + jnp.log1p(jnp.exp(-jnp.abs(x)))


def _gelu_tanh(x):
    cdf = 0.5 * (1.0 + jnp.tanh(math.sqrt(2.0 / math.pi) * (x + 0.044715 * (x * x * x))))
    return x * cdf


def _cparams(sem, vmem=None):
    return pltpu.CompilerParams(dimension_semantics=sem, vmem_limit_bytes=VMEM_LIMIT if vmem is None else vmem)


class _Rows:
    def __init__(self, batch, n_lat, n_ctx):
        self.batch, self.n_lat, self.n_ctx = batch, n_lat, n_ctx
        self.lat_rows = batch * n_lat
        self.rows = self.lat_rows + batch * n_ctx

    def mod_row(self, i, tm):
        return jnp.where(i < self.lat_rows // tm, i // (self.n_lat // tm), self.batch)

    def chunk_block(self, b, d, s, blk):
        nc, nl = self.n_ctx // blk, self.n_lat // blk
        c_ctx = jnp.where(d == 0, s, nc - 1 - s)
        c_lat = jnp.where(d == 0, s - nc, nl - 1 - (s - nc))
        return jnp.where(s < nc, self.lat_rows // blk + b * nc + c_ctx, b * nl + c_lat)

    def lat_chunk(self, d, s, blk):
        nc, nl = self.n_ctx // blk, self.n_lat // blk
        return jnp.where(s < nc, 0, jnp.where(d == 0, s - nc, nl - 1 - (s - nc)))


def _adaln_kernel(c_ref, w_ref, b_ref, o_ref):
    s = _silu(c_ref[...])
    w = w_ref[0]
    s_hi = _bf(s)
    s_lo = _bf(s - s_hi.astype(F32))
    w_hi = _bf(w)
    w_lo = _bf(w - w_hi.astype(F32))
    y = _dot(s_hi, w_hi) + (_dot(s_lo, w_hi) + _dot(s_hi, w_lo))
    o_ref[0] = y + b_ref[0]


def _adaln(cc, w_mod, b_mod):
    depth, _, n = w_mod.shape
    tn = 1024
    out = pl.pallas_call(
        _adaln_kernel,
        grid=(depth, n // tn),
        in_specs=[pl.BlockSpec((MOD_ROWS, D), lambda l, j: (0, 0)),
                  pl.BlockSpec((1, D, tn), lambda l, j: (l, 0, j)),
                  pl.BlockSpec((1, 1, tn), lambda l, j: (l, 0, j))],
        out_specs=pl.BlockSpec((1, MOD_ROWS, tn), lambda l, j: (l, 0, j)),
        out_shape=jax.ShapeDtypeStruct((depth, MOD_ROWS, n), F32),
        compiler_params=_cparams(("parallel", "parallel")),
    )(cc, w_mod, b_mod.reshape(depth, 1, n))
    return out.reshape(depth * MOD_ROWS * 6, 1, D)


def _mod_spec(rw, layer, which, tm):
    base = layer * MOD_ROWS * 6
    return pl.BlockSpec((1, 1, D), lambda i, *_: (base + rw.mod_row(i, tm) * 6 + which, 0, 0))


def _norm_mod(x, gain, shift, scale):
    y = x * lax.rsqrt(jnp.mean(x * x, axis=-1, keepdims=True) + EPS) * gain
    return y * (1.0 + scale) + shift


def _pick_rows(i, n_lat_tiles, x_ref, xc_ref, rows=slice(None)):
    return jnp.where(i < n_lat_tiles, x_ref[rows], xc_ref[rows])


def _stream_specs(rw, tm, split):
    if not split:
        return [pl.BlockSpec((tm, D), lambda i, *_: (i, 0))]
    nl = rw.lat_rows // tm
    return [pl.BlockSpec((tm, D), lambda i, *_: (jnp.minimum(i, nl - 1), 0)),
            pl.BlockSpec((tm, D), lambda i, *_: (jnp.maximum(i - nl, 0), 0))]


def _inproj_kernel(x_ref, *rest, has_extra, n_lat_tiles):
    if n_lat_tiles is not None:
        xc_ref, rest = rest[0], rest[1:]
    g_ref, sh_ref, sc_ref, w_ref = rest[:4]
    if has_extra:
        w2_ref, o_ref, o2_ref, h_ref = rest[4:]
    else:
        o_ref, h_ref = rest[4:]

    @pl.when(pl.program_id(1) == 0)
    def _():
        x = x_ref[...] if n_lat_tiles is None else _pick_rows(pl.program_id(0), n_lat_tiles, x_ref, xc_ref)
        h = _bf(_norm_mod(x, g_ref[...], sh_ref[0], sc_ref[0]))
        h_ref[...] = h
        if has_extra:
            o2_ref[...] = _dot(h, w2_ref[...])

    o_ref[...] = _bf(_dot(h_ref[...], w_ref[...]))


def _inproj(rw, x, gain, mod3, layer, w, w_extra=None, n_cols=None):
    n = w.shape[1] if n_cols is None else n_cols
    tm = TM_IN
    tn = max(t for t in range(256, TN_IN_MAX + 1, 256) if n % t == 0)
    has_extra = w_extra is not None
    split = isinstance(x, tuple)
    xs = list(x) if split else [x]
    in_specs = _stream_specs(rw, tm, split) + [
                pl.BlockSpec((1, D), lambda i, j: (0, 0)),
                _mod_spec(rw, layer, 0, tm),
                _mod_spec(rw, layer, 1, tm),
                pl.BlockSpec((D, tn), lambda i, j: (0, j))]
    out_specs = [pl.BlockSpec((tm, tn), lambda i, j: (i, j))]
    out_shape = [jax.ShapeDtypeStruct((rw.rows, n), BF16)]
    args = xs + [gain.reshape(1, D), mod3, mod3, w]
    if has_extra:
        ne = w_extra.shape[1]
        in_specs.append(pl.BlockSpec((D, ne), lambda i, j: (0, 0)))
        out_specs.append(pl.BlockSpec((tm, ne), lambda i, j: (i, 0)))
        out_shape.append(jax.ShapeDtypeStruct((rw.rows, ne), F32))
        args.append(w_extra)
    res = pl.pallas_call(
        functools.partial(_inproj_kernel, has_extra=has_extra, n_lat_tiles=rw.lat_rows // tm if split else None),
        grid=(rw.rows // tm, n // tn),
        in_specs=in_specs, out_specs=out_specs, out_shape=out_shape,
        scratch_shapes=[pltpu.VMEM((tm, D), BF16)],
        compiler_params=_cparams(("parallel", "arbitrary")),
    )(*args)
    return res if has_extra else res[0]


def _mlp_kernel(x_ref, g_ref, sh_ref, sc_ref, gt_ref, w1_ref, w2_ref, go_ref, o_ref, h_ref, acc_ref,
                *, final_norm):
    k = pl.program_id(1)

    @pl.when(k == 0)
    def _():
        h_ref[...] = _bf(_norm_mod(x_ref[...], g_ref[...], sh_ref[0], sc_ref[0]))
        acc_ref[...] = jnp.zeros_like(acc_ref)

    a = jnp.square(jnp.maximum(_dot(h_ref[...], _bf(w1_ref[...])), 0.0))
    acc_ref[...] += _dot(_bf(a), _bf(w2_ref[...]))

    @pl.when(k == pl.num_programs(1) - 1)
    def _():
        y = x_ref[...] + gt_ref[0] * acc_ref[...]
        if final_norm:
            y = y * lax.rsqrt(jnp.mean(y * y, axis=-1, keepdims=True) + EPS) * go_ref[...]
        o_ref[...] = y


def _mlp(rw, x, n_rows, gain, mod3, layer, w1, w2, out_gain, final_norm):
    tm, ck = TM_IN, FF_CHUNK
    return pl.pallas_call(
        functools.partial(_mlp_kernel, final_norm=final_norm),
        grid=(n_rows // tm, D_FF // ck),
        in_specs=[pl.BlockSpec((tm, D), lambda i, k: (i, 0)),
                  pl.BlockSpec((1, D), lambda i, k: (0, 0)),
                  _mod_spec(rw, layer, 3, tm),
                  _mod_spec(rw, layer, 4, tm),
                  _mod_spec(rw, layer, 5, tm),
                  pl.BlockSpec((None, D, ck), lambda i, k: (layer, 0, k)),
                  pl.BlockSpec((None, ck, D), lambda i, k: (layer, k, 0)),
                  pl.BlockSpec((1, D), lambda i, k: (0, 0))],
        out_specs=pl.BlockSpec((tm, D), lambda i, k: (i, 0)),
        out_shape=jax.ShapeDtypeStruct((n_rows, D), F32),
        scratch_shapes=[pltpu.VMEM((tm, D), BF16), pltpu.VMEM((tm, D), F32)],
        compiler_params=_cparams(("parallel", "arbitrary"), MLP_VMEM_LIMIT),
    )(x, gain.reshape(1, D), mod3, mod3, mod3, w1, w2, out_gain.reshape(1, D))


OUT_SUB = 256


def _row_blocks(n_rows):
    return [slice(r, r + OUT_SUB) for r in range(0, n_rows, OUT_SUB)]


def _gdn_out_kernel(x_ref, *rest, n_lat_tiles):
    if n_lat_tiles is not None:
        xc_ref, rest = rest[0], rest[1:]
    gt_ref, o0_ref, o1_ref, z_ref, ng_ref, w_ref, out_ref, a_ref = rest
    ng = ng_ref[...]
    for rows in _row_blocks(x_ref.shape[0]):
        for h in range(GDN_V_HEADS):
            sl = slice(h * GDN_HD, (h + 1) * GDN_HD)
            o = o0_ref[0, rows, sl].astype(F32) + o1_ref[0, rows, sl].astype(F32)
            y = o * lax.rsqrt(jnp.mean(o * o, axis=-1, keepdims=True) + EPS) * ng
            a_ref[rows, sl] = _bf(y * _silu(z_ref[rows, sl].astype(F32)))
        x = x_ref[rows] if n_lat_tiles is None else _pick_rows(pl.program_id(0), n_lat_tiles, x_ref, xc_ref, rows)
        out_ref[rows] = x + gt_ref[0] * _dot(a_ref[rows], w_ref[...])


def _ret_out_kernel(x_ref, gt_ref, o0_ref, o1_ref, z_ref, ng_ref, w_ref, out_ref, a_ref):
    for rows in _row_blocks(x_ref.shape[0]):
        for h in range(RET_HEADS):
            sl = slice(h * RET_DV, (h + 1) * RET_DV)
            o = o0_ref[0, rows, sl].astype(F32) + o1_ref[0, rows, sl].astype(F32)
            mu = jnp.mean(o, axis=-1, keepdims=True)
            oc = o - mu
            var = jnp.mean(oc * oc, axis=-1, keepdims=True)
            y = oc * lax.rsqrt(var + EPS) * ng_ref[:, sl]
            a_ref[rows, sl] = _bf(y * _silu(z_ref[rows, sl].astype(F32)))
        out_ref[rows] = x_ref[rows] + gt_ref[0] * _dot(a_ref[rows], w_ref[...])


def _lru_out_kernel(x_ref, gt_ref, h0_ref, h1_ref, gb_ref, w_ref, out_ref):
    for rows in _row_blocks(x_ref.shape[0]):
        a = (h0_ref[rows] + h1_ref[rows]) * _gelu_tanh(gb_ref[rows].astype(F32))
        out_ref[rows] = x_ref[rows] + gt_ref[0] * _dot(_bf(a), w_ref[...])


def _swa_out_kernel(x_ref, gt_ref, o_ref, w_ref, out_ref):
    out_ref[...] = x_ref[...] + gt_ref[0] * _dot(o_ref[...], w_ref[...])


def _outproj_call(kern, rw, n_rows, layer, x, mod3, extra_specs, extra_args, w, scratch):
    tm = TM_OUT
    k = w.shape[0]
    split = isinstance(x, tuple)
    xs = list(x) if split else [x]
    return pl.pallas_call(
        kern,
        grid=(n_rows // tm,),
        in_specs=_stream_specs(rw, tm, split) + [_mod_spec(rw, layer, 2, tm)] + extra_specs
                 + [pl.BlockSpec((k, D), lambda i: (0, 0))],
        out_specs=pl.BlockSpec((tm, D), lambda i: (i, 0)),
        out_shape=jax.ShapeDtypeStruct((n_rows, D), F32),
        scratch_shapes=scratch,
        compiler_params=_cparams(("parallel",)),
    )(*xs, mod3, *extra_args, w)


def _two_dir_out(kern, rw, layer, x, mod3, o, proj, z_block, norm_g, w):
    tm, k = TM_OUT, w.shape[0]
    ng = norm_g.reshape(1, -1)
    specs = [pl.BlockSpec((1, tm, k), lambda i: (0, i, 0)),
             pl.BlockSpec((1, tm, k), lambda i: (1, i, 0)),
             pl.BlockSpec((tm, k), lambda i: (i, z_block)),
             pl.BlockSpec(ng.shape, lambda i: (0, 0))]
    return _outproj_call(kern, rw, rw.rows, layer, x, mod3, specs, [o, o, proj, ng], w,
                         [pltpu.VMEM((tm, k), BF16)])


def _shift_matrices(t):
    delta = lax.broadcasted_iota(jnp.int32, (t, t), 0) - lax.broadcasted_iota(jnp.int32, (t, t), 1)
    return {k: _bf(jnp.where(jnp.logical_or(delta == k, delta == k - t), 1.0, 0.0)) for k in (2, 1, t - 1)}


def _conv_core(x_ref, p_ref, n_ref, w_ref, first, last, shifts=None):
    t = x_ref.shape[0]
    xb = x_ref[...]
    x = xb.astype(F32)
    prev = p_ref[...].astype(F32)[HALO - 8:] * jnp.where(first, 0.0, 1.0)
    nxt = n_ref[...].astype(F32)[:8] * jnp.where(last, 0.0, 1.0)
    if shifts is not None:
        assert xb.dtype == BF16
        r2, r1, rp = _dot(shifts[2], xb), _dot(shifts[1], xb), _dot(shifts[t - 1], xb)
    else:
        r2, r1, rp = pltpu.roll(x, 2, 0), pltpu.roll(x, 1, 0), pltpu.roll(x, t - 1, 0)
    row8 = lax.broadcasted_iota(jnp.int32, prev.shape, 0)
    f2 = jnp.where(row8 < 2, pltpu.roll(prev, 2, 0), r2[:8])
    f1 = jnp.where(row8 < 1, pltpu.roll(prev, 1, 0), r1[:8])
    l1 = jnp.where(row8 == 7, pltpu.roll(nxt, 7, 0), rp[t - 8:])
    xm2 = jnp.concatenate([f2, r2[8:]], axis=0)
    xm1 = jnp.concatenate([f1, r1[8:]], axis=0)
    xp1 = jnp.concatenate([rp[:t - 8], l1], axis=0)
    w = w_ref[...]
    return w[0:1] * xm2 + w[1:2] * xm1 + w[2:3] * x + w[3:4] * xp1


def _seg_flags(rw, i):
    per = rw.n_lat // TM_CONV
    is_ctx = i >= rw.lat_rows // TM_CONV
    assert rw.n_ctx == TM_CONV
    first = jnp.logical_or(is_ctx, i % per == 0)
    last = jnp.logical_or(is_ctx, i % per == per - 1)
    return first, last


def _gdn_conv_kernel(x_ref, p_ref, n_ref, w_ref, ab_ref, alog_ref, dtb_ref, o_ref, g_ref, b_ref, *, rw):
    i, j = pl.program_id(0), pl.program_id(1)
    first, last = _seg_flags(rw, i)
    s = _silu(_conv_core(x_ref, p_ref, n_ref, w_ref, first, last, _shift_matrices(x_ref.shape[0])))
    width = s.shape[1]
    qk_blocks = 2 * GDN_KEY // width

    @pl.when(j < qk_blocks)
    def _():
        for h in range(width // GDN_HD):
            sl = slice(h * GDN_HD, (h + 1) * GDN_HD)
            sh = s[:, sl]
            is_q = j * width + h * GDN_HD < GDN_KEY
            scale = jnp.where(is_q, GDN_HD ** -0.5, 1.0)
            o_ref[:, sl] = sh * (lax.rsqrt(jnp.sum(sh * sh, axis=-1, keepdims=True) + EPS) * scale)

    @pl.when(j >= qk_blocks)
    def _():
        o_ref[...] = s

    @pl.when(j == 0)
    def _():
        ab = ab_ref[...]
        hv = GDN_V_HEADS
        g_all = -jnp.exp(alog_ref[...]) * _softplus(ab + dtb_ref[...])
        b_all = jax.nn.sigmoid(ab)
        for d in range(2):
            g_ref[d] = g_all[:, 2 * d * hv:(2 * d + 1) * hv]
            b_ref[d] = b_all[:, (2 * d + 1) * hv:(2 * d + 2) * hv]


def _conv_specs(rw, cb):
    th = TM_CONV // HALO
    nh = rw.rows // HALO
    return [pl.BlockSpec((TM_CONV, cb), lambda i, j: (i, j)),
            pl.BlockSpec((HALO, cb), lambda i, j: (jnp.maximum(i * th - 1, 0), j)),
            pl.BlockSpec((HALO, cb), lambda i, j: (jnp.minimum((i + 1) * th, nh - 1), j)),
            pl.BlockSpec((4, cb), lambda i, j: (0, j))]


def _dir_masks(d, n):
    ii = lax.broadcasted_iota(jnp.int32, (n, n), 0)
    jj = lax.broadcasted_iota(jnp.int32, (n, n), 1)
    t = (ii - jj) * jnp.where(d == 0, 1, -1)
    return t, t >= 0, t > 0


def _gdn_core_kernel(qkv_ref, g_ref, b_ref, o_ref, s_ref):
    d, s = pl.program_id(1), pl.program_id(2)

    @pl.when(s == 0)
    def _():
        s_ref[...] = jnp.zeros_like(s_ref)

    c, hd = CHUNK, GDN_HD
    rep = GDN_V_HEADS // GDN_QK_HEADS
    heads = range(GDN_V_HEADS)
    n_sub = qkv_ref.shape[0] // c
    subs = range(n_sub)
    offs = [pl.multiple_of(jnp.where(d == 0, i, n_sub - 1 - i) * c, c) for i in subs]
    t, incl, strict = _dir_masks(d, c)
    cum = _bf(jnp.where(incl, 1.0, 0.0))
    cum_t = _bf(jnp.where(t <= 0, 1.0, 0.0))
    g = [g_ref[0, pl.ds(offs[i], c), :] for i in subs]
    beta = [b_ref[0, pl.ds(offs[i], c), :] for i in subs]
    parts = [_split3(g[i]) for i in subs]
    gc = [sum(_dot(cum, p) for p in parts[i]) for i in subs]
    gr = [sum(_dot_tn(p, cum_t) for p in parts[i]) for i in subs]
    gtot = [jnp.sum(g[i], axis=0, keepdims=True) for i in subs]

    def rows(i, lo):
        return qkv_ref[pl.ds(offs[i], c), lo:lo + hd]

    q_l = [[rows(i, hq * hd) for hq in range(GDN_QK_HEADS)] for i in subs]
    k_l = [[rows(i, GDN_KEY + hq * hd) for hq in range(GDN_QK_HEADS)] for i in subs]
    gram = [[_dot_nt(_bf(jnp.concatenate([k, q], axis=0)), _bf(k)) for q, k in zip(q_l[i], k_l[i])]
            for i in subs]
    zeros_cc = jnp.zeros((c, 2 * c), BF16)
    zeros_rhs = jnp.zeros((c, 2 * hd), BF16)
    lane_cc = lax.broadcasted_iota(jnp.int32, (c, 2 * c), 1)
    row_cc = lax.broadcasted_iota(jnp.int32, (c, 2 * c), 0)
    eye_hi = _bf(jnp.where(lane_cc == row_cc + c, 1.0, 0.0))
    is_hi = lane_cc >= c

    def prepare(chains):
        ids = range(len(chains))
        gcol = [gc[i][:, h:h + 1] for i, h in chains]
        bcol = [beta[i][:, h:h + 1] for i, h in chains]
        dec = [jnp.where(incl, jnp.exp(jnp.where(incl, gcol[n] - gr[i][h:h + 1, :], 0.0)), 0.0)
               for n, (i, h) in enumerate(chains)]
        eg = [jnp.exp(x) for x in gcol]
        m = [jnp.where(strict, -(gram[i][h // rep][:c] * dec[n] * bcol[n]), 0.0) for n, (i, h) in enumerate(chains)]
        rhs = [jnp.concatenate([k_l[i][h // rep] * (bcol[n] * eg[n]), rows(i, 2 * GDN_KEY + h * hd) * bcol[n]],
                               axis=1) for n, (i, h) in enumerate(chains)]

        m_hi = [_bf(x) for x in m]
        cb = [jnp.concatenate([m_hi[n], jnp.zeros((c, c), BF16)], axis=1) + eye_hi for n in ids]
        for _ in range(6):
            r = [_dot(cb[n], jnp.concatenate([cb[n], zeros_cc], axis=0)) for n in ids]
            cb = [_bf(r[n]) + jnp.where(is_hi, cb[n], jnp.zeros_like(cb[n])) for n in ids]
        nb = [cb[n] - eye_hi for n in ids]

        def apply_n(y):
            return [_dot(nb[n], jnp.concatenate([zeros_rhs, _bf(y[n])], axis=0)) for n in ids]

        nr = apply_n(rhs)
        x = [rhs[n] + nr[n] for n in ids]
        resid = [_dot(m_hi[n], _bf(x[n])) - nr[n] for n in ids]
        nr = apply_n(resid)
        x = [x[n] + (resid[n] + nr[n]) for n in ids]
        qk = [_bf(gram[i][h // rep][c:] * dec[n]) for n, (i, h) in enumerate(chains)]
        gt = [gtot[i][:, h:h + 1] for i, h in chains]
        kd = [_bf(k_l[i][h // rep] * jnp.exp(gt[n] - gcol[n])) for n, (i, h) in enumerate(chains)]
        wq = [_bf(jnp.concatenate([x[n][:, :hd], q_l[i][h // rep] * eg[n]], axis=0))
              for n, (i, h) in enumerate(chains)]
        decay = [jnp.exp(x) for x in gt]
        return x, qk, kd, wq, decay

    per_group = GDN_GROUP // GDN_V_HEADS
    st = [s_ref[h] for h in heads]
    for first in range(0, n_sub, per_group):
        chunk_ids = range(first, first + per_group)
        x, qk, kd, wq, decay = prepare([(i, h) for i in chunk_ids for h in heads])
        for j, i in enumerate(chunk_ids):
            base = j * GDN_V_HEADS
            r = [_dot(wq[base + h], _bf(st[h])) for h in heads]
            unb = [_bf(x[base + h][:, hd:] - r[h][:c]) for h in heads]
            for h in heads:
                o_ref[0, pl.ds(offs[i], c), h * hd:(h + 1) * hd] = _bf(r[h][c:] + _dot(qk[base + h], unb[h]))
            st = [st[h] * decay[base + h] + _dot_tn(kd[base + h], unb[h]) for h in heads]
    for h in heads:
        s_ref[h] = st[h]


def _gdn_core(rw, qkv, g_dir, b_dir):
    blk = GDN_BLOCK
    rb = lambda b, d, s: rw.chunk_block(b, d, s, blk)
    return pl.pallas_call(
        _gdn_core_kernel,
        grid=(rw.batch, 2, (rw.n_lat + rw.n_ctx) // blk),
        in_specs=[pl.BlockSpec((blk, GDN_CONV), lambda b, d, s: (rb(b, d, s), 0)),
                  pl.BlockSpec((1, blk, GDN_V_HEADS), lambda b, d, s: (d, rb(b, d, s), 0)),
                  pl.BlockSpec((1, blk, GDN_V_HEADS), lambda b, d, s: (d, rb(b, d, s), 0))],
        out_specs=pl.BlockSpec((1, blk, GDN_VAL), lambda b, d, s: (d, rb(b, d, s), 0)),
        out_shape=jax.ShapeDtypeStruct((2, rw.rows, GDN_VAL), BF16),
        scratch_shapes=[pltpu.VMEM((GDN_V_HEADS, GDN_HD, GDN_HD), F32)],
        compiler_params=_cparams(("parallel", "parallel", "arbitrary")),
    )(qkv, g_dir, b_dir)


def _rope_half(x, cos, sin_signed):
    return x * cos + pltpu.roll(x, x.shape[1] // 2, 1) * sin_signed


def _ret_core_kernel(qk_ref, v_ref, cos_ref, sin_ref, dl_ref, o_ref, s_ref, dec_ref, *, ctx_steps):
    d, s = pl.program_id(1), pl.program_id(2)
    c = qk_ref.shape[0]
    heads = range(RET_HEADS)
    is_lat = s >= ctx_steps
    cos = jnp.where(is_lat, cos_ref[...], 1.0)
    sin = jnp.where(is_lat, sin_ref[...], 0.0)
    t, incl, _ = _dir_masks(d, c)
    tf = t.astype(F32)
    pos = lax.broadcasted_iota(jnp.int32, (c, 1), 0)
    ridx = jnp.where(d == 0, pos + 1, c - pos).astype(F32)
    log_gamma = -_softplus(-dl_ref[0])

    def rope(x):
        hw = RET_DK // 2
        return jnp.concatenate([_rope_half(x[:, :hw], cos[:, :hw], sin[:, :hw]),
                                _rope_half(x[:, hw:], cos[:, hw:], sin[:, hw:])], axis=1)

    lg = [log_gamma[:, h:h + 1] for h in heads]

    @pl.when(s == 0)
    def _():
        s_ref[...] = jnp.zeros_like(s_ref)
        for h in heads:
            dec_ref[h] = jnp.where(incl, jnp.exp(jnp.where(incl, lg[h] * tf, 0.0)), 0.0)

    q = [rope(qk_ref[:, h * RET_DK:(h + 1) * RET_DK].astype(F32)) * (RET_DK ** -0.5) for h in heads]
    k = [rope(qk_ref[:, RET_KEY + h * RET_DK:RET_KEY + (h + 1) * RET_DK].astype(F32)) for h in heads]
    vb = [v_ref[:, h * RET_DV:(h + 1) * RET_DV] for h in heads]
    gcum = [lg[h] * ridx for h in heads]
    qk = [_bf(_dot_nt(_bf(q[h]), _bf(k[h])) * dec_ref[h]) for h in heads]
    st = [s_ref[h] for h in heads]
    for h in heads:
        o_ref[0, :, h * RET_DV:(h + 1) * RET_DV] = _bf(
            _dot(_bf(q[h] * jnp.exp(gcum[h])), _bf(st[h])) + _dot(qk[h], vb[h]))
    gtot = [lg[h] * float(c) for h in heads]
    kd = [_bf(k[h] * jnp.exp(gtot[h] - gcum[h])) for h in heads]
    for h in heads:
        s_ref[h] = st[h] * jnp.exp(gtot[h]) + _dot_tn(kd[h], vb[h])


def _ret_core(rw, proj, cos_tab, sin_tab, decay_logit):
    c = RET_CHUNK
    rb = lambda b, d, s: rw.chunk_block(b, d, s, c)
    dl = jnp.zeros((2, 1, 128), F32).at[:, 0, :RET_HEADS].set(decay_logit)
    return pl.pallas_call(
        functools.partial(_ret_core_kernel, ctx_steps=rw.n_ctx // c),
        grid=(rw.batch, 2, (rw.n_lat + rw.n_ctx) // c),
        in_specs=[pl.BlockSpec((c, 2 * RET_KEY), lambda b, d, s: (rb(b, d, s), 0)),
                  pl.BlockSpec((c, RET_VAL), lambda b, d, s: (rb(b, d, s), 1)),
                  pl.BlockSpec((c, RET_DK), lambda b, d, s: (rw.lat_chunk(d, s, c), 0)),
                  pl.BlockSpec((c, RET_DK), lambda b, d, s: (rw.lat_chunk(d, s, c), 0)),
                  pl.BlockSpec((1, 1, 128), lambda b, d, s: (d, 0, 0))],
        out_specs=pl.BlockSpec((1, c, RET_VAL), lambda b, d, s: (d, rb(b, d, s), 0)),
        out_shape=jax.ShapeDtypeStruct((2, rw.rows, RET_VAL), BF16),
        scratch_shapes=[pltpu.VMEM((RET_HEADS, RET_DK, RET_DV), F32), pltpu.VMEM((RET_HEADS, c, c), F32)],
        compiler_params=_cparams(("parallel", "parallel", "arbitrary")),
    )(proj, proj, cos_tab, sin_tab, dl)


def _lru_scan_kernel(x_ref, p_ref, n_ref, cw_ref, cb_ref, wg_ref, bg_ref, lam_ref, h_ref, carry_ref, *, rw, rev):
    @pl.when(pl.program_id(1) == 0)
    def _():
        carry_ref[...] = jnp.zeros_like(carry_ref)

    t = x_ref.shape[0]
    tile = rw.chunk_block(pl.program_id(0), 1 if rev else 0, pl.program_id(1), t)
    first, last = _seg_flags(rw, tile)
    xs = _conv_core(x_ref, p_ref, n_ref, cw_ref, first, last) + cb_ref[...]
    row8 = lax.broadcasted_iota(jnp.int32, (t // 8, 8, LRU_BW), 1)
    sp = _softplus(-lam_ref[...])
    for n in range(LRU_BLOCKS):
        sl = slice(n * LRU_BW, (n + 1) * LRU_BW)
        xn = xs[:, sl]
        gates = jax.nn.sigmoid(_dot(_bf(xn), wg_ref[n]) + bg_ref[n])
        log_a = -LRU_C * gates[:, :LRU_BW] * sp[:, sl]
        a = jnp.exp(log_a)
        y = -jnp.tanh(log_a) * (a * a + 1.0)
        b = jnp.where(y > 0.0, y * lax.rsqrt(y), 0.0) * gates[:, LRU_BW:] * xn
        groups = t // 8
        a3 = a.reshape(groups, 8, LRU_BW)
        b3 = b.reshape(groups, 8, LRU_BW)
        for k in (1, 2, 4):
            if rev:
                keep = row8 < 8 - k
                shift = 8 - k
            else:
                keep = row8 >= k
                shift = k
            sa = jnp.where(keep, pltpu.roll(a3, shift, 1), 1.0)
            sb = jnp.where(keep, pltpu.roll(b3, shift, 1), 0.0)
            b3 = b3 + a3 * sb
            a3 = a3 * sa
        carry = carry_ref[:, sl]
        for gi in (reversed(range(groups)) if rev else range(groups)):
            hg = a3[gi] * carry + b3[gi]
            carry = hg[0:1] if rev else hg[7:8]
            h_ref[gi * 8:(gi + 1) * 8, sl] = hg
        carry_ref[:, sl] = carry


def _lru_scan(rw, proj, conv_w, conv_b, w_gate, b_gate, lam, rev):
    t = TM_CONV
    th = t // HALO
    nh = rw.rows // HALO
    d = 1 if rev else 0
    rb = lambda b, s: rw.chunk_block(b, d, s, t)
    return pl.pallas_call(
        functools.partial(_lru_scan_kernel, rw=rw, rev=rev),
        grid=(rw.batch, (rw.n_lat + rw.n_ctx) // t),
        in_specs=[pl.BlockSpec((t, LRU_W), lambda b, s: (rb(b, s), 0)),
                  pl.BlockSpec((HALO, LRU_W), lambda b, s: (jnp.maximum(rb(b, s) * th - 1, 0), 0)),
                  pl.BlockSpec((HALO, LRU_W), lambda b, s: (jnp.minimum((rb(b, s) + 1) * th, nh - 1), 0)),
                  pl.BlockSpec((4, LRU_W), lambda b, s: (0, 0)),
                  pl.BlockSpec((1, LRU_W), lambda b, s: (0, 0)),
                  pl.BlockSpec((LRU_BLOCKS, LRU_BW, 2 * LRU_BW), lambda b, s: (0, 0, 0)),
                  pl.BlockSpec((LRU_BLOCKS, 1, 2 * LRU_BW), lambda b, s: (0, 0, 0)),
                  pl.BlockSpec((1, LRU_W), lambda b, s: (0, 0))],
        out_specs=pl.BlockSpec((t, LRU_W), lambda b, s: (rb(b, s), 0)),
        out_shape=jax.ShapeDtypeStruct((rw.rows, LRU_W), F32),
        scratch_shapes=[pltpu.VMEM((1, LRU_W), F32)],
        compiler_params=_cparams(("parallel", "arbitrary")),
    )(proj, proj, proj, conv_w, conv_b.reshape(1, LRU_W), _bf(w_gate),
      b_gate.reshape(LRU_BLOCKS, 1, 2 * LRU_BW), lam.reshape(1, LRU_W))


def _rope16(x, tab):
    cos, s1, s2 = tab[:, :128], tab[:, 128:256], tab[:, 256:]
    cols = []
    for cidx in range(x.shape[1] // 128):
        xc = x[:, cidx * 128:(cidx + 1) * 128]
        cols.append(xc * cos + pltpu.roll(xc, 112, 1) * s1 + pltpu.roll(xc, 16, 1) * s2)
    return cols[0] if len(cols) == 1 else jnp.concatenate(cols, axis=1)


def _swa_kernel(sink_ref, q_ref, kvp_ref, kvc_ref, kvn_ref, kvx_ref, tp_ref, tc_ref, tn_ref, o_ref):
    qb = pl.program_id(1)
    nb = pl.num_programs(1)
    blk = SWA_BLOCK
    n_loc = 3 * blk
    n_keys = n_loc + kvx_ref.shape[0]
    tab_c = tc_ref[...]
    keys = lambda ref: ref[:, :SWA_KVD].astype(F32)
    q = _rope16(q_ref[...].astype(F32), tab_c) * (SWA_HD ** -0.5 * LOG2E)
    k_all = jnp.concatenate([_rope16(keys(kvp_ref), tp_ref[...]), _rope16(keys(kvc_ref), tab_c),
                             _rope16(keys(kvn_ref), tn_ref[...]), keys(kvx_ref)], axis=0)
    v_all = jnp.concatenate([ref[:, SWA_KVD:] for ref in (kvp_ref, kvc_ref, kvn_ref, kvx_ref)], axis=0).astype(F32)
    assert SWA_WINDOW >= blk - 1
    qi = lax.broadcasted_iota(jnp.int32, (blk, blk), 0)
    kj = lax.broadcasted_iota(jnp.int32, (blk, blk), 1)
    valid_prev = jnp.logical_and(kj - blk - qi >= -SWA_WINDOW, qb > 0)
    valid_next = jnp.logical_and(kj + blk - qi <= SWA_WINDOW, qb < nb - 1)
    lane = lax.broadcasted_iota(jnp.int32, (n_keys, 128), 1)
    low = lane < SWA_HD
    out_low = lax.broadcasted_iota(jnp.int32, (blk, 128), 1) < SWA_HD

    def expo(scores, sink):
        sc = jnp.concatenate([jnp.where(valid_prev, scores[:, :blk], -jnp.inf), scores[:, blk:2 * blk],
                              jnp.where(valid_next, scores[:, 2 * blk:n_loc], -jnp.inf), scores[:, n_loc:]], axis=1)
        mx = jnp.maximum(jnp.max(sc, axis=-1, keepdims=True), sink)
        return _bf(jnp.exp2(sc - mx)), jnp.exp2(sink - mx)

    group = SWA_QH // SWA_KVH
    for g in range(SWA_KVH):
        col = slice((g // 2) * 128, (g // 2 + 1) * 128)
        native_low = g % 2 == 0
        sel = low if native_low else jnp.logical_not(low)
        k_nat = jnp.where(sel, k_all[:, col], 0.0)
        v_nat = jnp.where(sel, v_all[:, col], 0.0)
        k_oth = pltpu.roll(k_nat, SWA_HD, 1)
        v_oth = pltpu.roll(v_nat, SWA_HD, 1)
        k_lo, k_hi = (k_nat, k_oth) if native_low else (k_oth, k_nat)
        v_lo, v_hi = (v_nat, v_oth) if native_low else (v_oth, v_nat)
        v_lo = jnp.where(lane == SWA_HD, 1.0, v_lo)
        v_hi = jnp.where(lane == 0, 1.0, v_hi)
        k_lo, k_hi, v_lo, v_hi = _bf(k_lo), _bf(k_hi), _bf(v_lo), _bf(v_hi)
        for cidx in range(group // 2):
            qc = (g * group) // 2 + cidx
            qv = _bf(q[:, qc * 128:(qc + 1) * 128])
            e_lo, sink_lo = expo(_dot_nt(qv, k_lo), sink_ref[2 * qc] * LOG2E)
            e_hi, sink_hi = expo(_dot_nt(qv, k_hi), sink_ref[2 * qc + 1] * LOG2E)
            a_lo = _dot(e_lo, v_lo)
            a_hi = _dot(e_hi, v_hi)
            inv_lo = 1.0 / (a_lo[:, SWA_HD:SWA_HD + 1] + sink_lo)
            inv_hi = 1.0 / (a_hi[:, 0:1] + sink_hi)
            o_ref[:, qc * 128:(qc + 1) * 128] = _bf(jnp.where(out_low, a_lo * inv_lo, a_hi * inv_hi))


def _swa_core(rw, proj, sink, tab):
    blk = SWA_BLOCK
    nb = rw.n_lat // blk
    kvcol = SWA_QD // (2 * SWA_KVD)
    ctx_blk = lambda b: rw.lat_rows // rw.n_ctx + b
    prev = lambda q: jnp.maximum(q - 1, 0)
    nxt = lambda q: jnp.minimum(q + 1, nb - 1)
    kv = lambda f: pl.BlockSpec((blk, 2 * SWA_KVD), lambda b, q: (b * nb + f(q), kvcol))
    same = lambda q: q
    return pl.pallas_call(
        _swa_kernel,
        grid=(rw.batch, nb),
        in_specs=[pl.BlockSpec(memory_space=pltpu.SMEM),
                  pl.BlockSpec((blk, SWA_QD), lambda b, q: (b * nb + q, 0)),
                  kv(prev), kv(same), kv(nxt),
                  pl.BlockSpec((rw.n_ctx, 2 * SWA_KVD), lambda b, q: (ctx_blk(b), kvcol)),
                  pl.BlockSpec((blk, 384), lambda b, q: (prev(q), 0)),
                  pl.BlockSpec((blk, 384), lambda b, q: (q, 0)),
                  pl.BlockSpec((blk, 384), lambda b, q: (nxt(q), 0))],
        out_specs=pl.BlockSpec((blk, SWA_QD), lambda b, q: (b * nb + q, 0)),
        out_shape=jax.ShapeDtypeStruct((rw.lat_rows, SWA_QD), BF16),
        compiler_params=_cparams(("parallel", "parallel")),
    )(sink, proj, proj, proj, proj, proj, tab, tab, tab)


def _axial_angles(n_lat, dh):
    t = jnp.arange(n_lat)
    row = (t // GRID_W).astype(F32)
    col = (t % GRID_W).astype(F32)
    nf = dh // 4
    inv = ROPE_THETA ** (-jnp.arange(nf, dtype=F32) / nf)
    return row[:, None] * inv, col[:, None] * inv


def _ret_tables(n_lat):
    ar, ac = _axial_angles(n_lat, RET_DK)
    cos = jnp.concatenate([jnp.cos(ar), jnp.cos(ar), jnp.cos(ac), jnp.cos(ac)], axis=1)
    sin = jnp.concatenate([-jnp.sin(ar), jnp.sin(ar), -jnp.sin(ac), jnp.sin(ac)], axis=1)
    return cos, sin


def _swa_table(n_lat):
    ar, ac = _axial_angles(n_lat, SWA_HD)
    z = jnp.zeros_like(ar)
    cos = jnp.concatenate([jnp.cos(ar), jnp.cos(ar), jnp.cos(ac), jnp.cos(ac)], axis=1)
    s1 = jnp.concatenate([-jnp.sin(ar), z, -jnp.sin(ac), z], axis=1)
    s2 = jnp.concatenate([z, jnp.sin(ar), z, jnp.sin(ac)], axis=1)
    return jnp.concatenate([jnp.tile(cos, (1, 2)), jnp.tile(s1, (1, 2)), jnp.tile(s2, (1, 2))], axis=1)


def _gdn_layer(rw, layer, x, mod3, gain, w_in, conv_w, a_log, dt_bias, norm_g, w_out):
    n_ab = w_in.shape[1] - GDN_MAIN
    w_ab = _bf(jnp.pad(w_in[:, GDN_MAIN:], ((0, 0), (0, 128 - n_ab))))
    proj, ab = _inproj(rw, x, gain, mod3, layer, _bf(w_in), w_ab, n_cols=GDN_MAIN)
    hv = GDN_V_HEADS
    alog_row = jnp.zeros((1, 128), F32).at[0, 0:hv].set(a_log[0]).at[0, 2 * hv:3 * hv].set(a_log[1])
    dtb_row = jnp.zeros((1, 128), F32).at[0, 0:hv].set(dt_bias[0]).at[0, 2 * hv:3 * hv].set(dt_bias[1])
    cb = 2048
    row_spec = pl.BlockSpec((TM_CONV, 128), lambda i, j: (i, 0))
    vec_spec = pl.BlockSpec((1, 128), lambda i, j: (0, 0))
    dir_spec = pl.BlockSpec((2, TM_CONV, hv), lambda i, j: (0, i, 0))
    qkv, g_dir, b_dir = pl.pallas_call(
        functools.partial(_gdn_conv_kernel, rw=rw),
        grid=(rw.rows // TM_CONV, GDN_CONV // cb),
        in_specs=_conv_specs(rw, cb) + [row_spec, vec_spec, vec_spec],
        out_specs=[pl.BlockSpec((TM_CONV, cb), lambda i, j: (i, j)), dir_spec, dir_spec],
        out_shape=[jax.ShapeDtypeStruct((rw.rows, GDN_CONV), F32),
                   jax.ShapeDtypeStruct((2, rw.rows, hv), F32),
                   jax.ShapeDtypeStruct((2, rw.rows, hv), F32)],
        compiler_params=_cparams(("parallel", "arbitrary")),
    )(proj, proj, proj, conv_w, ab, alog_row, dtb_row)
    o = _gdn_core(rw, qkv, g_dir, b_dir)
    kern = functools.partial(_gdn_out_kernel, n_lat_tiles=rw.lat_rows // TM_OUT if isinstance(x, tuple) else None)
    return _two_dir_out(kern, rw, layer, x, mod3, o, proj, GDN_CONV // GDN_VAL, norm_g, _bf(w_out))


def _ret_layer(rw, layer, x, mod3, gain, w_in, decay_logit, norm_g, w_out):
    proj = _inproj(rw, x, gain, mod3, layer, _bf(w_in))
    cos_tab, sin_tab = _ret_tables(rw.n_lat)
    o = _ret_core(rw, proj, cos_tab, sin_tab, decay_logit)
    return _two_dir_out(_ret_out_kernel, rw, layer, x, mod3, o, proj, (2 * RET_KEY + RET_VAL) // RET_VAL,
                        norm_g, _bf(w_out))


def _lru_layer(rw, layer, x, mod3, gain, w_in, conv_w, conv_b, w_gate, b_gate, lam, w_out):
    proj = _inproj(rw, x, gain, mod3, layer, _bf(w_in))
    h0 = _lru_scan(rw, proj, conv_w, conv_b, w_gate[0], b_gate[0], lam[0], rev=False)
    h1 = _lru_scan(rw, proj, conv_w, conv_b, w_gate[1], b_gate[1], lam[1], rev=True)
    tm = TM_OUT
    specs = [pl.BlockSpec((tm, LRU_W), lambda i: (i, 0)),
             pl.BlockSpec((tm, LRU_W), lambda i: (i, 0)),
             pl.BlockSpec((tm, LRU_W), lambda i: (i, 1))]
    return _outproj_call(_lru_out_kernel, rw, rw.rows, layer, x, mod3, specs, [h0, h1, proj], _bf(w_out), [])


def _swa_layer(rw, layer, x, mod3, gain, w_in, sink, w_out):
    proj = _inproj(rw, x, gain, mod3, layer, _bf(w_in))
    o = _swa_core(rw, proj, sink, _swa_table(rw.n_lat))
    specs = [pl.BlockSpec((TM_OUT, SWA_QD), lambda i: (i, 0))]
    return _outproj_call(_swa_out_kernel, rw, rw.lat_rows, layer, x, mod3, specs, [o], _bf(w_out), [])


def kernel(x, c, ctx, c_ctx, norm_mix_g, norm_ffn_g, w_mod, b_mod, w_ff1, w_ff2, norm_out_g, gdn_w_in, gdn_conv_w, gdn_a_log, gdn_dt_bias, gdn_norm_g, gdn_w_out, ret_w_in, ret_decay_logit, ret_norm_g, ret_w_out, lru_w_in, lru_conv_w, lru_conv_b, lru_w_gate, lru_b_gate, lru_lambda, lru_w_out, swa_w_in, swa_sink, swa_w_out):
    batch, n_lat, _ = x.shape
    n_ctx = ctx.shape[1]
    assert batch + 1 <= MOD_ROWS and n_lat % TM_IN == 0 and (batch * n_ctx) % TM_IN == 0
    assert w_mod.shape[0] == DEPTH and gdn_w_in.shape[0] == 1 and swa_w_in.shape[0] == 1
    rw = _Rows(batch, n_lat, n_ctx)
    xf = (x.reshape(rw.lat_rows, D), ctx.reshape(batch * n_ctx, D))
    cc = jnp.zeros((MOD_ROWS, D), F32).at[:batch].set(c).at[batch].set(c_ctx)
    mod3 = _adaln(cc, w_mod, b_mod)

    def mlp(layer, xin, n_rows, final):
        return _mlp(rw, xin, n_rows, norm_ffn_g[layer], mod3, layer, w_ff1, w_ff2,
                    norm_out_g, final)

    xf = _gdn_layer(rw, 0, xf, mod3, norm_mix_g[0], gdn_w_in[0], gdn_conv_w[0], gdn_a_log[0], gdn_dt_bias[0],
                    gdn_norm_g[0], gdn_w_out[0])
    xf = mlp(0, xf, rw.rows, False)
    xf = _ret_layer(rw, 1, xf, mod3, norm_mix_g[1], ret_w_in[0], ret_decay_logit[0], ret_norm_g[0], ret_w_out[0])
    xf = mlp(1, xf, rw.rows, False)
    xf = _lru_layer(rw, 2, xf, mod3, norm_mix_g[2], lru_w_in[0], lru_conv_w[0], lru_conv_b[0], lru_w_gate[0],
                    lru_b_gate[0], lru_lambda[0], lru_w_out[0])
    xf = mlp(2, xf, rw.rows, False)
    xl = _swa_layer(rw, 3, xf, mod3, norm_mix_g[3], swa_w_in[0], swa_sink[0], swa_w_out[0])
    out = mlp(3, xl, rw.lat_rows, True)
    return out.reshape(batch, n_lat, D)
```
